```python
import math
import jax, jax.numpy as jnp
from jax import lax
import numpy as np

D_MODEL = 2048
BATCH = 4
SEQ = 4096
DEPTH = 1

MLSTM_HEADS = 8
MLSTM_HEAD_DIM = D_MODEL // 16
MLSTM_WIDTH = MLSTM_HEADS * MLSTM_HEAD_DIM
MLSTM_CHUNK = 64
CONV_WIDTH = 4

ATTN_GROUPS = ((128, 1), (512, 4), (2048, 16))
N_GROUPS = 3
HEADS_PER_GROUP = 4
ATTN_HEAD_DIM = D_MODEL // 16
ATTN_HEADS = HEADS_PER_GROUP * N_GROUPS
ATTN_WIDTH = ATTN_HEADS * ATTN_HEAD_DIM
ATTN_OUT_WIDTH = HEADS_PER_GROUP * ATTN_HEAD_DIM
ATTN_BLOCK = 128
REL_BUCKETS = 32
REL_MAX_DIST = 2048

N_BRANCHES = 2
IN_SIZES = (2 * MLSTM_WIDTH, MLSTM_WIDTH, MLSTM_WIDTH, MLSTM_HEADS, MLSTM_HEADS,
            ATTN_WIDTH, ATTN_WIDTH, ATTN_WIDTH, N_BRANCHES * D_MODEL)
IN_OFFSETS = tuple(sum(IN_SIZES[:i + 1]) for i in range(len(IN_SIZES) - 1))
D_IN = sum(IN_SIZES)

N_EXPERTS = 32
TOP_K = 4
D_FF_EXPERT = D_MODEL
SWIGLU_LIMIT = 7.0
SWIGLU_ALPHA = 1.702
MOE_BLOCK = 256
RMS_EPS = 1e-6

kernel_name = 'hybrid_mlstm_dilated_attn_moe_block'


def _rmsnorm(x, g):
    xf = x.astype(jnp.float32)
    y = xf * lax.rsqrt(jnp.mean(xf * xf, axis=-1, keepdims=True) + RMS_EPS) * g.astype(jnp.float32)
    return y.astype(x.dtype)


def _causal_conv(x, w, b):
    c = x.shape[-1]
    y = lax.conv_general_dilated(x, w[:, None, :].astype(x.dtype), window_strides=(1,),
                                 padding=[(CONV_WIDTH - 1, 0)],
                                 dimension_numbers=('NWC', 'WIO', 'NWC'),
                                 feature_group_count=c)
    return y + b.astype(x.dtype)


def _mlstm(q, k, v, log_i, log_f):
    bsz, nh, s, dh = q.shape
    nc = s // MLSTM_CHUNK

    def chunks(t):
        t = t.reshape(t.shape[:2] + (nc, MLSTM_CHUNK) + t.shape[3:])
        return jnp.moveaxis(t, 2, 0)

    qc, kc, vc, ic = chunks(q), chunks(k), chunks(v), chunks(log_i)
    bc = jnp.cumsum(chunks(log_f), axis=-1)
    causal = jnp.tril(jnp.ones((MLSTM_CHUNK, MLSTM_CHUNK), dtype=bool))

    def step(carry, xs):
        c_state, n_state, m_state = carry
        qt, kt, vt, it, bt = xs
        logw = bt[..., :, None] - bt[..., None, :] + it[..., None, :]
        logw = jnp.where(causal, logw, -jnp.inf)
        inter = bt + m_state[..., None]
        m_out = jnp.maximum(inter, jnp.max(logw, axis=-1))
        w = jnp.exp(logw - m_out[..., None]) * jnp.einsum('bhtk,bhsk->bhts', qt, kt)
        s_inter = jnp.exp(inter - m_out)
        num = (jnp.einsum('bhts,bhsv->bhtv', w, vt)
               + s_inter[..., None] * jnp.einsum('bhtk,bhkv->bhtv', qt, c_state))
        den = jnp.sum(w, axis=-1) + s_inter * jnp.einsum('bhtk,bhk->bht', qt, n_state)
        h = num / jnp.maximum(jnp.abs(den), jnp.exp(-m_out))[..., None]
        b_last = bt[..., -1]
        logw_state = b_last[..., None] - bt + it
        m_new = jnp.maximum(b_last + m_state, jnp.max(logw_state, axis=-1))
        ws = jnp.exp(logw_state - m_new[..., None])
        decay = jnp.exp(b_last + m_state - m_new)
        c_new = decay[..., None, None] * c_state + jnp.einsum('bhs,bhsk,bhsv->bhkv', ws, kt, vt)
        n_new = decay[..., None] * n_state + jnp.einsum('bhs,bhsk->bhk', ws, kt)
        return (c_new, n_new, m_new), h

    init = (jnp.zeros((bsz, nh, dh, dh), jnp.float32),
            jnp.zeros((bsz, nh, dh), jnp.float32),
            jnp.zeros((bsz, nh), jnp.float32))
    _, hs = lax.scan(step, init, (qc, kc, vc, ic, bc))
    return jnp.moveaxis(hs, 0, 2).reshape(bsz, nh, s, dh)


def _t5_bucket(dist):
    max_exact = REL_BUCKETS // 2
    d_f = jnp.maximum(dist, 1).astype(jnp.float32)
    large = max_exact + (jnp.log(d_f / max_exact) / math.log(REL_MAX_DIST / max_exact)
                         * (REL_BUCKETS - max_exact)).astype(jnp.int32)
    large = jnp.minimum(large, REL_BUCKETS - 1)
    return jnp.where(dist < max_exact, dist, large)


def _dilated_group_attention(q, k, v, bias_vec, window, dilation):
    bsz, s, nh, dh = q.shape
    n = s // dilation
    span = window // dilation
    qb_len = ATTN_BLOCK
    nprev = -(-span // qb_len)
    nb = -(-n // qb_len)
    kb_len = (nprev + 1) * qb_len

    def residues(t):
        return t.reshape(bsz, n, dilation, nh, dh).transpose(0, 2, 3, 1, 4).reshape(
            bsz * dilation, nh, n, dh)

    def pad_keys(t):
        t = jnp.pad(residues(t), ((0, 0), (0, 0), (nprev * qb_len, nb * qb_len - n), (0, 0)))
        t = t.reshape(bsz * dilation, nh, nb + nprev, qb_len, dh)
        return jnp.concatenate([t[:, :, j:j + nb] for j in range(nprev + 1)], axis=3)

    qr = jnp.pad(residues(q), ((0, 0), (0, 0), (0, nb * qb_len - n), (0, 0)))
    qr = qr.reshape(bsz * dilation, nh, nb, qb_len, dh)
    kr, vr = pad_keys(k), pad_keys(v)

    qpos = jnp.arange(qb_len)[:, None]
    kpos = jnp.arange(kb_len)[None, :]
    dist = qpos + nprev * qb_len - kpos
    key_sub = jnp.arange(nb)[:, None] * qb_len + kpos - nprev * qb_len
    valid = ((dist >= 0) & (dist <= span))[None] & (key_sub >= 0)[:, None, :]
    bias = bias_vec[:, jnp.clip(dist, 0, span)].astype(jnp.float32)

    scores = jnp.einsum('bhnqd,bhnkd->bhnqk', qr, kr).astype(jnp.float32) * (dh ** -0.5)
    scores = jnp.where(valid, scores + bias[:, None], -jnp.inf)
    mx = jnp.max(scores, axis=-1, keepdims=True)
    p = jnp.exp(scores - mx)
    denom = jnp.sum(p, axis=-1)
    out = jnp.einsum('bhnqk,bhnkd->bhnqd', p, vr.astype(jnp.float32)) / denom[..., None]
    lse = mx[..., 0] + jnp.log(denom)

    out = out.reshape(bsz * dilation, nh, nb * qb_len, dh)[:, :, :n]
    out = out.reshape(bsz, dilation, nh, n, dh).transpose(0, 3, 1, 2, 4).reshape(bsz, s, nh, dh)
    lse = lse.reshape(bsz * dilation, nh, nb * qb_len)[:, :, :n]
    lse = lse.reshape(bsz, dilation, nh, n).transpose(0, 3, 1, 2).reshape(bsz, s, nh)
    return out, lse


def _mixer_block(h, w_in, conv_w, conv_b, igate_b, fgate_b, mlstm_norm_g, rel_bias,
                 w_branch_mlstm, w_branch_attn, w_out):
    bsz, s, _ = h.shape
    proj = h @ w_in
    qk_m, v_m, o_m, i_m, f_m, q_a, k_a, v_a, gates = jnp.split(proj, IN_OFFSETS, axis=-1)

    qk_m = jax.nn.silu(_causal_conv(qk_m, conv_w, conv_b))
    q_m, k_m = jnp.split(qk_m, 2, axis=-1)

    def heads(t):
        return t.reshape(bsz, s, MLSTM_HEADS, MLSTM_HEAD_DIM).transpose(0, 2, 1, 3).astype(jnp.float32)

    log_i = (i_m + igate_b).astype(jnp.float32).transpose(0, 2, 1)
    log_f = jax.nn.log_sigmoid((f_m + fgate_b).astype(jnp.float32)).transpose(0, 2, 1)
    hm = _mlstm(heads(q_m), heads(k_m) * (MLSTM_HEAD_DIM ** -0.5), heads(v_m), log_i, log_f)
    hm = _rmsnorm(hm.transpose(0, 2, 1, 3), mlstm_norm_g.reshape(MLSTM_HEADS, MLSTM_HEAD_DIM))
    hm = hm * jax.nn.sigmoid(o_m.astype(jnp.float32)).reshape(bsz, s, MLSTM_HEADS, MLSTM_HEAD_DIM)
    hm = hm.reshape(bsz, s, MLSTM_WIDTH).astype(h.dtype)

    q_a = q_a.reshape(bsz, s, N_GROUPS, HEADS_PER_GROUP, ATTN_HEAD_DIM)
    k_a = k_a.reshape(bsz, s, N_GROUPS, HEADS_PER_GROUP, ATTN_HEAD_DIM)
    v_a = v_a.reshape(bsz, s, N_GROUPS, HEADS_PER_GROUP, ATTN_HEAD_DIM)
    outs, lses = [], []
    for g, (window, dilation) in enumerate(ATTN_GROUPS):
        buckets = _t5_bucket(jnp.arange(window // dilation + 1, dtype=jnp.int32) * dilation)
        bias_vec = rel_bias[buckets][:, g * HEADS_PER_GROUP:(g + 1) * HEADS_PER_GROUP].T
        o_g, lse_g = _dilated_group_attention(q_a[:, :, g], k_a[:, :, g], v_a[:, :, g],
                                              bias_vec, window, dilation)
        outs.append(o_g)
        lses.append(lse_g)
    wts = jax.nn.softmax(jnp.stack(lses, axis=0), axis=0)
    ha = jnp.sum(wts[..., None] * jnp.stack(outs, axis=0), axis=0)
    ha = ha.reshape(bsz, s, ATTN_OUT_WIDTH).astype(h.dtype)

    g_m, g_a = jnp.split(jax.nn.sigmoid(gates.astype(jnp.float32)), N_BRANCHES, axis=-1)
    merged = (g_m * (hm @ w_branch_mlstm).astype(jnp.float32)
              + g_a * (ha @ w_branch_attn).astype(jnp.float32))
    return merged.astype(h.dtype) @ w_out


def _moe(h, router_w, router_b, w_gate_up, b_gate_up, w_down, b_down):
    bsz, s, d = h.shape
    n_tok = bsz * s
    n_asg = n_tok * TOP_K
    hf = h.reshape(n_tok, d)
    logits = (hf @ router_w + router_b).astype(jnp.float32)
    top_val, top_idx = lax.top_k(logits, TOP_K)
    gates = jax.nn.softmax(top_val, axis=-1)
    expert_flat = top_idx.reshape(n_asg)
    gate_flat = gates.reshape(n_asg)

    order = jnp.argsort(expert_flat)
    sorted_e = expert_flat[order]
    counts = jnp.zeros((N_EXPERTS,), jnp.int32).at[expert_flat].add(1)
    padded = (counts + MOE_BLOCK - 1) // MOE_BLOCK * MOE_BLOCK
    pend = jnp.cumsum(padded)
    pstart = pend - padded
    start = jnp.cumsum(counts) - counts
    dest = pstart[sorted_e] + jnp.arange(n_asg, dtype=jnp.int32) - start[sorted_e]
    n_pad = -(-(n_asg + N_EXPERTS * (MOE_BLOCK - 1)) // MOE_BLOCK) * MOE_BLOCK
    n_blk = n_pad // MOE_BLOCK
    buf_tok = jnp.zeros((n_pad,), jnp.int32).at[dest].set((order // TOP_K).astype(jnp.int32))
    buf_w = jnp.zeros((n_pad,), jnp.float32).at[dest].set(gate_flat[order])
    block_e = jnp.minimum(jnp.searchsorted(pend, jnp.arange(n_blk, dtype=jnp.int32) * MOE_BLOCK,
                                           side='right'), N_EXPERTS - 1)
    xb = hf[buf_tok].reshape(n_blk, MOE_BLOCK, d)

    def expert_block(args):
        xblk, e = args
        gu = xblk @ w_gate_up[e] + b_gate_up[e]
        gate, up = jnp.split(gu, 2, axis=-1)
        gate = jnp.minimum(gate, SWIGLU_LIMIT)
        up = jnp.clip(up, -SWIGLU_LIMIT, SWIGLU_LIMIT)
        glu = gate * jax.nn.sigmoid(SWIGLU_ALPHA * gate)
        return ((up + 1.0) * glu) @ w_down[e] + b_down[e]

    yb = lax.map(expert_block, (xb, block_e)).reshape(n_pad, d)
    out = jnp.zeros((n_tok, d), jnp.float32).at[buf_tok].add(yb.astype(jnp.float32) * buf_w[:, None])
    return out.reshape(bsz, s, d).astype(h.dtype)


def setup_inputs(seed: int = 0) -> dict:
    key = jax.random.key(seed)
    ks = jax.random.split(key, 20)
    f32 = jnp.float32

    def nrm(k, shape, scale):
        return jax.random.normal(k, shape, f32) * scale

    return {
        'x': nrm(ks[0], (BATCH, SEQ, D_MODEL), 1.0),
        'norm_mix_g': 1.0 + nrm(ks[1], (DEPTH, D_MODEL), 0.01),
        'w_in': nrm(ks[2], (DEPTH, D_MODEL, D_IN), D_MODEL ** -0.5),
        'conv_w': nrm(ks[3], (DEPTH, CONV_WIDTH, 2 * MLSTM_WIDTH), CONV_WIDTH ** -0.5),
        'conv_b': nrm(ks[4], (DEPTH, 2 * MLSTM_WIDTH), 0.01),
        'igate_b': nrm(ks[5], (DEPTH, MLSTM_HEADS), 0.1),
        'fgate_b': jnp.linspace(3.0, 6.0, MLSTM_HEADS, dtype=f32) + nrm(ks[6], (DEPTH, MLSTM_HEADS), 0.1),
        'mlstm_norm_g': 1.0 + nrm(ks[7], (DEPTH, MLSTM_WIDTH), 0.01),
        'rel_bias': nrm(ks[8], (REL_BUCKETS, ATTN_HEADS), 0.5),
        'w_branch_mlstm': nrm(ks[9], (DEPTH, MLSTM_WIDTH, D_MODEL), MLSTM_WIDTH ** -0.5),
        'w_branch_attn': nrm(ks[10], (DEPTH, ATTN_OUT_WIDTH, D_MODEL), ATTN_OUT_WIDTH ** -0.5),
        'w_out': nrm(ks[11], (DEPTH, D_MODEL, D_MODEL), D_MODEL ** -0.5),
        'norm_moe_g': 1.0 + nrm(ks[12], (DEPTH, D_MODEL), 0.01),
        'router_w': nrm(ks[13], (DEPTH, D_MODEL, N_EXPERTS), D_MODEL ** -0.5),
        'router_b': nrm(ks[14], (DEPTH, N_EXPERTS), 0.01),
        'w_gate_up': nrm(ks[15], (DEPTH, N_EXPERTS, D_MODEL, 2 * D_FF_EXPERT), D_MODEL ** -0.5),
        'b_gate_up': nrm(ks[16], (DEPTH, N_EXPERTS, 2 * D_FF_EXPERT), 0.01),
        'w_down': nrm(ks[17], (DEPTH, N_EXPERTS, D_FF_EXPERT, D_MODEL), D_FF_EXPERT ** -0.5),
        'b_down': nrm(ks[18], (DEPTH, N_EXPERTS, D_MODEL), 0.01),
        'norm_final_g': 1.0 + nrm(ks[19], (D_MODEL,), 0.01),
    }


def reference(x, norm_mix_g, w_in, conv_w, conv_b, igate_b, fgate_b, mlstm_norm_g, rel_bias,
              w_branch_mlstm, w_branch_attn, w_out, norm_moe_g, router_w, router_b,
              w_gate_up, b_gate_up, w_down, b_down, norm_final_g):
    for l in range(DEPTH):
        h = _rmsnorm(x, norm_mix_g[l])
        x = x + _mixer_block(h, w_in[l], conv_w[l], conv_b[l], igate_b[l], fgate_b[l],
                             mlstm_norm_g[l], rel_bias, w_branch_mlstm[l], w_branch_attn[l],
                             w_out[l])
        h = _rmsnorm(x, norm_moe_g[l])
        x = x + _moe(h, router_w[l], router_b[l], w_gate_up[l], b_gate_up[l],
                     w_down[l], b_down[l])
    return _rmsnorm(x, norm_final_g)
```

```python
import functools
import math

import jax
import jax.numpy as jnp
from jax import lax
from jax.experimental import pallas as pl
from jax.experimental.pallas import tpu as pltpu

F32 = jnp.float32
BF16 = jnp.bfloat16

D_MODEL = 2048
MLSTM_HEADS = 8
HEAD_DIM = 128
MLSTM_WIDTH = MLSTM_HEADS * HEAD_DIM
CONV_WIDTH = 4
ATTN_GROUPS = ((128, 1), (512, 4), (2048, 16))
N_GROUPS = 3
HEADS_PER_GROUP = 4
ATTN_HEADS = HEADS_PER_GROUP * N_GROUPS
ATTN_WIDTH = ATTN_HEADS * HEAD_DIM
GROUP_WIDTH = HEADS_PER_GROUP * HEAD_DIM
ATTN_BLOCK = 128
ATTN_SPAN = 128
REL_BUCKETS = 32
REL_MAX_DIST = 2048
N_EXPERTS = 32
TOP_K = 4
D_FF = D_MODEL
SWIGLU_LIMIT = 7.0
SWIGLU_ALPHA = 1.702
RMS_EPS = 1e-6
NEG = -1e30

COL_QK = 0
COL_V = 2 * MLSTM_WIDTH
COL_O = COL_V + MLSTM_WIDTH
COL_GATES = COL_O + MLSTM_WIDTH
COL_QA = COL_GATES + 2 * D_MODEL
COL_KA = COL_QA + ATTN_WIDTH
COL_VA = COL_KA + ATTN_WIDTH
D_MAIN = COL_VA + ATTN_WIDTH
LANES = 128
IF_WIDTH = 2 * LANES

MLSTM_CHUNK = 128
MOE_TILE = 256
VMEM_LIMIT = 56 * 1024 * 1024


def _sigmoid(x):
    return 1.0 / (1.0 + jnp.exp(-x))


def _log_sigmoid(x):
    return -(jnp.maximum(-x, 0.0) + jnp.log1p(jnp.exp(-jnp.abs(x))))


def _params(sem):
    return pltpu.CompilerParams(dimension_semantics=sem, vmem_limit_bytes=VMEM_LIMIT)


def _in_proj_kernel(x_ref, g_ref, w_ref, wif_ref, o_ref, oif_ref, h_ref, *, bm, rows):
    @pl.when(pl.program_id(1) == 0)
    def _():
        for r in range(0, bm, rows):
            x = x_ref[r:r + rows, :]
            ms = jnp.mean(x * x, axis=-1, keepdims=True)
            h = (x * lax.rsqrt(ms + RMS_EPS) * g_ref[...]).astype(BF16)
            h_ref[r:r + rows, :] = h
            oif_ref[r:r + rows, :] = jnp.dot(h, wif_ref[...], preferred_element_type=F32)

    o_ref[...] = jnp.dot(h_ref[...], w_ref[...], preferred_element_type=F32).astype(BF16)


def _in_proj(x2, g, w_main, w_if, *, bm=1024, bn=1280):
    n = x2.shape[0]
    bm = min(bm, n)
    kern = functools.partial(_in_proj_kernel, bm=bm, rows=256)
    return pl.pallas_call(
        kern,
        grid=(n // bm, D_MAIN // bn),
        in_specs=[
            pl.BlockSpec((bm, D_MODEL), lambda i, j: (i, 0)),
            pl.BlockSpec((1, D_MODEL), lambda i, j: (0, 0)),
            pl.BlockSpec((D_MODEL, bn), lambda i, j: (0, j)),
            pl.BlockSpec((D_MODEL, IF_WIDTH), lambda i, j: (0, 0)),
        ],
        out_specs=[
            pl.BlockSpec((bm, bn), lambda i, j: (i, j)),
            pl.BlockSpec((bm, IF_WIDTH), lambda i, j: (i, 0)),
        ],
        out_shape=[
            jax.ShapeDtypeStruct((n, D_MAIN), BF16),
            jax.ShapeDtypeStruct((n, IF_WIDTH), F32),
        ],
        scratch_shapes=[pltpu.VMEM((bm, D_MODEL), BF16)],
        compiler_params=_params(("arbitrary", "arbitrary")),
        name="in_proj",
    )(x2, g, w_main, w_if)


def _mlstm_kernel(qk_ref, v_ref, og_ref, if_ref, cw_ref, cb_ref, gb_ref, ng_ref, out_ref,
                  tail_ref, c_ref, m_ref, *, chunk):
    L = chunk
    dh = HEAD_DIM

    @pl.when(pl.program_id(1) == 0)
    def _():
        tail_ref[...] = jnp.zeros_like(tail_ref)
        c_ref[...] = jnp.zeros_like(c_ref)
        m_ref[...] = jnp.zeros_like(m_ref)

    cur = qk_ref[0].astype(F32)
    ext = jnp.concatenate([tail_ref[...], cur], axis=0)
    acc = cur * cw_ref[CONV_WIDTH - 1:CONV_WIDTH, :] + cb_ref[...]
    for s in range(1, CONV_WIDTH):
        acc = acc + ext[8 - s:8 - s + L, :] * cw_ref[CONV_WIDTH - 1 - s:CONV_WIDTH - s, :]
    tail_ref[...] = cur[L - 8:, :]
    qk = acc * _sigmoid(acc)
    q_all = qk[:, :MLSTM_WIDTH].astype(BF16)
    k_all = (qk[:, MLSTM_WIDTH:] * (dh ** -0.5)).astype(BF16)

    gi = if_ref[0][:, :LANES] + gb_ref[:, :LANES]
    lf = _log_sigmoid(if_ref[0][:, LANES:] + gb_ref[:, LANES:])
    row = lax.broadcasted_iota(jnp.int32, (L, L), 0)
    col = lax.broadcasted_iota(jnp.int32, (L, L), 1)
    causal = col <= row
    tri = causal.astype(F32)
    cum = jnp.dot(tri, lf, preferred_element_type=F32, precision=lax.Precision.HIGHEST)
    rmat = gi - cum
    rmat_t = rmat.T
    ones_blk = jnp.ones((L, dh), BF16)

    for h in range(MLSTM_HEADS):
        hs = slice(h * dh, (h + 1) * dh)
        b = cum[:, h:h + 1]
        r_col = rmat[:, h:h + 1]
        r_row = rmat_t[h:h + 1, :]
        m_prev = m_ref[h:h + 1, 0:1]
        logw = jnp.where(causal, b + r_row, NEG)
        inter = b + m_prev
        m_out = jnp.maximum(inter, jnp.max(logw, axis=1, keepdims=True))
        p = jnp.exp(logw - m_out)
        qh = q_all[:, hs]
        kh = k_all[:, hs]
        s = lax.dot_general(qh, kh, (((1,), (1,)), ((), ())), preferred_element_type=F32)
        w = (p * s).astype(BF16)
        vaug = jnp.concatenate([v_ref[0][:, hs], ones_blk], axis=1)
        c_prev = c_ref[h]
        nd = (jnp.dot(w, vaug, preferred_element_type=F32)
              + jnp.exp(inter - m_out)
              * jnp.dot(qh, c_prev.astype(BF16), preferred_element_type=F32))
        hh = nd[:, :dh] / jnp.maximum(jnp.abs(nd[:, dh:]), jnp.exp(-m_out))

        b_last = b[L - 1:L, :]
        m_new = b_last + jnp.maximum(m_prev, jnp.max(r_col, axis=0, keepdims=True))
        ws = jnp.exp(b_last + r_col - m_new)
        decay = jnp.exp(b_last + m_prev - m_new)
        wv = (ws * vaug.astype(F32)).astype(BF16)
        kv = lax.dot_general(kh, wv, (((0,), (0,)), ((), ())), preferred_element_type=F32)
        c_ref[h] = decay * c_prev + kv
        m_ref[h:h + 1, :] = jnp.broadcast_to(m_new, (1, LANES))

        y = hh * lax.rsqrt(jnp.mean(hh * hh, axis=-1, keepdims=True) + RMS_EPS) * ng_ref[:, hs]
        y = y * _sigmoid(og_ref[0][:, hs].astype(F32))
        out_ref[0, :, hs] = y.astype(BF16)


def _mlstm(proj3, ifg3, conv_w, conv_b, gate_b, norm_g, *, chunk=MLSTM_CHUNK):
    bsz, s, _ = proj3.shape
    w = MLSTM_WIDTH
    kern = functools.partial(_mlstm_kernel, chunk=chunk)
    return pl.pallas_call(
        kern,
        grid=(bsz, s // chunk),
        in_specs=[
            pl.BlockSpec((1, chunk, 2 * w), lambda b, c: (b, c, COL_QK // (2 * w))),
            pl.BlockSpec((1, chunk, w), lambda b, c: (b, c, COL_V // w)),
            pl.BlockSpec((1, chunk, w), lambda b, c: (b, c, COL_O // w)),
            pl.BlockSpec((1, chunk, IF_WIDTH), lambda b, c: (b, c, 0)),
            pl.BlockSpec((CONV_WIDTH, 2 * w), lambda b, c: (0, 0)),
            pl.BlockSpec((1, 2 * w), lambda b, c: (0, 0)),
            pl.BlockSpec((1, IF_WIDTH), lambda b, c: (0, 0)),
            pl.BlockSpec((1, w), lambda b, c: (0, 0)),
        ],
        out_specs=pl.BlockSpec((1, chunk, w), lambda b, c: (b, c, 0)),
        out_shape=jax.ShapeDtypeStruct((bsz, s, w), BF16),
        scratch_shapes=[
            pltpu.VMEM((8, 2 * w), F32),
            pltpu.VMEM((MLSTM_HEADS, HEAD_DIM, 2 * HEAD_DIM), F32),
            pltpu.VMEM((MLSTM_HEADS, LANES), F32),
        ],
        compiler_params=_params(("arbitrary", "arbitrary")),
        name="mlstm",
    )(proj3, proj3, proj3, ifg3, conv_w, conv_b, gate_b, norm_g)


def _attn_kernel(q_ref, kp_ref, kc_ref, vp_ref, vc_ref, bias_ref, o_ref, lse_ref):
    qb = ATTN_BLOCK
    dh = HEAD_DIM
    first = pl.program_id(2) == 0
    kcol = lax.broadcasted_iota(jnp.int32, (qb, 2 * qb), 1)
    dead = jnp.logical_and(first, kcol < qb)
    lane = lax.broadcasted_iota(jnp.int32, (qb, LANES), 1)
    lse_all = jnp.zeros((qb, LANES), F32)
    for j in range(HEADS_PER_GROUP):
        hs = slice(j * dh, (j + 1) * dh)
        q = q_ref[0][:, hs]
        k = jnp.concatenate([kp_ref[0][:, hs], kc_ref[0][:, hs]], axis=0)
        v = jnp.concatenate([vp_ref[0][:, hs], vc_ref[0][:, hs]], axis=0)
        s = lax.dot_general(q, k, (((1,), (1,)), ((), ())), preferred_element_type=F32)
        s = s * (dh ** -0.5) + bias_ref[j]
        s = jnp.where(dead, NEG, s)
        mx = jnp.max(s, axis=1, keepdims=True)
        p = jnp.exp(s - mx)
        den = jnp.sum(p, axis=1, keepdims=True)
        o = jnp.dot(p.astype(BF16), v, preferred_element_type=F32) / den
        o_ref[0, :, hs] = o
        lse_all = jnp.where(lane == j, mx + jnp.log(den), lse_all)
    lse_ref[0] = lse_all


def _attn_group(proj3, bias, g, dilation):
    bsz, s, _ = proj3.shape
    n = s // dilation
    nb = n // ATTN_BLOCK
    gw = GROUP_WIDTH
    pv = proj3.reshape(bsz, n, dilation * D_MAIN)
    per = D_MAIN // gw

    def spec(col, prev):
        cb = col // gw + g

        def imap(b, r, i):
            return (b, jnp.maximum(i - 1, 0) if prev else i, r * per + cb)

        return pl.BlockSpec((1, ATTN_BLOCK, gw), imap)

    out, lse = pl.pallas_call(
        _attn_kernel,
        grid=(bsz, dilation, nb),
        in_specs=[
            spec(COL_QA, False),
            spec(COL_KA, True), spec(COL_KA, False),
            spec(COL_VA, True), spec(COL_VA, False),
            pl.BlockSpec((HEADS_PER_GROUP, ATTN_BLOCK, 2 * ATTN_BLOCK), lambda b, r, i: (0, 0, 0)),
        ],
        out_specs=[
            pl.BlockSpec((1, ATTN_BLOCK, gw), lambda b, r, i: (b, i, r)),
            pl.BlockSpec((1, ATTN_BLOCK, LANES), lambda b, r, i: (b, i, r)),
        ],
        out_shape=[
            jax.ShapeDtypeStruct((bsz, n, dilation * gw), F32),
            jax.ShapeDtypeStruct((bsz, n, dilation * LANES), F32),
        ],
        compiler_params=_params(("arbitrary", "arbitrary", "arbitrary")),
        name=f"attn_g{g}",
    )(pv, pv, pv, pv, pv, bias)
    return out.reshape(bsz * s, gw), lse.reshape(bsz * s, LANES)


def _t5_bucket(dist):
    max_exact = REL_BUCKETS // 2
    d_f = jnp.maximum(dist, 1).astype(F32)
    large = max_exact + (jnp.log(d_f / max_exact) / math.log(REL_MAX_DIST / max_exact)
                         * (REL_BUCKETS - max_exact)).astype(jnp.int32)
    large = jnp.minimum(large, REL_BUCKETS - 1)
    return jnp.where(dist < max_exact, dist, large)


def _attn_bias(rel_bias, g, dilation):
    span = ATTN_SPAN
    buckets = _t5_bucket(jnp.arange(span + 1, dtype=jnp.int32) * dilation)
    vec = rel_bias[buckets][:, g * HEADS_PER_GROUP:(g + 1) * HEADS_PER_GROUP].T.astype(F32)
    qpos = jnp.arange(ATTN_BLOCK)[:, None]
    kpos = jnp.arange(2 * ATTN_BLOCK)[None, :]
    dist = qpos + ATTN_BLOCK - kpos
    valid = (dist >= 0) & (dist <= span)
    return jnp.where(valid[None], vec[:, jnp.clip(dist, 0, span)], NEG)


def _merge_kernel(hm_ref, o0_ref, o1_ref, o2_ref, l0_ref, l1_ref, l2_ref, gm_ref, ga_ref, x_ref,
                  wbm_ref, wba_ref, wo_ref, ng_ref, rw_ref, rb_ref,
                  x1_ref, h2_ref, idx_ref, gate_ref):
    dh = HEAD_DIM
    l0, l1, l2 = l0_ref[...], l1_ref[...], l2_ref[...]
    mx = jnp.maximum(jnp.maximum(l0, l1), l2)
    e0, e1, e2 = jnp.exp(l0 - mx), jnp.exp(l1 - mx), jnp.exp(l2 - mx)
    den = e0 + e1 + e2
    w0, w1, w2 = e0 / den, e1 / den, e2 / den
    parts = []
    for j in range(HEADS_PER_GROUP):
        hs = slice(j * dh, (j + 1) * dh)
        parts.append(w0[:, j:j + 1] * o0_ref[:, hs] + w1[:, j:j + 1] * o1_ref[:, hs]
                     + w2[:, j:j + 1] * o2_ref[:, hs])
    ha = jnp.concatenate(parts, axis=1).astype(BF16)
    ym = jnp.dot(hm_ref[...], wbm_ref[...], preferred_element_type=F32)
    ya = jnp.dot(ha, wba_ref[...], preferred_element_type=F32)
    merged = (_sigmoid(gm_ref[...].astype(F32)) * ym + _sigmoid(ga_ref[...].astype(F32)) * ya)
    x1 = x_ref[...] + jnp.dot(merged.astype(BF16), wo_ref[...], preferred_element_type=F32)
    x1_ref[...] = x1
    h2 = x1 * lax.rsqrt(jnp.mean(x1 * x1, axis=-1, keepdims=True) + RMS_EPS) * ng_ref[...]
    h2_ref[...] = h2
    logits = jnp.dot(h2, rw_ref[...], preferred_element_type=F32,
                     precision=lax.Precision.HIGHEST) + rb_ref[...]
    lane = lax.broadcasted_iota(jnp.int32, logits.shape, 1)
    idx_all = jnp.zeros(logits.shape, jnp.int32)
    val_all = jnp.zeros(logits.shape, F32)
    top0 = None
    esum = None
    for k in range(TOP_K):
        m = jnp.max(logits, axis=1, keepdims=True)
        sel = jnp.min(jnp.where(logits == m, lane, LANES), axis=1, keepdims=True)
        if k == 0:
            top0 = m
        e = jnp.exp(m - top0)
        esum = e if k == 0 else esum + e
        idx_all = jnp.where(lane == k, sel, idx_all)
        val_all = jnp.where(lane == k, e, val_all)
        logits = jnp.where(lane == sel, -jnp.inf, logits)
    idx_ref[...] = idx_all
    gate_ref[...] = val_all / esum


def _merge(hm, outs, lses, proj, x2, wbm, wba, wo, ng, rw, rb, *, bm=256):
    n = x2.shape[0]
    d = D_MODEL
    gcol = COL_GATES // d

    def rows(width):
        return pl.BlockSpec((bm, width), lambda i: (i, 0))

    def full(a, b):
        return pl.BlockSpec((a, b), lambda i: (0, 0))

    return pl.pallas_call(
        _merge_kernel,
        grid=(n // bm,),
        in_specs=[
            rows(MLSTM_WIDTH),
            rows(GROUP_WIDTH), rows(GROUP_WIDTH), rows(GROUP_WIDTH),
            rows(LANES), rows(LANES), rows(LANES),
            pl.BlockSpec((bm, d), lambda i: (i, gcol)),
            pl.BlockSpec((bm, d), lambda i: (i, gcol + 1)),
            rows(d),
            full(MLSTM_WIDTH, d), full(GROUP_WIDTH, d), full(d, d),
            full(1, d), full(d, LANES), full(1, LANES),
        ],
        out_specs=[rows(d), rows(d), rows(LANES), rows(LANES)],
        out_shape=[
            jax.ShapeDtypeStruct((n, d), F32),
            jax.ShapeDtypeStruct((n, d), F32),
            jax.ShapeDtypeStruct((n, LANES), jnp.int32),
            jax.ShapeDtypeStruct((n, LANES), F32),
        ],
        compiler_params=_params(("arbitrary",)),
        name="merge",
    )(hm, outs[0], outs[1], outs[2], lses[0], lses[1], lses[2], proj, proj, x2,
      wbm, wba, wo, ng, rw, rb)


def _dispatch_kernel(tok_ref, h2_hbm, out_ref, buf_ref, sem, *, rows):
    def issue(s, carry):
        tok = tok_ref[s]
        pltpu.make_async_copy(h2_hbm.at[pl.ds(tok, 1), :], buf_ref.at[pl.ds(s, 1), :], sem).start()
        return carry

    lax.fori_loop(0, rows, issue, 0)
    pltpu.make_async_copy(h2_hbm.at[pl.ds(0, rows), :], buf_ref, sem).wait()
    out_ref[...] = buf_ref[...].astype(BF16)


def _dispatch(buf_tok, h2, *, rows=1024):
    n_pad = buf_tok.shape[0]
    d = h2.shape[1]
    kern = functools.partial(_dispatch_kernel, rows=rows)
    return pl.pallas_call(
        kern,
        grid=(n_pad // rows,),
        in_specs=[
            pl.BlockSpec((rows,), lambda i: (i,), memory_space=pltpu.SMEM),
            pl.BlockSpec(memory_space=pl.ANY),
        ],
        out_specs=pl.BlockSpec((rows, d), lambda i: (i, 0)),
        out_shape=jax.ShapeDtypeStruct((n_pad, d), BF16),
        scratch_shapes=[pltpu.VMEM((rows, d), F32), pltpu.SemaphoreType.DMA(())],
        compiler_params=_params(("arbitrary",)),
        name="dispatch",
    )(buf_tok, h2)


def _cast_rows(src_ref, dst_ref, rows):
    total = src_ref.shape[0]

    def body(i, carry):
        r = pl.multiple_of(i * rows, rows)
        dst_ref[pl.ds(r, rows), :] = src_ref[pl.ds(r, rows), :].astype(BF16)
        return carry

    lax.fori_loop(0, total // rows, body, 0)


def _gate_up_kernel(be_ref, x_ref, wg_ref, wu_ref, bg_ref, bu_ref, o_ref, wg_bf, wu_bf):
    j = pl.program_id(1)
    changed = jnp.logical_or(j == 0, be_ref[j] != be_ref[jnp.maximum(j - 1, 0)])

    @pl.when(changed)
    def _():
        _cast_rows(wg_ref.at[0], wg_bf, 256)
        _cast_rows(wu_ref.at[0], wu_bf, 256)

    x = x_ref[...]
    gate = jnp.dot(x, wg_bf[...], preferred_element_type=F32) + bg_ref[0]
    up = jnp.dot(x, wu_bf[...], preferred_element_type=F32) + bu_ref[0]
    gate = jnp.minimum(gate, SWIGLU_LIMIT)
    up = jnp.clip(up, -SWIGLU_LIMIT, SWIGLU_LIMIT)
    glu = gate * _sigmoid(SWIGLU_ALPHA * gate)
    o_ref[...] = ((up + 1.0) * glu).astype(BF16)


def _gate_up(block_e, xb, w_gate_up, b_gate_up3, *, tm=MOE_TILE, tf=1024):
    n_pad, d = xb.shape
    kt = D_FF // tf
    return pl.pallas_call(
        _gate_up_kernel,
        grid_spec=pltpu.PrefetchScalarGridSpec(
            num_scalar_prefetch=1,
            grid=(kt, n_pad // tm),
            in_specs=[
                pl.BlockSpec((tm, d), lambda k, j, be: (j, 0)),
                pl.BlockSpec((1, d, tf), lambda k, j, be: (be[j], 0, k)),
                pl.BlockSpec((1, d, tf), lambda k, j, be: (be[j], 0, kt + k)),
                pl.BlockSpec((1, 1, tf), lambda k, j, be: (be[j], 0, k)),
                pl.BlockSpec((1, 1, tf), lambda k, j, be: (be[j], 0, kt + k)),
            ],
            out_specs=pl.BlockSpec((tm, tf), lambda k, j, be: (j, k)),
            scratch_shapes=[pltpu.VMEM((d, tf), BF16), pltpu.VMEM((d, tf), BF16)],
        ),
        out_shape=jax.ShapeDtypeStruct((n_pad, D_FF), BF16),
        compiler_params=_params(("arbitrary", "arbitrary")),
        name="gate_up",
    )(block_e, xb, w_gate_up, w_gate_up, b_gate_up3, b_gate_up3)


def _down_kernel(be_ref, h_ref, wd_ref, bd_ref, o_ref, wd_bf):
    j = pl.program_id(1)
    changed = jnp.logical_or(j == 0, be_ref[j] != be_ref[jnp.maximum(j - 1, 0)])

    @pl.when(changed)
    def _():
        _cast_rows(wd_ref.at[0], wd_bf, 256)

    o_ref[...] = jnp.dot(h_ref[...], wd_bf[...], preferred_element_type=F32) + bd_ref[0]


def _down(block_e, hb, w_down, b_down3, *, tm=MOE_TILE, tn=1024):
    n_pad, f = hb.shape
    d = D_MODEL
    return pl.pallas_call(
        _down_kernel,
        grid_spec=pltpu.PrefetchScalarGridSpec(
            num_scalar_prefetch=1,
            grid=(d // tn, n_pad // tm),
            in_specs=[
                pl.BlockSpec((tm, f), lambda k, j, be: (j, 0)),
                pl.BlockSpec((1, f, tn), lambda k, j, be: (be[j], 0, k)),
                pl.BlockSpec((1, 1, tn), lambda k, j, be: (be[j], 0, k)),
            ],
            out_specs=pl.BlockSpec((tm, tn), lambda k, j, be: (j, k)),
            scratch_shapes=[pltpu.VMEM((f, tn), BF16)],
        ),
        out_shape=jax.ShapeDtypeStruct((n_pad, d), F32),
        compiler_params=_params(("arbitrary", "arbitrary")),
        name="down",
    )(block_e, hb, w_down, b_down3)


def _combine_kernel(dest_ref, y_hbm, gate_ref, x1_ref, ng_ref, o_ref, buf_ref, sem, *, rows):
    def issue(t, carry):
        for k in range(TOP_K):
            slot = dest_ref[t * TOP_K + k]
            pltpu.make_async_copy(y_hbm.at[pl.ds(slot, 1), :], buf_ref.at[k, pl.ds(t, 1), :],
                                  sem).start()
        return carry

    lax.fori_loop(0, rows, issue, 0)
    for k in range(TOP_K):
        pltpu.make_async_copy(y_hbm.at[pl.ds(0, rows), :], buf_ref.at[k], sem).wait()
    acc = x1_ref[...]
    for k in range(TOP_K):
        acc = acc + gate_ref[:, k:k + 1] * buf_ref[k]
    o_ref[...] = acc * lax.rsqrt(jnp.mean(acc * acc, axis=-1, keepdims=True) + RMS_EPS) * ng_ref[...]


def _combine(dest, yb, gates, x1, ng, *, rows=256):
    n, d = x1.shape
    kern = functools.partial(_combine_kernel, rows=rows)
    return pl.pallas_call(
        kern,
        grid=(n // rows,),
        in_specs=[
            pl.BlockSpec((rows * TOP_K,), lambda i: (i,), memory_space=pltpu.SMEM),
            pl.BlockSpec(memory_space=pl.ANY),
            pl.BlockSpec((rows, LANES), lambda i: (i, 0)),
            pl.BlockSpec((rows, d), lambda i: (i, 0)),
            pl.BlockSpec((1, d), lambda i: (0, 0)),
        ],
        out_specs=pl.BlockSpec((rows, d), lambda i: (i, 0)),
        out_shape=jax.ShapeDtypeStruct((n, d), F32),
        scratch_shapes=[pltpu.VMEM((TOP_K, rows, d), F32), pltpu.SemaphoreType.DMA(())],
        compiler_params=_params(("arbitrary",)),
        name="combine",
    )(dest, yb, gates, x1, ng)


def _routing(top_idx, tm):
    n_tok = top_idx.shape[0]
    n_asg = n_tok * TOP_K
    e_flat = top_idx.reshape(n_asg)
    onehot = (e_flat[:, None] == jnp.arange(N_EXPERTS, dtype=jnp.int32)[None, :]).astype(jnp.int32)
    cum = jnp.cumsum(onehot, axis=0)
    rank = jnp.sum(onehot * cum, axis=1) - 1
    counts = cum[-1]
    padded = (counts + tm - 1) // tm * tm
    pend = jnp.cumsum(padded)
    pstart = pend - padded
    dest = (pstart[e_flat] + rank).astype(jnp.int32)
    big = 1024
    n_pad = -(-(n_asg + N_EXPERTS * (tm - 1)) // big) * big
    n_blk = n_pad // tm
    buf_tok = jnp.zeros((n_pad,), jnp.int32).at[dest].set(
        jnp.arange(n_asg, dtype=jnp.int32) // TOP_K, unique_indices=True)
    block_e = jnp.minimum(
        jnp.searchsorted(pend, jnp.arange(n_blk, dtype=jnp.int32) * tm, side='right'),
        N_EXPERTS - 1).astype(jnp.int32)
    return dest, buf_tok, block_e


def _layer(x, norm_mix_g, w_in, conv_w, conv_b, igate_b, fgate_b, mlstm_norm_g, rel_bias,
           w_branch_mlstm, w_branch_attn, w_out, norm_moe_g, router_w, router_b,
           w_gate_up, b_gate_up, w_down, b_down, out_norm_g):
    bsz, s, d = x.shape
    n = bsz * s
    x2 = x.reshape(n, d)
    w2, w1 = 2 * MLSTM_WIDTH, MLSTM_WIDTH
    o_qk, o_v, o_o = 0, w2, w2 + w1
    o_i = o_o + w1
    o_f = o_i + MLSTM_HEADS
    o_qa = o_f + MLSTM_HEADS
    o_ka, o_va = o_qa + ATTN_WIDTH, o_qa + 2 * ATTN_WIDTH
    o_g = o_va + ATTN_WIDTH
    w_main = jnp.concatenate(
        [w_in[:, o_qk:o_i], w_in[:, o_g:], w_in[:, o_qa:o_g]], axis=1).astype(BF16)
    zpad = jnp.zeros((d, LANES - MLSTM_HEADS), w_in.dtype)
    w_if = jnp.concatenate([w_in[:, o_i:o_f], zpad, w_in[:, o_f:o_qa], zpad], axis=1).astype(BF16)
    bpad = jnp.zeros((LANES - MLSTM_HEADS,), F32)
    gate_b = jnp.concatenate([igate_b, bpad, fgate_b, bpad]).reshape(1, IF_WIDTH)

    proj, ifg = _in_proj(x2, norm_mix_g.reshape(1, d), w_main, w_if)
    proj3 = proj.reshape(bsz, s, D_MAIN)
    hm = _mlstm(proj3, ifg.reshape(bsz, s, IF_WIDTH), conv_w, conv_b.reshape(1, -1), gate_b,
                mlstm_norm_g.reshape(1, -1)).reshape(n, MLSTM_WIDTH)

    outs, lses = [], []
    for g, (_, dilation) in enumerate(ATTN_GROUPS):
        o_g_, lse_g = _attn_group(proj3, _attn_bias(rel_bias, g, dilation), g, dilation)
        outs.append(o_g_)
        lses.append(lse_g)

    rw = jnp.concatenate([router_w, jnp.zeros((d, LANES - N_EXPERTS), F32)], axis=1)
    rb = jnp.concatenate([router_b, jnp.full((LANES - N_EXPERTS,), NEG, F32)]).reshape(1, LANES)
    x1, h2, idx, gates = _merge(
        hm, outs, lses, proj, x2, w_branch_mlstm.astype(BF16), w_branch_attn.astype(BF16),
        w_out.astype(BF16), norm_moe_g.reshape(1, d), rw, rb)

    dest, buf_tok, block_e = _routing(idx[:, :TOP_K], MOE_TILE)
    xb = _dispatch(buf_tok, h2)
    hb = _gate_up(block_e, xb, w_gate_up, b_gate_up.reshape(N_EXPERTS, 1, 2 * D_FF))
    yb = _down(block_e, hb, w_down, b_down.reshape(N_EXPERTS, 1, d))
    out = _combine(dest, yb, gates, x1, out_norm_g.reshape(1, d))
    return out.reshape(bsz, s, d)


def kernel(x, norm_mix_g, w_in, conv_w, conv_b, igate_b, fgate_b, mlstm_norm_g, rel_bias,
           w_branch_mlstm, w_branch_attn, w_out, norm_moe_g, router_w, router_b,
           w_gate_up, b_gate_up, w_down, b_down, norm_final_g):
    assert w_in.shape[0] == 1, "single-layer block"
    return _layer(x, norm_mix_g[0], w_in[0], conv_w[0], conv_b[0], igate_b[0], fgate_b[0],
                  mlstm_norm_g[0], rel_bias, w_branch_mlstm[0], w_branch_attn[0], w_out[0],
                  norm_moe_g[0], router_w[0], router_b[0], w_gate_up[0], b_gate_up[0],
                  w_down[0], b_down[0], norm_final_g)
```

```python
import functools
import math

import jax
import jax.numpy as jnp
from jax import lax
from jax.experimental import pallas as pl
from jax.experimental.pallas import tpu as pltpu

F32 = jnp.float32
BF16 = jnp.bfloat16

D_MODEL = 2048
MLSTM_HEADS = 8
HEAD_DIM = 128
MLSTM_WIDTH = MLSTM_HEADS * HEAD_DIM
CONV_WIDTH = 4
ATTN_GROUPS = ((128, 1), (512, 4), (2048, 16))
N_GROUPS = 3
HEADS_PER_GROUP = 4
ATTN_HEADS = HEADS_PER_GROUP * N_GROUPS
ATTN_WIDTH = ATTN_HEADS * HEAD_DIM
GROUP_WIDTH = HEADS_PER_GROUP * HEAD_DIM
ATTN_BLOCK = 128
ATTN_SPAN = 128
REL_BUCKETS = 32
REL_MAX_DIST = 2048
N_EXPERTS = 32
TOP_K = 4
D_FF = D_MODEL
SWIGLU_LIMIT = 7.0
SWIGLU_ALPHA = 1.702
RMS_EPS = 1e-6
NEG = -1e30

COL_QK = 0
COL_V = 2 * MLSTM_WIDTH
COL_O = COL_V + MLSTM_WIDTH
COL_GATES = COL_O + MLSTM_WIDTH
D_MAIN = COL_GATES + 2 * D_MODEL
QKV_WIDTH = 3 * GROUP_WIDTH
LANES = 128
IF_WIDTH = 2 * LANES

MLSTM_CHUNK = 128
MOE_TILE = 512
VMEM_LIMIT = 56 * 1024 * 1024


def _sigmoid(x):
    return 1.0 / (1.0 + jnp.exp(-x))


def _log_sigmoid(x):
    return -(jnp.maximum(-x, 0.0) + jnp.log1p(jnp.exp(-jnp.abs(x))))


def _params(sem):
    return pltpu.CompilerParams(dimension_semantics=sem, vmem_limit_bytes=VMEM_LIMIT)


def _in_proj_kernel(x_ref, g_ref, w_ref, wif_ref, o_ref, oif_ref, h_ref, *, bm, rows):
    @pl.when(pl.program_id(1) == 0)
    def _():
        for r in range(0, bm, rows):
            x = x_ref[r:r + rows, :]
            ms = jnp.mean(x * x, axis=-1, keepdims=True)
            h = (x * lax.rsqrt(ms + RMS_EPS) * g_ref[...]).astype(BF16)
            h_ref[r:r + rows, :] = h
            oif_ref[r:r + rows, :] = jnp.dot(h, wif_ref[...], preferred_element_type=F32)

    o_ref[...] = jnp.dot(h_ref[...], w_ref[...], preferred_element_type=F32).astype(BF16)


def _residue_perm(size, dilation, inverse):
    per = size // dilation
    i = lax.broadcasted_iota(jnp.int32, (size, size), 0)
    j = lax.broadcasted_iota(jnp.int32, (size, size), 1)
    if inverse:
        src = (i & (dilation - 1)) * per + (i >> (dilation.bit_length() - 1))
    else:
        src = (i & (per - 1)) * dilation + (i >> (per.bit_length() - 1))
    return j == src


def _attn_proj_kernel(x_ref, g_ref, w_ref, o_ref, h_ref, *, bm, dilation, rows):
    per = rows // dilation
    if dilation > 1:
        perm = _residue_perm(rows, dilation, False).astype(BF16)
    for c in range(0, bm, rows):
        x = x_ref[c:c + rows, :]
        ms = jnp.mean(x * x, axis=-1, keepdims=True)
        h = (x * lax.rsqrt(ms + RMS_EPS) * g_ref[...]).astype(BF16)
        if dilation > 1:
            h = jnp.dot(perm, h, preferred_element_type=F32).astype(BF16)
        h_ref[c:c + rows, :] = h
    res = jnp.dot(h_ref[...], w_ref[...], preferred_element_type=F32).astype(BF16)
    for c in range(bm // rows):
        for r in range(dilation):
            o_ref[0, r, c * per:(c + 1) * per, :] = res[c * rows + r * per:c * rows + (r + 1) * per, :]


def _attn_proj(x2, g, w_qkv, bsz, s, dilation, *, bm=1024):
    n_blk = s // bm
    per = bm // dilation
    kern = functools.partial(_attn_proj_kernel, bm=bm, dilation=dilation, rows=256)
    return pl.pallas_call(
        kern,
        grid=(bsz, n_blk),
        in_specs=[
            pl.BlockSpec((bm, D_MODEL), lambda b, i: (b * n_blk + i, 0)),
            pl.BlockSpec((1, D_MODEL), lambda b, i: (0, 0)),
            pl.BlockSpec((D_MODEL, QKV_WIDTH), lambda b, i: (0, 0),
                         pipeline_mode=pl.Buffered(1)),
        ],
        out_specs=pl.BlockSpec((1, dilation, per, QKV_WIDTH), lambda b, i: (b, 0, i, 0)),
        out_shape=jax.ShapeDtypeStruct((bsz, dilation, s // dilation, QKV_WIDTH), BF16),
        scratch_shapes=[pltpu.VMEM((bm, D_MODEL), BF16)],
        compiler_params=_params(("arbitrary", "arbitrary")),
        name=f"attn_proj_d{dilation}",
    )(x2, g, w_qkv)


def _in_proj(x2, g, w_main, w_if, *, bm=1024, bn=1024):
    n = x2.shape[0]
    bm = min(bm, n)
    kern = functools.partial(_in_proj_kernel, bm=bm, rows=256)
    return pl.pallas_call(
        kern,
        grid=(n // bm, D_MAIN // bn),
        in_specs=[
            pl.BlockSpec((bm, D_MODEL), lambda i, j: (i, 0)),
            pl.BlockSpec((1, D_MODEL), lambda i, j: (0, 0)),
            pl.BlockSpec((D_MODEL, bn), lambda i, j: (0, j)),
            pl.BlockSpec((D_MODEL, IF_WIDTH), lambda i, j: (0, 0)),
        ],
        out_specs=[
            pl.BlockSpec((bm, bn), lambda i, j: (i, j)),
            pl.BlockSpec((bm, IF_WIDTH), lambda i, j: (i, 0)),
        ],
        out_shape=[
            jax.ShapeDtypeStruct((n, D_MAIN), BF16),
            jax.ShapeDtypeStruct((n, IF_WIDTH), F32),
        ],
        scratch_shapes=[pltpu.VMEM((bm, D_MODEL), BF16)],
        compiler_params=_params(("arbitrary", "arbitrary")),
        name="in_proj",
    )(x2, g, w_main, w_if)


def _mlstm_kernel(qk_ref, v_ref, og_ref, if_ref, cw_ref, cb_ref, gb_ref, ng_ref, out_ref,
                  tail_ref, c_ref, m_ref, *, chunk):
    L = chunk
    dh = HEAD_DIM

    @pl.when(pl.program_id(1) == 0)
    def _():
        tail_ref[...] = jnp.zeros_like(tail_ref)
        c_ref[...] = jnp.zeros_like(c_ref)
        m_ref[...] = jnp.zeros_like(m_ref)

    cur = qk_ref[0].astype(F32)
    ext = jnp.concatenate([tail_ref[...], cur], axis=0)
    acc = cur * cw_ref[CONV_WIDTH - 1:CONV_WIDTH, :] + cb_ref[...]
    for s in range(1, CONV_WIDTH):
        acc = acc + ext[8 - s:8 - s + L, :] * cw_ref[CONV_WIDTH - 1 - s:CONV_WIDTH - s, :]
    tail_ref[...] = cur[L - 8:, :]
    qk = acc * _sigmoid(acc)
    q_all = qk[:, :MLSTM_WIDTH].astype(BF16)
    k_all = (qk[:, MLSTM_WIDTH:] * (dh ** -0.5)).astype(BF16)

    gi = if_ref[0][:, :LANES] + gb_ref[:, :LANES]
    lf = _log_sigmoid(if_ref[0][:, LANES:] + gb_ref[:, LANES:])
    row = lax.broadcasted_iota(jnp.int32, (L, L), 0)
    col = lax.broadcasted_iota(jnp.int32, (L, L), 1)
    causal = col <= row
    tri = causal.astype(F32)
    cum = jnp.dot(tri, lf, preferred_element_type=F32, precision=lax.Precision.HIGHEST)
    rmat = gi - cum
    rmat_t = rmat.T
    ones_blk = jnp.ones((L, dh), BF16)

    for h in range(MLSTM_HEADS):
        hs = slice(h * dh, (h + 1) * dh)
        b = cum[:, h:h + 1]
        r_col = rmat[:, h:h + 1]
        r_row = rmat_t[h:h + 1, :]
        m_prev = m_ref[h:h + 1, 0:1]
        logw = jnp.where(causal, b + r_row, NEG)
        inter = b + m_prev
        m_out = jnp.maximum(inter, jnp.max(logw, axis=1, keepdims=True))
        p = jnp.exp(logw - m_out)
        qh = q_all[:, hs]
        kh = k_all[:, hs]
        s = lax.dot_general(qh, kh, (((1,), (1,)), ((), ())), preferred_element_type=F32)
        w = (p * s).astype(BF16)
        vaug = jnp.concatenate([v_ref[0][:, hs], ones_blk], axis=1)
        c_prev = c_ref[h]
        nd = (jnp.dot(w, vaug, preferred_element_type=F32)
              + jnp.exp(inter - m_out)
              * jnp.dot(qh, c_prev.astype(BF16), preferred_element_type=F32))
        hh = nd[:, :dh] / jnp.maximum(jnp.abs(nd[:, dh:]), jnp.exp(-m_out))

        b_last = b[L - 1:L, :]
        m_new = b_last + jnp.maximum(m_prev, jnp.max(r_col, axis=0, keepdims=True))
        ws = jnp.exp(b_last + r_col - m_new)
        decay = jnp.exp(b_last + m_prev - m_new)
        wv = (ws * vaug.astype(F32)).astype(BF16)
        kv = lax.dot_general(kh, wv, (((0,), (0,)), ((), ())), preferred_element_type=F32)
        c_ref[h] = decay * c_prev + kv
        m_ref[h:h + 1, :] = jnp.broadcast_to(m_new, (1, LANES))

        y = hh * lax.rsqrt(jnp.mean(hh * hh, axis=-1, keepdims=True) + RMS_EPS) * ng_ref[:, hs]
        y = y * _sigmoid(og_ref[0][:, hs].astype(F32))
        out_ref[0, :, hs] = y.astype(BF16)


def _mlstm(proj3, ifg3, conv_w, conv_b, gate_b, norm_g, *, chunk=MLSTM_CHUNK):
    bsz, s, _ = proj3.shape
    w = MLSTM_WIDTH
    kern = functools.partial(_mlstm_kernel, chunk=chunk)
    return pl.pallas_call(
        kern,
        grid=(bsz, s // chunk),
        in_specs=[
            pl.BlockSpec((1, chunk, 2 * w), lambda b, c: (b, c, COL_QK // (2 * w))),
            pl.BlockSpec((1, chunk, w), lambda b, c: (b, c, COL_V // w)),
            pl.BlockSpec((1, chunk, w), lambda b, c: (b, c, COL_O // w)),
            pl.BlockSpec((1, chunk, IF_WIDTH), lambda b, c: (b, c, 0)),
            pl.BlockSpec((CONV_WIDTH, 2 * w), lambda b, c: (0, 0)),
            pl.BlockSpec((1, 2 * w), lambda b, c: (0, 0)),
            pl.BlockSpec((1, IF_WIDTH), lambda b, c: (0, 0)),
            pl.BlockSpec((1, w), lambda b, c: (0, 0)),
        ],
        out_specs=pl.BlockSpec((1, chunk, w), lambda b, c: (b, c, 0)),
        out_shape=jax.ShapeDtypeStruct((bsz, s, w), BF16),
        scratch_shapes=[
            pltpu.VMEM((8, 2 * w), F32),
            pltpu.VMEM((MLSTM_HEADS, HEAD_DIM, 2 * HEAD_DIM), F32),
            pltpu.VMEM((MLSTM_HEADS, LANES), F32),
        ],
        compiler_params=_params(("arbitrary", "arbitrary")),
        name="mlstm",
    )(proj3, proj3, proj3, ifg3, conv_w, conv_b, gate_b, norm_g)


def _attn_kernel(q_ref, kp_ref, kc_ref, vp_ref, vc_ref, bias_ref, o_ref, lse_ref):
    qb = ATTN_BLOCK
    dh = HEAD_DIM
    first = pl.program_id(2) == 0
    kcol = lax.broadcasted_iota(jnp.int32, (qb, 2 * qb), 1)
    dead = jnp.logical_and(first, kcol < qb)
    lane = lax.broadcasted_iota(jnp.int32, (qb, LANES), 1)
    lse_all = jnp.zeros((qb, LANES), F32)
    for j in range(HEADS_PER_GROUP):
        hs = slice(j * dh, (j + 1) * dh)
        q = q_ref[0, 0][:, hs]
        k = jnp.concatenate([kp_ref[0, 0][:, hs], kc_ref[0, 0][:, hs]], axis=0)
        v = jnp.concatenate([vp_ref[0, 0][:, hs], vc_ref[0, 0][:, hs]], axis=0)
        s = lax.dot_general(q, k, (((1,), (1,)), ((), ())), preferred_element_type=F32)
        s = s * (dh ** -0.5) + bias_ref[j]
        s = jnp.where(dead, NEG, s)
        mx = jnp.max(s, axis=1, keepdims=True)
        p = jnp.exp(s - mx)
        den = jnp.sum(p, axis=1, keepdims=True)
        o = jnp.dot(p.astype(BF16), v, preferred_element_type=F32) / den
        o_ref[0, 0, :, hs] = o
        lse_all = jnp.where(lane == j, mx + jnp.log(den), lse_all)
    lse_ref[0, 0] = lse_all


def _attn_group(qkv, bias, g):
    bsz, dilation, n, _ = qkv.shape
    nb = n // ATTN_BLOCK
    gw = GROUP_WIDTH

    def spec(part, prev):
        def imap(b, r, i):
            return (b, r, jnp.maximum(i - 1, 0) if prev else i, part)

        return pl.BlockSpec((1, 1, ATTN_BLOCK, gw), imap)

    return pl.pallas_call(
        _attn_kernel,
        grid=(bsz, dilation, nb),
        in_specs=[
            spec(0, False),
            spec(1, True), spec(1, False),
            spec(2, True), spec(2, False),
            pl.BlockSpec((HEADS_PER_GROUP, ATTN_BLOCK, 2 * ATTN_BLOCK), lambda b, r, i: (0, 0, 0)),
        ],
        out_specs=[
            pl.BlockSpec((1, 1, ATTN_BLOCK, gw), lambda b, r, i: (b, r, i, 0)),
            pl.BlockSpec((1, 1, ATTN_BLOCK, LANES), lambda b, r, i: (b, r, i, 0)),
        ],
        out_shape=[
            jax.ShapeDtypeStruct((bsz, dilation, n, gw), F32),
            jax.ShapeDtypeStruct((bsz, dilation, n, LANES), F32),
        ],
        compiler_params=_params(("arbitrary", "arbitrary", "arbitrary")),
        name=f"attn_g{g}",
    )(qkv, qkv, qkv, qkv, qkv, bias)


def _t5_bucket(dist):
    max_exact = REL_BUCKETS // 2
    d_f = jnp.maximum(dist, 1).astype(F32)
    large = max_exact + (jnp.log(d_f / max_exact) / math.log(REL_MAX_DIST / max_exact)
                         * (REL_BUCKETS - max_exact)).astype(jnp.int32)
    large = jnp.minimum(large, REL_BUCKETS - 1)
    return jnp.where(dist < max_exact, dist, large)


def _attn_bias(rel_bias, g, dilation):
    span = ATTN_SPAN
    buckets = _t5_bucket(jnp.arange(span + 1, dtype=jnp.int32) * dilation)
    vec = rel_bias[buckets][:, g * HEADS_PER_GROUP:(g + 1) * HEADS_PER_GROUP].T.astype(F32)
    qpos = jnp.arange(ATTN_BLOCK)[:, None]
    kpos = jnp.arange(2 * ATTN_BLOCK)[None, :]
    dist = qpos + ATTN_BLOCK - kpos
    valid = (dist >= 0) & (dist <= span)
    onehot = (dist[:, :, None] == jnp.arange(span + 1)[None, None, :]).astype(F32)
    table = jnp.einsum('qkj,hj->hqk', onehot, vec, precision=lax.Precision.HIGHEST)
    return jnp.where(valid[None], table, NEG)


def _merge_kernel(hm_ref, o0_ref, o1_ref, o2_ref, l0_ref, l1_ref, l2_ref, gm_ref, ga_ref, x_ref,
                  wbm_ref, wba_ref, wo_ref, ng_ref, rw_ref, rb_ref,
                  x1_ref, h2_ref, idx_ref, gate_ref):
    dh = HEAD_DIM
    bm = x_ref.shape[0]

    def token_order(ref):
        dilation = ref.shape[1]
        if dilation == 1:
            return ref[0, 0]
        perm = _residue_perm(bm, dilation, True).astype(F32)
        return jnp.dot(perm, ref[0].reshape(bm, ref.shape[3]), preferred_element_type=F32,
                       precision=lax.Precision.HIGHEST)

    l0, l1, l2 = token_order(l0_ref), token_order(l1_ref), token_order(l2_ref)
    o0, o1, o2 = token_order(o0_ref), token_order(o1_ref), token_order(o2_ref)
    mx = jnp.maximum(jnp.maximum(l0, l1), l2)
    e0, e1, e2 = jnp.exp(l0 - mx), jnp.exp(l1 - mx), jnp.exp(l2 - mx)
    den = e0 + e1 + e2
    w0, w1, w2 = e0 / den, e1 / den, e2 / den
    parts = []
    for j in range(HEADS_PER_GROUP):
        hs = slice(j * dh, (j + 1) * dh)
        parts.append(w0[:, j:j + 1] * o0[:, hs] + w1[:, j:j + 1] * o1[:, hs]
                     + w2[:, j:j + 1] * o2[:, hs])
    ha = jnp.concatenate(parts, axis=1).astype(BF16)
    ym = jnp.dot(hm_ref[...], wbm_ref[...], preferred_element_type=F32)
    ya = jnp.dot(ha, wba_ref[...], preferred_element_type=F32)
    merged = (_sigmoid(gm_ref[...].astype(F32)) * ym + _sigmoid(ga_ref[...].astype(F32)) * ya)
    x1 = x_ref[...] + jnp.dot(merged.astype(BF16), wo_ref[...], preferred_element_type=F32)
    x1_ref[...] = x1
    h2 = x1 * lax.rsqrt(jnp.mean(x1 * x1, axis=-1, keepdims=True) + RMS_EPS) * ng_ref[...]
    h2_ref[...] = h2
    logits = jnp.dot(h2, rw_ref[...], preferred_element_type=F32,
                     precision=lax.Precision.HIGHEST) + rb_ref[...]
    lane = lax.broadcasted_iota(jnp.int32, logits.shape, 1)
    idx_all = jnp.zeros(logits.shape, jnp.int32)
    val_all = jnp.zeros(logits.shape, F32)
    top0 = None
    esum = None
    for k in range(TOP_K):
        m = jnp.max(logits, axis=1, keepdims=True)
        sel = jnp.min(jnp.where(logits == m, lane, LANES), axis=1, keepdims=True)
        if k == 0:
            top0 = m
        e = jnp.exp(m - top0)
        esum = e if k == 0 else esum + e
        idx_all = jnp.where(lane == k, sel, idx_all)
        val_all = jnp.where(lane == k, e, val_all)
        logits = jnp.where(lane == sel, -jnp.inf, logits)
    idx_ref[...] = idx_all
    gate_ref[...] = val_all / esum


def _merge(hm, outs, lses, proj, x2, wbm, wba, wo, ng, rw, rb, *, bm=256):
    n = x2.shape[0]
    d = D_MODEL
    gcol = COL_GATES // d
    n_blk = outs[0].shape[2] // bm

    def rows(width):
        return pl.BlockSpec((bm, width), lambda i: (i, 0))

    def full(a, b):
        return pl.BlockSpec((a, b), lambda i: (0, 0), pipeline_mode=pl.Buffered(1))

    def residue(arr):
        dilation, width = arr.shape[1], arr.shape[3]
        return pl.BlockSpec((1, dilation, bm // dilation, width),
                            lambda i: (i // n_blk, 0, i % n_blk, 0))

    return pl.pallas_call(
        _merge_kernel,
        grid=(n // bm,),
        in_specs=[
            rows(MLSTM_WIDTH),
            residue(outs[0]), residue(outs[1]), residue(outs[2]),
            residue(lses[0]), residue(lses[1]), residue(lses[2]),
            pl.BlockSpec((bm, d), lambda i: (i, gcol)),
            pl.BlockSpec((bm, d), lambda i: (i, gcol + 1)),
            rows(d),
            full(MLSTM_WIDTH, d), full(GROUP_WIDTH, d), full(d, d),
            full(1, d), full(d, LANES), full(1, LANES),
        ],
        out_specs=[rows(d), rows(d), rows(LANES), rows(LANES)],
        out_shape=[
            jax.ShapeDtypeStruct((n, d), F32),
            jax.ShapeDtypeStruct((n, d), F32),
            jax.ShapeDtypeStruct((n, LANES), jnp.int32),
            jax.ShapeDtypeStruct((n, LANES), F32),
        ],
        compiler_params=_params(("arbitrary",)),
        name="merge",
    )(hm, outs[0], outs[1], outs[2], lses[0], lses[1], lses[2], proj, proj, x2,
      wbm, wba, wo, ng, rw, rb)


def _dispatch_kernel(tok_ref, h2_hbm, out_ref, buf_ref, sem, *, rows):
    def issue(s, carry):
        tok = tok_ref[s]
        pltpu.make_async_copy(h2_hbm.at[pl.ds(tok, 1), :], buf_ref.at[pl.ds(s, 1), :], sem).start()
        return carry

    lax.fori_loop(0, rows, issue, 0)
    pltpu.make_async_copy(h2_hbm.at[pl.ds(0, rows), :], buf_ref, sem).wait()
    out_ref[...] = buf_ref[...].astype(BF16)


def _dispatch(buf_tok, h2, *, rows=1024):
    n_pad = buf_tok.shape[0]
    d = h2.shape[1]
    kern = functools.partial(_dispatch_kernel, rows=rows)
    return pl.pallas_call(
        kern,
        grid=(n_pad // rows,),
        in_specs=[
            pl.BlockSpec((rows,), lambda i: (i,), memory_space=pltpu.SMEM),
            pl.BlockSpec(memory_space=pl.ANY),
        ],
        out_specs=pl.BlockSpec((rows, d), lambda i: (i, 0)),
        out_shape=jax.ShapeDtypeStruct((n_pad, d), BF16),
        scratch_shapes=[pltpu.VMEM((rows, d), F32), pltpu.SemaphoreType.DMA(())],
        compiler_params=_params(("arbitrary",)),
        name="dispatch",
    )(buf_tok, h2)


def _cast_rows(src_ref, dst_ref, rows):
    total = src_ref.shape[0]

    def body(i, carry):
        r = pl.multiple_of(i * rows, rows)
        dst_ref[pl.ds(r, rows), :] = src_ref[pl.ds(r, rows), :].astype(BF16)
        return carry

    lax.fori_loop(0, total // rows, body, 0)


def _block_state(be_ref, nu_ref):
    j = pl.program_id(1)
    changed = jnp.logical_or(j == 0, be_ref[j] != be_ref[jnp.maximum(j - 1, 0)])
    return j < nu_ref[0], changed


def _gate_up_kernel(be_ref, nu_ref, x_ref, wg_ref, wu_ref, bg_ref, bu_ref, o_ref, wg_bf, wu_bf):
    used, changed = _block_state(be_ref, nu_ref)

    @pl.when(jnp.logical_and(used, changed))
    def _():
        _cast_rows(wg_ref.at[0], wg_bf, 256)
        _cast_rows(wu_ref.at[0], wu_bf, 256)

    @pl.when(used)
    def _():
        x = x_ref[...]
        gate = jnp.dot(x, wg_bf[...], preferred_element_type=F32) + bg_ref[0]
        up = jnp.dot(x, wu_bf[...], preferred_element_type=F32) + bu_ref[0]
        gate = jnp.minimum(gate, SWIGLU_LIMIT)
        up = jnp.clip(up, -SWIGLU_LIMIT, SWIGLU_LIMIT)
        glu = gate * _sigmoid(SWIGLU_ALPHA * gate)
        o_ref[...] = ((up + 1.0) * glu).astype(BF16)

    @pl.when(jnp.logical_not(used))
    def _():
        o_ref[...] = jnp.zeros_like(o_ref)


def _used_block(j, nu):
    return jnp.minimum(j, nu[0] - 1)


def _gate_up(block_e, n_used, xb, w_gate_up, b_gate_up3, *, tm=MOE_TILE, tf=1024):
    n_pad, d = xb.shape
    kt = D_FF // tf

    def wspec(width, off):
        return pl.BlockSpec((1, width, tf),
                            lambda k, j, be, nu: (be[_used_block(j, nu)], 0, off + k))

    return pl.pallas_call(
        _gate_up_kernel,
        grid_spec=pltpu.PrefetchScalarGridSpec(
            num_scalar_prefetch=2,
            grid=(kt, n_pad // tm),
            in_specs=[
                pl.BlockSpec((tm, d), lambda k, j, be, nu: (_used_block(j, nu), 0)),
                wspec(d, 0), wspec(d, kt), wspec(1, 0), wspec(1, kt),
            ],
            out_specs=pl.BlockSpec((tm, tf), lambda k, j, be, nu: (j, k)),
            scratch_shapes=[pltpu.VMEM((d, tf), BF16), pltpu.VMEM((d, tf), BF16)],
        ),
        out_shape=jax.ShapeDtypeStruct((n_pad, D_FF), BF16),
        compiler_params=_params(("arbitrary", "arbitrary")),
        name="gate_up",
    )(block_e, n_used, xb, w_gate_up, w_gate_up, b_gate_up3, b_gate_up3)


def _down_kernel(be_ref, nu_ref, h_ref, wd_ref, bd_ref, o_ref, wd_bf):
    used, changed = _block_state(be_ref, nu_ref)

    @pl.when(jnp.logical_and(used, changed))
    def _():
        _cast_rows(wd_ref.at[0], wd_bf, 256)

    @pl.when(used)
    def _():
        o_ref[...] = jnp.dot(h_ref[...], wd_bf[...], preferred_element_type=F32) + bd_ref[0]

    @pl.when(jnp.logical_not(used))
    def _():
        o_ref[...] = jnp.zeros_like(o_ref)


def _down(block_e, n_used, hb, w_down, b_down3, *, tm=MOE_TILE, tn=1024):
    n_pad, f = hb.shape
    d = D_MODEL

    def wspec(width):
        return pl.BlockSpec((1, width, tn), lambda k, j, be, nu: (be[_used_block(j, nu)], 0, k))

    return pl.pallas_call(
        _down_kernel,
        grid_spec=pltpu.PrefetchScalarGridSpec(
            num_scalar_prefetch=2,
            grid=(d // tn, n_pad // tm),
            in_specs=[
                pl.BlockSpec((tm, f), lambda k, j, be, nu: (_used_block(j, nu), 0)),
                wspec(f), wspec(1),
            ],
            out_specs=pl.BlockSpec((tm, tn), lambda k, j, be, nu: (j, k)),
            scratch_shapes=[pltpu.VMEM((f, tn), BF16)],
        ),
        out_shape=jax.ShapeDtypeStruct((n_pad, d), F32),
        compiler_params=_params(("arbitrary", "arbitrary")),
        name="down",
    )(block_e, n_used, hb, w_down, b_down3)


def _combine_kernel(dest_ref, y_hbm, gate_ref, x1_ref, ng_ref, o_ref, buf_ref, sem, *, rows):
    def issue(t, carry):
        for k in range(TOP_K):
            slot = dest_ref[t * TOP_K + k]
            pltpu.make_async_copy(y_hbm.at[pl.ds(slot, 1), :], buf_ref.at[k, pl.ds(t, 1), :],
                                  sem).start()
        return carry

    lax.fori_loop(0, rows, issue, 0)
    for k in range(TOP_K):
        pltpu.make_async_copy(y_hbm.at[pl.ds(0, rows), :], buf_ref.at[k], sem).wait()
    acc = x1_ref[...]
    for k in range(TOP_K):
        acc = acc + gate_ref[:, k:k + 1] * buf_ref[k]
    o_ref[...] = acc * lax.rsqrt(jnp.mean(acc * acc, axis=-1, keepdims=True) + RMS_EPS) * ng_ref[...]


def _combine(dest, yb, gates, x1, ng, *, rows=256):
    n, d = x1.shape
    kern = functools.partial(_combine_kernel, rows=rows)
    return pl.pallas_call(
        kern,
        grid=(n // rows,),
        in_specs=[
            pl.BlockSpec((rows * TOP_K,), lambda i: (i,), memory_space=pltpu.SMEM),
            pl.BlockSpec(memory_space=pl.ANY),
            pl.BlockSpec((rows, LANES), lambda i: (i, 0)),
            pl.BlockSpec((rows, d), lambda i: (i, 0)),
            pl.BlockSpec((1, d), lambda i: (0, 0)),
        ],
        out_specs=pl.BlockSpec((rows, d), lambda i: (i, 0)),
        out_shape=jax.ShapeDtypeStruct((n, d), F32),
        scratch_shapes=[pltpu.VMEM((TOP_K, rows, d), F32), pltpu.SemaphoreType.DMA(())],
        compiler_params=_params(("arbitrary",)),
        name="combine",
    )(dest, yb, gates, x1, ng)


def _routing(top_idx, tm):
    n_tok = top_idx.shape[0]
    n_asg = n_tok * TOP_K
    e_flat = top_idx.reshape(n_asg)
    onehot = (e_flat[:, None] == jnp.arange(N_EXPERTS, dtype=jnp.int32)[None, :]).astype(jnp.int32)
    cum = jnp.cumsum(onehot, axis=0)
    rank = jnp.sum(onehot * cum, axis=1) - 1
    counts = cum[-1]
    padded = (counts + tm - 1) // tm * tm
    pend = jnp.cumsum(padded)
    pstart = pend - padded
    dest = (pstart[e_flat] + rank).astype(jnp.int32)
    big = 1024
    n_pad = -(-(n_asg + N_EXPERTS * (tm - 1)) // big) * big
    n_blk = n_pad // tm
    buf_tok = jnp.zeros((n_pad,), jnp.int32).at[dest].set(
        jnp.arange(n_asg, dtype=jnp.int32) // TOP_K, unique_indices=True)
    block_start = jnp.arange(n_blk, dtype=jnp.int32) * tm
    block_e = jnp.minimum(
        jnp.sum((pend[None, :] <= block_start[:, None]).astype(jnp.int32), axis=1),
        N_EXPERTS - 1).astype(jnp.int32)
    n_used = (pend[-1:] // tm).astype(jnp.int32)
    return dest, buf_tok, block_e, n_used


def _layer(x, norm_mix_g, w_in, conv_w, conv_b, igate_b, fgate_b, mlstm_norm_g, rel_bias,
           w_branch_mlstm, w_branch_attn, w_out, norm_moe_g, router_w, router_b,
           w_gate_up, b_gate_up, w_down, b_down, out_norm_g):
    bsz, s, d = x.shape
    n = bsz * s
    x2 = x.reshape(n, d)
    w2, w1 = 2 * MLSTM_WIDTH, MLSTM_WIDTH
    o_qk, o_v, o_o = 0, w2, w2 + w1
    o_i = o_o + w1
    o_f = o_i + MLSTM_HEADS
    o_qa = o_f + MLSTM_HEADS
    o_ka, o_va = o_qa + ATTN_WIDTH, o_qa + 2 * ATTN_WIDTH
    o_g = o_va + ATTN_WIDTH
    w_main = jnp.concatenate([w_in[:, o_qk:o_i], w_in[:, o_g:]], axis=1).astype(BF16)
    zpad = jnp.zeros((d, LANES - MLSTM_HEADS), w_in.dtype)
    w_if = jnp.concatenate([w_in[:, o_i:o_f], zpad, w_in[:, o_f:o_qa], zpad], axis=1).astype(BF16)
    bpad = jnp.zeros((LANES - MLSTM_HEADS,), F32)
    gate_b = jnp.concatenate([igate_b, bpad, fgate_b, bpad]).reshape(1, IF_WIDTH)

    norm_g = norm_mix_g.reshape(1, d)
    proj, ifg = _in_proj(x2, norm_g, w_main, w_if)
    proj3 = proj.reshape(bsz, s, D_MAIN)
    hm = _mlstm(proj3, ifg.reshape(bsz, s, IF_WIDTH), conv_w, conv_b.reshape(1, -1), gate_b,
                mlstm_norm_g.reshape(1, -1)).reshape(n, MLSTM_WIDTH)

    outs, lses = [], []
    gw = GROUP_WIDTH
    for g, (_, dilation) in enumerate(ATTN_GROUPS):
        w_qkv = jnp.concatenate(
            [w_in[:, o + g * gw:o + (g + 1) * gw] for o in (o_qa, o_ka, o_va)], axis=1).astype(BF16)
        qkv = _attn_proj(x2, norm_g, w_qkv, bsz, s, dilation)
        o_g_, lse_g = _attn_group(qkv, _attn_bias(rel_bias, g, dilation), g)
        outs.append(o_g_)
        lses.append(lse_g)

    rw = jnp.concatenate([router_w, jnp.zeros((d, LANES - N_EXPERTS), F32)], axis=1)
    rb = jnp.concatenate([router_b, jnp.full((LANES - N_EXPERTS,), NEG, F32)]).reshape(1, LANES)
    x1, h2, idx, gates = _merge(
        hm, outs, lses, proj, x2, w_branch_mlstm.astype(BF16), w_branch_attn.astype(BF16),
        w_out.astype(BF16), norm_moe_g.reshape(1, d), rw, rb)

    dest, buf_tok, block_e, n_used = _routing(idx[:, :TOP_K], MOE_TILE)
    xb = _dispatch(buf_tok, h2)
    hb = _gate_up(block_e, n_used, xb, w_gate_up, b_gate_up.reshape(N_EXPERTS, 1, 2 * D_FF))
    yb = _down(block_e, n_used, hb, w_down, b_down.reshape(N_EXPERTS, 1, d))
    out = _combine(dest, yb, gates, x1, out_norm_g.reshape(1, d))
    return out.reshape(bsz, s, d)


def kernel(x, norm_mix_g, w_in, conv_w, conv_b, igate_b, fgate_b, mlstm_norm_g, rel_bias,
           w_branch_mlstm, w_branch_attn, w_out, norm_moe_g, router_w, router_b,
           w_gate_up, b_gate_up, w_down, b_down, norm_final_g):
    assert w_in.shape[0] == 1, "single-layer block"
    return _layer(x, norm_mix_g[0], w_in[0], conv_w[0], conv_b[0], igate_b[0], fgate_b[0],
                  mlstm_norm_g[0], rel_bias, w_branch_mlstm[0], w_branch_attn[0], w_out[0],
                  norm_moe_g[0], router_w[0], router_b[0], w_gate_up[0], b_gate_up[0],
                  w_down[0], b_down[0], norm_final_g)
```

```python
import functools
import math

import jax
import jax.numpy as jnp
from jax import lax
from jax.experimental import pallas as pl
from jax.experimental.pallas import tpu as pltpu

F32 = jnp.float32
BF16 = jnp.bfloat16

D_MODEL = 2048
MLSTM_HEADS = 8
HEAD_DIM = 128
MLSTM_WIDTH = MLSTM_HEADS * HEAD_DIM
CONV_WIDTH = 4
ATTN_GROUPS = ((128, 1), (512, 4), (2048, 16))
N_GROUPS = 3
HEADS_PER_GROUP = 4
ATTN_HEADS = HEADS_PER_GROUP * N_GROUPS
ATTN_WIDTH = ATTN_HEADS * HEAD_DIM
GROUP_WIDTH = HEADS_PER_GROUP * HEAD_DIM
ATTN_BLOCK = 128
ATTN_SPAN = 128
REL_BUCKETS = 32
REL_MAX_DIST = 2048
N_EXPERTS = 32
TOP_K = 4
D_FF = D_MODEL
SWIGLU_LIMIT = 7.0
SWIGLU_ALPHA = 1.702
RMS_EPS = 1e-6
NEG = -1e30

COL_QK = 0
COL_V = 2 * MLSTM_WIDTH
COL_O = COL_V + MLSTM_WIDTH
COL_GATES = COL_O + MLSTM_WIDTH
D_MAIN = COL_GATES + 2 * D_MODEL
QKV_WIDTH = 3 * GROUP_WIDTH
LANES = 128
IF_WIDTH = 2 * LANES

MLSTM_CHUNK = 128
MOE_TILE = 512
VMEM_LIMIT = 56 * 1024 * 1024


def _sigmoid(x):
    return 1.0 / (1.0 + jnp.exp(-x))


def _log_sigmoid(x):
    return -(jnp.maximum(-x, 0.0) + jnp.log1p(jnp.exp(-jnp.abs(x))))


def _params(sem):
    return pltpu.CompilerParams(dimension_semantics=sem, vmem_limit_bytes=VMEM_LIMIT)


def _in_proj_kernel(x_ref, g_ref, w_ref, wif_ref, o_ref, oif_ref, h_ref, *, bm, rows):
    @pl.when(pl.program_id(1) == 0)
    def _():
        for r in range(0, bm, rows):
            x = x_ref[r:r + rows, :]
            ms = jnp.mean(x * x, axis=-1, keepdims=True)
            h = (x * lax.rsqrt(ms + RMS_EPS) * g_ref[...]).astype(BF16)
            h_ref[r:r + rows, :] = h
            oif_ref[r:r + rows, :] = jnp.dot(h, wif_ref[...], preferred_element_type=F32)

    o_ref[...] = jnp.dot(h_ref[...], w_ref[...], preferred_element_type=F32).astype(BF16)


def _residue_perm(size, dilation, inverse):
    per = size // dilation
    i = lax.broadcasted_iota(jnp.int32, (size, size), 0)
    j = lax.broadcasted_iota(jnp.int32, (size, size), 1)
    if inverse:
        src = (i & (dilation - 1)) * per + (i >> (dilation.bit_length() - 1))
    else:
        src = (i & (per - 1)) * dilation + (i >> (per.bit_length() - 1))
    return j == src


def _attn_proj_kernel(x_ref, g_ref, w_ref, o_ref, h_ref, *, bm, dilation, rows):
    per = rows // dilation
    if dilation > 1:
        perm = _residue_perm(rows, dilation, False).astype(BF16)
    for c in range(0, bm, rows):
        x = x_ref[c:c + rows, :]
        ms = jnp.mean(x * x, axis=-1, keepdims=True)
        h = (x * lax.rsqrt(ms + RMS_EPS) * g_ref[...]).astype(BF16)
        if dilation > 1:
            h = jnp.dot(perm, h, preferred_element_type=F32).astype(BF16)
        h_ref[c:c + rows, :] = h
    res = jnp.dot(h_ref[...], w_ref[...], preferred_element_type=F32).astype(BF16)
    for c in range(bm // rows):
        for r in range(dilation):
            o_ref[0, r, c * per:(c + 1) * per, :] = res[c * rows + r * per:c * rows + (r + 1) * per, :]


def _attn_proj(x2, g, w_qkv, bsz, s, dilation, *, bm=1024):
    n_blk = s // bm
    per = bm // dilation
    kern = functools.partial(_attn_proj_kernel, bm=bm, dilation=dilation, rows=256)
    return pl.pallas_call(
        kern,
        grid=(bsz, n_blk),
        in_specs=[
            pl.BlockSpec((bm, D_MODEL), lambda b, i: (b * n_blk + i, 0)),
            pl.BlockSpec((1, D_MODEL), lambda b, i: (0, 0)),
            pl.BlockSpec((D_MODEL, QKV_WIDTH), lambda b, i: (0, 0),
                         pipeline_mode=pl.Buffered(1)),
        ],
        out_specs=pl.BlockSpec((1, dilation, per, QKV_WIDTH), lambda b, i: (b, 0, i, 0)),
        out_shape=jax.ShapeDtypeStruct((bsz, dilation, s // dilation, QKV_WIDTH), BF16),
        scratch_shapes=[pltpu.VMEM((bm, D_MODEL), BF16)],
        compiler_params=_params(("arbitrary", "arbitrary")),
        name=f"attn_proj_d{dilation}",
    )(x2, g, w_qkv)


def _in_proj(x2, g, w_main, w_if, *, bm=1024, bn=1024):
    n = x2.shape[0]
    bm = min(bm, n)
    kern = functools.partial(_in_proj_kernel, bm=bm, rows=256)
    return pl.pallas_call(
        kern,
        grid=(n // bm, D_MAIN // bn),
        in_specs=[
            pl.BlockSpec((bm, D_MODEL), lambda i, j: (i, 0)),
            pl.BlockSpec((1, D_MODEL), lambda i, j: (0, 0)),
            pl.BlockSpec((D_MODEL, bn), lambda i, j: (0, j)),
            pl.BlockSpec((D_MODEL, IF_WIDTH), lambda i, j: (0, 0)),
        ],
        out_specs=[
            pl.BlockSpec((bm, bn), lambda i, j: (i, j)),
            pl.BlockSpec((bm, IF_WIDTH), lambda i, j: (i, 0)),
        ],
        out_shape=[
            jax.ShapeDtypeStruct((n, D_MAIN), BF16),
            jax.ShapeDtypeStruct((n, IF_WIDTH), F32),
        ],
        scratch_shapes=[pltpu.VMEM((bm, D_MODEL), BF16)],
        compiler_params=_params(("arbitrary", "arbitrary")),
        name="in_proj",
    )(x2, g, w_main, w_if)


def _mlstm_kernel(qk_ref, v_ref, og_ref, if_ref, cw_ref, cb_ref, gb_ref, ng_ref, out_ref,
                  tail_ref, c_ref, m_ref, *, chunk):
    L = chunk
    dh = HEAD_DIM

    @pl.when(pl.program_id(1) == 0)
    def _():
        tail_ref[...] = jnp.zeros_like(tail_ref)
        c_ref[...] = jnp.zeros_like(c_ref)
        m_ref[...] = jnp.zeros_like(m_ref)

    cur = qk_ref[0].astype(F32)
    ext = jnp.concatenate([tail_ref[...], cur], axis=0)
    acc = cur * cw_ref[CONV_WIDTH - 1:CONV_WIDTH, :] + cb_ref[...]
    for s in range(1, CONV_WIDTH):
        acc = acc + ext[8 - s:8 - s + L, :] * cw_ref[CONV_WIDTH - 1 - s:CONV_WIDTH - s, :]
    tail_ref[...] = cur[L - 8:, :]
    qk = acc * _sigmoid(acc)
    q_all = qk[:, :MLSTM_WIDTH].astype(BF16)
    k_all = (qk[:, MLSTM_WIDTH:] * (dh ** -0.5)).astype(BF16)

    gi = if_ref[0][:, :LANES] + gb_ref[:, :LANES]
    lf = _log_sigmoid(if_ref[0][:, LANES:] + gb_ref[:, LANES:])
    row = lax.broadcasted_iota(jnp.int32, (L, L), 0)
    col = lax.broadcasted_iota(jnp.int32, (L, L), 1)
    causal = col <= row
    tri = causal.astype(F32)
    cum = jnp.dot(tri, lf, preferred_element_type=F32, precision=lax.Precision.HIGHEST)
    rmat = gi - cum
    rmat_t = rmat.T
    ones_blk = jnp.ones((L, dh), BF16)

    for h in range(MLSTM_HEADS):
        hs = slice(h * dh, (h + 1) * dh)
        b = cum[:, h:h + 1]
        r_col = rmat[:, h:h + 1]
        r_row = rmat_t[h:h + 1, :]
        m_prev = m_ref[h:h + 1, 0:1]
        logw = jnp.where(causal, b + r_row, NEG)
        inter = b + m_prev
        m_out = jnp.maximum(inter, jnp.max(logw, axis=1, keepdims=True))
        p = jnp.exp(logw - m_out)
        qh = q_all[:, hs]
        kh = k_all[:, hs]
        s = lax.dot_general(qh, kh, (((1,), (1,)), ((), ())), preferred_element_type=F32)
        w = (p * s).astype(BF16)
        vaug = jnp.concatenate([v_ref[0][:, hs], ones_blk], axis=1)
        c_prev = c_ref[h]
        nd = (jnp.dot(w, vaug, preferred_element_type=F32)
              + jnp.exp(inter - m_out)
              * jnp.dot(qh, c_prev.astype(BF16), preferred_element_type=F32))
        hh = nd[:, :dh] / jnp.maximum(jnp.abs(nd[:, dh:]), jnp.exp(-m_out))

        b_last = b[L - 1:L, :]
        m_new = b_last + jnp.maximum(m_prev, jnp.max(r_col, axis=0, keepdims=True))
        ws = jnp.exp(b_last + r_col - m_new)
        decay = jnp.exp(b_last + m_prev - m_new)
        wv = (ws * vaug.astype(F32)).astype(BF16)
        kv = lax.dot_general(kh, wv, (((0,), (0,)), ((), ())), preferred_element_type=F32)
        c_ref[h] = decay * c_prev + kv
        m_ref[h:h + 1, :] = jnp.broadcast_to(m_new, (1, LANES))

        y = hh * lax.rsqrt(jnp.mean(hh * hh, axis=-1, keepdims=True) + RMS_EPS) * ng_ref[:, hs]
        y = y * _sigmoid(og_ref[0][:, hs].astype(F32))
        out_ref[0, :, hs] = y.astype(BF16)


def _mlstm(proj3, ifg3, conv_w, conv_b, gate_b, norm_g, *, chunk=MLSTM_CHUNK):
    bsz, s, _ = proj3.shape
    w = MLSTM_WIDTH
    kern = functools.partial(_mlstm_kernel, chunk=chunk)
    return pl.pallas_call(
        kern,
        grid=(bsz, s // chunk),
        in_specs=[
            pl.BlockSpec((1, chunk, 2 * w), lambda b, c: (b, c, COL_QK // (2 * w))),
            pl.BlockSpec((1, chunk, w), lambda b, c: (b, c, COL_V // w)),
            pl.BlockSpec((1, chunk, w), lambda b, c: (b, c, COL_O // w)),
            pl.BlockSpec((1, chunk, IF_WIDTH), lambda b, c: (b, c, 0)),
            pl.BlockSpec((CONV_WIDTH, 2 * w), lambda b, c: (0, 0)),
            pl.BlockSpec((1, 2 * w), lambda b, c: (0, 0)),
            pl.BlockSpec((1, IF_WIDTH), lambda b, c: (0, 0)),
            pl.BlockSpec((1, w), lambda b, c: (0, 0)),
        ],
        out_specs=pl.BlockSpec((1, chunk, w), lambda b, c: (b, c, 0)),
        out_shape=jax.ShapeDtypeStruct((bsz, s, w), BF16),
        scratch_shapes=[
            pltpu.VMEM((8, 2 * w), F32),
            pltpu.VMEM((MLSTM_HEADS, HEAD_DIM, 2 * HEAD_DIM), F32),
            pltpu.VMEM((MLSTM_HEADS, LANES), F32),
        ],
        compiler_params=_params(("arbitrary", "arbitrary")),
        name="mlstm",
    )(proj3, proj3, proj3, ifg3, conv_w, conv_b, gate_b, norm_g)


def _attn_kernel(q_ref, kp_ref, kc_ref, vp_ref, vc_ref, bias_ref, o_ref, lse_ref):
    qb = ATTN_BLOCK
    dh = HEAD_DIM
    first = pl.program_id(2) == 0
    kcol = lax.broadcasted_iota(jnp.int32, (qb, 2 * qb), 1)
    dead = jnp.logical_and(first, kcol < qb)
    lane = lax.broadcasted_iota(jnp.int32, (qb, LANES), 1)
    lse_all = jnp.zeros((qb, LANES), F32)
    for j in range(HEADS_PER_GROUP):
        hs = slice(j * dh, (j + 1) * dh)
        q = q_ref[0, 0][:, hs]
        k = jnp.concatenate([kp_ref[0, 0][:, hs], kc_ref[0, 0][:, hs]], axis=0)
        v = jnp.concatenate([vp_ref[0, 0][:, hs], vc_ref[0, 0][:, hs]], axis=0)
        s = lax.dot_general(q, k, (((1,), (1,)), ((), ())), preferred_element_type=F32)
        s = s * (dh ** -0.5) + bias_ref[j]
        s = jnp.where(dead, NEG, s)
        mx = jnp.max(s, axis=1, keepdims=True)
        p = jnp.exp(s - mx)
        den = jnp.sum(p, axis=1, keepdims=True)
        o = jnp.dot(p.astype(BF16), v, preferred_element_type=F32) / den
        o_ref[0, 0, :, hs] = o
        lse_all = jnp.where(lane == j, mx + jnp.log(den), lse_all)
    lse_ref[0, 0] = lse_all


def _attn_group(qkv, bias, g):
    bsz, dilation, n, _ = qkv.shape
    nb = n // ATTN_BLOCK
    gw = GROUP_WIDTH

    def spec(part, prev):
        def imap(b, r, i):
            return (b, r, jnp.maximum(i - 1, 0) if prev else i, part)

        return pl.BlockSpec((1, 1, ATTN_BLOCK, gw), imap)

    return pl.pallas_call(
        _attn_kernel,
        grid=(bsz, dilation, nb),
        in_specs=[
            spec(0, False),
            spec(1, True), spec(1, False),
            spec(2, True), spec(2, False),
            pl.BlockSpec((HEADS_PER_GROUP, ATTN_BLOCK, 2 * ATTN_BLOCK), lambda b, r, i: (0, 0, 0)),
        ],
        out_specs=[
            pl.BlockSpec((1, 1, ATTN_BLOCK, gw), lambda b, r, i: (b, r, i, 0)),
            pl.BlockSpec((1, 1, ATTN_BLOCK, LANES), lambda b, r, i: (b, r, i, 0)),
        ],
        out_shape=[
            jax.ShapeDtypeStruct((bsz, dilation, n, gw), F32),
            jax.ShapeDtypeStruct((bsz, dilation, n, LANES), F32),
        ],
        compiler_params=_params(("arbitrary", "arbitrary", "arbitrary")),
        name=f"attn_g{g}",
    )(qkv, qkv, qkv, qkv, qkv, bias)


def _t5_bucket(dist):
    max_exact = REL_BUCKETS // 2
    d_f = jnp.maximum(dist, 1).astype(F32)
    large = max_exact + (jnp.log(d_f / max_exact) / math.log(REL_MAX_DIST / max_exact)
                         * (REL_BUCKETS - max_exact)).astype(jnp.int32)
    large = jnp.minimum(large, REL_BUCKETS - 1)
    return jnp.where(dist < max_exact, dist, large)


def _attn_bias(rel_bias, g, dilation):
    span = ATTN_SPAN
    buckets = _t5_bucket(jnp.arange(span + 1, dtype=jnp.int32) * dilation)
    vec = rel_bias[buckets][:, g * HEADS_PER_GROUP:(g + 1) * HEADS_PER_GROUP].T.astype(F32)
    qpos = jnp.arange(ATTN_BLOCK)[:, None]
    kpos = jnp.arange(2 * ATTN_BLOCK)[None, :]
    dist = qpos + ATTN_BLOCK - kpos
    valid = (dist >= 0) & (dist <= span)
    onehot = (dist[:, :, None] == jnp.arange(span + 1)[None, None, :]).astype(F32)
    table = jnp.einsum('qkj,hj->hqk', onehot, vec, precision=lax.Precision.HIGHEST)
    return jnp.where(valid[None], table, NEG)


def _merge_kernel(hm_ref, o0_ref, o1_ref, o2_ref, l0_ref, l1_ref, l2_ref, gm_ref, ga_ref, x_ref,
                  wbm_ref, wba_ref, wo_ref, ng_ref, rw_ref, rb_ref,
                  x1_ref, h2_ref, idx_ref, gate_ref):
    dh = HEAD_DIM
    bm = x_ref.shape[0]

    def token_order(ref):
        dilation = ref.shape[1]
        if dilation == 1:
            return ref[0, 0]
        perm = _residue_perm(bm, dilation, True).astype(F32)
        return jnp.dot(perm, ref[0].reshape(bm, ref.shape[3]), preferred_element_type=F32,
                       precision=lax.Precision.HIGHEST)

    l0, l1, l2 = token_order(l0_ref), token_order(l1_ref), token_order(l2_ref)
    o0, o1, o2 = token_order(o0_ref), token_order(o1_ref), token_order(o2_ref)
    mx = jnp.maximum(jnp.maximum(l0, l1), l2)
    e0, e1, e2 = jnp.exp(l0 - mx), jnp.exp(l1 - mx), jnp.exp(l2 - mx)
    den = e0 + e1 + e2
    w0, w1, w2 = e0 / den, e1 / den, e2 / den
    parts = []
    for j in range(HEADS_PER_GROUP):
        hs = slice(j * dh, (j + 1) * dh)
        parts.append(w0[:, j:j + 1] * o0[:, hs] + w1[:, j:j + 1] * o1[:, hs]
                     + w2[:, j:j + 1] * o2[:, hs])
    ha = jnp.concatenate(parts, axis=1).astype(BF16)
    ym = jnp.dot(hm_ref[...], wbm_ref[...], preferred_element_type=F32)
    ya = jnp.dot(ha, wba_ref[...], preferred_element_type=F32)
    merged = (_sigmoid(gm_ref[...].astype(F32)) * ym + _sigmoid(ga_ref[...].astype(F32)) * ya)
    x1 = x_ref[...] + jnp.dot(merged.astype(BF16), wo_ref[...], preferred_element_type=F32)
    x1_ref[...] = x1
    h2 = x1 * lax.rsqrt(jnp.mean(x1 * x1, axis=-1, keepdims=True) + RMS_EPS) * ng_ref[...]
    _store_token_tiles(h2_ref, _pack_pairs(h2))
    logits = jnp.dot(h2, rw_ref[...], preferred_element_type=F32,
                     precision=lax.Precision.HIGHEST) + rb_ref[...]
    lane = lax.broadcasted_iota(jnp.int32, logits.shape, 1)
    idx_all = jnp.zeros(logits.shape, jnp.int32)
    val_all = jnp.zeros(logits.shape, F32)
    top0 = None
    esum = None
    for k in range(TOP_K):
        m = jnp.max(logits, axis=1, keepdims=True)
        sel = jnp.min(jnp.where(logits == m, lane, LANES), axis=1, keepdims=True)
        if k == 0:
            top0 = m
        e = jnp.exp(m - top0)
        esum = e if k == 0 else esum + e
        idx_all = jnp.where(lane == k, sel, idx_all)
        val_all = jnp.where(lane == k, e, val_all)
        logits = jnp.where(lane == sel, -jnp.inf, logits)
    idx_ref[...] = idx_all
    gate_ref[...] = val_all / esum


def _merge(hm, outs, lses, proj, x2, wbm, wba, wo, ng, rw, rb, *, bm=256):
    n = x2.shape[0]
    d = D_MODEL
    gcol = COL_GATES // d
    n_blk = outs[0].shape[2] // bm

    def rows(width):
        return pl.BlockSpec((bm, width), lambda i: (i, 0))

    def full(a, b):
        return pl.BlockSpec((a, b), lambda i: (0, 0), pipeline_mode=pl.Buffered(1))

    def residue(arr):
        dilation, width = arr.shape[1], arr.shape[3]
        return pl.BlockSpec((1, dilation, bm // dilation, width),
                            lambda i: (i // n_blk, 0, i % n_blk, 0))

    return pl.pallas_call(
        _merge_kernel,
        grid=(n // bm,),
        in_specs=[
            rows(MLSTM_WIDTH),
            residue(outs[0]), residue(outs[1]), residue(outs[2]),
            residue(lses[0]), residue(lses[1]), residue(lses[2]),
            pl.BlockSpec((bm, d), lambda i: (i, gcol)),
            pl.BlockSpec((bm, d), lambda i: (i, gcol + 1)),
            rows(d),
            full(MLSTM_WIDTH, d), full(GROUP_WIDTH, d), full(d, d),
            full(1, d), full(d, LANES), full(1, LANES),
        ],
        out_specs=[rows(d), pl.BlockSpec((bm * PACK_ROWS, LANES), lambda i: (i, 0)),
                   rows(LANES), rows(LANES)],
        out_shape=[
            jax.ShapeDtypeStruct((n, d), F32),
            jax.ShapeDtypeStruct((n * PACK_ROWS, LANES), jnp.uint32),
            jax.ShapeDtypeStruct((n, LANES), jnp.int32),
            jax.ShapeDtypeStruct((n, LANES), F32),
        ],
        compiler_params=_params(("arbitrary",)),
        name="merge",
    )(hm, outs[0], outs[1], outs[2], lses[0], lses[1], lses[2], proj, proj, x2,
      wbm, wba, wo, ng, rw, rb)


PACK_ROWS = 8


def _pack_pairs(x):
    w = x.shape[1] // 2
    lo = lax.bitcast_convert_type(x[:, :w].astype(BF16).astype(F32), jnp.uint32) >> 16
    hi = lax.bitcast_convert_type(x[:, w:].astype(BF16).astype(F32), jnp.uint32)
    return (hi & jnp.uint32(0xFFFF0000)) | lo


def _unpack_pairs(words):
    lo = lax.bitcast_convert_type(words << 16, F32)
    hi = lax.bitcast_convert_type(words & jnp.uint32(0xFFFF0000), F32)
    return lo, hi


def _store_token_tiles(ref, words):
    rows = words.shape[0]
    for s in range(PACK_ROWS):
        ref[pl.ds(s, rows, stride=PACK_ROWS), :] = words[:, s * LANES:(s + 1) * LANES]


def _dispatch_kernel(fs_ref, fl_ref, nu_ref, dest_ref, hp_ref, xb_hbm, zero_ref, sem, zsem, *,
                     tokens, tm, n_blk):
    pr = PACK_ROWS
    fill_sizes = [1 << b for b in reversed(range((tm - 1).bit_length()))]

    def fill_copies(e):
        off = fs_ref[e]
        for p in fill_sizes:
            take = (fl_ref[e] & p) != 0
            dst = xb_hbm.at[pl.ds(pl.multiple_of(off * pr, pr), p * pr), :]
            yield take, pltpu.make_async_copy(zero_ref.at[pl.ds(0, p * pr), :], dst, zsem)
            off = off + jnp.where(take, p, 0)

    def block_copy(b):
        dst = xb_hbm.at[pl.ds(pl.multiple_of(b * (tm * pr), tm * pr), tm * pr), :]
        return pltpu.make_async_copy(zero_ref, dst, zsem)

    @pl.when(pl.program_id(0) == 0)
    def _():
        zero_ref[...] = jnp.zeros_like(zero_ref)
        for start in (True, False):
            def per_expert(e, carry):
                for take, cp in fill_copies(e):
                    @pl.when(take)
                    def _():
                        cp.start() if start else cp.wait()
                return carry

            def per_block(b, carry):
                block_copy(b).start() if start else block_copy(b).wait()
                return carry

            lax.fori_loop(0, N_EXPERTS, per_expert, 0)
            lax.fori_loop(nu_ref[0], n_blk, per_block, 0)

    def issue(t, carry):
        src = hp_ref.at[pl.ds(pl.multiple_of(t * pr, pr), pr), :]
        for k in range(TOP_K):
            slot = dest_ref[t * TOP_K + k]
            pltpu.make_async_copy(src, xb_hbm.at[pl.ds(pl.multiple_of(slot * pr, pr), pr), :],
                                  sem).start()
        return carry

    lax.fori_loop(0, tokens, issue, 0)
    for k in range(TOP_K):
        pltpu.make_async_copy(hp_ref, xb_hbm.at[pl.ds(0, tokens * pr), :], sem).wait()


def _dispatch(fill_start, fill_len, n_used, dest, h2p, n_pad, *, tokens=256, tm=MOE_TILE):
    pr = PACK_ROWS
    n_tok = h2p.shape[0] // pr
    kern = functools.partial(_dispatch_kernel, tokens=tokens, tm=tm, n_blk=n_pad // tm)
    return pl.pallas_call(
        kern,
        grid_spec=pltpu.PrefetchScalarGridSpec(
            num_scalar_prefetch=3,
            grid=(n_tok // tokens,),
            in_specs=[
                pl.BlockSpec((tokens * TOP_K,), lambda i, fs, fl, nu: (i,),
                             memory_space=pltpu.SMEM),
                pl.BlockSpec((tokens * pr, LANES), lambda i, fs, fl, nu: (i, 0)),
            ],
            out_specs=pl.BlockSpec(memory_space=pl.ANY),
            scratch_shapes=[pltpu.VMEM((tm * pr, LANES), jnp.uint32),
                            pltpu.SemaphoreType.DMA(()), pltpu.SemaphoreType.DMA(())],
        ),
        out_shape=jax.ShapeDtypeStruct((n_pad * pr, LANES), jnp.uint32),
        compiler_params=_params(("arbitrary",)),
        name="dispatch",
    )(fill_start, fill_len, n_used, dest, h2p)


def _cast_rows(src_ref, dst_ref, rows):
    total = src_ref.shape[0]

    def body(i, carry):
        r = pl.multiple_of(i * rows, rows)
        dst_ref[pl.ds(r, rows), :] = src_ref[pl.ds(r, rows), :].astype(BF16)
        return carry

    lax.fori_loop(0, total // rows, body, 0)


def _block_state(be_ref, nu_ref):
    j = pl.program_id(1)
    changed = jnp.logical_or(j == 0, be_ref[j] != be_ref[jnp.maximum(j - 1, 0)])
    return j < nu_ref[0], changed


def _gate_up_kernel(be_ref, nu_ref, x_ref, wg_ref, wu_ref, bg_ref, bu_ref, o_ref, wg_bf, wu_bf):
    used, changed = _block_state(be_ref, nu_ref)

    @pl.when(jnp.logical_and(used, changed))
    def _():
        _cast_rows(wg_ref.at[0], wg_bf, 256)
        _cast_rows(wu_ref.at[0], wu_bf, 256)

    @pl.when(used)
    def _():
        tm = x_ref.shape[0] // PACK_ROWS
        halves = [_unpack_pairs(x_ref[pl.ds(s, tm, stride=PACK_ROWS), :]) for s in range(PACK_ROWS)]
        x = jnp.concatenate([h[0].astype(BF16) for h in halves]
                            + [h[1].astype(BF16) for h in halves], axis=1)
        gate = jnp.dot(x, wg_bf[...], preferred_element_type=F32) + bg_ref[0]
        up = jnp.dot(x, wu_bf[...], preferred_element_type=F32) + bu_ref[0]
        gate = jnp.minimum(gate, SWIGLU_LIMIT)
        up = jnp.clip(up, -SWIGLU_LIMIT, SWIGLU_LIMIT)
        glu = gate * _sigmoid(SWIGLU_ALPHA * gate)
        o_ref[...] = ((up + 1.0) * glu).astype(BF16)

    @pl.when(jnp.logical_not(used))
    def _():
        o_ref[...] = jnp.zeros_like(o_ref)


def _used_block(j, nu):
    return jnp.minimum(j, nu[0] - 1)


def _gate_up(block_e, n_used, xb, w_gate_up, b_gate_up3, *, tm=MOE_TILE, tf=1024):
    n_pad, d = xb.shape[0] // PACK_ROWS, D_MODEL
    kt = D_FF // tf

    def wspec(width, off):
        return pl.BlockSpec((1, width, tf),
                            lambda k, j, be, nu: (be[_used_block(j, nu)], 0, off + k))

    return pl.pallas_call(
        _gate_up_kernel,
        grid_spec=pltpu.PrefetchScalarGridSpec(
            num_scalar_prefetch=2,
            grid=(kt, n_pad // tm),
            in_specs=[
                pl.BlockSpec((tm * PACK_ROWS, LANES), lambda k, j, be, nu: (_used_block(j, nu), 0)),
                wspec(d, 0), wspec(d, kt), wspec(1, 0), wspec(1, kt),
            ],
            out_specs=pl.BlockSpec((tm, tf), lambda k, j, be, nu: (j, k)),
            scratch_shapes=[pltpu.VMEM((d, tf), BF16), pltpu.VMEM((d, tf), BF16)],
        ),
        out_shape=jax.ShapeDtypeStruct((n_pad, D_FF), BF16),
        compiler_params=_params(("arbitrary", "arbitrary")),
        name="gate_up",
    )(block_e, n_used, xb, w_gate_up, w_gate_up, b_gate_up3, b_gate_up3)


def _down_kernel(be_ref, nu_ref, h_ref, wd_ref, bd_ref, o_ref, wd_bf):
    used, changed = _block_state(be_ref, nu_ref)

    @pl.when(jnp.logical_and(used, changed))
    def _():
        _cast_rows(wd_ref.at[0], wd_bf, 256)

    @pl.when(used)
    def _():
        o_ref[...] = jnp.dot(h_ref[...], wd_bf[...], preferred_element_type=F32) + bd_ref[0]

    @pl.when(jnp.logical_not(used))
    def _():
        o_ref[...] = jnp.zeros_like(o_ref)


def _down(block_e, n_used, hb, w_down, b_down3, *, tm=MOE_TILE, tn=1024):
    n_pad, f = hb.shape
    d = D_MODEL

    def wspec(width):
        return pl.BlockSpec((1, width, tn), lambda k, j, be, nu: (be[_used_block(j, nu)], 0, k))

    return pl.pallas_call(
        _down_kernel,
        grid_spec=pltpu.PrefetchScalarGridSpec(
            num_scalar_prefetch=2,
            grid=(d // tn, n_pad // tm),
            in_specs=[
                pl.BlockSpec((tm, f), lambda k, j, be, nu: (_used_block(j, nu), 0)),
                wspec(f), wspec(1),
            ],
            out_specs=pl.BlockSpec((tm, tn), lambda k, j, be, nu: (j, k)),
            scratch_shapes=[pltpu.VMEM((f, tn), BF16)],
        ),
        out_shape=jax.ShapeDtypeStruct((n_pad, d), F32),
        compiler_params=_params(("arbitrary", "arbitrary")),
        name="down",
    )(block_e, n_used, hb, w_down, b_down3)


def _combine_kernel(dest_ref, y_hbm, gate_ref, x1_ref, ng_ref, o_ref, buf_ref, sem, *, rows):
    def issue(t, carry):
        for k in range(TOP_K):
            slot = dest_ref[t * TOP_K + k]
            pltpu.make_async_copy(y_hbm.at[pl.ds(slot, 1), :], buf_ref.at[k, pl.ds(t, 1), :],
                                  sem).start()
        return carry

    lax.fori_loop(0, rows, issue, 0)
    for k in range(TOP_K):
        pltpu.make_async_copy(y_hbm.at[pl.ds(0, rows), :], buf_ref.at[k], sem).wait()
    acc = x1_ref[...]
    for k in range(TOP_K):
        acc = acc + gate_ref[:, k:k + 1] * buf_ref[k]
    o_ref[...] = acc * lax.rsqrt(jnp.mean(acc * acc, axis=-1, keepdims=True) + RMS_EPS) * ng_ref[...]


def _combine(dest, yb, gates, x1, ng, *, rows=256):
    n, d = x1.shape
    kern = functools.partial(_combine_kernel, rows=rows)
    return pl.pallas_call(
        kern,
        grid=(n // rows,),
        in_specs=[
            pl.BlockSpec((rows * TOP_K,), lambda i: (i,), memory_space=pltpu.SMEM),
            pl.BlockSpec(memory_space=pl.ANY),
            pl.BlockSpec((rows, LANES), lambda i: (i, 0)),
            pl.BlockSpec((rows, d), lambda i: (i, 0)),
            pl.BlockSpec((1, d), lambda i: (0, 0)),
        ],
        out_specs=pl.BlockSpec((rows, d), lambda i: (i, 0)),
        out_shape=jax.ShapeDtypeStruct((n, d), F32),
        scratch_shapes=[pltpu.VMEM((TOP_K, rows, d), F32), pltpu.SemaphoreType.DMA(())],
        compiler_params=_params(("arbitrary",)),
        name="combine",
    )(dest, yb, gates, x1, ng)


def _routing(top_idx, tm):
    n_tok = top_idx.shape[0]
    n_asg = n_tok * TOP_K
    e_flat = top_idx.reshape(n_asg)
    onehot = (e_flat[:, None] == jnp.arange(N_EXPERTS, dtype=jnp.int32)[None, :]).astype(jnp.int32)
    cum = jnp.cumsum(onehot, axis=0)
    rank = jnp.sum(onehot * cum, axis=1) - 1
    counts = cum[-1]
    padded = (counts + tm - 1) // tm * tm
    pend = jnp.cumsum(padded)
    pstart = pend - padded
    dest = (pstart[e_flat] + rank).astype(jnp.int32)
    n_blk = -(-(n_asg + N_EXPERTS * (tm - 1)) // tm)
    n_pad = n_blk * tm
    fill_start = (pstart + counts).astype(jnp.int32)
    fill_len = (padded - counts).astype(jnp.int32)
    block_start = jnp.arange(n_blk, dtype=jnp.int32) * tm
    block_e = jnp.minimum(
        jnp.sum((pend[None, :] <= block_start[:, None]).astype(jnp.int32), axis=1),
        N_EXPERTS - 1).astype(jnp.int32)
    n_used = (pend[-1:] // tm).astype(jnp.int32)
    return dest, fill_start, fill_len, block_e, n_used, n_pad


def _layer(x, norm_mix_g, w_in, conv_w, conv_b, igate_b, fgate_b, mlstm_norm_g, rel_bias,
           w_branch_mlstm, w_branch_attn, w_out, norm_moe_g, router_w, router_b,
           w_gate_up, b_gate_up, w_down, b_down, out_norm_g):
    bsz, s, d = x.shape
    n = bsz * s
    x2 = x.reshape(n, d)
    w2, w1 = 2 * MLSTM_WIDTH, MLSTM_WIDTH
    o_qk, o_v, o_o = 0, w2, w2 + w1
    o_i = o_o + w1
    o_f = o_i + MLSTM_HEADS
    o_qa = o_f + MLSTM_HEADS
    o_ka, o_va = o_qa + ATTN_WIDTH, o_qa + 2 * ATTN_WIDTH
    o_g = o_va + ATTN_WIDTH
    w_main = jnp.concatenate([w_in[:, o_qk:o_i], w_in[:, o_g:]], axis=1).astype(BF16)
    zpad = jnp.zeros((d, LANES - MLSTM_HEADS), w_in.dtype)
    w_if = jnp.concatenate([w_in[:, o_i:o_f], zpad, w_in[:, o_f:o_qa], zpad], axis=1).astype(BF16)
    bpad = jnp.zeros((LANES - MLSTM_HEADS,), F32)
    gate_b = jnp.concatenate([igate_b, bpad, fgate_b, bpad]).reshape(1, IF_WIDTH)

    norm_g = norm_mix_g.reshape(1, d)
    proj, ifg = _in_proj(x2, norm_g, w_main, w_if)
    proj3 = proj.reshape(bsz, s, D_MAIN)
    hm = _mlstm(proj3, ifg.reshape(bsz, s, IF_WIDTH), conv_w, conv_b.reshape(1, -1), gate_b,
                mlstm_norm_g.reshape(1, -1)).reshape(n, MLSTM_WIDTH)

    outs, lses = [], []
    gw = GROUP_WIDTH
    for g, (_, dilation) in enumerate(ATTN_GROUPS):
        w_qkv = jnp.concatenate(
            [w_in[:, o + g * gw:o + (g + 1) * gw] for o in (o_qa, o_ka, o_va)], axis=1).astype(BF16)
        qkv = _attn_proj(x2, norm_g, w_qkv, bsz, s, dilation)
        o_g_, lse_g = _attn_group(qkv, _attn_bias(rel_bias, g, dilation), g)
        outs.append(o_g_)
        lses.append(lse_g)

    rw = jnp.concatenate([router_w, jnp.zeros((d, LANES - N_EXPERTS), F32)], axis=1)
    rb = jnp.concatenate([router_b, jnp.full((LANES - N_EXPERTS,), NEG, F32)]).reshape(1, LANES)
    x1, h2, idx, gates = _merge(
        hm, outs, lses, proj, x2, w_branch_mlstm.astype(BF16), w_branch_attn.astype(BF16),
        w_out.astype(BF16), norm_moe_g.reshape(1, d), rw, rb)

    dest, fill_start, fill_len, block_e, n_used, n_pad = _routing(idx[:, :TOP_K], MOE_TILE)
    xb = _dispatch(fill_start, fill_len, n_used, dest, h2, n_pad)
    hb = _gate_up(block_e, n_used, xb, w_gate_up, b_gate_up.reshape(N_EXPERTS, 1, 2 * D_FF))
    yb = _down(block_e, n_used, hb, w_down, b_down.reshape(N_EXPERTS, 1, d))
    out = _combine(dest, yb, gates, x1, out_norm_g.reshape(1, d))
    return out.reshape(bsz, s, d)


def kernel(x, norm_mix_g, w_in, conv_w, conv_b, igate_b, fgate_b, mlstm_norm_g, rel_bias,
           w_branch_mlstm, w_branch_attn, w_out, norm_moe_g, router_w, router_b,
           w_gate_up, b_gate_up, w_down, b_down, norm_final_g):
    assert w_in.shape[0] == 1, "single-layer block"
    return _layer(x, norm_mix_g[0], w_in[0], conv_w[0], conv_b[0], igate_b[0], fgate_b[0],
                  mlstm_norm_g[0], rel_bias, w_branch_mlstm[0], w_branch_attn[0], w_out[0],
                  norm_moe_g[0], router_w[0], router_b[0], w_gate_up[0], b_gate_up[0],
                  w_down[0], b_down[0], norm_final_g)
```

```python
import functools
import math

import jax
import jax.numpy as jnp
from jax import lax
from jax.experimental import pallas as pl
from jax.experimental.pallas import tpu as pltpu

F32 = jnp.float32
BF16 = jnp.bfloat16

D_MODEL = 2048
MLSTM_HEADS = 8
HEAD_DIM = 128
MLSTM_WIDTH = MLSTM_HEADS * HEAD_DIM
CONV_WIDTH = 4
ATTN_GROUPS = ((128, 1), (512, 4), (2048, 16))
N_GROUPS = 3
HEADS_PER_GROUP = 4
ATTN_HEADS = HEADS_PER_GROUP * N_GROUPS
ATTN_WIDTH = ATTN_HEADS * HEAD_DIM
GROUP_WIDTH = HEADS_PER_GROUP * HEAD_DIM
ATTN_BLOCK = 128
ATTN_SPAN = 128
REL_BUCKETS = 32
REL_MAX_DIST = 2048
N_EXPERTS = 32
TOP_K = 4
D_FF = D_MODEL
SWIGLU_LIMIT = 7.0
SWIGLU_ALPHA = 1.702
RMS_EPS = 1e-6
NEG = -1e30

COL_QK = 0
COL_V = 2 * MLSTM_WIDTH
COL_O = COL_V + MLSTM_WIDTH
COL_GATES = COL_O + MLSTM_WIDTH
D_MAIN = COL_GATES + 2 * D_MODEL
QKV_WIDTH = 3 * GROUP_WIDTH
LANES = 128
IF_WIDTH = 2 * LANES

MLSTM_CHUNK = 128
MOE_TILE = 512
VMEM_LIMIT = 56 * 1024 * 1024


def _sigmoid(x):
    return 0.5 * jnp.tanh(0.5 * x) + 0.5


def _bf16_parts(x, n_parts):
    parts = []
    for _ in range(n_parts):
        p = x.astype(BF16)
        parts.append(p)
        x = x - p.astype(F32)
    return parts


def _log_sigmoid(x):
    return -(jnp.maximum(-x, 0.0) + jnp.log1p(jnp.exp(-jnp.abs(x))))


def _params(sem):
    return pltpu.CompilerParams(dimension_semantics=sem, vmem_limit_bytes=VMEM_LIMIT)


def _in_proj_kernel(x_ref, g_ref, w_ref, wif_ref, o_ref, oif_ref, h_ref, *, bm, rows):
    @pl.when(pl.program_id(1) == 0)
    def _():
        for r in range(0, bm, rows):
            x = x_ref[r:r + rows, :]
            ms = jnp.mean(x * x, axis=-1, keepdims=True)
            h = (x * lax.rsqrt(ms + RMS_EPS) * g_ref[...]).astype(BF16)
            h_ref[r:r + rows, :] = h
            oif_ref[r:r + rows, :] = jnp.dot(h, wif_ref[...], preferred_element_type=F32)

    o_ref[...] = jnp.dot(h_ref[...], w_ref[...], preferred_element_type=F32).astype(BF16)


def _residue_perm(size, dilation, inverse):
    per = size // dilation
    i = lax.broadcasted_iota(jnp.int32, (size, size), 0)
    j = lax.broadcasted_iota(jnp.int32, (size, size), 1)
    if inverse:
        src = (i & (dilation - 1)) * per + (i >> (dilation.bit_length() - 1))
    else:
        src = (i & (per - 1)) * dilation + (i >> (per.bit_length() - 1))
    return j == src


def _attn_proj_kernel(x_ref, g_ref, w_ref, o_ref, h_ref, *, bm, dilation, rows):
    per = rows // dilation
    if dilation > 1:
        perm = _residue_perm(rows, dilation, False).astype(BF16)
    for c in range(0, bm, rows):
        x = x_ref[c:c + rows, :]
        ms = jnp.mean(x * x, axis=-1, keepdims=True)
        h = (x * lax.rsqrt(ms + RMS_EPS) * g_ref[...]).astype(BF16)
        if dilation > 1:
            h = jnp.dot(perm, h, preferred_element_type=F32).astype(BF16)
        h_ref[c:c + rows, :] = h
    res = jnp.dot(h_ref[...], w_ref[...], preferred_element_type=F32).astype(BF16)
    for c in range(bm // rows):
        for r in range(dilation):
            o_ref[0, r, c * per:(c + 1) * per, :] = res[c * rows + r * per:c * rows + (r + 1) * per, :]


def _attn_proj(x2, g, w_qkv, bsz, s, dilation, *, bm=1024):
    n_blk = s // bm
    per = bm // dilation
    kern = functools.partial(_attn_proj_kernel, bm=bm, dilation=dilation, rows=256)
    return pl.pallas_call(
        kern,
        grid=(bsz, n_blk),
        in_specs=[
            pl.BlockSpec((bm, D_MODEL), lambda b, i: (b * n_blk + i, 0)),
            pl.BlockSpec((1, D_MODEL), lambda b, i: (0, 0)),
            pl.BlockSpec((D_MODEL, QKV_WIDTH), lambda b, i: (0, 0),
                         pipeline_mode=pl.Buffered(1)),
        ],
        out_specs=pl.BlockSpec((1, dilation, per, QKV_WIDTH), lambda b, i: (b, 0, i, 0)),
        out_shape=jax.ShapeDtypeStruct((bsz, dilation, s // dilation, QKV_WIDTH), BF16),
        scratch_shapes=[pltpu.VMEM((bm, D_MODEL), BF16)],
        compiler_params=_params(("arbitrary", "arbitrary")),
        name=f"attn_proj_d{dilation}",
    )(x2, g, w_qkv)


def _in_proj(x2, g, w_main, w_if, *, bm=1024, bn=1024):
    n = x2.shape[0]
    bm = min(bm, n)
    kern = functools.partial(_in_proj_kernel, bm=bm, rows=256)
    return pl.pallas_call(
        kern,
        grid=(n // bm, D_MAIN // bn),
        in_specs=[
            pl.BlockSpec((bm, D_MODEL), lambda i, j: (i, 0)),
            pl.BlockSpec((1, D_MODEL), lambda i, j: (0, 0)),
            pl.BlockSpec((D_MODEL, bn), lambda i, j: (0, j)),
            pl.BlockSpec((D_MODEL, IF_WIDTH), lambda i, j: (0, 0)),
        ],
        out_specs=[
            pl.BlockSpec((bm, bn), lambda i, j: (i, j)),
            pl.BlockSpec((bm, IF_WIDTH), lambda i, j: (i, 0)),
        ],
        out_shape=[
            jax.ShapeDtypeStruct((n, D_MAIN), BF16),
            jax.ShapeDtypeStruct((n, IF_WIDTH), F32),
        ],
        scratch_shapes=[pltpu.VMEM((bm, D_MODEL), BF16)],
        compiler_params=_params(("arbitrary", "arbitrary")),
        name="in_proj",
    )(x2, g, w_main, w_if)


def _mlstm_kernel(qk_ref, v_ref, og_ref, if_ref, cw_ref, cb_ref, gb_ref, ng_ref, out_ref,
                  tail_ref, c_ref, m_ref, *, chunk):
    L = chunk
    dh = HEAD_DIM

    @pl.when(pl.program_id(1) == 0)
    def _():
        tail_ref[...] = jnp.zeros_like(tail_ref)
        c_ref[...] = jnp.zeros_like(c_ref)
        m_ref[...] = jnp.zeros_like(m_ref)

    cur = qk_ref[0].astype(F32)
    ext = jnp.concatenate([tail_ref[...], cur], axis=0)
    acc = cur * cw_ref[CONV_WIDTH - 1:CONV_WIDTH, :] + cb_ref[...]
    for s in range(1, CONV_WIDTH):
        acc = acc + ext[8 - s:8 - s + L, :] * cw_ref[CONV_WIDTH - 1 - s:CONV_WIDTH - s, :]
    tail_ref[...] = cur[L - 8:, :]
    qk = acc * _sigmoid(acc)
    q_all = qk[:, :MLSTM_WIDTH].astype(BF16)
    k_all = (qk[:, MLSTM_WIDTH:] * (dh ** -0.5)).astype(BF16)

    gi = if_ref[0][:, :LANES] + gb_ref[:, :LANES]
    lf = _log_sigmoid(if_ref[0][:, LANES:] + gb_ref[:, LANES:])
    row = lax.broadcasted_iota(jnp.int32, (L, L), 0)
    col = lax.broadcasted_iota(jnp.int32, (L, L), 1)
    causal = col <= row
    tri = causal.astype(F32)
    cum = jnp.dot(tri, lf, preferred_element_type=F32, precision=lax.Precision.HIGHEST)
    rmat = gi - cum
    rmat_t = rmat.T
    ones_blk = jnp.ones((L, dh), BF16)

    for h in range(MLSTM_HEADS):
        hs = slice(h * dh, (h + 1) * dh)
        b = cum[:, h:h + 1]
        r_col = rmat[:, h:h + 1]
        r_row = rmat_t[h:h + 1, :]
        m_prev = m_ref[h:h + 1, 0:1]
        logw = jnp.where(causal, b + r_row, NEG)
        inter = b + m_prev
        m_out = jnp.maximum(inter, jnp.max(logw, axis=1, keepdims=True))
        p = jnp.exp(logw - m_out)
        qh = q_all[:, hs]
        kh = k_all[:, hs]
        s = lax.dot_general(qh, kh, (((1,), (1,)), ((), ())), preferred_element_type=F32)
        w = (p * s).astype(BF16)
        vaug = jnp.concatenate([v_ref[0][:, hs], ones_blk], axis=1)
        c_prev = c_ref[h]
        nd = (jnp.dot(w, vaug, preferred_element_type=F32)
              + jnp.exp(inter - m_out)
              * jnp.dot(qh, c_prev.astype(BF16), preferred_element_type=F32))
        hh = nd[:, :dh] / jnp.maximum(jnp.abs(nd[:, dh:]), jnp.exp(-m_out))

        b_last = b[L - 1:L, :]
        m_new = b_last + jnp.maximum(m_prev, jnp.max(r_col, axis=0, keepdims=True))
        ws = jnp.exp(b_last + r_col - m_new)
        decay = jnp.exp(b_last + m_prev - m_new)
        wv = (ws * vaug.astype(F32)).astype(BF16)
        kv = lax.dot_general(kh, wv, (((0,), (0,)), ((), ())), preferred_element_type=F32)
        c_ref[h] = decay * c_prev + kv
        m_ref[h:h + 1, :] = jnp.broadcast_to(m_new, (1, LANES))

        y = hh * lax.rsqrt(jnp.mean(hh * hh, axis=-1, keepdims=True) + RMS_EPS) * ng_ref[:, hs]
        y = y * _sigmoid(og_ref[0][:, hs].astype(F32))
        out_ref[0, :, hs] = y.astype(BF16)


def _mlstm(proj3, ifg3, conv_w, conv_b, gate_b, norm_g, *, chunk=MLSTM_CHUNK):
    bsz, s, _ = proj3.shape
    w = MLSTM_WIDTH
    kern = functools.partial(_mlstm_kernel, chunk=chunk)
    return pl.pallas_call(
        kern,
        grid=(bsz, s // chunk),
        in_specs=[
            pl.BlockSpec((1, chunk, 2 * w), lambda b, c: (b, c, COL_QK // (2 * w))),
            pl.BlockSpec((1, chunk, w), lambda b, c: (b, c, COL_V // w)),
            pl.BlockSpec((1, chunk, w), lambda b, c: (b, c, COL_O // w)),
            pl.BlockSpec((1, chunk, IF_WIDTH), lambda b, c: (b, c, 0)),
            pl.BlockSpec((CONV_WIDTH, 2 * w), lambda b, c: (0, 0)),
            pl.BlockSpec((1, 2 * w), lambda b, c: (0, 0)),
            pl.BlockSpec((1, IF_WIDTH), lambda b, c: (0, 0)),
            pl.BlockSpec((1, w), lambda b, c: (0, 0)),
        ],
        out_specs=pl.BlockSpec((1, chunk, w), lambda b, c: (b, c, 0)),
        out_shape=jax.ShapeDtypeStruct((bsz, s, w), BF16),
        scratch_shapes=[
            pltpu.VMEM((8, 2 * w), F32),
            pltpu.VMEM((MLSTM_HEADS, HEAD_DIM, 2 * HEAD_DIM), F32),
            pltpu.VMEM((MLSTM_HEADS, LANES), F32),
        ],
        compiler_params=_params(("arbitrary", "arbitrary")),
        name="mlstm",
    )(proj3, proj3, proj3, ifg3, conv_w, conv_b, gate_b, norm_g)


def _attn_kernel(q_ref, kp_ref, kc_ref, vp_ref, vc_ref, bias_ref, o_ref, lse_ref):
    qb = ATTN_BLOCK
    dh = HEAD_DIM
    first = pl.program_id(2) == 0
    kcol = lax.broadcasted_iota(jnp.int32, (qb, 2 * qb), 1)
    dead = jnp.logical_and(first, kcol < qb)
    lane = lax.broadcasted_iota(jnp.int32, (qb, LANES), 1)
    lse_all = jnp.zeros((qb, LANES), F32)
    for j in range(HEADS_PER_GROUP):
        hs = slice(j * dh, (j + 1) * dh)
        q = q_ref[0, 0][:, hs]
        k = jnp.concatenate([kp_ref[0, 0][:, hs], kc_ref[0, 0][:, hs]], axis=0)
        v = jnp.concatenate([vp_ref[0, 0][:, hs], vc_ref[0, 0][:, hs]], axis=0)
        s = lax.dot_general(q, k, (((1,), (1,)), ((), ())), preferred_element_type=F32)
        s = s * (dh ** -0.5) + bias_ref[j]
        s = jnp.where(dead, NEG, s)
        mx = jnp.max(s, axis=1, keepdims=True)
        p = jnp.exp(s - mx)
        den = jnp.sum(p, axis=1, keepdims=True)
        o = jnp.dot(p.astype(BF16), v, preferred_element_type=F32) / den
        o_ref[0, 0, :, hs] = o.astype(BF16)
        lse_all = jnp.where(lane == j, mx + jnp.log(den), lse_all)
    lse_ref[0, 0] = lse_all


def _attn_group(qkv, bias, g):
    bsz, dilation, n, _ = qkv.shape
    nb = n // ATTN_BLOCK
    gw = GROUP_WIDTH

    def spec(part, prev):
        def imap(b, r, i):
            return (b, r, jnp.maximum(i - 1, 0) if prev else i, part)

        return pl.BlockSpec((1, 1, ATTN_BLOCK, gw), imap)

    return pl.pallas_call(
        _attn_kernel,
        grid=(bsz, dilation, nb),
        in_specs=[
            spec(0, False),
            spec(1, True), spec(1, False),
            spec(2, True), spec(2, False),
            pl.BlockSpec((HEADS_PER_GROUP, ATTN_BLOCK, 2 * ATTN_BLOCK), lambda b, r, i: (0, 0, 0)),
        ],
        out_specs=[
            pl.BlockSpec((1, 1, ATTN_BLOCK, gw), lambda b, r, i: (b, r, i, 0)),
            pl.BlockSpec((1, 1, ATTN_BLOCK, LANES), lambda b, r, i: (b, r, i, 0)),
        ],
        out_shape=[
            jax.ShapeDtypeStruct((bsz, dilation, n, gw), BF16),
            jax.ShapeDtypeStruct((bsz, dilation, n, LANES), F32),
        ],
        compiler_params=_params(("arbitrary", "arbitrary", "arbitrary")),
        name=f"attn_g{g}",
    )(qkv, qkv, qkv, qkv, qkv, bias)


def _t5_bucket(dist):
    max_exact = REL_BUCKETS // 2
    d_f = jnp.maximum(dist, 1).astype(F32)
    large = max_exact + (jnp.log(d_f / max_exact) / math.log(REL_MAX_DIST / max_exact)
                         * (REL_BUCKETS - max_exact)).astype(jnp.int32)
    large = jnp.minimum(large, REL_BUCKETS - 1)
    return jnp.where(dist < max_exact, dist, large)


def _attn_bias(rel_bias, g, dilation):
    span = ATTN_SPAN
    buckets = _t5_bucket(jnp.arange(span + 1, dtype=jnp.int32) * dilation)
    vec = rel_bias[buckets][:, g * HEADS_PER_GROUP:(g + 1) * HEADS_PER_GROUP].T.astype(F32)
    qpos = jnp.arange(ATTN_BLOCK)[:, None]
    kpos = jnp.arange(2 * ATTN_BLOCK)[None, :]
    dist = qpos + ATTN_BLOCK - kpos
    valid = (dist >= 0) & (dist <= span)
    onehot = (dist[:, :, None] == jnp.arange(span + 1)[None, None, :]).astype(F32)
    table = jnp.einsum('qkj,hj->hqk', onehot, vec, precision=lax.Precision.HIGHEST)
    return jnp.where(valid[None], table, NEG)


def _merge_kernel(hm_ref, o0_ref, o1_ref, o2_ref, l0_ref, l1_ref, l2_ref, gm_ref, ga_ref, x_ref,
                  wbm_ref, wba_ref, wo_ref, ng_ref, rw_ref, rb_ref,
                  x1_ref, h2_ref, idx_ref, gate_ref):
    dh = HEAD_DIM
    bm = x_ref.shape[0]

    def token_order(ref):
        dilation = ref.shape[1]
        if dilation == 1:
            return ref[0, 0]
        perm = _residue_perm(bm, dilation, True).astype(BF16)
        blk = ref[0].reshape(bm, ref.shape[3])
        parts = [blk] if blk.dtype == BF16 else _bf16_parts(blk, 3)
        out = None
        for p in parts:
            moved = jnp.dot(perm, p, preferred_element_type=F32)
            out = moved if out is None else out + moved
        return out

    l0, l1, l2 = token_order(l0_ref), token_order(l1_ref), token_order(l2_ref)
    o0, o1, o2 = token_order(o0_ref), token_order(o1_ref), token_order(o2_ref)
    mx = jnp.maximum(jnp.maximum(l0, l1), l2)
    e0, e1, e2 = jnp.exp(l0 - mx), jnp.exp(l1 - mx), jnp.exp(l2 - mx)
    den = e0 + e1 + e2
    w0, w1, w2 = e0 / den, e1 / den, e2 / den
    parts = []
    for j in range(HEADS_PER_GROUP):
        hs = slice(j * dh, (j + 1) * dh)
        parts.append(w0[:, j:j + 1] * o0[:, hs] + w1[:, j:j + 1] * o1[:, hs]
                     + w2[:, j:j + 1] * o2[:, hs])
    ha = jnp.concatenate(parts, axis=1).astype(BF16)
    ym = jnp.dot(hm_ref[...], wbm_ref[...], preferred_element_type=F32)
    ya = jnp.dot(ha, wba_ref[...], preferred_element_type=F32)
    merged = (_sigmoid(gm_ref[...].astype(F32)) * ym + _sigmoid(ga_ref[...].astype(F32)) * ya)
    x1 = x_ref[...] + jnp.dot(merged.astype(BF16), wo_ref[...], preferred_element_type=F32)
    x1_ref[...] = x1
    h2 = x1 * lax.rsqrt(jnp.mean(x1 * x1, axis=-1, keepdims=True) + RMS_EPS) * ng_ref[...]
    _store_token_tiles(h2_ref, _pack_pairs(h2))
    h_hi, h_lo = _bf16_parts(h2, 2)
    logits = jnp.dot(jnp.concatenate([h_hi, h_lo, h_hi], axis=1), rw_ref[...],
                     preferred_element_type=F32) + rb_ref[...]
    lane = lax.broadcasted_iota(jnp.int32, logits.shape, 1)
    idx_all = jnp.zeros(logits.shape, jnp.int32)
    val_all = jnp.zeros(logits.shape, F32)
    top0 = None
    esum = None
    for k in range(TOP_K):
        m = jnp.max(logits, axis=1, keepdims=True)
        sel = jnp.min(jnp.where(logits == m, lane, LANES), axis=1, keepdims=True)
        if k == 0:
            top0 = m
        e = jnp.exp(m - top0)
        esum = e if k == 0 else esum + e
        idx_all = jnp.where(lane == k, sel, idx_all)
        val_all = jnp.where(lane == k, e, val_all)
        logits = jnp.where(lane == sel, -jnp.inf, logits)
    idx_ref[...] = idx_all
    gate_ref[...] = val_all / esum


def _merge(hm, outs, lses, proj, x2, wbm, wba, wo, ng, rw, rb, *, bm=256):
    n = x2.shape[0]
    d = D_MODEL
    gcol = COL_GATES // d
    n_blk = outs[0].shape[2] // bm

    def rows(width):
        return pl.BlockSpec((bm, width), lambda i: (i, 0))

    def full(a, b):
        return pl.BlockSpec((a, b), lambda i: (0, 0), pipeline_mode=pl.Buffered(1))

    def residue(arr):
        dilation, width = arr.shape[1], arr.shape[3]
        return pl.BlockSpec((1, dilation, bm // dilation, width),
                            lambda i: (i // n_blk, 0, i % n_blk, 0))

    return pl.pallas_call(
        _merge_kernel,
        grid=(n // bm,),
        in_specs=[
            rows(MLSTM_WIDTH),
            residue(outs[0]), residue(outs[1]), residue(outs[2]),
            residue(lses[0]), residue(lses[1]), residue(lses[2]),
            pl.BlockSpec((bm, d), lambda i: (i, gcol)),
            pl.BlockSpec((bm, d), lambda i: (i, gcol + 1)),
            rows(d),
            full(MLSTM_WIDTH, d), full(GROUP_WIDTH, d), full(d, d),
            full(1, d), full(3 * d, LANES), full(1, LANES),
        ],
        out_specs=[rows(d), pl.BlockSpec((bm * PACK_ROWS, LANES), lambda i: (i, 0)),
                   rows(LANES), rows(LANES)],
        out_shape=[
            jax.ShapeDtypeStruct((n, d), F32),
            jax.ShapeDtypeStruct((n * PACK_ROWS, LANES), jnp.uint32),
            jax.ShapeDtypeStruct((n, LANES), jnp.int32),
            jax.ShapeDtypeStruct((n, LANES), F32),
        ],
        compiler_params=_params(("arbitrary",)),
        name="merge",
    )(hm, outs[0], outs[1], outs[2], lses[0], lses[1], lses[2], proj, proj, x2,
      wbm, wba, wo, ng, rw, rb)


PACK_ROWS = 8


def _pack_pairs(x):
    w = x.shape[1] // 2
    lo = lax.bitcast_convert_type(x[:, :w].astype(BF16).astype(F32), jnp.uint32) >> 16
    hi = lax.bitcast_convert_type(x[:, w:].astype(BF16).astype(F32), jnp.uint32)
    return (hi & jnp.uint32(0xFFFF0000)) | lo


def _unpack_pairs(words):
    lo = lax.bitcast_convert_type(words << 16, F32)
    hi = lax.bitcast_convert_type(words & jnp.uint32(0xFFFF0000), F32)
    return lo, hi


def _store_token_tiles(ref, words):
    rows = words.shape[0]
    for s in range(PACK_ROWS):
        ref[pl.ds(s, rows, stride=PACK_ROWS), :] = words[:, s * LANES:(s + 1) * LANES]


def _dispatch_kernel(fs_ref, fl_ref, nu_ref, dest_ref, hp_ref, xb_hbm, zero_ref, sem, zsem, *,
                     tokens, tm, n_blk):
    pr = PACK_ROWS
    fill_sizes = [1 << b for b in reversed(range((tm - 1).bit_length()))]

    def fill_copies(e):
        off = fs_ref[e]
        for p in fill_sizes:
            take = (fl_ref[e] & p) != 0
            dst = xb_hbm.at[pl.ds(pl.multiple_of(off * pr, pr), p * pr), :]
            yield take, pltpu.make_async_copy(zero_ref.at[pl.ds(0, p * pr), :], dst, zsem)
            off = off + jnp.where(take, p, 0)

    def block_copy(b):
        dst = xb_hbm.at[pl.ds(pl.multiple_of(b * (tm * pr), tm * pr), tm * pr), :]
        return pltpu.make_async_copy(zero_ref, dst, zsem)

    @pl.when(pl.program_id(0) == 0)
    def _():
        zero_ref[...] = jnp.zeros_like(zero_ref)
        for start in (True, False):
            def per_expert(e, carry):
                for take, cp in fill_copies(e):
                    @pl.when(take)
                    def _():
                        cp.start() if start else cp.wait()
                return carry

            def per_block(b, carry):
                block_copy(b).start() if start else block_copy(b).wait()
                return carry

            lax.fori_loop(0, N_EXPERTS, per_expert, 0)
            lax.fori_loop(nu_ref[0], n_blk, per_block, 0)

    def issue(t, carry):
        src = hp_ref.at[pl.ds(pl.multiple_of(t * pr, pr), pr), :]
        for k in range(TOP_K):
            slot = dest_ref[t * TOP_K + k]
            pltpu.make_async_copy(src, xb_hbm.at[pl.ds(pl.multiple_of(slot * pr, pr), pr), :],
                                  sem).start()
        return carry

    lax.fori_loop(0, tokens, issue, 0)
    for k in range(TOP_K):
        pltpu.make_async_copy(hp_ref, xb_hbm.at[pl.ds(0, tokens * pr), :], sem).wait()


def _dispatch(fill_start, fill_len, n_used, dest, h2p, n_pad, *, tokens=256, tm=MOE_TILE):
    pr = PACK_ROWS
    n_tok = h2p.shape[0] // pr
    kern = functools.partial(_dispatch_kernel, tokens=tokens, tm=tm, n_blk=n_pad // tm)
    return pl.pallas_call(
        kern,
        grid_spec=pltpu.PrefetchScalarGridSpec(
            num_scalar_prefetch=3,
            grid=(n_tok // tokens,),
            in_specs=[
                pl.BlockSpec((tokens * TOP_K,), lambda i, fs, fl, nu: (i,),
                             memory_space=pltpu.SMEM),
                pl.BlockSpec((tokens * pr, LANES), lambda i, fs, fl, nu: (i, 0)),
            ],
            out_specs=pl.BlockSpec(memory_space=pl.ANY),
            scratch_shapes=[pltpu.VMEM((tm * pr, LANES), jnp.uint32),
                            pltpu.SemaphoreType.DMA(()), pltpu.SemaphoreType.DMA(())],
        ),
        out_shape=jax.ShapeDtypeStruct((n_pad * pr, LANES), jnp.uint32),
        compiler_params=_params(("arbitrary",)),
        name="dispatch",
    )(fill_start, fill_len, n_used, dest, h2p)


def _cast_rows(src_ref, dst_ref, rows):
    total = src_ref.shape[0]

    def body(i, carry):
        r = pl.multiple_of(i * rows, rows)
        dst_ref[pl.ds(r, rows), :] = src_ref[pl.ds(r, rows), :].astype(BF16)
        return carry

    lax.fori_loop(0, total // rows, body, 0)


def _stream_expert_weights(be_ref, nu_ref, ne_ref, copies, convert):
    k, j = pl.program_id(0), pl.program_id(1)
    expert = be_ref[j]
    used = j < nu_ref[0]
    run_start = jnp.logical_or(j == 0, expert != be_ref[jnp.maximum(j - 1, 0)])

    @pl.when(jnp.logical_and(k == 0, j == 0))
    def _():
        for cp in copies(expert, k):
            cp.start()

    @pl.when(jnp.logical_and(used, run_start))
    def _():
        for cp in copies(expert, k):
            cp.wait()
        convert()
        last_run = ne_ref[j] < 0
        nxt_e = jnp.where(last_run, be_ref[0], ne_ref[j])
        nxt_k = jnp.where(last_run, k + 1, k)

        @pl.when(nxt_k < pl.num_programs(0))
        def _():
            for cp in copies(nxt_e, nxt_k):
                cp.start()

    return used


def _gate_up_kernel(be_ref, nu_ref, ne_ref, x_ref, w_hbm, bg_ref, bu_ref, o_ref,
                    stage_ref, w_bf, sem):
    tf = o_ref.shape[1]
    kt = pl.num_programs(0)

    def copies(expert, k):
        return [pltpu.make_async_copy(
            w_hbm.at[expert, :, pl.ds(pl.multiple_of((half * kt + k) * tf, tf), tf)],
            stage_ref.at[half], sem.at[half]) for half in range(2)]

    def convert():
        for half in range(2):
            _cast_rows(stage_ref.at[half], w_bf.at[half], 256)

    used = _stream_expert_weights(be_ref, nu_ref, ne_ref, copies, convert)

    @pl.when(used)
    def _():
        tm = x_ref.shape[0] // PACK_ROWS
        halves = [_unpack_pairs(x_ref[pl.ds(s, tm, stride=PACK_ROWS), :]) for s in range(PACK_ROWS)]
        x = jnp.concatenate([h[0].astype(BF16) for h in halves]
                            + [h[1].astype(BF16) for h in halves], axis=1)
        gate = jnp.dot(x, w_bf[0], preferred_element_type=F32) + bg_ref[0]
        up = jnp.dot(x, w_bf[1], preferred_element_type=F32) + bu_ref[0]
        gate = jnp.minimum(gate, SWIGLU_LIMIT)
        up = jnp.clip(up, -SWIGLU_LIMIT, SWIGLU_LIMIT)
        glu = gate * _sigmoid(SWIGLU_ALPHA * gate)
        o_ref[...] = ((up + 1.0) * glu).astype(BF16)

    @pl.when(jnp.logical_not(used))
    def _():
        o_ref[...] = jnp.zeros_like(o_ref)


def _used_block(j, nu):
    return jnp.minimum(j, nu[0] - 1)


def _gate_up(block_e, n_used, next_e, xb, w_gate_up, b_gate_up3, *, tm=MOE_TILE, tf=1024):
    n_pad, d = xb.shape[0] // PACK_ROWS, D_MODEL
    kt = D_FF // tf

    def bspec(off):
        return pl.BlockSpec((1, 1, tf),
                            lambda k, j, be, nu, ne: (be[_used_block(j, nu)], 0, off + k))

    return pl.pallas_call(
        _gate_up_kernel,
        grid_spec=pltpu.PrefetchScalarGridSpec(
            num_scalar_prefetch=3,
            grid=(kt, n_pad // tm),
            in_specs=[
                pl.BlockSpec((tm * PACK_ROWS, LANES),
                             lambda k, j, be, nu, ne: (_used_block(j, nu), 0)),
                pl.BlockSpec(memory_space=pl.ANY),
                bspec(0), bspec(kt),
            ],
            out_specs=pl.BlockSpec((tm, tf), lambda k, j, be, nu, ne: (j, k)),
            scratch_shapes=[pltpu.VMEM((2, d, tf), F32), pltpu.VMEM((2, d, tf), BF16),
                            pltpu.SemaphoreType.DMA((2,))],
        ),
        out_shape=jax.ShapeDtypeStruct((n_pad, D_FF), BF16),
        compiler_params=_params(("arbitrary", "arbitrary")),
        name="gate_up",
    )(block_e, n_used, next_e, xb, w_gate_up, b_gate_up3, b_gate_up3)


def _down_kernel(be_ref, nu_ref, ne_ref, h_ref, w_hbm, bd_ref, o_ref, stage_ref, w_bf, sem):
    tn = o_ref.shape[1]

    def copies(expert, k):
        return [pltpu.make_async_copy(
            w_hbm.at[expert, :, pl.ds(pl.multiple_of(k * tn, tn), tn)], stage_ref, sem.at[0])]

    def convert():
        _cast_rows(stage_ref, w_bf, 256)

    used = _stream_expert_weights(be_ref, nu_ref, ne_ref, copies, convert)

    @pl.when(used)
    def _():
        o_ref[...] = jnp.dot(h_ref[...], w_bf[...], preferred_element_type=F32) + bd_ref[0]

    @pl.when(jnp.logical_not(used))
    def _():
        o_ref[...] = jnp.zeros_like(o_ref)


def _down(block_e, n_used, next_e, hb, w_down, b_down3, *, tm=MOE_TILE, tn=1024):
    n_pad, f = hb.shape
    d = D_MODEL
    return pl.pallas_call(
        _down_kernel,
        grid_spec=pltpu.PrefetchScalarGridSpec(
            num_scalar_prefetch=3,
            grid=(d // tn, n_pad // tm),
            in_specs=[
                pl.BlockSpec((tm, f), lambda k, j, be, nu, ne: (_used_block(j, nu), 0)),
                pl.BlockSpec(memory_space=pl.ANY),
                pl.BlockSpec((1, 1, tn),
                             lambda k, j, be, nu, ne: (be[_used_block(j, nu)], 0, k)),
            ],
            out_specs=pl.BlockSpec((tm, tn), lambda k, j, be, nu, ne: (j, k)),
            scratch_shapes=[pltpu.VMEM((f, tn), F32), pltpu.VMEM((f, tn), BF16),
                            pltpu.SemaphoreType.DMA((1,))],
        ),
        out_shape=jax.ShapeDtypeStruct((n_pad, d), F32),
        compiler_params=_params(("arbitrary", "arbitrary")),
        name="down",
    )(block_e, n_used, next_e, hb, w_down, b_down3)


def _combine_kernel(dest_ref, y_hbm, gate_ref, x1_ref, ng_ref, o_ref, buf_ref, sem, *, rows):
    def issue(t, carry):
        for k in range(TOP_K):
            slot = dest_ref[t * TOP_K + k]
            pltpu.make_async_copy(y_hbm.at[pl.ds(slot, 1), :], buf_ref.at[k, pl.ds(t, 1), :],
                                  sem).start()
        return carry

    lax.fori_loop(0, rows, issue, 0)
    for k in range(TOP_K):
        pltpu.make_async_copy(y_hbm.at[pl.ds(0, rows), :], buf_ref.at[k], sem).wait()
    acc = x1_ref[...]
    for k in range(TOP_K):
        acc = acc + gate_ref[:, k:k + 1] * buf_ref[k]
    o_ref[...] = acc * lax.rsqrt(jnp.mean(acc * acc, axis=-1, keepdims=True) + RMS_EPS) * ng_ref[...]


def _combine(dest, yb, gates, x1, ng, *, rows=256):
    n, d = x1.shape
    kern = functools.partial(_combine_kernel, rows=rows)
    return pl.pallas_call(
        kern,
        grid=(n // rows,),
        in_specs=[
            pl.BlockSpec((rows * TOP_K,), lambda i: (i,), memory_space=pltpu.SMEM),
            pl.BlockSpec(memory_space=pl.ANY),
            pl.BlockSpec((rows, LANES), lambda i: (i, 0)),
            pl.BlockSpec((rows, d), lambda i: (i, 0)),
            pl.BlockSpec((1, d), lambda i: (0, 0)),
        ],
        out_specs=pl.BlockSpec((rows, d), lambda i: (i, 0)),
        out_shape=jax.ShapeDtypeStruct((n, d), F32),
        scratch_shapes=[pltpu.VMEM((TOP_K, rows, d), F32), pltpu.SemaphoreType.DMA(())],
        compiler_params=_params(("arbitrary",)),
        name="combine",
    )(dest, yb, gates, x1, ng)


def _routing(top_idx, tm):
    n_tok = top_idx.shape[0]
    n_asg = n_tok * TOP_K
    e_flat = top_idx.reshape(n_asg)
    onehot = (e_flat[:, None] == jnp.arange(N_EXPERTS, dtype=jnp.int32)[None, :]).astype(jnp.int32)
    cum = jnp.cumsum(onehot, axis=0)
    rank = jnp.sum(onehot * cum, axis=1) - 1
    counts = cum[-1]
    padded = (counts + tm - 1) // tm * tm
    pend = jnp.cumsum(padded)
    pstart = pend - padded
    dest = (pstart[e_flat] + rank).astype(jnp.int32)
    n_blk = -(-(n_asg + N_EXPERTS * (tm - 1)) // tm)
    n_pad = n_blk * tm
    fill_start = (pstart + counts).astype(jnp.int32)
    fill_len = (padded - counts).astype(jnp.int32)
    block_start = jnp.arange(n_blk, dtype=jnp.int32) * tm
    block_e = jnp.minimum(
        jnp.sum((pend[None, :] <= block_start[:, None]).astype(jnp.int32), axis=1),
        N_EXPERTS - 1).astype(jnp.int32)
    n_used = (pend[-1:] // tm).astype(jnp.int32)
    run_end = pend[block_e] // tm
    next_e = jnp.where(run_end < n_used[0], block_e[jnp.minimum(run_end, n_blk - 1)], -1)
    return dest, fill_start, fill_len, block_e, n_used, next_e.astype(jnp.int32), n_pad


def _layer(x, norm_mix_g, w_in, conv_w, conv_b, igate_b, fgate_b, mlstm_norm_g, rel_bias,
           w_branch_mlstm, w_branch_attn, w_out, norm_moe_g, router_w, router_b,
           w_gate_up, b_gate_up, w_down, b_down, out_norm_g):
    bsz, s, d = x.shape
    n = bsz * s
    x2 = x.reshape(n, d)
    w2, w1 = 2 * MLSTM_WIDTH, MLSTM_WIDTH
    o_qk, o_v, o_o = 0, w2, w2 + w1
    o_i = o_o + w1
    o_f = o_i + MLSTM_HEADS
    o_qa = o_f + MLSTM_HEADS
    o_ka, o_va = o_qa + ATTN_WIDTH, o_qa + 2 * ATTN_WIDTH
    o_g = o_va + ATTN_WIDTH
    w_main = jnp.concatenate([w_in[:, o_qk:o_i], w_in[:, o_g:]], axis=1).astype(BF16)
    zpad = jnp.zeros((d, LANES - MLSTM_HEADS), w_in.dtype)
    w_if = jnp.concatenate([w_in[:, o_i:o_f], zpad, w_in[:, o_f:o_qa], zpad], axis=1).astype(BF16)
    bpad = jnp.zeros((LANES - MLSTM_HEADS,), F32)
    gate_b = jnp.concatenate([igate_b, bpad, fgate_b, bpad]).reshape(1, IF_WIDTH)

    norm_g = norm_mix_g.reshape(1, d)
    proj, ifg = _in_proj(x2, norm_g, w_main, w_if)
    proj3 = proj.reshape(bsz, s, D_MAIN)
    hm = _mlstm(proj3, ifg.reshape(bsz, s, IF_WIDTH), conv_w, conv_b.reshape(1, -1), gate_b,
                mlstm_norm_g.reshape(1, -1)).reshape(n, MLSTM_WIDTH)

    outs, lses = [], []
    gw = GROUP_WIDTH
    for g, (_, dilation) in enumerate(ATTN_GROUPS):
        w_qkv = jnp.concatenate(
            [w_in[:, o + g * gw:o + (g + 1) * gw] for o in (o_qa, o_ka, o_va)], axis=1).astype(BF16)
        qkv = _attn_proj(x2, norm_g, w_qkv, bsz, s, dilation)
        o_g_, lse_g = _attn_group(qkv, _attn_bias(rel_bias, g, dilation), g)
        outs.append(o_g_)
        lses.append(lse_g)

    rw = jnp.concatenate([router_w, jnp.zeros((d, LANES - N_EXPERTS), F32)], axis=1)
    rw_hi = rw.astype(BF16)
    rw_lo = (rw - rw_hi.astype(F32)).astype(BF16)
    rw = jnp.concatenate([rw_hi, rw_hi, rw_lo], axis=0)
    rb = jnp.concatenate([router_b, jnp.full((LANES - N_EXPERTS,), NEG, F32)]).reshape(1, LANES)
    x1, h2, idx, gates = _merge(
        hm, outs, lses, proj, x2, w_branch_mlstm.astype(BF16), w_branch_attn.astype(BF16),
        w_out.astype(BF16), norm_moe_g.reshape(1, d), rw, rb)

    dest, fill_start, fill_len, block_e, n_used, next_e, n_pad = _routing(idx[:, :TOP_K], MOE_TILE)
    xb = _dispatch(fill_start, fill_len, n_used, dest, h2, n_pad)
    hb = _gate_up(block_e, n_used, next_e, xb, w_gate_up,
                  b_gate_up.reshape(N_EXPERTS, 1, 2 * D_FF))
    yb = _down(block_e, n_used, next_e, hb, w_down, b_down.reshape(N_EXPERTS, 1, d))
    out = _combine(dest, yb, gates, x1, out_norm_g.reshape(1, d))
    return out.reshape(bsz, s, d)


def kernel(x, norm_mix_g, w_in, conv_w, conv_b, igate_b, fgate_b, mlstm_norm_g, rel_bias,
           w_branch_mlstm, w_branch_attn, w_out, norm_moe_g, router_w, router_b,
           w_gate_up, b_gate_up, w_down, b_down, norm_final_g):
    assert w_in.shape[0] == 1, "single-layer block"
    return _layer(x, norm_mix_g[0], w_in[0], conv_w[0], conv_b[0], igate_b[0], fgate_b[0],
                  mlstm_norm_g[0], rel_bias, w_branch_mlstm[0], w_branch_attn[0], w_out[0],
                  norm_moe_g[0], router_w[0], router_b[0], w_gate_up[0], b_gate_up[0],
                  w_down[0], b_down[0], norm_final_g)
```

```python
import functools
import math

import jax
import jax.numpy as jnp
from jax import lax
from jax.experimental import pallas as pl
from jax.experimental.pallas import tpu as pltpu

F32 = jnp.float32
BF16 = jnp.bfloat16

D_MODEL = 2048
MLSTM_HEADS = 8
HEAD_DIM = 128
MLSTM_WIDTH = MLSTM_HEADS * HEAD_DIM
CONV_WIDTH = 4
ATTN_GROUPS = ((128, 1), (512, 4), (2048, 16))
N_GROUPS = 3
HEADS_PER_GROUP = 4
ATTN_HEADS = HEADS_PER_GROUP * N_GROUPS
ATTN_WIDTH = ATTN_HEADS * HEAD_DIM
GROUP_WIDTH = HEADS_PER_GROUP * HEAD_DIM
ATTN_BLOCK = 128
ATTN_SPAN = 128
REL_BUCKETS = 32
REL_MAX_DIST = 2048
N_EXPERTS = 32
TOP_K = 4
D_FF = D_MODEL
SWIGLU_LIMIT = 7.0
SWIGLU_ALPHA = 1.702
RMS_EPS = 1e-6
NEG = -1e30

COL_QK = 0
COL_V = 2 * MLSTM_WIDTH
COL_O = COL_V + MLSTM_WIDTH
COL_GATES = COL_O + MLSTM_WIDTH
D_MAIN = COL_GATES + 2 * D_MODEL
QKV_WIDTH = 3 * GROUP_WIDTH
LANES = 128
IF_WIDTH = 2 * LANES

MLSTM_CHUNK = 128
MOE_TILE = 512
VMEM_LIMIT = 56 * 1024 * 1024


def _sigmoid(x):
    return 0.5 * jnp.tanh(0.5 * x) + 0.5


def _bf16_parts(x, n_parts):
    parts = []
    for _ in range(n_parts):
        p = x.astype(BF16)
        parts.append(p)
        x = x - p.astype(F32)
    return parts


def _log_sigmoid(x):
    return -(jnp.maximum(-x, 0.0) + jnp.log1p(jnp.exp(-jnp.abs(x))))


def _params(sem):
    return pltpu.CompilerParams(dimension_semantics=sem, vmem_limit_bytes=VMEM_LIMIT)


def _in_proj_kernel(x_ref, g_ref, w_ref, wif_ref, o_ref, oif_ref, h_ref, *, bm, rows):
    @pl.when(pl.program_id(1) == 0)
    def _():
        for r in range(0, bm, rows):
            x = x_ref[r:r + rows, :]
            ms = jnp.mean(x * x, axis=-1, keepdims=True)
            h = (x * lax.rsqrt(ms + RMS_EPS) * g_ref[...]).astype(BF16)
            h_ref[r:r + rows, :] = h
            oif_ref[r:r + rows, :] = jnp.dot(h, wif_ref[...], preferred_element_type=F32)

    o_ref[...] = jnp.dot(h_ref[...], w_ref[...], preferred_element_type=F32).astype(BF16)


def _residue_perm(size, dilation, inverse):
    per = size // dilation
    i = lax.broadcasted_iota(jnp.int32, (size, size), 0)
    j = lax.broadcasted_iota(jnp.int32, (size, size), 1)
    if inverse:
        src = (i & (dilation - 1)) * per + (i >> (dilation.bit_length() - 1))
    else:
        src = (i & (per - 1)) * dilation + (i >> (per.bit_length() - 1))
    return j == src


def _attn_proj_kernel(x_ref, g_ref, w_ref, o_ref, h_ref, *, bm, dilation, rows):
    per = rows // dilation
    if dilation > 1:
        perm = _residue_perm(rows, dilation, False).astype(BF16)
    for c in range(0, bm, rows):
        x = x_ref[c:c + rows, :]
        ms = jnp.mean(x * x, axis=-1, keepdims=True)
        h = (x * lax.rsqrt(ms + RMS_EPS) * g_ref[...]).astype(BF16)
        if dilation > 1:
            h = jnp.dot(perm, h, preferred_element_type=F32).astype(BF16)
        h_ref[c:c + rows, :] = h
    res = jnp.dot(h_ref[...], w_ref[...], preferred_element_type=F32).astype(BF16)
    for c in range(bm // rows):
        for r in range(dilation):
            o_ref[0, r, c * per:(c + 1) * per, :] = res[c * rows + r * per:c * rows + (r + 1) * per, :]


def _attn_proj(x2, g, w_qkv, bsz, s, dilation, *, bm=1024):
    n_blk = s // bm
    per = bm // dilation
    kern = functools.partial(_attn_proj_kernel, bm=bm, dilation=dilation, rows=256)
    return pl.pallas_call(
        kern,
        grid=(bsz, n_blk),
        in_specs=[
            pl.BlockSpec((bm, D_MODEL), lambda b, i: (b * n_blk + i, 0)),
            pl.BlockSpec((1, D_MODEL), lambda b, i: (0, 0)),
            pl.BlockSpec((D_MODEL, QKV_WIDTH), lambda b, i: (0, 0),
                         pipeline_mode=pl.Buffered(1)),
        ],
        out_specs=pl.BlockSpec((1, dilation, per, QKV_WIDTH), lambda b, i: (b, 0, i, 0)),
        out_shape=jax.ShapeDtypeStruct((bsz, dilation, s // dilation, QKV_WIDTH), BF16),
        scratch_shapes=[pltpu.VMEM((bm, D_MODEL), BF16)],
        compiler_params=_params(("arbitrary", "arbitrary")),
        name=f"attn_proj_d{dilation}",
    )(x2, g, w_qkv)


def _in_proj(x2, g, w_main, w_if, *, bm=1024, bn=1024):
    n = x2.shape[0]
    bm = min(bm, n)
    kern = functools.partial(_in_proj_kernel, bm=bm, rows=256)
    return pl.pallas_call(
        kern,
        grid=(n // bm, D_MAIN // bn),
        in_specs=[
            pl.BlockSpec((bm, D_MODEL), lambda i, j: (i, 0)),
            pl.BlockSpec((1, D_MODEL), lambda i, j: (0, 0)),
            pl.BlockSpec((D_MODEL, bn), lambda i, j: (0, j)),
            pl.BlockSpec((D_MODEL, IF_WIDTH), lambda i, j: (0, 0)),
        ],
        out_specs=[
            pl.BlockSpec((bm, bn), lambda i, j: (i, j)),
            pl.BlockSpec((bm, IF_WIDTH), lambda i, j: (i, 0)),
        ],
        out_shape=[
            jax.ShapeDtypeStruct((n, D_MAIN), BF16),
            jax.ShapeDtypeStruct((n, IF_WIDTH), F32),
        ],
        scratch_shapes=[pltpu.VMEM((bm, D_MODEL), BF16)],
        compiler_params=_params(("arbitrary", "arbitrary")),
        name="in_proj",
    )(x2, g, w_main, w_if)


def _mlstm_kernel(qk_ref, v_ref, og_ref, if_ref, cw_ref, cb_ref, gb_ref, ng_ref, out_ref,
                  tail_ref, c_ref, m_ref, *, chunk):
    L = chunk
    dh = HEAD_DIM

    @pl.when(pl.program_id(1) == 0)
    def _():
        tail_ref[...] = jnp.zeros_like(tail_ref)
        c_ref[...] = jnp.zeros_like(c_ref)
        m_ref[...] = jnp.zeros_like(m_ref)

    cur = qk_ref[0].astype(F32)
    ext = jnp.concatenate([tail_ref[...], cur], axis=0)
    acc = cur * cw_ref[CONV_WIDTH - 1:CONV_WIDTH, :] + cb_ref[...]
    for s in range(1, CONV_WIDTH):
        acc = acc + ext[8 - s:8 - s + L, :] * cw_ref[CONV_WIDTH - 1 - s:CONV_WIDTH - s, :]
    tail_ref[...] = cur[L - 8:, :]
    qk = acc * _sigmoid(acc)
    q_all = qk[:, :MLSTM_WIDTH].astype(BF16)
    k_all = (qk[:, MLSTM_WIDTH:] * (dh ** -0.5)).astype(BF16)

    gi = if_ref[0][:, :LANES] + gb_ref[:, :LANES]
    lf = _log_sigmoid(if_ref[0][:, LANES:] + gb_ref[:, LANES:])
    row = lax.broadcasted_iota(jnp.int32, (L, L), 0)
    col = lax.broadcasted_iota(jnp.int32, (L, L), 1)
    causal = col <= row
    tri = causal.astype(F32)
    cum = jnp.dot(tri, lf, preferred_element_type=F32, precision=lax.Precision.HIGHEST)
    rmat = gi - cum
    rmat_t = rmat.T
    ones_blk = jnp.ones((L, dh), BF16)

    for h in range(MLSTM_HEADS):
        hs = slice(h * dh, (h + 1) * dh)
        b = cum[:, h:h + 1]
        r_col = rmat[:, h:h + 1]
        r_row = rmat_t[h:h + 1, :]
        m_prev = m_ref[h:h + 1, 0:1]
        logw = jnp.where(causal, b + r_row, NEG)
        inter = b + m_prev
        m_out = jnp.maximum(inter, jnp.max(logw, axis=1, keepdims=True))
        p = jnp.exp(logw - m_out)
        qh = q_all[:, hs]
        kh = k_all[:, hs]
        s = lax.dot_general(qh, kh, (((1,), (1,)), ((), ())), preferred_element_type=F32)
        w = (p * s).astype(BF16)
        vaug = jnp.concatenate([v_ref[0][:, hs], ones_blk], axis=1)
        c_prev = c_ref[h]
        nd = (jnp.dot(w, vaug, preferred_element_type=F32)
              + jnp.exp(inter - m_out)
              * jnp.dot(qh, c_prev.astype(BF16), preferred_element_type=F32))
        hh = nd[:, :dh] / jnp.maximum(jnp.abs(nd[:, dh:]), jnp.exp(-m_out))

        b_last = b[L - 1:L, :]
        m_new = b_last + jnp.maximum(m_prev, jnp.max(r_col, axis=0, keepdims=True))
        ws = jnp.exp(b_last + r_col - m_new)
        decay = jnp.exp(b_last + m_prev - m_new)
        wv = (ws * vaug.astype(F32)).astype(BF16)
        kv = lax.dot_general(kh, wv, (((0,), (0,)), ((), ())), preferred_element_type=F32)
        c_ref[h] = decay * c_prev + kv
        m_ref[h:h + 1, :] = jnp.broadcast_to(m_new, (1, LANES))

        y = hh * lax.rsqrt(jnp.mean(hh * hh, axis=-1, keepdims=True) + RMS_EPS) * ng_ref[:, hs]
        y = y * _sigmoid(og_ref[0][:, hs].astype(F32))
        out_ref[0, :, hs] = y.astype(BF16)


def _mlstm(proj3, ifg3, conv_w, conv_b, gate_b, norm_g, *, chunk=MLSTM_CHUNK):
    bsz, s, _ = proj3.shape
    w = MLSTM_WIDTH
    kern = functools.partial(_mlstm_kernel, chunk=chunk)
    return pl.pallas_call(
        kern,
        grid=(bsz, s // chunk),
        in_specs=[
            pl.BlockSpec((1, chunk, 2 * w), lambda b, c: (b, c, COL_QK // (2 * w))),
            pl.BlockSpec((1, chunk, w), lambda b, c: (b, c, COL_V // w)),
            pl.BlockSpec((1, chunk, w), lambda b, c: (b, c, COL_O // w)),
            pl.BlockSpec((1, chunk, IF_WIDTH), lambda b, c: (b, c, 0)),
            pl.BlockSpec((CONV_WIDTH, 2 * w), lambda b, c: (0, 0)),
            pl.BlockSpec((1, 2 * w), lambda b, c: (0, 0)),
            pl.BlockSpec((1, IF_WIDTH), lambda b, c: (0, 0)),
            pl.BlockSpec((1, w), lambda b, c: (0, 0)),
        ],
        out_specs=pl.BlockSpec((1, chunk, w), lambda b, c: (b, c, 0)),
        out_shape=jax.ShapeDtypeStruct((bsz, s, w), BF16),
        scratch_shapes=[
            pltpu.VMEM((8, 2 * w), F32),
            pltpu.VMEM((MLSTM_HEADS, HEAD_DIM, 2 * HEAD_DIM), F32),
            pltpu.VMEM((MLSTM_HEADS, LANES), F32),
        ],
        compiler_params=_params(("arbitrary", "arbitrary")),
        name="mlstm",
    )(proj3, proj3, proj3, ifg3, conv_w, conv_b, gate_b, norm_g)


def _attn_kernel(q_ref, kp_ref, kc_ref, vp_ref, vc_ref, bias_ref, o_ref, lse_ref):
    qb = ATTN_BLOCK
    dh = HEAD_DIM
    first = pl.program_id(2) == 0
    kcol = lax.broadcasted_iota(jnp.int32, (qb, 2 * qb), 1)
    dead = jnp.logical_and(first, kcol < qb)
    lane = lax.broadcasted_iota(jnp.int32, (qb, LANES), 1)
    lse_all = jnp.zeros((qb, LANES), F32)
    for j in range(HEADS_PER_GROUP):
        hs = slice(j * dh, (j + 1) * dh)
        q = q_ref[0, 0][:, hs]
        k = jnp.concatenate([kp_ref[0, 0][:, hs], kc_ref[0, 0][:, hs]], axis=0)
        v = jnp.concatenate([vp_ref[0, 0][:, hs], vc_ref[0, 0][:, hs]], axis=0)
        s = lax.dot_general(q, k, (((1,), (1,)), ((), ())), preferred_element_type=F32)
        s = s * (dh ** -0.5) + bias_ref[j]
        s = jnp.where(dead, NEG, s)
        mx = jnp.max(s, axis=1, keepdims=True)
        p = jnp.exp(s - mx)
        den = jnp.sum(p, axis=1, keepdims=True)
        o = jnp.dot(p.astype(BF16), v, preferred_element_type=F32) / den
        o_ref[0, 0, :, hs] = o.astype(BF16)
        lse_all = jnp.where(lane == j, mx + jnp.log(den), lse_all)
    lse_ref[0, 0] = lse_all


def _attn_group(qkv, bias, g):
    bsz, dilation, n, _ = qkv.shape
    nb = n // ATTN_BLOCK
    gw = GROUP_WIDTH

    def spec(part, prev):
        def imap(b, r, i):
            return (b, r, jnp.maximum(i - 1, 0) if prev else i, part)

        return pl.BlockSpec((1, 1, ATTN_BLOCK, gw), imap)

    return pl.pallas_call(
        _attn_kernel,
        grid=(bsz, dilation, nb),
        in_specs=[
            spec(0, False),
            spec(1, True), spec(1, False),
            spec(2, True), spec(2, False),
            pl.BlockSpec((HEADS_PER_GROUP, ATTN_BLOCK, 2 * ATTN_BLOCK), lambda b, r, i: (0, 0, 0)),
        ],
        out_specs=[
            pl.BlockSpec((1, 1, ATTN_BLOCK, gw), lambda b, r, i: (b, r, i, 0)),
            pl.BlockSpec((1, 1, ATTN_BLOCK, LANES), lambda b, r, i: (b, r, i, 0)),
        ],
        out_shape=[
            jax.ShapeDtypeStruct((bsz, dilation, n, gw), BF16),
            jax.ShapeDtypeStruct((bsz, dilation, n, LANES), F32),
        ],
        compiler_params=_params(("arbitrary", "arbitrary", "arbitrary")),
        name=f"attn_g{g}",
    )(qkv, qkv, qkv, qkv, qkv, bias)


def _t5_bucket(dist):
    max_exact = REL_BUCKETS // 2
    d_f = jnp.maximum(dist, 1).astype(F32)
    large = max_exact + (jnp.log(d_f / max_exact) / math.log(REL_MAX_DIST / max_exact)
                         * (REL_BUCKETS - max_exact)).astype(jnp.int32)
    large = jnp.minimum(large, REL_BUCKETS - 1)
    return jnp.where(dist < max_exact, dist, large)


def _attn_bias(rel_bias, g, dilation):
    span = ATTN_SPAN
    buckets = _t5_bucket(jnp.arange(span + 1, dtype=jnp.int32) * dilation)
    vec = rel_bias[buckets][:, g * HEADS_PER_GROUP:(g + 1) * HEADS_PER_GROUP].T.astype(F32)
    qpos = jnp.arange(ATTN_BLOCK)[:, None]
    kpos = jnp.arange(2 * ATTN_BLOCK)[None, :]
    dist = qpos + ATTN_BLOCK - kpos
    valid = (dist >= 0) & (dist <= span)
    onehot = (dist[:, :, None] == jnp.arange(span + 1)[None, None, :]).astype(F32)
    table = jnp.einsum('qkj,hj->hqk', onehot, vec, precision=lax.Precision.HIGHEST)
    return jnp.where(valid[None], table, NEG)


def _merge_kernel(hm_ref, o0_ref, o1_ref, o2_ref, l0_ref, l1_ref, l2_ref, gm_ref, ga_ref, x_ref,
                  wbm_ref, wba_ref, wo_ref, ng_ref, rw_ref, rb_ref,
                  x1_ref, h2_ref, idx_ref, gate_ref):
    dh = HEAD_DIM
    bm = x_ref.shape[0]

    def token_order(ref):
        dilation = ref.shape[1]
        if dilation == 1:
            return ref[0, 0]
        perm = _residue_perm(bm, dilation, True).astype(BF16)
        blk = ref[0].reshape(bm, ref.shape[3])
        parts = [blk] if blk.dtype == BF16 else _bf16_parts(blk, 3)
        out = None
        for p in parts:
            moved = jnp.dot(perm, p, preferred_element_type=F32)
            out = moved if out is None else out + moved
        return out

    l0, l1, l2 = token_order(l0_ref), token_order(l1_ref), token_order(l2_ref)
    o0, o1, o2 = token_order(o0_ref), token_order(o1_ref), token_order(o2_ref)
    mx = jnp.maximum(jnp.maximum(l0, l1), l2)
    e0, e1, e2 = jnp.exp(l0 - mx), jnp.exp(l1 - mx), jnp.exp(l2 - mx)
    den = e0 + e1 + e2
    w0, w1, w2 = e0 / den, e1 / den, e2 / den
    parts = []
    for j in range(HEADS_PER_GROUP):
        hs = slice(j * dh, (j + 1) * dh)
        parts.append(w0[:, j:j + 1] * o0[:, hs] + w1[:, j:j + 1] * o1[:, hs]
                     + w2[:, j:j + 1] * o2[:, hs])
    ha = jnp.concatenate(parts, axis=1).astype(BF16)
    ym = jnp.dot(hm_ref[...], wbm_ref[...], preferred_element_type=F32)
    ya = jnp.dot(ha, wba_ref[...], preferred_element_type=F32)
    merged = (_sigmoid(gm_ref[...].astype(F32)) * ym + _sigmoid(ga_ref[...].astype(F32)) * ya)
    x1 = x_ref[...] + jnp.dot(merged.astype(BF16), wo_ref[...], preferred_element_type=F32)
    x1_ref[...] = x1
    h2 = x1 * lax.rsqrt(jnp.mean(x1 * x1, axis=-1, keepdims=True) + RMS_EPS) * ng_ref[...]
    _store_token_tiles(h2_ref, _pack_pairs(h2))
    h_hi, h_lo = _bf16_parts(h2, 2)
    logits = jnp.dot(jnp.concatenate([h_hi, h_lo, h_hi], axis=1), rw_ref[...],
                     preferred_element_type=F32) + rb_ref[...]
    lane = lax.broadcasted_iota(jnp.int32, logits.shape, 1)
    idx_all = jnp.zeros(logits.shape, jnp.int32)
    val_all = jnp.zeros(logits.shape, F32)
    top0 = None
    esum = None
    for k in range(TOP_K):
        m = jnp.max(logits, axis=1, keepdims=True)
        sel = jnp.min(jnp.where(logits == m, lane, LANES), axis=1, keepdims=True)
        if k == 0:
            top0 = m
        e = jnp.exp(m - top0)
        esum = e if k == 0 else esum + e
        idx_all = jnp.where(lane == k, sel, idx_all)
        val_all = jnp.where(lane == k, e, val_all)
        logits = jnp.where(lane == sel, -jnp.inf, logits)
    idx_ref[...] = idx_all
    gate_ref[...] = val_all / esum


def _merge(hm, outs, lses, proj, x2, wbm, wba, wo, ng, rw, rb, *, bm=256):
    n = x2.shape[0]
    d = D_MODEL
    gcol = COL_GATES // d
    n_blk = outs[0].shape[2] // bm

    def rows(width):
        return pl.BlockSpec((bm, width), lambda i: (i, 0))

    def full(a, b):
        return pl.BlockSpec((a, b), lambda i: (0, 0), pipeline_mode=pl.Buffered(1))

    def residue(arr):
        dilation, width = arr.shape[1], arr.shape[3]
        return pl.BlockSpec((1, dilation, bm // dilation, width),
                            lambda i: (i // n_blk, 0, i % n_blk, 0))

    return pl.pallas_call(
        _merge_kernel,
        grid=(n // bm,),
        in_specs=[
            rows(MLSTM_WIDTH),
            residue(outs[0]), residue(outs[1]), residue(outs[2]),
            residue(lses[0]), residue(lses[1]), residue(lses[2]),
            pl.BlockSpec((bm, d), lambda i: (i, gcol)),
            pl.BlockSpec((bm, d), lambda i: (i, gcol + 1)),
            rows(d),
            full(MLSTM_WIDTH, d), full(GROUP_WIDTH, d), full(d, d),
            full(1, d), full(3 * d, LANES), full(1, LANES),
        ],
        out_specs=[rows(d), pl.BlockSpec((bm * PACK_ROWS, LANES), lambda i: (i, 0)),
                   rows(LANES), rows(LANES)],
        out_shape=[
            jax.ShapeDtypeStruct((n, d), F32),
            jax.ShapeDtypeStruct((n * PACK_ROWS, LANES), jnp.uint32),
            jax.ShapeDtypeStruct((n, LANES), jnp.int32),
            jax.ShapeDtypeStruct((n, LANES), F32),
        ],
        compiler_params=_params(("arbitrary",)),
        name="merge",
    )(hm, outs[0], outs[1], outs[2], lses[0], lses[1], lses[2], proj, proj, x2,
      wbm, wba, wo, ng, rw, rb)


PACK_ROWS = 8


def _pack_pairs(x):
    w = x.shape[1] // 2
    lo = lax.bitcast_convert_type(x[:, :w].astype(BF16).astype(F32), jnp.uint32) >> 16
    hi = lax.bitcast_convert_type(x[:, w:].astype(BF16).astype(F32), jnp.uint32)
    return (hi & jnp.uint32(0xFFFF0000)) | lo


def _unpack_pairs(words):
    lo = lax.bitcast_convert_type(words << 16, F32)
    hi = lax.bitcast_convert_type(words & jnp.uint32(0xFFFF0000), F32)
    return lo, hi


def _store_token_tiles(ref, words):
    rows = words.shape[0]
    for s in range(PACK_ROWS):
        ref[pl.ds(s, rows, stride=PACK_ROWS), :] = words[:, s * LANES:(s + 1) * LANES]


def _dispatch_kernel(fs_ref, fl_ref, nu_ref, dest_ref, hp_ref, xb_hbm, zero_ref, sem, zsem, *,
                     tokens, tm, n_blk):
    pr = PACK_ROWS
    fill_sizes = [1 << b for b in reversed(range((tm - 1).bit_length()))]

    def fill_copies(e):
        off = fs_ref[e]
        for p in fill_sizes:
            take = (fl_ref[e] & p) != 0
            dst = xb_hbm.at[pl.ds(pl.multiple_of(off * pr, pr), p * pr), :]
            yield take, pltpu.make_async_copy(zero_ref.at[pl.ds(0, p * pr), :], dst, zsem)
            off = off + jnp.where(take, p, 0)

    def block_copy(b):
        dst = xb_hbm.at[pl.ds(pl.multiple_of(b * (tm * pr), tm * pr), tm * pr), :]
        return pltpu.make_async_copy(zero_ref, dst, zsem)

    @pl.when(pl.program_id(0) == 0)
    def _():
        zero_ref[...] = jnp.zeros_like(zero_ref)
        for start in (True, False):
            def per_expert(e, carry):
                for take, cp in fill_copies(e):
                    @pl.when(take)
                    def _():
                        cp.start() if start else cp.wait()
                return carry

            def per_block(b, carry):
                block_copy(b).start() if start else block_copy(b).wait()
                return carry

            lax.fori_loop(0, N_EXPERTS, per_expert, 0)
            lax.fori_loop(nu_ref[0], n_blk, per_block, 0)

    def issue(t, carry):
        src = hp_ref.at[pl.ds(pl.multiple_of(t * pr, pr), pr), :]
        for k in range(TOP_K):
            slot = dest_ref[t * TOP_K + k]
            pltpu.make_async_copy(src, xb_hbm.at[pl.ds(pl.multiple_of(slot * pr, pr), pr), :],
                                  sem).start()
        return carry

    lax.fori_loop(0, tokens, issue, 0)
    for k in range(TOP_K):
        pltpu.make_async_copy(hp_ref, xb_hbm.at[pl.ds(0, tokens * pr), :], sem).wait()


def _dispatch(fill_start, fill_len, n_used, dest, h2p, n_pad, *, tokens=256, tm=MOE_TILE):
    pr = PACK_ROWS
    n_tok = h2p.shape[0] // pr
    kern = functools.partial(_dispatch_kernel, tokens=tokens, tm=tm, n_blk=n_pad // tm)
    return pl.pallas_call(
        kern,
        grid_spec=pltpu.PrefetchScalarGridSpec(
            num_scalar_prefetch=3,
            grid=(n_tok // tokens,),
            in_specs=[
                pl.BlockSpec((tokens * TOP_K,), lambda i, fs, fl, nu: (i,),
                             memory_space=pltpu.SMEM),
                pl.BlockSpec((tokens * pr, LANES), lambda i, fs, fl, nu: (i, 0)),
            ],
            out_specs=pl.BlockSpec(memory_space=pl.ANY),
            scratch_shapes=[pltpu.VMEM((tm * pr, LANES), jnp.uint32),
                            pltpu.SemaphoreType.DMA(()), pltpu.SemaphoreType.DMA(())],
        ),
        out_shape=jax.ShapeDtypeStruct((n_pad * pr, LANES), jnp.uint32),
        compiler_params=_params(("arbitrary",)),
        name="dispatch",
    )(fill_start, fill_len, n_used, dest, h2p)


def _cast_rows(src_ref, dst_ref, rows):
    total = src_ref.shape[0]

    def body(i, carry):
        r = pl.multiple_of(i * rows, rows)
        dst_ref[pl.ds(r, rows), :] = src_ref[pl.ds(r, rows), :].astype(BF16)
        return carry

    lax.fori_loop(0, total // rows, body, 0)


def _stream_expert_weights(be_ref, nu_ref, ne_ref, copies, convert):
    k, j = pl.program_id(0), pl.program_id(1)
    expert = be_ref[j]
    used = j < nu_ref[0]
    run_start = jnp.logical_or(j == 0, expert != be_ref[jnp.maximum(j - 1, 0)])

    @pl.when(jnp.logical_and(k == 0, j == 0))
    def _():
        for cp in copies(expert, k):
            cp.start()

    @pl.when(jnp.logical_and(used, run_start))
    def _():
        for cp in copies(expert, k):
            cp.wait()
        convert()
        last_run = ne_ref[j] < 0
        nxt_e = jnp.where(last_run, be_ref[0], ne_ref[j])
        nxt_k = jnp.where(last_run, k + 1, k)

        @pl.when(nxt_k < pl.num_programs(0))
        def _():
            for cp in copies(nxt_e, nxt_k):
                cp.start()

    return used


def _gate_up_kernel(be_ref, nu_ref, ne_ref, x_ref, w_hbm, bg_ref, bu_ref, o_ref,
                    stage_ref, w_bf, sem):
    tf = o_ref.shape[1]
    kt = pl.num_programs(0)

    def copies(expert, k):
        return [pltpu.make_async_copy(
            w_hbm.at[expert, :, pl.ds(pl.multiple_of((half * kt + k) * tf, tf), tf)],
            stage_ref.at[half], sem.at[half]) for half in range(2)]

    def convert():
        for half in range(2):
            _cast_rows(stage_ref.at[half], w_bf.at[half], 256)

    used = _stream_expert_weights(be_ref, nu_ref, ne_ref, copies, convert)

    @pl.when(used)
    def _():
        tm = x_ref.shape[0] // PACK_ROWS
        halves = [_unpack_pairs(x_ref[pl.ds(s, tm, stride=PACK_ROWS), :]) for s in range(PACK_ROWS)]
        x = jnp.concatenate([h[0].astype(BF16) for h in halves]
                            + [h[1].astype(BF16) for h in halves], axis=1)
        gate = jnp.dot(x, w_bf[0], preferred_element_type=F32) + bg_ref[0]
        up = jnp.dot(x, w_bf[1], preferred_element_type=F32) + bu_ref[0]
        gate = jnp.minimum(gate, SWIGLU_LIMIT)
        up = jnp.clip(up, -SWIGLU_LIMIT, SWIGLU_LIMIT)
        glu = gate * _sigmoid(SWIGLU_ALPHA * gate)
        o_ref[...] = ((up + 1.0) * glu).astype(BF16)

    @pl.when(jnp.logical_not(used))
    def _():
        o_ref[...] = jnp.zeros_like(o_ref)


def _used_block(j, nu):
    return jnp.minimum(j, nu[0] - 1)


def _gate_up(block_e, n_used, next_e, xb, w_gate_up, b_gate_up3, *, tm=MOE_TILE, tf=1024):
    n_pad, d = xb.shape[0] // PACK_ROWS, D_MODEL
    kt = D_FF // tf

    def bspec(off):
        return pl.BlockSpec((1, 1, tf),
                            lambda k, j, be, nu, ne: (be[_used_block(j, nu)], 0, off + k))

    return pl.pallas_call(
        _gate_up_kernel,
        grid_spec=pltpu.PrefetchScalarGridSpec(
            num_scalar_prefetch=3,
            grid=(kt, n_pad // tm),
            in_specs=[
                pl.BlockSpec((tm * PACK_ROWS, LANES),
                             lambda k, j, be, nu, ne: (_used_block(j, nu), 0)),
                pl.BlockSpec(memory_space=pl.ANY),
                bspec(0), bspec(kt),
            ],
            out_specs=pl.BlockSpec((tm, tf), lambda k, j, be, nu, ne: (j, k)),
            scratch_shapes=[pltpu.VMEM((2, d, tf), F32), pltpu.VMEM((2, d, tf), BF16),
                            pltpu.SemaphoreType.DMA((2,))],
        ),
        out_shape=jax.ShapeDtypeStruct((n_pad, D_FF), BF16),
        compiler_params=_params(("arbitrary", "arbitrary")),
        name="gate_up",
    )(block_e, n_used, next_e, xb, w_gate_up, b_gate_up3, b_gate_up3)


def _down_kernel(be_ref, nu_ref, ne_ref, h_ref, w_hbm, bd_ref, o_ref, stage_ref, w_bf, sem):
    def copies(expert, k):
        del k
        return [pltpu.make_async_copy(w_hbm.at[expert], stage_ref, sem.at[0])]

    def convert():
        _cast_rows(stage_ref, w_bf, 256)

    used = _stream_expert_weights(be_ref, nu_ref, ne_ref, copies, convert)

    @pl.when(used)
    def _():
        y = jnp.dot(h_ref[...], w_bf[...], preferred_element_type=F32) + bd_ref[0]
        _store_token_tiles(o_ref, _pack_pairs(y))

    @pl.when(jnp.logical_not(used))
    def _():
        o_ref[...] = jnp.zeros_like(o_ref)


def _down(block_e, n_used, next_e, hb, w_down, b_down3, *, tm=MOE_TILE):
    n_pad, f = hb.shape
    d = D_MODEL
    return pl.pallas_call(
        _down_kernel,
        grid_spec=pltpu.PrefetchScalarGridSpec(
            num_scalar_prefetch=3,
            grid=(1, n_pad // tm),
            in_specs=[
                pl.BlockSpec((tm, f), lambda k, j, be, nu, ne: (_used_block(j, nu), 0)),
                pl.BlockSpec(memory_space=pl.ANY),
                pl.BlockSpec((1, 1, d),
                             lambda k, j, be, nu, ne: (be[_used_block(j, nu)], 0, 0)),
            ],
            out_specs=pl.BlockSpec((tm * PACK_ROWS, LANES), lambda k, j, be, nu, ne: (j, 0)),
            scratch_shapes=[pltpu.VMEM((f, d), F32), pltpu.VMEM((f, d), BF16),
                            pltpu.SemaphoreType.DMA((1,))],
        ),
        out_shape=jax.ShapeDtypeStruct((n_pad * PACK_ROWS, LANES), jnp.uint32),
        compiler_params=_params(("arbitrary", "arbitrary")),
        name="down",
    )(block_e, n_used, next_e, hb, w_down, b_down3)


def _combine_kernel(dest_ref, y_hbm, gate_ref, x1_ref, ng_ref, o_ref, buf_ref, sem, *, rows):
    pr = PACK_ROWS
    half = x1_ref.shape[1] // 2

    def issue(t, carry):
        for k in range(TOP_K):
            slot = dest_ref[t * TOP_K + k]
            pltpu.make_async_copy(
                y_hbm.at[pl.ds(pl.multiple_of(slot * pr, pr), pr), :],
                buf_ref.at[pl.ds(pl.multiple_of((k * rows + t) * pr, pr), pr), :], sem).start()
        return carry

    lax.fori_loop(0, rows, issue, 0)
    for k in range(TOP_K):
        pltpu.make_async_copy(y_hbm.at[pl.ds(0, rows * pr), :],
                              buf_ref.at[pl.ds(k * rows * pr, rows * pr), :], sem).wait()
    gates = [gate_ref[:, k:k + 1] for k in range(TOP_K)]
    lows, highs = [], []
    for s in range(pr):
        lo = x1_ref[:, s * LANES:(s + 1) * LANES]
        hi = x1_ref[:, half + s * LANES:half + (s + 1) * LANES]
        for k in range(TOP_K):
            y_lo, y_hi = _unpack_pairs(buf_ref[pl.ds(k * rows * pr + s, rows, stride=pr), :])
            lo = lo + gates[k] * y_lo
            hi = hi + gates[k] * y_hi
        lows.append(lo)
        highs.append(hi)
    acc = jnp.concatenate(lows + highs, axis=1)
    o_ref[...] = acc * lax.rsqrt(jnp.mean(acc * acc, axis=-1, keepdims=True) + RMS_EPS) * ng_ref[...]


def _combine(dest, yb, gates, x1, ng, *, rows=256):
    n, d = x1.shape
    kern = functools.partial(_combine_kernel, rows=rows)
    return pl.pallas_call(
        kern,
        grid=(n // rows,),
        in_specs=[
            pl.BlockSpec((rows * TOP_K,), lambda i: (i,), memory_space=pltpu.SMEM),
            pl.BlockSpec(memory_space=pl.ANY),
            pl.BlockSpec((rows, LANES), lambda i: (i, 0)),
            pl.BlockSpec((rows, d), lambda i: (i, 0)),
            pl.BlockSpec((1, d), lambda i: (0, 0)),
        ],
        out_specs=pl.BlockSpec((rows, d), lambda i: (i, 0)),
        out_shape=jax.ShapeDtypeStruct((n, d), F32),
        scratch_shapes=[pltpu.VMEM((TOP_K * rows * PACK_ROWS, LANES), jnp.uint32),
                        pltpu.SemaphoreType.DMA(())],
        compiler_params=_params(("arbitrary",)),
        name="combine",
    )(dest, yb, gates, x1, ng)


def _routing(top_idx, tm):
    n_tok = top_idx.shape[0]
    n_asg = n_tok * TOP_K
    e_flat = top_idx.reshape(n_asg)
    onehot = (e_flat[:, None] == jnp.arange(N_EXPERTS, dtype=jnp.int32)[None, :]).astype(jnp.int32)
    cum = jnp.cumsum(onehot, axis=0)
    rank = jnp.sum(onehot * cum, axis=1) - 1
    counts = cum[-1]
    padded = (counts + tm - 1) // tm * tm
    pend = jnp.cumsum(padded)
    pstart = pend - padded
    dest = (pstart[e_flat] + rank).astype(jnp.int32)
    n_blk = -(-(n_asg + N_EXPERTS * (tm - 1)) // tm)
    n_pad = n_blk * tm
    fill_start = (pstart + counts).astype(jnp.int32)
    fill_len = (padded - counts).astype(jnp.int32)
    block_start = jnp.arange(n_blk, dtype=jnp.int32) * tm
    block_e = jnp.minimum(
        jnp.sum((pend[None, :] <= block_start[:, None]).astype(jnp.int32), axis=1),
        N_EXPERTS - 1).astype(jnp.int32)
    n_used = (pend[-1:] // tm).astype(jnp.int32)
    run_end = pend[block_e] // tm
    next_e = jnp.where(run_end < n_used[0], block_e[jnp.minimum(run_end, n_blk - 1)], -1)
    return dest, fill_start, fill_len, block_e, n_used, next_e.astype(jnp.int32), n_pad


def _layer(x, norm_mix_g, w_in, conv_w, conv_b, igate_b, fgate_b, mlstm_norm_g, rel_bias,
           w_branch_mlstm, w_branch_attn, w_out, norm_moe_g, router_w, router_b,
           w_gate_up, b_gate_up, w_down, b_down, out_norm_g):
    bsz, s, d = x.shape
    n = bsz * s
    x2 = x.reshape(n, d)
    w2, w1 = 2 * MLSTM_WIDTH, MLSTM_WIDTH
    o_qk, o_v, o_o = 0, w2, w2 + w1
    o_i = o_o + w1
    o_f = o_i + MLSTM_HEADS
    o_qa = o_f + MLSTM_HEADS
    o_ka, o_va = o_qa + ATTN_WIDTH, o_qa + 2 * ATTN_WIDTH
    o_g = o_va + ATTN_WIDTH
    w_main = jnp.concatenate([w_in[:, o_qk:o_i], w_in[:, o_g:]], axis=1).astype(BF16)
    zpad = jnp.zeros((d, LANES - MLSTM_HEADS), w_in.dtype)
    w_if = jnp.concatenate([w_in[:, o_i:o_f], zpad, w_in[:, o_f:o_qa], zpad], axis=1).astype(BF16)
    bpad = jnp.zeros((LANES - MLSTM_HEADS,), F32)
    gate_b = jnp.concatenate([igate_b, bpad, fgate_b, bpad]).reshape(1, IF_WIDTH)

    norm_g = norm_mix_g.reshape(1, d)
    proj, ifg = _in_proj(x2, norm_g, w_main, w_if)
    proj3 = proj.reshape(bsz, s, D_MAIN)
    hm = _mlstm(proj3, ifg.reshape(bsz, s, IF_WIDTH), conv_w, conv_b.reshape(1, -1), gate_b,
                mlstm_norm_g.reshape(1, -1)).reshape(n, MLSTM_WIDTH)

    outs, lses = [], []
    gw = GROUP_WIDTH
    for g, (_, dilation) in enumerate(ATTN_GROUPS):
        w_qkv = jnp.concatenate(
            [w_in[:, o + g * gw:o + (g + 1) * gw] for o in (o_qa, o_ka, o_va)], axis=1).astype(BF16)
        qkv = _attn_proj(x2, norm_g, w_qkv, bsz, s, dilation)
        o_g_, lse_g = _attn_group(qkv, _attn_bias(rel_bias, g, dilation), g)
        outs.append(o_g_)
        lses.append(lse_g)

    rw = jnp.concatenate([router_w, jnp.zeros((d, LANES - N_EXPERTS), F32)], axis=1)
    rw_hi = rw.astype(BF16)
    rw_lo = (rw - rw_hi.astype(F32)).astype(BF16)
    rw = jnp.concatenate([rw_hi, rw_hi, rw_lo], axis=0)
    rb = jnp.concatenate([router_b, jnp.full((LANES - N_EXPERTS,), NEG, F32)]).reshape(1, LANES)
    x1, h2, idx, gates = _merge(
        hm, outs, lses, proj, x2, w_branch_mlstm.astype(BF16), w_branch_attn.astype(BF16),
        w_out.astype(BF16), norm_moe_g.reshape(1, d), rw, rb)

    dest, fill_start, fill_len, block_e, n_used, next_e, n_pad = _routing(idx[:, :TOP_K], MOE_TILE)
    xb = _dispatch(fill_start, fill_len, n_used, dest, h2, n_pad)
    hb = _gate_up(block_e, n_used, next_e, xb, w_gate_up,
                  b_gate_up.reshape(N_EXPERTS, 1, 2 * D_FF))
    yb = _down(block_e, n_used, next_e, hb, w_down, b_down.reshape(N_EXPERTS, 1, d))
    out = _combine(dest, yb, gates, x1, out_norm_g.reshape(1, d))
    return out.reshape(bsz, s, d)


def kernel(x, norm_mix_g, w_in, conv_w, conv_b, igate_b, fgate_b, mlstm_norm_g, rel_bias,
           w_branch_mlstm, w_branch_attn, w_out, norm_moe_g, router_w, router_b,
           w_gate_up, b_gate_up, w_down, b_down, norm_final_g):
    assert w_in.shape[0] == 1, "single-layer block"
    return _layer(x, norm_mix_g[0], w_in[0], conv_w[0], conv_b[0], igate_b[0], fgate_b[0],
                  mlstm_norm_g[0], rel_bias, w_branch_mlstm[0], w_branch_attn[0], w_out[0],
                  norm_moe_g[0], router_w[0], router_b[0], w_gate_up[0], b_gate_up[0],
                  w_down[0], b_down[0], norm_final_g)
```

```python
import functools
import math

import jax
import jax.numpy as jnp
from jax import lax
from jax.experimental import pallas as pl
from jax.experimental.pallas import tpu as pltpu

F32 = jnp.float32
BF16 = jnp.bfloat16

D_MODEL = 2048
MLSTM_HEADS = 8
HEAD_DIM = 128
MLSTM_WIDTH = MLSTM_HEADS * HEAD_DIM
CONV_WIDTH = 4
ATTN_GROUPS = ((128, 1), (512, 4), (2048, 16))
N_GROUPS = 3
HEADS_PER_GROUP = 4
ATTN_HEADS = HEADS_PER_GROUP * N_GROUPS
ATTN_WIDTH = ATTN_HEADS * HEAD_DIM
GROUP_WIDTH = HEADS_PER_GROUP * HEAD_DIM
ATTN_BLOCK = 128
ATTN_SPAN = 128
REL_BUCKETS = 32
REL_MAX_DIST = 2048
N_EXPERTS = 32
TOP_K = 4
D_FF = D_MODEL
SWIGLU_LIMIT = 7.0
SWIGLU_ALPHA = 1.702
RMS_EPS = 1e-6
NEG = -1e30

COL_QK = 0
COL_V = 2 * MLSTM_WIDTH
COL_O = COL_V + MLSTM_WIDTH
COL_GATES = COL_O + MLSTM_WIDTH
D_MAIN = COL_GATES + 2 * D_MODEL
QKV_WIDTH = 3 * GROUP_WIDTH
LANES = 128
IF_WIDTH = 2 * LANES

MLSTM_CHUNK = 128
MOE_TILE = 512
VMEM_LIMIT = 56 * 1024 * 1024


def _sigmoid(x):
    return 0.5 * jnp.tanh(0.5 * x) + 0.5


def _bf16_parts(x, n_parts):
    parts = []
    for _ in range(n_parts):
        p = x.astype(BF16)
        parts.append(p)
        x = x - p.astype(F32)
    return parts


def _log_sigmoid(x):
    return -(jnp.maximum(-x, 0.0) + jnp.log1p(jnp.exp(-jnp.abs(x))))


def _params(sem):
    return pltpu.CompilerParams(dimension_semantics=sem, vmem_limit_bytes=VMEM_LIMIT)


def _in_proj_kernel(x_ref, g_ref, w_ref, wif_ref, o_ref, oif_ref, h_ref, *, bm, rows):
    @pl.when(pl.program_id(1) == 0)
    def _():
        for r in range(0, bm, rows):
            x = x_ref[r:r + rows, :]
            ms = jnp.mean(x * x, axis=-1, keepdims=True)
            h = (x * lax.rsqrt(ms + RMS_EPS) * g_ref[...]).astype(BF16)
            h_ref[r:r + rows, :] = h
            oif_ref[r:r + rows, :] = jnp.dot(h, wif_ref[...], preferred_element_type=F32)

    o_ref[...] = jnp.dot(h_ref[...], w_ref[...], preferred_element_type=F32).astype(BF16)


def _residue_perm(size, dilation, inverse):
    per = size // dilation
    i = lax.broadcasted_iota(jnp.int32, (size, size), 0)
    j = lax.broadcasted_iota(jnp.int32, (size, size), 1)
    if inverse:
        src = (i & (dilation - 1)) * per + (i >> (dilation.bit_length() - 1))
    else:
        src = (i & (per - 1)) * dilation + (i >> (per.bit_length() - 1))
    return j == src


def _attn_proj_kernel(x_ref, g_ref, w_ref, o_ref, h_ref, *, bm, dilation, rows):
    per = rows // dilation
    if dilation > 1:
        perm = _residue_perm(rows, dilation, False).astype(BF16)
    for c in range(0, bm, rows):
        x = x_ref[c:c + rows, :]
        ms = jnp.mean(x * x, axis=-1, keepdims=True)
        h = (x * lax.rsqrt(ms + RMS_EPS) * g_ref[...]).astype(BF16)
        if dilation > 1:
            h = jnp.dot(perm, h, preferred_element_type=F32).astype(BF16)
        h_ref[c:c + rows, :] = h
    res = jnp.dot(h_ref[...], w_ref[...], preferred_element_type=F32).astype(BF16)
    for c in range(bm // rows):
        for r in range(dilation):
            o_ref[0, r, c * per:(c + 1) * per, :] = res[c * rows + r * per:c * rows + (r + 1) * per, :]


def _attn_proj(x2, g, w_qkv, bsz, s, dilation, *, bm=1024):
    n_blk = s // bm
    per = bm // dilation
    kern = functools.partial(_attn_proj_kernel, bm=bm, dilation=dilation, rows=256)
    return pl.pallas_call(
        kern,
        grid=(bsz, n_blk),
        in_specs=[
            pl.BlockSpec((bm, D_MODEL), lambda b, i: (b * n_blk + i, 0)),
            pl.BlockSpec((1, D_MODEL), lambda b, i: (0, 0)),
            pl.BlockSpec((D_MODEL, QKV_WIDTH), lambda b, i: (0, 0),
                         pipeline_mode=pl.Buffered(1)),
        ],
        out_specs=pl.BlockSpec((1, dilation, per, QKV_WIDTH), lambda b, i: (b, 0, i, 0)),
        out_shape=jax.ShapeDtypeStruct((bsz, dilation, s // dilation, QKV_WIDTH), BF16),
        scratch_shapes=[pltpu.VMEM((bm, D_MODEL), BF16)],
        compiler_params=_params(("arbitrary", "arbitrary")),
        name=f"attn_proj_d{dilation}",
    )(x2, g, w_qkv)


def _in_proj(x2, g, w_main, w_if, *, bm=1024, bn=1024):
    n = x2.shape[0]
    bm = min(bm, n)
    kern = functools.partial(_in_proj_kernel, bm=bm, rows=256)
    return pl.pallas_call(
        kern,
        grid=(n // bm, D_MAIN // bn),
        in_specs=[
            pl.BlockSpec((bm, D_MODEL), lambda i, j: (i, 0)),
            pl.BlockSpec((1, D_MODEL), lambda i, j: (0, 0)),
            pl.BlockSpec((D_MODEL, bn), lambda i, j: (0, j)),
            pl.BlockSpec((D_MODEL, IF_WIDTH), lambda i, j: (0, 0)),
        ],
        out_specs=[
            pl.BlockSpec((bm, bn), lambda i, j: (i, j)),
            pl.BlockSpec((bm, IF_WIDTH), lambda i, j: (i, 0)),
        ],
        out_shape=[
            jax.ShapeDtypeStruct((n, D_MAIN), BF16),
            jax.ShapeDtypeStruct((n, IF_WIDTH), F32),
        ],
        scratch_shapes=[pltpu.VMEM((bm, D_MODEL), BF16)],
        compiler_params=_params(("arbitrary", "arbitrary")),
        name="in_proj",
    )(x2, g, w_main, w_if)


def _mlstm_kernel(qk_ref, v_ref, og_ref, if_ref, cw_ref, cb_ref, gb_ref, ng_ref, out_ref,
                  tail_ref, c_ref, m_ref, *, chunk):
    L = chunk
    dh = HEAD_DIM

    @pl.when(pl.program_id(1) == 0)
    def _():
        tail_ref[...] = jnp.zeros_like(tail_ref)
        c_ref[...] = jnp.zeros_like(c_ref)
        m_ref[...] = jnp.zeros_like(m_ref)

    cur = qk_ref[0].astype(F32)
    ext = jnp.concatenate([tail_ref[...], cur], axis=0)
    acc = cur * cw_ref[CONV_WIDTH - 1:CONV_WIDTH, :] + cb_ref[...]
    for s in range(1, CONV_WIDTH):
        acc = acc + ext[8 - s:8 - s + L, :] * cw_ref[CONV_WIDTH - 1 - s:CONV_WIDTH - s, :]
    tail_ref[...] = cur[L - 8:, :]
    qk = acc * _sigmoid(acc)
    q_all = qk[:, :MLSTM_WIDTH].astype(BF16)
    k_all = (qk[:, MLSTM_WIDTH:] * (dh ** -0.5)).astype(BF16)

    gi = if_ref[0][:, :LANES] + gb_ref[:, :LANES]
    lf = _log_sigmoid(if_ref[0][:, LANES:] + gb_ref[:, LANES:])
    row = lax.broadcasted_iota(jnp.int32, (L, L), 0)
    col = lax.broadcasted_iota(jnp.int32, (L, L), 1)
    causal = col <= row
    tri = causal.astype(F32)
    cum = jnp.dot(tri, lf, preferred_element_type=F32, precision=lax.Precision.HIGHEST)
    rmat = gi - cum
    rmat_t = rmat.T
    ones_blk = jnp.ones((L, dh), BF16)

    for h in range(MLSTM_HEADS):
        hs = slice(h * dh, (h + 1) * dh)
        b = cum[:, h:h + 1]
        r_col = rmat[:, h:h + 1]
        r_row = rmat_t[h:h + 1, :]
        m_prev = m_ref[h:h + 1, 0:1]
        logw = jnp.where(causal, b + r_row, NEG)
        inter = b + m_prev
        m_out = jnp.maximum(inter, jnp.max(logw, axis=1, keepdims=True))
        p = jnp.exp(logw - m_out)
        qh = q_all[:, hs]
        kh = k_all[:, hs]
        s = lax.dot_general(qh, kh, (((1,), (1,)), ((), ())), preferred_element_type=F32)
        w = (p * s).astype(BF16)
        vaug = jnp.concatenate([v_ref[0][:, hs], ones_blk], axis=1)
        c_prev = c_ref[h]
        nd = (jnp.dot(w, vaug, preferred_element_type=F32)
              + jnp.exp(inter - m_out)
              * jnp.dot(qh, c_prev.astype(BF16), preferred_element_type=F32))
        hh = nd[:, :dh] / jnp.maximum(jnp.abs(nd[:, dh:]), jnp.exp(-m_out))

        b_last = b[L - 1:L, :]
        m_new = b_last + jnp.maximum(m_prev, jnp.max(r_col, axis=0, keepdims=True))
        ws = jnp.exp(b_last + r_col - m_new)
        decay = jnp.exp(b_last + m_prev - m_new)
        wv = (ws * vaug.astype(F32)).astype(BF16)
        kv = lax.dot_general(kh, wv, (((0,), (0,)), ((), ())), preferred_element_type=F32)
        c_ref[h] = decay * c_prev + kv
        m_ref[h:h + 1, :] = jnp.broadcast_to(m_new, (1, LANES))

        y = hh * lax.rsqrt(jnp.mean(hh * hh, axis=-1, keepdims=True) + RMS_EPS) * ng_ref[:, hs]
        y = y * _sigmoid(og_ref[0][:, hs].astype(F32))
        out_ref[0, :, hs] = y.astype(BF16)


def _mlstm(proj3, ifg3, conv_w, conv_b, gate_b, norm_g, *, chunk=MLSTM_CHUNK):
    bsz, s, _ = proj3.shape
    w = MLSTM_WIDTH
    kern = functools.partial(_mlstm_kernel, chunk=chunk)
    return pl.pallas_call(
        kern,
        grid=(bsz, s // chunk),
        in_specs=[
            pl.BlockSpec((1, chunk, 2 * w), lambda b, c: (b, c, COL_QK // (2 * w))),
            pl.BlockSpec((1, chunk, w), lambda b, c: (b, c, COL_V // w)),
            pl.BlockSpec((1, chunk, w), lambda b, c: (b, c, COL_O // w)),
            pl.BlockSpec((1, chunk, IF_WIDTH), lambda b, c: (b, c, 0)),
            pl.BlockSpec((CONV_WIDTH, 2 * w), lambda b, c: (0, 0)),
            pl.BlockSpec((1, 2 * w), lambda b, c: (0, 0)),
            pl.BlockSpec((1, IF_WIDTH), lambda b, c: (0, 0)),
            pl.BlockSpec((1, w), lambda b, c: (0, 0)),
        ],
        out_specs=pl.BlockSpec((1, chunk, w), lambda b, c: (b, c, 0)),
        out_shape=jax.ShapeDtypeStruct((bsz, s, w), BF16),
        scratch_shapes=[
            pltpu.VMEM((8, 2 * w), F32),
            pltpu.VMEM((MLSTM_HEADS, HEAD_DIM, 2 * HEAD_DIM), F32),
            pltpu.VMEM((MLSTM_HEADS, LANES), F32),
        ],
        compiler_params=_params(("arbitrary", "arbitrary")),
        name="mlstm",
    )(proj3, proj3, proj3, ifg3, conv_w, conv_b, gate_b, norm_g)


def _attn_kernel(q_ref, kp_ref, kc_ref, vp_ref, vc_ref, bias_ref, o_ref, lse_ref):
    qb = ATTN_BLOCK
    dh = HEAD_DIM
    first = pl.program_id(2) == 0
    kcol = lax.broadcasted_iota(jnp.int32, (qb, 2 * qb), 1)
    dead = jnp.logical_and(first, kcol < qb)
    lane = lax.broadcasted_iota(jnp.int32, (qb, LANES), 1)
    lse_all = jnp.zeros((qb, LANES), F32)
    for j in range(HEADS_PER_GROUP):
        hs = slice(j * dh, (j + 1) * dh)
        q = q_ref[0, 0][:, hs]
        k = jnp.concatenate([kp_ref[0, 0][:, hs], kc_ref[0, 0][:, hs]], axis=0)
        v = jnp.concatenate([vp_ref[0, 0][:, hs], vc_ref[0, 0][:, hs]], axis=0)
        s = lax.dot_general(q, k, (((1,), (1,)), ((), ())), preferred_element_type=F32)
        s = s * (dh ** -0.5) + bias_ref[j]
        s = jnp.where(dead, NEG, s)
        mx = jnp.max(s, axis=1, keepdims=True)
        p = jnp.exp(s - mx)
        den = jnp.sum(p, axis=1, keepdims=True)
        o = jnp.dot(p.astype(BF16), v, preferred_element_type=F32) / den
        o_ref[0, 0, :, hs] = o.astype(BF16)
        lse_all = jnp.where(lane == j, mx + jnp.log(den), lse_all)
    lse_ref[0, 0] = lse_all


def _attn_group(qkv, bias, g):
    bsz, dilation, n, _ = qkv.shape
    nb = n // ATTN_BLOCK
    gw = GROUP_WIDTH

    def spec(part, prev):
        def imap(b, r, i):
            return (b, r, jnp.maximum(i - 1, 0) if prev else i, part)

        return pl.BlockSpec((1, 1, ATTN_BLOCK, gw), imap)

    return pl.pallas_call(
        _attn_kernel,
        grid=(bsz, dilation, nb),
        in_specs=[
            spec(0, False),
            spec(1, True), spec(1, False),
            spec(2, True), spec(2, False),
            pl.BlockSpec((HEADS_PER_GROUP, ATTN_BLOCK, 2 * ATTN_BLOCK), lambda b, r, i: (0, 0, 0)),
        ],
        out_specs=[
            pl.BlockSpec((1, 1, ATTN_BLOCK, gw), lambda b, r, i: (b, r, i, 0)),
            pl.BlockSpec((1, 1, ATTN_BLOCK, LANES), lambda b, r, i: (b, r, i, 0)),
        ],
        out_shape=[
            jax.ShapeDtypeStruct((bsz, dilation, n, gw), BF16),
            jax.ShapeDtypeStruct((bsz, dilation, n, LANES), F32),
        ],
        compiler_params=_params(("arbitrary", "arbitrary", "arbitrary")),
        name=f"attn_g{g}",
    )(qkv, qkv, qkv, qkv, qkv, bias)


def _t5_bucket(dist):
    max_exact = REL_BUCKETS // 2
    d_f = jnp.maximum(dist, 1).astype(F32)
    large = max_exact + (jnp.log(d_f / max_exact) / math.log(REL_MAX_DIST / max_exact)
                         * (REL_BUCKETS - max_exact)).astype(jnp.int32)
    large = jnp.minimum(large, REL_BUCKETS - 1)
    return jnp.where(dist < max_exact, dist, large)


def _attn_bias(rel_bias, g, dilation):
    span = ATTN_SPAN
    buckets = _t5_bucket(jnp.arange(span + 1, dtype=jnp.int32) * dilation)
    vec = rel_bias[buckets][:, g * HEADS_PER_GROUP:(g + 1) * HEADS_PER_GROUP].T.astype(F32)
    qpos = jnp.arange(ATTN_BLOCK)[:, None]
    kpos = jnp.arange(2 * ATTN_BLOCK)[None, :]
    dist = qpos + ATTN_BLOCK - kpos
    valid = (dist >= 0) & (dist <= span)
    onehot = (dist[:, :, None] == jnp.arange(span + 1)[None, None, :]).astype(F32)
    table = jnp.einsum('qkj,hj->hqk', onehot, vec, precision=lax.Precision.HIGHEST)
    return jnp.where(valid[None], table, NEG)


def _merge_kernel(hm_ref, o0_ref, o1_ref, o2_ref, l0_ref, l1_ref, l2_ref, gm_ref, ga_ref, x_ref,
                  wbm_ref, wba_ref, wo_ref, ng_ref, rw_ref, rb_ref,
                  x1_ref, h2_ref, idx_ref, gate_ref):
    dh = HEAD_DIM
    bm = x_ref.shape[0]

    def token_order(ref):
        dilation = ref.shape[1]
        if dilation == 1:
            return ref[0, 0]
        perm = _residue_perm(bm, dilation, True).astype(BF16)
        blk = ref[0].reshape(bm, ref.shape[3])
        parts = [blk] if blk.dtype == BF16 else _bf16_parts(blk, 3)
        out = None
        for p in parts:
            moved = jnp.dot(perm, p, preferred_element_type=F32)
            out = moved if out is None else out + moved
        return out

    l0, l1, l2 = token_order(l0_ref), token_order(l1_ref), token_order(l2_ref)
    o0, o1, o2 = token_order(o0_ref), token_order(o1_ref), token_order(o2_ref)
    mx = jnp.maximum(jnp.maximum(l0, l1), l2)
    e0, e1, e2 = jnp.exp(l0 - mx), jnp.exp(l1 - mx), jnp.exp(l2 - mx)
    den = e0 + e1 + e2
    w0, w1, w2 = e0 / den, e1 / den, e2 / den
    parts = []
    for j in range(HEADS_PER_GROUP):
        hs = slice(j * dh, (j + 1) * dh)
        parts.append(w0[:, j:j + 1] * o0[:, hs] + w1[:, j:j + 1] * o1[:, hs]
                     + w2[:, j:j + 1] * o2[:, hs])
    ha = jnp.concatenate(parts, axis=1).astype(BF16)
    ym = jnp.dot(hm_ref[...], wbm_ref[...], preferred_element_type=F32)
    ya = jnp.dot(ha, wba_ref[...], preferred_element_type=F32)
    merged = (_sigmoid(gm_ref[...].astype(F32)) * ym + _sigmoid(ga_ref[...].astype(F32)) * ya)
    x1 = x_ref[...] + jnp.dot(merged.astype(BF16), wo_ref[...], preferred_element_type=F32)
    x1_ref[...] = x1
    h2 = x1 * lax.rsqrt(jnp.mean(x1 * x1, axis=-1, keepdims=True) + RMS_EPS) * ng_ref[...]
    _store_token_tiles(h2_ref, _pack_pairs(h2))
    h_hi, h_lo = _bf16_parts(h2, 2)
    logits = jnp.dot(jnp.concatenate([h_hi, h_lo, h_hi], axis=1), rw_ref[...],
                     preferred_element_type=F32) + rb_ref[...]
    lane = lax.broadcasted_iota(jnp.int32, logits.shape, 1)
    idx_all = jnp.zeros(logits.shape, jnp.int32)
    val_all = jnp.zeros(logits.shape, F32)
    top0 = None
    esum = None
    for k in range(TOP_K):
        m = jnp.max(logits, axis=1, keepdims=True)
        sel = jnp.min(jnp.where(logits == m, lane, LANES), axis=1, keepdims=True)
        if k == 0:
            top0 = m
        e = jnp.exp(m - top0)
        esum = e if k == 0 else esum + e
        idx_all = jnp.where(lane == k, sel, idx_all)
        val_all = jnp.where(lane == k, e, val_all)
        logits = jnp.where(lane == sel, -jnp.inf, logits)
    idx_ref[...] = idx_all
    gate_ref[...] = val_all / esum


def _merge(hm, outs, lses, proj, x2, wbm, wba, wo, ng, rw, rb, *, bm=256):
    n = x2.shape[0]
    d = D_MODEL
    gcol = COL_GATES // d
    n_blk = outs[0].shape[2] // bm

    def rows(width):
        return pl.BlockSpec((bm, width), lambda i: (i, 0))

    def full(a, b):
        return pl.BlockSpec((a, b), lambda i: (0, 0), pipeline_mode=pl.Buffered(1))

    def residue(arr):
        dilation, width = arr.shape[1], arr.shape[3]
        return pl.BlockSpec((1, dilation, bm // dilation, width),
                            lambda i: (i // n_blk, 0, i % n_blk, 0))

    return pl.pallas_call(
        _merge_kernel,
        grid=(n // bm,),
        in_specs=[
            rows(MLSTM_WIDTH),
            residue(outs[0]), residue(outs[1]), residue(outs[2]),
            residue(lses[0]), residue(lses[1]), residue(lses[2]),
            pl.BlockSpec((bm, d), lambda i: (i, gcol)),
            pl.BlockSpec((bm, d), lambda i: (i, gcol + 1)),
            rows(d),
            full(MLSTM_WIDTH, d), full(GROUP_WIDTH, d), full(d, d),
            full(1, d), full(3 * d, LANES), full(1, LANES),
        ],
        out_specs=[rows(d), pl.BlockSpec((bm * PACK_ROWS, LANES), lambda i: (i, 0)),
                   rows(LANES), rows(LANES)],
        out_shape=[
            jax.ShapeDtypeStruct((n, d), F32),
            jax.ShapeDtypeStruct((n * PACK_ROWS, LANES), jnp.uint32),
            jax.ShapeDtypeStruct((n, LANES), jnp.int32),
            jax.ShapeDtypeStruct((n, LANES), F32),
        ],
        compiler_params=_params(("arbitrary",)),
        name="merge",
    )(hm, outs[0], outs[1], outs[2], lses[0], lses[1], lses[2], proj, proj, x2,
      wbm, wba, wo, ng, rw, rb)


PACK_ROWS = 8


def _pack_pairs(x):
    w = x.shape[1] // 2
    lo = lax.bitcast_convert_type(x[:, :w].astype(BF16).astype(F32), jnp.uint32) >> 16
    hi = lax.bitcast_convert_type(x[:, w:].astype(BF16).astype(F32), jnp.uint32)
    return (hi & jnp.uint32(0xFFFF0000)) | lo


def _unpack_pairs(words):
    lo = lax.bitcast_convert_type(words << 16, F32)
    hi = lax.bitcast_convert_type(words & jnp.uint32(0xFFFF0000), F32)
    return lo, hi


def _store_token_tiles(ref, words):
    rows = words.shape[0]
    for s in range(PACK_ROWS):
        ref[pl.ds(s, rows, stride=PACK_ROWS), :] = words[:, s * LANES:(s + 1) * LANES]


def _dispatch_kernel(fs_ref, fl_ref, nu_ref, dest_ref, hp_ref, xb_hbm, zero_ref, sem, zsem, *,
                     tokens, tm, n_blk):
    pr = PACK_ROWS
    fill_sizes = [1 << b for b in reversed(range((tm - 1).bit_length()))]

    def fill_copies(e):
        off = fs_ref[e]
        for p in fill_sizes:
            take = (fl_ref[e] & p) != 0
            dst = xb_hbm.at[pl.ds(pl.multiple_of(off * pr, pr), p * pr), :]
            yield take, pltpu.make_async_copy(zero_ref.at[pl.ds(0, p * pr), :], dst, zsem)
            off = off + jnp.where(take, p, 0)

    def block_copy(b):
        dst = xb_hbm.at[pl.ds(pl.multiple_of(b * (tm * pr), tm * pr), tm * pr), :]
        return pltpu.make_async_copy(zero_ref, dst, zsem)

    @pl.when(pl.program_id(0) == 0)
    def _():
        zero_ref[...] = jnp.zeros_like(zero_ref)
        for start in (True, False):
            def per_expert(e, carry):
                for take, cp in fill_copies(e):
                    @pl.when(take)
                    def _():
                        cp.start() if start else cp.wait()
                return carry

            def per_block(b, carry):
                block_copy(b).start() if start else block_copy(b).wait()
                return carry

            lax.fori_loop(0, N_EXPERTS, per_expert, 0)
            lax.fori_loop(nu_ref[0], n_blk, per_block, 0)

    def issue(t, carry):
        src = hp_ref.at[pl.ds(pl.multiple_of(t * pr, pr), pr), :]
        for k in range(TOP_K):
            slot = dest_ref[t * TOP_K + k]
            pltpu.make_async_copy(src, xb_hbm.at[pl.ds(pl.multiple_of(slot * pr, pr), pr), :],
                                  sem).start(priority=k % 2)
        return carry

    lax.fori_loop(0, tokens, issue, 0)
    for k in range(TOP_K):
        pltpu.make_async_copy(hp_ref, xb_hbm.at[pl.ds(0, tokens * pr), :], sem).wait()


def _dispatch(fill_start, fill_len, n_used, dest, h2p, n_pad, *, tokens=256, tm=MOE_TILE):
    pr = PACK_ROWS
    n_tok = h2p.shape[0] // pr
    kern = functools.partial(_dispatch_kernel, tokens=tokens, tm=tm, n_blk=n_pad // tm)
    return pl.pallas_call(
        kern,
        grid_spec=pltpu.PrefetchScalarGridSpec(
            num_scalar_prefetch=3,
            grid=(n_tok // tokens,),
            in_specs=[
                pl.BlockSpec((tokens * TOP_K,), lambda i, fs, fl, nu: (i,),
                             memory_space=pltpu.SMEM),
                pl.BlockSpec((tokens * pr, LANES), lambda i, fs, fl, nu: (i, 0)),
            ],
            out_specs=pl.BlockSpec(memory_space=pl.ANY),
            scratch_shapes=[pltpu.VMEM((tm * pr, LANES), jnp.uint32),
                            pltpu.SemaphoreType.DMA(()), pltpu.SemaphoreType.DMA(())],
        ),
        out_shape=jax.ShapeDtypeStruct((n_pad * pr, LANES), jnp.uint32),
        compiler_params=_params(("arbitrary",)),
        name="dispatch",
    )(fill_start, fill_len, n_used, dest, h2p)


def _cast_rows(src_ref, dst_ref, rows):
    total = src_ref.shape[0]

    def body(i, carry):
        r = pl.multiple_of(i * rows, rows)
        dst_ref[pl.ds(r, rows), :] = src_ref[pl.ds(r, rows), :].astype(BF16)
        return carry

    lax.fori_loop(0, total // rows, body, 0)


def _stream_expert_weights(be_ref, nu_ref, ne_ref, copies, convert):
    k, j = pl.program_id(0), pl.program_id(1)
    expert = be_ref[j]
    used = j < nu_ref[0]
    run_start = jnp.logical_or(j == 0, expert != be_ref[jnp.maximum(j - 1, 0)])

    @pl.when(jnp.logical_and(k == 0, j == 0))
    def _():
        for cp in copies(expert, k):
            cp.start()

    @pl.when(jnp.logical_and(used, run_start))
    def _():
        for cp in copies(expert, k):
            cp.wait()
        convert()
        last_run = ne_ref[j] < 0
        nxt_e = jnp.where(last_run, be_ref[0], ne_ref[j])
        nxt_k = jnp.where(last_run, k + 1, k)

        @pl.when(nxt_k < pl.num_programs(0))
        def _():
            for cp in copies(nxt_e, nxt_k):
                cp.start()

    return used


def _gate_up_kernel(be_ref, nu_ref, ne_ref, x_ref, w_hbm, bg_ref, bu_ref, o_ref,
                    stage_ref, w_bf, sem):
    tf = o_ref.shape[1]
    kt = pl.num_programs(0)

    def copies(expert, k):
        return [pltpu.make_async_copy(
            w_hbm.at[expert, :, pl.ds(pl.multiple_of((half * kt + k) * tf, tf), tf)],
            stage_ref.at[half], sem.at[half]) for half in range(2)]

    def convert():
        for half in range(2):
            _cast_rows(stage_ref.at[half], w_bf.at[half], 256)

    used = _stream_expert_weights(be_ref, nu_ref, ne_ref, copies, convert)

    @pl.when(used)
    def _():
        tm = x_ref.shape[0] // PACK_ROWS
        halves = [_unpack_pairs(x_ref[pl.ds(s, tm, stride=PACK_ROWS), :]) for s in range(PACK_ROWS)]
        x = jnp.concatenate([h[0].astype(BF16) for h in halves]
                            + [h[1].astype(BF16) for h in halves], axis=1)
        gate = jnp.dot(x, w_bf[0], preferred_element_type=F32) + bg_ref[0]
        up = jnp.dot(x, w_bf[1], preferred_element_type=F32) + bu_ref[0]
        gate = jnp.minimum(gate, SWIGLU_LIMIT)
        up = jnp.clip(up, -SWIGLU_LIMIT, SWIGLU_LIMIT)
        glu = gate * _sigmoid(SWIGLU_ALPHA * gate)
        o_ref[...] = ((up + 1.0) * glu).astype(BF16)

    @pl.when(jnp.logical_not(used))
    def _():
        o_ref[...] = jnp.zeros_like(o_ref)


def _used_block(j, nu):
    return jnp.minimum(j, nu[0] - 1)


def _gate_up(block_e, n_used, next_e, xb, w_gate_up, b_gate_up3, *, tm=MOE_TILE, tf=1024):
    n_pad, d = xb.shape[0] // PACK_ROWS, D_MODEL
    kt = D_FF // tf

    def bspec(off):
        return pl.BlockSpec((1, 1, tf),
                            lambda k, j, be, nu, ne: (be[_used_block(j, nu)], 0, off + k))

    return pl.pallas_call(
        _gate_up_kernel,
        grid_spec=pltpu.PrefetchScalarGridSpec(
            num_scalar_prefetch=3,
            grid=(kt, n_pad // tm),
            in_specs=[
                pl.BlockSpec((tm * PACK_ROWS, LANES),
                             lambda k, j, be, nu, ne: (_used_block(j, nu), 0)),
                pl.BlockSpec(memory_space=pl.ANY),
                bspec(0), bspec(kt),
            ],
            out_specs=pl.BlockSpec((tm, tf), lambda k, j, be, nu, ne: (j, k)),
            scratch_shapes=[pltpu.VMEM((2, d, tf), F32), pltpu.VMEM((2, d, tf), BF16),
                            pltpu.SemaphoreType.DMA((2,))],
        ),
        out_shape=jax.ShapeDtypeStruct((n_pad, D_FF), BF16),
        compiler_params=_params(("arbitrary", "arbitrary")),
        name="gate_up",
    )(block_e, n_used, next_e, xb, w_gate_up, b_gate_up3, b_gate_up3)


def _down_kernel(be_ref, nu_ref, ne_ref, h_ref, w_hbm, bd_ref, o_ref, stage_ref, w_bf, sem):
    def copies(expert, k):
        del k
        return [pltpu.make_async_copy(w_hbm.at[expert], stage_ref, sem.at[0])]

    def convert():
        _cast_rows(stage_ref, w_bf, 256)

    used = _stream_expert_weights(be_ref, nu_ref, ne_ref, copies, convert)

    @pl.when(used)
    def _():
        y = jnp.dot(h_ref[...], w_bf[...], preferred_element_type=F32) + bd_ref[0]
        _store_token_tiles(o_ref, _pack_pairs(y))

    @pl.when(jnp.logical_not(used))
    def _():
        o_ref[...] = jnp.zeros_like(o_ref)


def _down(block_e, n_used, next_e, hb, w_down, b_down3, *, tm=MOE_TILE):
    n_pad, f = hb.shape
    d = D_MODEL
    return pl.pallas_call(
        _down_kernel,
        grid_spec=pltpu.PrefetchScalarGridSpec(
            num_scalar_prefetch=3,
            grid=(1, n_pad // tm),
            in_specs=[
                pl.BlockSpec((tm, f), lambda k, j, be, nu, ne: (_used_block(j, nu), 0)),
                pl.BlockSpec(memory_space=pl.ANY),
                pl.BlockSpec((1, 1, d),
                             lambda k, j, be, nu, ne: (be[_used_block(j, nu)], 0, 0)),
            ],
            out_specs=pl.BlockSpec((tm * PACK_ROWS, LANES), lambda k, j, be, nu, ne: (j, 0)),
            scratch_shapes=[pltpu.VMEM((f, d), F32), pltpu.VMEM((f, d), BF16),
                            pltpu.SemaphoreType.DMA((1,))],
        ),
        out_shape=jax.ShapeDtypeStruct((n_pad * PACK_ROWS, LANES), jnp.uint32),
        compiler_params=_params(("arbitrary", "arbitrary")),
        name="down",
    )(block_e, n_used, next_e, hb, w_down, b_down3)


def _combine_kernel(dest_ref, dest_next_ref, y_hbm, gate_ref, x1_ref, ng_ref, o_ref, buf_ref, sem,
                    *, rows):
    pr = PACK_ROWS
    half = x1_ref.shape[1] // 2
    i = pl.program_id(0)
    slot_rows = TOP_K * rows * pr

    def gather(idx_ref, buf):
        def issue(t, carry):
            for k in range(TOP_K):
                slot = idx_ref[t * TOP_K + k]
                pltpu.make_async_copy(
                    y_hbm.at[pl.ds(pl.multiple_of(slot * pr, pr), pr), :],
                    buf_ref.at[pl.ds(pl.multiple_of(buf * slot_rows + (k * rows + t) * pr, pr), pr), :],
                    sem.at[buf]).start(priority=k % 2)
            return carry

        lax.fori_loop(0, rows, issue, 0)

    @pl.when(i == 0)
    def _():
        gather(dest_ref, 0)

    @pl.when(i + 1 < pl.num_programs(0))
    def _():
        gather(dest_next_ref, (i + 1) % 2)

    cur = i % 2
    base = pl.multiple_of(cur * slot_rows, slot_rows)
    pltpu.make_async_copy(y_hbm.at[pl.ds(0, slot_rows), :],
                          buf_ref.at[pl.ds(base, slot_rows), :], sem.at[cur]).wait()
    gates = [gate_ref[:, k:k + 1] for k in range(TOP_K)]
    lows, highs = [], []
    for s in range(pr):
        lo = x1_ref[:, s * LANES:(s + 1) * LANES]
        hi = x1_ref[:, half + s * LANES:half + (s + 1) * LANES]
        for k in range(TOP_K):
            y_lo, y_hi = _unpack_pairs(
                buf_ref[pl.ds(base + (k * rows * pr + s), rows, stride=pr), :])
            lo = lo + gates[k] * y_lo
            hi = hi + gates[k] * y_hi
        lows.append(lo)
        highs.append(hi)
    acc = jnp.concatenate(lows + highs, axis=1)
    o_ref[...] = acc * lax.rsqrt(jnp.mean(acc * acc, axis=-1, keepdims=True) + RMS_EPS) * ng_ref[...]


def _combine(dest, yb, gates, x1, ng, *, rows=256):
    n, d = x1.shape
    kern = functools.partial(_combine_kernel, rows=rows)
    n_steps = n // rows
    return pl.pallas_call(
        kern,
        grid=(n_steps,),
        in_specs=[
            pl.BlockSpec((rows * TOP_K,), lambda i: (i,), memory_space=pltpu.SMEM),
            pl.BlockSpec((rows * TOP_K,), lambda i: (jnp.minimum(i + 1, n_steps - 1),),
                         memory_space=pltpu.SMEM),
            pl.BlockSpec(memory_space=pl.ANY),
            pl.BlockSpec((rows, LANES), lambda i: (i, 0)),
            pl.BlockSpec((rows, d), lambda i: (i, 0)),
            pl.BlockSpec((1, d), lambda i: (0, 0)),
        ],
        out_specs=pl.BlockSpec((rows, d), lambda i: (i, 0)),
        out_shape=jax.ShapeDtypeStruct((n, d), F32),
        scratch_shapes=[pltpu.VMEM((2 * TOP_K * rows * PACK_ROWS, LANES), jnp.uint32),
                        pltpu.SemaphoreType.DMA((2,))],
        compiler_params=_params(("arbitrary",)),
        name="combine",
    )(dest, dest, yb, gates, x1, ng)


def _routing(top_idx, tm):
    n_tok = top_idx.shape[0]
    n_asg = n_tok * TOP_K
    e_flat = top_idx.reshape(n_asg)
    onehot = (e_flat[:, None] == jnp.arange(N_EXPERTS, dtype=jnp.int32)[None, :]).astype(jnp.int32)
    cum = jnp.cumsum(onehot, axis=0)
    rank = jnp.sum(onehot * cum, axis=1) - 1
    counts = cum[-1]
    padded = (counts + tm - 1) // tm * tm
    pend = jnp.cumsum(padded)
    pstart = pend - padded
    dest = (pstart[e_flat] + rank).astype(jnp.int32)
    n_blk = -(-(n_asg + N_EXPERTS * (tm - 1)) // tm)
    n_pad = n_blk * tm
    fill_start = (pstart + counts).astype(jnp.int32)
    fill_len = (padded - counts).astype(jnp.int32)
    block_start = jnp.arange(n_blk, dtype=jnp.int32) * tm
    block_e = jnp.minimum(
        jnp.sum((pend[None, :] <= block_start[:, None]).astype(jnp.int32), axis=1),
        N_EXPERTS - 1).astype(jnp.int32)
    n_used = (pend[-1:] // tm).astype(jnp.int32)
    run_end = pend[block_e] // tm
    next_e = jnp.where(run_end < n_used[0], block_e[jnp.minimum(run_end, n_blk - 1)], -1)
    return dest, fill_start, fill_len, block_e, n_used, next_e.astype(jnp.int32), n_pad


def _layer(x, norm_mix_g, w_in, conv_w, conv_b, igate_b, fgate_b, mlstm_norm_g, rel_bias,
           w_branch_mlstm, w_branch_attn, w_out, norm_moe_g, router_w, router_b,
           w_gate_up, b_gate_up, w_down, b_down, out_norm_g):
    bsz, s, d = x.shape
    n = bsz * s
    x2 = x.reshape(n, d)
    w2, w1 = 2 * MLSTM_WIDTH, MLSTM_WIDTH
    o_qk, o_v, o_o = 0, w2, w2 + w1
    o_i = o_o + w1
    o_f = o_i + MLSTM_HEADS
    o_qa = o_f + MLSTM_HEADS
    o_ka, o_va = o_qa + ATTN_WIDTH, o_qa + 2 * ATTN_WIDTH
    o_g = o_va + ATTN_WIDTH
    w_main = jnp.concatenate([w_in[:, o_qk:o_i], w_in[:, o_g:]], axis=1).astype(BF16)
    zpad = jnp.zeros((d, LANES - MLSTM_HEADS), w_in.dtype)
    w_if = jnp.concatenate([w_in[:, o_i:o_f], zpad, w_in[:, o_f:o_qa], zpad], axis=1).astype(BF16)
    bpad = jnp.zeros((LANES - MLSTM_HEADS,), F32)
    gate_b = jnp.concatenate([igate_b, bpad, fgate_b, bpad]).reshape(1, IF_WIDTH)

    norm_g = norm_mix_g.reshape(1, d)
    proj, ifg = _in_proj(x2, norm_g, w_main, w_if)
    proj3 = proj.reshape(bsz, s, D_MAIN)
    hm = _mlstm(proj3, ifg.reshape(bsz, s, IF_WIDTH), conv_w, conv_b.reshape(1, -1), gate_b,
                mlstm_norm_g.reshape(1, -1)).reshape(n, MLSTM_WIDTH)

    outs, lses = [], []
    gw = GROUP_WIDTH
    for g, (_, dilation) in enumerate(ATTN_GROUPS):
        w_qkv = jnp.concatenate(
            [w_in[:, o + g * gw:o + (g + 1) * gw] for o in (o_qa, o_ka, o_va)], axis=1).astype(BF16)
        qkv = _attn_proj(x2, norm_g, w_qkv, bsz, s, dilation)
        o_g_, lse_g = _attn_group(qkv, _attn_bias(rel_bias, g, dilation), g)
        outs.append(o_g_)
        lses.append(lse_g)

    rw = jnp.concatenate([router_w, jnp.zeros((d, LANES - N_EXPERTS), F32)], axis=1)
    rw_hi = rw.astype(BF16)
    rw_lo = (rw - rw_hi.astype(F32)).astype(BF16)
    rw = jnp.concatenate([rw_hi, rw_hi, rw_lo], axis=0)
    rb = jnp.concatenate([router_b, jnp.full((LANES - N_EXPERTS,), NEG, F32)]).reshape(1, LANES)
    x1, h2, idx, gates = _merge(
        hm, outs, lses, proj, x2, w_branch_mlstm.astype(BF16), w_branch_attn.astype(BF16),
        w_out.astype(BF16), norm_moe_g.reshape(1, d), rw, rb)

    dest, fill_start, fill_len, block_e, n_used, next_e, n_pad = _routing(idx[:, :TOP_K], MOE_TILE)
    xb = _dispatch(fill_start, fill_len, n_used, dest, h2, n_pad)
    hb = _gate_up(block_e, n_used, next_e, xb, w_gate_up,
                  b_gate_up.reshape(N_EXPERTS, 1, 2 * D_FF))
    yb = _down(block_e, n_used, next_e, hb, w_down, b_down.reshape(N_EXPERTS, 1, d))
    out = _combine(dest, yb, gates, x1, out_norm_g.reshape(1, d))
    return out.reshape(bsz, s, d)


def kernel(x, norm_mix_g, w_in, conv_w, conv_b, igate_b, fgate_b, mlstm_norm_g, rel_bias,
           w_branch_mlstm, w_branch_attn, w_out, norm_moe_g, router_w, router_b,
           w_gate_up, b_gate_up, w_down, b_down, norm_final_g):
    assert w_in.shape[0] == 1, "single-layer block"
    return _layer(x, norm_mix_g[0], w_in[0], conv_w[0], conv_b[0], igate_b[0], fgate_b[0],
                  mlstm_norm_g[0], rel_bias, w_branch_mlstm[0], w_branch_attn[0], w_out[0],
                  norm_moe_g[0], router_w[0], router_b[0], w_gate_up[0], b_gate_up[0],
                  w_down[0], b_down[0], norm_final_g)
```

```python
import functools
import math

import jax
import jax.numpy as jnp
from jax import lax
from jax.experimental import pallas as pl
from jax.experimental.pallas import tpu as pltpu

F32 = jnp.float32
BF16 = jnp.bfloat16

D_MODEL = 2048
MLSTM_HEADS = 8
HEAD_DIM = 128
MLSTM_WIDTH = MLSTM_HEADS * HEAD_DIM
CONV_WIDTH = 4
ATTN_GROUPS = ((128, 1), (512, 4), (2048, 16))
N_GROUPS = 3
HEADS_PER_GROUP = 4
ATTN_HEADS = HEADS_PER_GROUP * N_GROUPS
ATTN_WIDTH = ATTN_HEADS * HEAD_DIM
GROUP_WIDTH = HEADS_PER_GROUP * HEAD_DIM
ATTN_BLOCK = 128
ATTN_SPAN = 128
REL_BUCKETS = 32
REL_MAX_DIST = 2048
N_EXPERTS = 32
TOP_K = 4
D_FF = D_MODEL
SWIGLU_LIMIT = 7.0
SWIGLU_ALPHA = 1.702
RMS_EPS = 1e-6
NEG = -1e30

COL_QK = 0
COL_V = 2 * MLSTM_WIDTH
COL_O = COL_V + MLSTM_WIDTH
COL_GATES = COL_O + MLSTM_WIDTH
D_MAIN = COL_GATES + 2 * D_MODEL
QKV_WIDTH = 3 * GROUP_WIDTH
LANES = 128
IF_WIDTH = 2 * LANES

MLSTM_CHUNK = 128
MOE_TILE = 512
VMEM_LIMIT = 56 * 1024 * 1024


def _sigmoid(x):
    return 0.5 * jnp.tanh(0.5 * x) + 0.5


def _bf16_parts(x, n_parts):
    parts = []
    for _ in range(n_parts):
        p = x.astype(BF16)
        parts.append(p)
        x = x - p.astype(F32)
    return parts


def _log_sigmoid(x):
    return -(jnp.maximum(-x, 0.0) + jnp.log1p(jnp.exp(-jnp.abs(x))))


def _params(sem):
    return pltpu.CompilerParams(dimension_semantics=sem, vmem_limit_bytes=VMEM_LIMIT)


def _in_proj_kernel(x_ref, g_ref, w_ref, wif_ref, o_ref, oif_ref, h_ref, *, bm, rows):
    @pl.when(pl.program_id(1) == 0)
    def _():
        for r in range(0, bm, rows):
            x = x_ref[r:r + rows, :]
            ms = jnp.mean(x * x, axis=-1, keepdims=True)
            h = (x * lax.rsqrt(ms + RMS_EPS) * g_ref[...]).astype(BF16)
            h_ref[r:r + rows, :] = h
            oif_ref[r:r + rows, :] = jnp.dot(h, wif_ref[...], preferred_element_type=F32)

    o_ref[...] = jnp.dot(h_ref[...], w_ref[...], preferred_element_type=F32).astype(BF16)


def _residue_perm(size, dilation, inverse):
    per = size // dilation
    i = lax.broadcasted_iota(jnp.int32, (size, size), 0)
    j = lax.broadcasted_iota(jnp.int32, (size, size), 1)
    if inverse:
        src = (i & (dilation - 1)) * per + (i >> (dilation.bit_length() - 1))
    else:
        src = (i & (per - 1)) * dilation + (i >> (per.bit_length() - 1))
    return j == src


def _attn_proj_kernel(x_ref, g_ref, w_ref, o_ref, h_ref, *, bm, dilation, rows):
    per = rows // dilation
    if dilation > 1:
        perm = _residue_perm(rows, dilation, False).astype(BF16)
    for c in range(0, bm, rows):
        x = x_ref[c:c + rows, :]
        ms = jnp.mean(x * x, axis=-1, keepdims=True)
        h = (x * lax.rsqrt(ms + RMS_EPS) * g_ref[...]).astype(BF16)
        if dilation > 1:
            h = jnp.dot(perm, h, preferred_element_type=F32).astype(BF16)
        h_ref[c:c + rows, :] = h
    res = jnp.dot(h_ref[...], w_ref[...], preferred_element_type=F32).astype(BF16)
    for c in range(bm // rows):
        for r in range(dilation):
            o_ref[0, r, c * per:(c + 1) * per, :] = res[c * rows + r * per:c * rows + (r + 1) * per, :]


def _attn_proj(x2, g, w_qkv, bsz, s, dilation, *, bm=1024):
    n_blk = s // bm
    per = bm // dilation
    kern = functools.partial(_attn_proj_kernel, bm=bm, dilation=dilation, rows=256)
    return pl.pallas_call(
        kern,
        grid=(bsz, n_blk),
        in_specs=[
            pl.BlockSpec((bm, D_MODEL), lambda b, i: (b * n_blk + i, 0)),
            pl.BlockSpec((1, D_MODEL), lambda b, i: (0, 0)),
            pl.BlockSpec((D_MODEL, QKV_WIDTH), lambda b, i: (0, 0),
                         pipeline_mode=pl.Buffered(1)),
        ],
        out_specs=pl.BlockSpec((1, dilation, per, QKV_WIDTH), lambda b, i: (b, 0, i, 0)),
        out_shape=jax.ShapeDtypeStruct((bsz, dilation, s // dilation, QKV_WIDTH), BF16),
        scratch_shapes=[pltpu.VMEM((bm, D_MODEL), BF16)],
        compiler_params=_params(("arbitrary", "arbitrary")),
        name=f"attn_proj_d{dilation}",
    )(x2, g, w_qkv)


def _in_proj(x2, g, w_main, w_if, *, bm=1024, bn=1024):
    n = x2.shape[0]
    bm = min(bm, n)
    kern = functools.partial(_in_proj_kernel, bm=bm, rows=256)
    return pl.pallas_call(
        kern,
        grid=(n // bm, D_MAIN // bn),
        in_specs=[
            pl.BlockSpec((bm, D_MODEL), lambda i, j: (i, 0)),
            pl.BlockSpec((1, D_MODEL), lambda i, j: (0, 0)),
            pl.BlockSpec((D_MODEL, bn), lambda i, j: (0, j)),
            pl.BlockSpec((D_MODEL, IF_WIDTH), lambda i, j: (0, 0)),
        ],
        out_specs=[
            pl.BlockSpec((bm, bn), lambda i, j: (i, j)),
            pl.BlockSpec((bm, IF_WIDTH), lambda i, j: (i, 0)),
        ],
        out_shape=[
            jax.ShapeDtypeStruct((n, D_MAIN), BF16),
            jax.ShapeDtypeStruct((n, IF_WIDTH), F32),
        ],
        scratch_shapes=[pltpu.VMEM((bm, D_MODEL), BF16)],
        compiler_params=_params(("arbitrary", "arbitrary")),
        name="in_proj",
    )(x2, g, w_main, w_if)


def _mlstm_kernel(qk_ref, v_ref, og_ref, if_ref, cw_ref, cb_ref, gb_ref, ng_ref, out_ref,
                  tail_ref, c_ref, m_ref, *, chunk):
    L = chunk
    dh = HEAD_DIM

    @pl.when(pl.program_id(1) == 0)
    def _():
        tail_ref[...] = jnp.zeros_like(tail_ref)
        c_ref[...] = jnp.zeros_like(c_ref)
        m_ref[...] = jnp.zeros_like(m_ref)

    cur = qk_ref[0].astype(F32)
    ext = jnp.concatenate([tail_ref[...], cur], axis=0)
    acc = cur * cw_ref[CONV_WIDTH - 1:CONV_WIDTH, :] + cb_ref[...]
    for s in range(1, CONV_WIDTH):
        acc = acc + ext[8 - s:8 - s + L, :] * cw_ref[CONV_WIDTH - 1 - s:CONV_WIDTH - s, :]
    tail_ref[...] = cur[L - 8:, :]
    qk = acc * _sigmoid(acc)
    q_all = qk[:, :MLSTM_WIDTH].astype(BF16)
    k_all = (qk[:, MLSTM_WIDTH:] * (dh ** -0.5)).astype(BF16)

    gi = if_ref[0][:, :LANES] + gb_ref[:, :LANES]
    lf = _log_sigmoid(if_ref[0][:, LANES:] + gb_ref[:, LANES:])
    row = lax.broadcasted_iota(jnp.int32, (L, L), 0)
    col = lax.broadcasted_iota(jnp.int32, (L, L), 1)
    causal = col <= row
    tri = causal.astype(F32)
    cum = jnp.dot(tri, lf, preferred_element_type=F32, precision=lax.Precision.HIGHEST)
    rmat = gi - cum
    rmat_t = rmat.T
    ones_blk = jnp.ones((L, dh), BF16)

    for h in range(MLSTM_HEADS):
        hs = slice(h * dh, (h + 1) * dh)
        b = cum[:, h:h + 1]
        r_col = rmat[:, h:h + 1]
        r_row = rmat_t[h:h + 1, :]
        m_prev = m_ref[h:h + 1, 0:1]
        logw = jnp.where(causal, b + r_row, NEG)
        inter = b + m_prev
        m_out = jnp.maximum(inter, jnp.max(logw, axis=1, keepdims=True))
        p = jnp.exp(logw - m_out)
        qh = q_all[:, hs]
        kh = k_all[:, hs]
        s = lax.dot_general(qh, kh, (((1,), (1,)), ((), ())), preferred_element_type=F32)
        w = (p * s).astype(BF16)
        vaug = jnp.concatenate([v_ref[0][:, hs], ones_blk], axis=1)
        c_prev = c_ref[h]
        nd = (jnp.dot(w, vaug, preferred_element_type=F32)
              + jnp.exp(inter - m_out)
              * jnp.dot(qh, c_prev.astype(BF16), preferred_element_type=F32))
        hh = nd[:, :dh] / jnp.maximum(jnp.abs(nd[:, dh:]), jnp.exp(-m_out))

        b_last = b[L - 1:L, :]
        m_new = b_last + jnp.maximum(m_prev, jnp.max(r_col, axis=0, keepdims=True))
        ws = jnp.exp(b_last + r_col - m_new)
        decay = jnp.exp(b_last + m_prev - m_new)
        wv = (ws * vaug.astype(F32)).astype(BF16)
        kv = lax.dot_general(kh, wv, (((0,), (0,)), ((), ())), preferred_element_type=F32)
        c_ref[h] = decay * c_prev + kv
        m_ref[h:h + 1, :] = jnp.broadcast_to(m_new, (1, LANES))

        y = hh * lax.rsqrt(jnp.mean(hh * hh, axis=-1, keepdims=True) + RMS_EPS) * ng_ref[:, hs]
        y = y * _sigmoid(og_ref[0][:, hs].astype(F32))
        out_ref[0, :, hs] = y.astype(BF16)


def _mlstm(proj3, ifg3, conv_w, conv_b, gate_b, norm_g, *, chunk=MLSTM_CHUNK):
    bsz, s, _ = proj3.shape
    w = MLSTM_WIDTH
    kern = functools.partial(_mlstm_kernel, chunk=chunk)
    return pl.pallas_call(
        kern,
        grid=(bsz, s // chunk),
        in_specs=[
            pl.BlockSpec((1, chunk, 2 * w), lambda b, c: (b, c, COL_QK // (2 * w))),
            pl.BlockSpec((1, chunk, w), lambda b, c: (b, c, COL_V // w)),
            pl.BlockSpec((1, chunk, w), lambda b, c: (b, c, COL_O // w)),
            pl.BlockSpec((1, chunk, IF_WIDTH), lambda b, c: (b, c, 0)),
            pl.BlockSpec((CONV_WIDTH, 2 * w), lambda b, c: (0, 0)),
            pl.BlockSpec((1, 2 * w), lambda b, c: (0, 0)),
            pl.BlockSpec((1, IF_WIDTH), lambda b, c: (0, 0)),
            pl.BlockSpec((1, w), lambda b, c: (0, 0)),
        ],
        out_specs=pl.BlockSpec((1, chunk, w), lambda b, c: (b, c, 0)),
        out_shape=jax.ShapeDtypeStruct((bsz, s, w), BF16),
        scratch_shapes=[
            pltpu.VMEM((8, 2 * w), F32),
            pltpu.VMEM((MLSTM_HEADS, HEAD_DIM, 2 * HEAD_DIM), F32),
            pltpu.VMEM((MLSTM_HEADS, LANES), F32),
        ],
        compiler_params=_params(("arbitrary", "arbitrary")),
        name="mlstm",
    )(proj3, proj3, proj3, ifg3, conv_w, conv_b, gate_b, norm_g)


def _attn_kernel(q_ref, kp_ref, kc_ref, vp_ref, vc_ref, bias_ref, o_ref, lse_ref):
    qb = ATTN_BLOCK
    dh = HEAD_DIM
    first = pl.program_id(2) == 0
    kcol = lax.broadcasted_iota(jnp.int32, (qb, 2 * qb), 1)
    dead = jnp.logical_and(first, kcol < qb)
    lane = lax.broadcasted_iota(jnp.int32, (qb, LANES), 1)
    lse_all = jnp.zeros((qb, LANES), F32)
    for j in range(HEADS_PER_GROUP):
        hs = slice(j * dh, (j + 1) * dh)
        q = q_ref[0, 0][:, hs]
        k = jnp.concatenate([kp_ref[0, 0][:, hs], kc_ref[0, 0][:, hs]], axis=0)
        v = jnp.concatenate([vp_ref[0, 0][:, hs], vc_ref[0, 0][:, hs]], axis=0)
        s = lax.dot_general(q, k, (((1,), (1,)), ((), ())), preferred_element_type=F32)
        s = s * (dh ** -0.5) + bias_ref[j]
        s = jnp.where(dead, NEG, s)
        mx = jnp.max(s, axis=1, keepdims=True)
        p = jnp.exp(s - mx)
        den = jnp.sum(p, axis=1, keepdims=True)
        o = jnp.dot(p.astype(BF16), v, preferred_element_type=F32) / den
        o_ref[0, 0, :, hs] = o.astype(BF16)
        lse_all = jnp.where(lane == j, mx + jnp.log(den), lse_all)
    lse_ref[0, 0] = lse_all


def _attn_group(qkv, bias, g):
    bsz, dilation, n, _ = qkv.shape
    nb = n // ATTN_BLOCK
    gw = GROUP_WIDTH

    def spec(part, prev):
        def imap(b, r, i):
            return (b, r, jnp.maximum(i - 1, 0) if prev else i, part)

        return pl.BlockSpec((1, 1, ATTN_BLOCK, gw), imap)

    return pl.pallas_call(
        _attn_kernel,
        grid=(bsz, dilation, nb),
        in_specs=[
            spec(0, False),
            spec(1, True), spec(1, False),
            spec(2, True), spec(2, False),
            pl.BlockSpec((HEADS_PER_GROUP, ATTN_BLOCK, 2 * ATTN_BLOCK), lambda b, r, i: (0, 0, 0)),
        ],
        out_specs=[
            pl.BlockSpec((1, 1, ATTN_BLOCK, gw), lambda b, r, i: (b, r, i, 0)),
            pl.BlockSpec((1, 1, ATTN_BLOCK, LANES), lambda b, r, i: (b, r, i, 0)),
        ],
        out_shape=[
            jax.ShapeDtypeStruct((bsz, dilation, n, gw), BF16),
            jax.ShapeDtypeStruct((bsz, dilation, n, LANES), F32),
        ],
        compiler_params=_params(("arbitrary", "arbitrary", "arbitrary")),
        name=f"attn_g{g}",
    )(qkv, qkv, qkv, qkv, qkv, bias)


def _t5_bucket(dist):
    max_exact = REL_BUCKETS // 2
    d_f = jnp.maximum(dist, 1).astype(F32)
    large = max_exact + (jnp.log(d_f / max_exact) / math.log(REL_MAX_DIST / max_exact)
                         * (REL_BUCKETS - max_exact)).astype(jnp.int32)
    large = jnp.minimum(large, REL_BUCKETS - 1)
    return jnp.where(dist < max_exact, dist, large)


def _attn_bias(rel_bias, g, dilation):
    span = ATTN_SPAN
    buckets = _t5_bucket(jnp.arange(span + 1, dtype=jnp.int32) * dilation)
    vec = rel_bias[buckets][:, g * HEADS_PER_GROUP:(g + 1) * HEADS_PER_GROUP].T.astype(F32)
    qpos = jnp.arange(ATTN_BLOCK)[:, None]
    kpos = jnp.arange(2 * ATTN_BLOCK)[None, :]
    dist = qpos + ATTN_BLOCK - kpos
    valid = (dist >= 0) & (dist <= span)
    onehot = (dist[:, :, None] == jnp.arange(span + 1)[None, None, :]).astype(F32)
    table = jnp.einsum('qkj,hj->hqk', onehot, vec, precision=lax.Precision.HIGHEST)
    return jnp.where(valid[None], table, NEG)


def _merge_kernel(hm_ref, o0_ref, o1_ref, o2_ref, l0_ref, l1_ref, l2_ref, gm_ref, ga_ref, x_ref,
                  wbm_ref, wba_ref, wo_ref, ng_ref, rw_ref, rb_ref,
                  x1_ref, h2_ref, idx_ref, gate_ref):
    dh = HEAD_DIM
    bm = x_ref.shape[0]

    def token_order(ref):
        dilation = ref.shape[1]
        if dilation == 1:
            return ref[0, 0]
        perm = _residue_perm(bm, dilation, True).astype(BF16)
        blk = ref[0].reshape(bm, ref.shape[3])
        parts = [blk] if blk.dtype == BF16 else _bf16_parts(blk, 3)
        out = None
        for p in parts:
            moved = jnp.dot(perm, p, preferred_element_type=F32)
            out = moved if out is None else out + moved
        return out

    l0, l1, l2 = token_order(l0_ref), token_order(l1_ref), token_order(l2_ref)
    o0, o1, o2 = token_order(o0_ref), token_order(o1_ref), token_order(o2_ref)
    mx = jnp.maximum(jnp.maximum(l0, l1), l2)
    e0, e1, e2 = jnp.exp(l0 - mx), jnp.exp(l1 - mx), jnp.exp(l2 - mx)
    den = e0 + e1 + e2
    w0, w1, w2 = e0 / den, e1 / den, e2 / den
    parts = []
    for j in range(HEADS_PER_GROUP):
        hs = slice(j * dh, (j + 1) * dh)
        parts.append(w0[:, j:j + 1] * o0[:, hs] + w1[:, j:j + 1] * o1[:, hs]
                     + w2[:, j:j + 1] * o2[:, hs])
    ha = jnp.concatenate(parts, axis=1).astype(BF16)
    ym = jnp.dot(hm_ref[...], wbm_ref[...], preferred_element_type=F32)
    ya = jnp.dot(ha, wba_ref[...], preferred_element_type=F32)
    merged = (_sigmoid(gm_ref[...].astype(F32)) * ym + _sigmoid(ga_ref[...].astype(F32)) * ya)
    x1 = x_ref[...] + jnp.dot(merged.astype(BF16), wo_ref[...], preferred_element_type=F32)
    x1_ref[...] = x1
    h2 = x1 * lax.rsqrt(jnp.mean(x1 * x1, axis=-1, keepdims=True) + RMS_EPS) * ng_ref[...]
    _store_token_tiles(h2_ref, _pack_pairs(h2))
    h_hi, h_lo = _bf16_parts(h2, 2)
    logits = jnp.dot(jnp.concatenate([h_hi, h_lo, h_hi], axis=1), rw_ref[...],
                     preferred_element_type=F32) + rb_ref[...]
    lane = lax.broadcasted_iota(jnp.int32, logits.shape, 1)
    idx_all = jnp.zeros(logits.shape, jnp.int32)
    val_all = jnp.zeros(logits.shape, F32)
    top0 = None
    esum = None
    for k in range(TOP_K):
        m = jnp.max(logits, axis=1, keepdims=True)
        sel = jnp.min(jnp.where(logits == m, lane, LANES), axis=1, keepdims=True)
        if k == 0:
            top0 = m
        e = jnp.exp(m - top0)
        esum = e if k == 0 else esum + e
        idx_all = jnp.where(lane == k, sel, idx_all)
        val_all = jnp.where(lane == k, e, val_all)
        logits = jnp.where(lane == sel, -jnp.inf, logits)
    idx_ref[...] = idx_all
    gate_ref[...] = val_all / esum


def _merge(hm, outs, lses, proj, x2, wbm, wba, wo, ng, rw, rb, *, bm=256):
    n = x2.shape[0]
    d = D_MODEL
    gcol = COL_GATES // d
    n_blk = outs[0].shape[2] // bm

    def rows(width):
        return pl.BlockSpec((bm, width), lambda i: (i, 0))

    def full(a, b):
        return pl.BlockSpec((a, b), lambda i: (0, 0), pipeline_mode=pl.Buffered(1))

    def residue(arr):
        dilation, width = arr.shape[1], arr.shape[3]
        return pl.BlockSpec((1, dilation, bm // dilation, width),
                            lambda i: (i // n_blk, 0, i % n_blk, 0))

    return pl.pallas_call(
        _merge_kernel,
        grid=(n // bm,),
        in_specs=[
            rows(MLSTM_WIDTH),
            residue(outs[0]), residue(outs[1]), residue(outs[2]),
            residue(lses[0]), residue(lses[1]), residue(lses[2]),
            pl.BlockSpec((bm, d), lambda i: (i, gcol)),
            pl.BlockSpec((bm, d), lambda i: (i, gcol + 1)),
            rows(d),
            full(MLSTM_WIDTH, d), full(GROUP_WIDTH, d), full(d, d),
            full(1, d), full(3 * d, LANES), full(1, LANES),
        ],
        out_specs=[rows(d), pl.BlockSpec((bm * PACK_ROWS, LANES), lambda i: (i, 0)),
                   rows(LANES), rows(LANES)],
        out_shape=[
            jax.ShapeDtypeStruct((n, d), F32),
            jax.ShapeDtypeStruct((n * PACK_ROWS, LANES), jnp.uint32),
            jax.ShapeDtypeStruct((n, LANES), jnp.int32),
            jax.ShapeDtypeStruct((n, LANES), F32),
        ],
        compiler_params=_params(("arbitrary",)),
        name="merge",
    )(hm, outs[0], outs[1], outs[2], lses[0], lses[1], lses[2], proj, proj, x2,
      wbm, wba, wo, ng, rw, rb)


PACK_ROWS = 8


def _pack_pairs(x):
    w = x.shape[1] // 2
    lo = lax.bitcast_convert_type(x[:, :w].astype(BF16).astype(F32), jnp.uint32) >> 16
    hi = lax.bitcast_convert_type(x[:, w:].astype(BF16).astype(F32), jnp.uint32)
    return (hi & jnp.uint32(0xFFFF0000)) | lo


def _unpack_pairs(words):
    lo = lax.bitcast_convert_type(words << 16, F32)
    hi = lax.bitcast_convert_type(words & jnp.uint32(0xFFFF0000), F32)
    return lo, hi


def _store_token_tiles(ref, words):
    rows = words.shape[0]
    for s in range(PACK_ROWS):
        ref[pl.ds(s, rows, stride=PACK_ROWS), :] = words[:, s * LANES:(s + 1) * LANES]


def _dispatch_kernel(fs_ref, fl_ref, nu_ref, dest_ref, hp_ref, xb_hbm, zero_ref, sem, zsem, *,
                     tokens, tm, n_blk):
    pr = PACK_ROWS
    fill_sizes = [1 << b for b in reversed(range((tm - 1).bit_length()))]

    def fill_copies(e):
        off = fs_ref[e]
        for p in fill_sizes:
            take = (fl_ref[e] & p) != 0
            dst = xb_hbm.at[pl.ds(pl.multiple_of(off * pr, pr), p * pr), :]
            yield take, pltpu.make_async_copy(zero_ref.at[pl.ds(0, p * pr), :], dst, zsem)
            off = off + jnp.where(take, p, 0)

    def block_copy(b):
        dst = xb_hbm.at[pl.ds(pl.multiple_of(b * (tm * pr), tm * pr), tm * pr), :]
        return pltpu.make_async_copy(zero_ref, dst, zsem)

    @pl.when(pl.program_id(0) == 0)
    def _():
        zero_ref[...] = jnp.zeros_like(zero_ref)
        for start in (True, False):
            def per_expert(e, carry):
                for take, cp in fill_copies(e):
                    @pl.when(take)
                    def _():
                        cp.start() if start else cp.wait()
                return carry

            def per_block(b, carry):
                block_copy(b).start() if start else block_copy(b).wait()
                return carry

            lax.fori_loop(0, N_EXPERTS, per_expert, 0)
            lax.fori_loop(nu_ref[0], n_blk, per_block, 0)

    def issue(t, carry):
        src = hp_ref.at[pl.ds(pl.multiple_of(t * pr, pr), pr), :]
        for k in range(TOP_K):
            slot = dest_ref[t * TOP_K + k]
            pltpu.make_async_copy(src, xb_hbm.at[pl.ds(pl.multiple_of(slot * pr, pr), pr), :],
                                  sem).start(priority=k % 2)
        return carry

    lax.fori_loop(0, tokens, issue, 0)
    for k in range(TOP_K):
        pltpu.make_async_copy(hp_ref, xb_hbm.at[pl.ds(0, tokens * pr), :], sem).wait()


def _dispatch(fill_start, fill_len, n_used, dest, h2p, n_pad, *, tokens=256, tm=MOE_TILE):
    pr = PACK_ROWS
    n_tok = h2p.shape[0] // pr
    kern = functools.partial(_dispatch_kernel, tokens=tokens, tm=tm, n_blk=n_pad // tm)
    return pl.pallas_call(
        kern,
        grid_spec=pltpu.PrefetchScalarGridSpec(
            num_scalar_prefetch=3,
            grid=(n_tok // tokens,),
            in_specs=[
                pl.BlockSpec((tokens * TOP_K,), lambda i, fs, fl, nu: (i,),
                             memory_space=pltpu.SMEM),
                pl.BlockSpec((tokens * pr, LANES), lambda i, fs, fl, nu: (i, 0)),
            ],
            out_specs=pl.BlockSpec(memory_space=pl.ANY),
            scratch_shapes=[pltpu.VMEM((tm * pr, LANES), jnp.uint32),
                            pltpu.SemaphoreType.DMA(()), pltpu.SemaphoreType.DMA(())],
        ),
        out_shape=jax.ShapeDtypeStruct((n_pad * pr, LANES), jnp.uint32),
        compiler_params=_params(("arbitrary",)),
        name="dispatch",
    )(fill_start, fill_len, n_used, dest, h2p)


def _cast_rows(src_ref, dst_ref, rows):
    total = src_ref.shape[0]

    def body(i, carry):
        r = pl.multiple_of(i * rows, rows)
        dst_ref[pl.ds(r, rows), :] = src_ref[pl.ds(r, rows), :].astype(BF16)
        return carry

    lax.fori_loop(0, total // rows, body, 0)


def _stream_expert_weights(be_ref, nu_ref, ne_ref, copies, convert):
    k, j = pl.program_id(0), pl.program_id(1)
    expert = be_ref[j]
    used = j < nu_ref[0]
    run_start = jnp.logical_or(j == 0, expert != be_ref[jnp.maximum(j - 1, 0)])

    @pl.when(jnp.logical_and(k == 0, j == 0))
    def _():
        for cp in copies(expert, k):
            cp.start()

    @pl.when(jnp.logical_and(used, run_start))
    def _():
        for cp in copies(expert, k):
            cp.wait()
        convert()
        last_run = ne_ref[j] < 0
        nxt_e = jnp.where(last_run, be_ref[0], ne_ref[j])
        nxt_k = jnp.where(last_run, k + 1, k)

        @pl.when(nxt_k < pl.num_programs(0))
        def _():
            for cp in copies(nxt_e, nxt_k):
                cp.start()

    return used


def _per_block_rows(used, bv_ref, o_ref, compute, store=None, rows_per_token=1):
    tm = o_ref.shape[0] // rows_per_token
    few = bv_ref[pl.program_id(1)] <= tm // 2

    def run(rows):
        res = compute(rows)
        head = o_ref.at[pl.ds(0, rows * rows_per_token), :]
        if store is None:
            head[...] = res
        else:
            store(head, res)
        if rows < tm:
            tail = o_ref.at[pl.ds(rows * rows_per_token, (tm - rows) * rows_per_token), :]
            tail[...] = jnp.zeros_like(tail)

    @pl.when(jnp.logical_and(used, few))
    def _():
        run(tm // 2)

    @pl.when(jnp.logical_and(used, jnp.logical_not(few)))
    def _():
        run(tm)

    @pl.when(jnp.logical_not(used))
    def _():
        o_ref[...] = jnp.zeros_like(o_ref)


def _gate_up_kernel(be_ref, nu_ref, ne_ref, bv_ref, x_ref, w_hbm, bg_ref, bu_ref, o_ref,
                    stage_ref, w_bf, sem):
    tf = o_ref.shape[1]
    kt = pl.num_programs(0)

    def copies(expert, k):
        return [pltpu.make_async_copy(
            w_hbm.at[expert, :, pl.ds(pl.multiple_of((half * kt + k) * tf, tf), tf)],
            stage_ref.at[half], sem.at[half]) for half in range(2)]

    def convert():
        for half in range(2):
            _cast_rows(stage_ref.at[half], w_bf.at[half], 256)

    used = _stream_expert_weights(be_ref, nu_ref, ne_ref, copies, convert)

    def compute(rows):
        halves = [_unpack_pairs(x_ref[pl.ds(s, rows, stride=PACK_ROWS), :])
                  for s in range(PACK_ROWS)]
        x = jnp.concatenate([h[0].astype(BF16) for h in halves]
                            + [h[1].astype(BF16) for h in halves], axis=1)
        gate = jnp.dot(x, w_bf[0], preferred_element_type=F32) + bg_ref[0]
        up = jnp.dot(x, w_bf[1], preferred_element_type=F32) + bu_ref[0]
        gate = jnp.minimum(gate, SWIGLU_LIMIT)
        up = jnp.clip(up, -SWIGLU_LIMIT, SWIGLU_LIMIT)
        glu = gate * _sigmoid(SWIGLU_ALPHA * gate)
        return ((up + 1.0) * glu).astype(BF16)

    _per_block_rows(used, bv_ref, o_ref, compute)


def _used_block(j, nu):
    return jnp.minimum(j, nu[0] - 1)


def _gate_up(sched, xb, w_gate_up, b_gate_up3, *, tm=MOE_TILE, tf=1024):
    n_pad, d = xb.shape[0] // PACK_ROWS, D_MODEL
    kt = D_FF // tf

    def bspec(off):
        return pl.BlockSpec((1, 1, tf),
                            lambda k, j, be, nu, ne, bv: (be[_used_block(j, nu)], 0, off + k))

    return pl.pallas_call(
        _gate_up_kernel,
        grid_spec=pltpu.PrefetchScalarGridSpec(
            num_scalar_prefetch=4,
            grid=(kt, n_pad // tm),
            in_specs=[
                pl.BlockSpec((tm * PACK_ROWS, LANES),
                             lambda k, j, be, nu, ne, bv: (_used_block(j, nu), 0)),
                pl.BlockSpec(memory_space=pl.ANY),
                bspec(0), bspec(kt),
            ],
            out_specs=pl.BlockSpec((tm, tf), lambda k, j, be, nu, ne, bv: (j, k)),
            scratch_shapes=[pltpu.VMEM((2, d, tf), F32), pltpu.VMEM((2, d, tf), BF16),
                            pltpu.SemaphoreType.DMA((2,))],
        ),
        out_shape=jax.ShapeDtypeStruct((n_pad, D_FF), BF16),
        compiler_params=_params(("arbitrary", "arbitrary")),
        name="gate_up",
    )(*sched, xb, w_gate_up, b_gate_up3, b_gate_up3)


def _down_kernel(be_ref, nu_ref, ne_ref, bv_ref, h_ref, w_hbm, bd_ref, o_ref, stage_ref, w_bf,
                 sem):
    def copies(expert, k):
        del k
        return [pltpu.make_async_copy(w_hbm.at[expert], stage_ref, sem.at[0])]

    def convert():
        _cast_rows(stage_ref, w_bf, 256)

    used = _stream_expert_weights(be_ref, nu_ref, ne_ref, copies, convert)

    def compute(rows):
        y = jnp.dot(h_ref[0:rows, :], w_bf[...], preferred_element_type=F32) + bd_ref[0]
        return _pack_pairs(y)

    _per_block_rows(used, bv_ref, o_ref, compute, store=_store_token_tiles,
                    rows_per_token=PACK_ROWS)


def _down(sched, hb, w_down, b_down3, *, tm=MOE_TILE):
    n_pad, f = hb.shape
    d = D_MODEL
    return pl.pallas_call(
        _down_kernel,
        grid_spec=pltpu.PrefetchScalarGridSpec(
            num_scalar_prefetch=4,
            grid=(1, n_pad // tm),
            in_specs=[
                pl.BlockSpec((tm, f), lambda k, j, be, nu, ne, bv: (_used_block(j, nu), 0)),
                pl.BlockSpec(memory_space=pl.ANY),
                pl.BlockSpec((1, 1, d),
                             lambda k, j, be, nu, ne, bv: (be[_used_block(j, nu)], 0, 0)),
            ],
            out_specs=pl.BlockSpec((tm * PACK_ROWS, LANES), lambda k, j, be, nu, ne, bv: (j, 0)),
            scratch_shapes=[pltpu.VMEM((f, d), F32), pltpu.VMEM((f, d), BF16),
                            pltpu.SemaphoreType.DMA((1,))],
        ),
        out_shape=jax.ShapeDtypeStruct((n_pad * PACK_ROWS, LANES), jnp.uint32),
        compiler_params=_params(("arbitrary", "arbitrary")),
        name="down",
    )(*sched, hb, w_down, b_down3)


def _combine_kernel(dest_ref, dest_next_ref, y_hbm, gate_ref, x1_ref, ng_ref, o_ref, buf_ref, sem,
                    *, rows):
    pr = PACK_ROWS
    half = x1_ref.shape[1] // 2
    i = pl.program_id(0)
    slot_rows = TOP_K * rows * pr

    def gather(idx_ref, buf):
        def issue(t, carry):
            for k in range(TOP_K):
                slot = idx_ref[t * TOP_K + k]
                pltpu.make_async_copy(
                    y_hbm.at[pl.ds(pl.multiple_of(slot * pr, pr), pr), :],
                    buf_ref.at[pl.ds(pl.multiple_of(buf * slot_rows + (k * rows + t) * pr, pr), pr), :],
                    sem.at[buf]).start(priority=k % 2)
            return carry

        lax.fori_loop(0, rows, issue, 0)

    @pl.when(i == 0)
    def _():
        gather(dest_ref, 0)

    @pl.when(i + 1 < pl.num_programs(0))
    def _():
        gather(dest_next_ref, (i + 1) % 2)

    cur = i % 2
    base = pl.multiple_of(cur * slot_rows, slot_rows)
    pltpu.make_async_copy(y_hbm.at[pl.ds(0, slot_rows), :],
                          buf_ref.at[pl.ds(base, slot_rows), :], sem.at[cur]).wait()
    gates = [gate_ref[:, k:k + 1] for k in range(TOP_K)]
    lows, highs = [], []
    for s in range(pr):
        lo = x1_ref[:, s * LANES:(s + 1) * LANES]
        hi = x1_ref[:, half + s * LANES:half + (s + 1) * LANES]
        for k in range(TOP_K):
            y_lo, y_hi = _unpack_pairs(
                buf_ref[pl.ds(base + (k * rows * pr + s), rows, stride=pr), :])
            lo = lo + gates[k] * y_lo
            hi = hi + gates[k] * y_hi
        lows.append(lo)
        highs.append(hi)
    acc = jnp.concatenate(lows + highs, axis=1)
    o_ref[...] = acc * lax.rsqrt(jnp.mean(acc * acc, axis=-1, keepdims=True) + RMS_EPS) * ng_ref[...]


def _combine(dest, yb, gates, x1, ng, *, rows=256):
    n, d = x1.shape
    kern = functools.partial(_combine_kernel, rows=rows)
    n_steps = n // rows
    return pl.pallas_call(
        kern,
        grid=(n_steps,),
        in_specs=[
            pl.BlockSpec((rows * TOP_K,), lambda i: (i,), memory_space=pltpu.SMEM),
            pl.BlockSpec((rows * TOP_K,), lambda i: (jnp.minimum(i + 1, n_steps - 1),),
                         memory_space=pltpu.SMEM),
            pl.BlockSpec(memory_space=pl.ANY),
            pl.BlockSpec((rows, LANES), lambda i: (i, 0)),
            pl.BlockSpec((rows, d), lambda i: (i, 0)),
            pl.BlockSpec((1, d), lambda i: (0, 0)),
        ],
        out_specs=pl.BlockSpec((rows, d), lambda i: (i, 0)),
        out_shape=jax.ShapeDtypeStruct((n, d), F32),
        scratch_shapes=[pltpu.VMEM((2 * TOP_K * rows * PACK_ROWS, LANES), jnp.uint32),
                        pltpu.SemaphoreType.DMA((2,))],
        compiler_params=_params(("arbitrary",)),
        name="combine",
    )(dest, dest, yb, gates, x1, ng)


def _routing(top_idx, tm):
    n_tok = top_idx.shape[0]
    n_asg = n_tok * TOP_K
    e_flat = top_idx.reshape(n_asg)
    onehot = (e_flat[:, None] == jnp.arange(N_EXPERTS, dtype=jnp.int32)[None, :]).astype(jnp.int32)
    cum = jnp.cumsum(onehot, axis=0)
    rank = jnp.sum(onehot * cum, axis=1) - 1
    counts = cum[-1]
    padded = (counts + tm - 1) // tm * tm
    pend = jnp.cumsum(padded)
    pstart = pend - padded
    dest = (pstart[e_flat] + rank).astype(jnp.int32)
    n_blk = -(-(n_asg + N_EXPERTS * (tm - 1)) // tm)
    n_pad = n_blk * tm
    fill_start = (pstart + counts).astype(jnp.int32)
    fill_len = (padded - counts).astype(jnp.int32)
    block_start = jnp.arange(n_blk, dtype=jnp.int32) * tm
    block_e = jnp.minimum(
        jnp.sum((pend[None, :] <= block_start[:, None]).astype(jnp.int32), axis=1),
        N_EXPERTS - 1).astype(jnp.int32)
    n_used = (pend[-1:] // tm).astype(jnp.int32)
    run_end = pend[block_e] // tm
    next_e = jnp.where(run_end < n_used[0], block_e[jnp.minimum(run_end, n_blk - 1)], -1)
    block_rows = jnp.clip(fill_start[block_e] - block_start, 0, tm)
    sched = (block_e, n_used, next_e.astype(jnp.int32), block_rows.astype(jnp.int32))
    return dest, fill_start, fill_len, sched, n_pad


def _layer(x, norm_mix_g, w_in, conv_w, conv_b, igate_b, fgate_b, mlstm_norm_g, rel_bias,
           w_branch_mlstm, w_branch_attn, w_out, norm_moe_g, router_w, router_b,
           w_gate_up, b_gate_up, w_down, b_down, out_norm_g):
    bsz, s, d = x.shape
    n = bsz * s
    x2 = x.reshape(n, d)
    w2, w1 = 2 * MLSTM_WIDTH, MLSTM_WIDTH
    o_qk, o_v, o_o = 0, w2, w2 + w1
    o_i = o_o + w1
    o_f = o_i + MLSTM_HEADS
    o_qa = o_f + MLSTM_HEADS
    o_ka, o_va = o_qa + ATTN_WIDTH, o_qa + 2 * ATTN_WIDTH
    o_g = o_va + ATTN_WIDTH
    w_main = jnp.concatenate([w_in[:, o_qk:o_i], w_in[:, o_g:]], axis=1).astype(BF16)
    zpad = jnp.zeros((d, LANES - MLSTM_HEADS), w_in.dtype)
    w_if = jnp.concatenate([w_in[:, o_i:o_f], zpad, w_in[:, o_f:o_qa], zpad], axis=1).astype(BF16)
    bpad = jnp.zeros((LANES - MLSTM_HEADS,), F32)
    gate_b = jnp.concatenate([igate_b, bpad, fgate_b, bpad]).reshape(1, IF_WIDTH)

    norm_g = norm_mix_g.reshape(1, d)
    proj, ifg = _in_proj(x2, norm_g, w_main, w_if)
    proj3 = proj.reshape(bsz, s, D_MAIN)
    hm = _mlstm(proj3, ifg.reshape(bsz, s, IF_WIDTH), conv_w, conv_b.reshape(1, -1), gate_b,
                mlstm_norm_g.reshape(1, -1)).reshape(n, MLSTM_WIDTH)

    outs, lses = [], []
    gw = GROUP_WIDTH
    for g, (_, dilation) in enumerate(ATTN_GROUPS):
        w_qkv = jnp.concatenate(
            [w_in[:, o + g * gw:o + (g + 1) * gw] for o in (o_qa, o_ka, o_va)], axis=1).astype(BF16)
        qkv = _attn_proj(x2, norm_g, w_qkv, bsz, s, dilation)
        o_g_, lse_g = _attn_group(qkv, _attn_bias(rel_bias, g, dilation), g)
        outs.append(o_g_)
        lses.append(lse_g)

    rw = jnp.concatenate([router_w, jnp.zeros((d, LANES - N_EXPERTS), F32)], axis=1)
    rw_hi = rw.astype(BF16)
    rw_lo = (rw - rw_hi.astype(F32)).astype(BF16)
    rw = jnp.concatenate([rw_hi, rw_hi, rw_lo], axis=0)
    rb = jnp.concatenate([router_b, jnp.full((LANES - N_EXPERTS,), NEG, F32)]).reshape(1, LANES)
    x1, h2, idx, gates = _merge(
        hm, outs, lses, proj, x2, w_branch_mlstm.astype(BF16), w_branch_attn.astype(BF16),
        w_out.astype(BF16), norm_moe_g.reshape(1, d), rw, rb)

    dest, fill_start, fill_len, sched, n_pad = _routing(idx[:, :TOP_K], MOE_TILE)
    xb = _dispatch(fill_start, fill_len, sched[1], dest, h2, n_pad)
    hb = _gate_up(sched, xb, w_gate_up, b_gate_up.reshape(N_EXPERTS, 1, 2 * D_FF))
    yb = _down(sched, hb, w_down, b_down.reshape(N_EXPERTS, 1, d))
    out = _combine(dest, yb, gates, x1, out_norm_g.reshape(1, d))
    return out.reshape(bsz, s, d)


def kernel(x, norm_mix_g, w_in, conv_w, conv_b, igate_b, fgate_b, mlstm_norm_g, rel_bias,
           w_branch_mlstm, w_branch_attn, w_out, norm_moe_g, router_w, router_b,
           w_gate_up, b_gate_up, w_down, b_down, norm_final_g):
    assert w_in.shape[0] == 1, "single-layer block"
    return _layer(x, norm_mix_g[0], w_in[0], conv_w[0], conv_b[0], igate_b[0], fgate_b[0],
                  mlstm_norm_g[0], rel_bias, w_branch_mlstm[0], w_branch_attn[0], w_out[0],
                  norm_moe_g[0], router_w[0], router_b[0], w_gate_up[0], b_gate_up[0],
                  w_down[0], b_down[0], norm_final_g)
```

```python
import functools
import math

import jax
import jax.numpy as jnp
from jax import lax
from jax.experimental import pallas as pl
from jax.experimental.pallas import tpu as pltpu

F32 = jnp.float32
BF16 = jnp.bfloat16

D_MODEL = 2048
MLSTM_HEADS = 8
HEAD_DIM = 128
MLSTM_WIDTH = MLSTM_HEADS * HEAD_DIM
CONV_WIDTH = 4
ATTN_GROUPS = ((128, 1), (512, 4), (2048, 16))
N_GROUPS = 3
HEADS_PER_GROUP = 4
ATTN_HEADS = HEADS_PER_GROUP * N_GROUPS
ATTN_WIDTH = ATTN_HEADS * HEAD_DIM
GROUP_WIDTH = HEADS_PER_GROUP * HEAD_DIM
ATTN_BLOCK = 128
ATTN_SPAN = 128
REL_BUCKETS = 32
REL_MAX_DIST = 2048
N_EXPERTS = 32
TOP_K = 4
D_FF = D_MODEL
SWIGLU_LIMIT = 7.0
SWIGLU_ALPHA = 1.702
RMS_EPS = 1e-6
NEG = -1e30

COL_QK = 0
COL_V = 2 * MLSTM_WIDTH
COL_O = COL_V + MLSTM_WIDTH
COL_GATES = COL_O + MLSTM_WIDTH
D_MAIN = COL_GATES + 2 * D_MODEL
QKV_WIDTH = 3 * GROUP_WIDTH
LANES = 128
IF_WIDTH = 2 * LANES

MLSTM_CHUNK = 128
MOE_TILE = 512
VMEM_LIMIT = 56 * 1024 * 1024


def _sigmoid(x):
    return 0.5 * jnp.tanh(0.5 * x) + 0.5


def _bf16_parts(x, n_parts):
    parts = []
    for _ in range(n_parts):
        p = x.astype(BF16)
        parts.append(p)
        x = x - p.astype(F32)
    return parts


def _log_sigmoid(x):
    return -(jnp.maximum(-x, 0.0) + jnp.log1p(jnp.exp(-jnp.abs(x))))


def _params(sem):
    return pltpu.CompilerParams(dimension_semantics=sem, vmem_limit_bytes=VMEM_LIMIT)


def _in_proj_kernel(x_ref, g_ref, w_ref, wif_ref, o_ref, oif_ref, h_ref, *, bm, rows):
    @pl.when(pl.program_id(1) == 0)
    def _():
        for r in range(0, bm, rows):
            x = x_ref[r:r + rows, :]
            ms = jnp.mean(x * x, axis=-1, keepdims=True)
            h = (x * lax.rsqrt(ms + RMS_EPS) * g_ref[...]).astype(BF16)
            h_ref[r:r + rows, :] = h
            oif_ref[r:r + rows, :] = jnp.dot(h, wif_ref[...], preferred_element_type=F32)

    o_ref[...] = jnp.dot(h_ref[...], w_ref[...], preferred_element_type=F32).astype(BF16)


def _residue_perm(size, dilation, inverse):
    per = size // dilation
    i = lax.broadcasted_iota(jnp.int32, (size, size), 0)
    j = lax.broadcasted_iota(jnp.int32, (size, size), 1)
    if inverse:
        src = (i & (dilation - 1)) * per + (i >> (dilation.bit_length() - 1))
    else:
        src = (i & (per - 1)) * dilation + (i >> (per.bit_length() - 1))
    return j == src


def _attn_proj_kernel(x_ref, g_ref, w_ref, o_ref, h_ref, *, bm, dilation, rows):
    per = rows // dilation
    if dilation > 1:
        perm = _residue_perm(rows, dilation, False).astype(BF16)
    for c in range(0, bm, rows):
        x = x_ref[c:c + rows, :]
        ms = jnp.mean(x * x, axis=-1, keepdims=True)
        h = (x * lax.rsqrt(ms + RMS_EPS) * g_ref[...]).astype(BF16)
        if dilation > 1:
            h = jnp.dot(perm, h, preferred_element_type=F32).astype(BF16)
        h_ref[c:c + rows, :] = h
    res = jnp.dot(h_ref[...], w_ref[...], preferred_element_type=F32).astype(BF16)
    for c in range(bm // rows):
        for r in range(dilation):
            o_ref[0, r, c * per:(c + 1) * per, :] = res[c * rows + r * per:c * rows + (r + 1) * per, :]


def _attn_proj(x2, g, w_qkv, bsz, s, dilation, *, bm=1024):
    n_blk = s // bm
    per = bm // dilation
    kern = functools.partial(_attn_proj_kernel, bm=bm, dilation=dilation, rows=256)
    return pl.pallas_call(
        kern,
        grid=(bsz, n_blk),
        in_specs=[
            pl.BlockSpec((bm, D_MODEL), lambda b, i: (b * n_blk + i, 0)),
            pl.BlockSpec((1, D_MODEL), lambda b, i: (0, 0)),
            pl.BlockSpec((D_MODEL, QKV_WIDTH), lambda b, i: (0, 0),
                         pipeline_mode=pl.Buffered(1)),
        ],
        out_specs=pl.BlockSpec((1, dilation, per, QKV_WIDTH), lambda b, i: (b, 0, i, 0)),
        out_shape=jax.ShapeDtypeStruct((bsz, dilation, s // dilation, QKV_WIDTH), BF16),
        scratch_shapes=[pltpu.VMEM((bm, D_MODEL), BF16)],
        compiler_params=_params(("arbitrary", "arbitrary")),
        name=f"attn_proj_d{dilation}",
    )(x2, g, w_qkv)


def _in_proj(x2, g, w_main, w_if, *, bm=1024, bn=1024):
    n = x2.shape[0]
    bm = min(bm, n)
    kern = functools.partial(_in_proj_kernel, bm=bm, rows=256)
    return pl.pallas_call(
        kern,
        grid=(n // bm, D_MAIN // bn),
        in_specs=[
            pl.BlockSpec((bm, D_MODEL), lambda i, j: (i, 0)),
            pl.BlockSpec((1, D_MODEL), lambda i, j: (0, 0)),
            pl.BlockSpec((D_MODEL, bn), lambda i, j: (0, j)),
            pl.BlockSpec((D_MODEL, IF_WIDTH), lambda i, j: (0, 0)),
        ],
        out_specs=[
            pl.BlockSpec((bm, bn), lambda i, j: (i, j)),
            pl.BlockSpec((bm, IF_WIDTH), lambda i, j: (i, 0)),
        ],
        out_shape=[
            jax.ShapeDtypeStruct((n, D_MAIN), BF16),
            jax.ShapeDtypeStruct((n, IF_WIDTH), F32),
        ],
        scratch_shapes=[pltpu.VMEM((bm, D_MODEL), BF16)],
        compiler_params=_params(("arbitrary", "arbitrary")),
        name="in_proj",
    )(x2, g, w_main, w_if)


def _mlstm_kernel(qk_ref, v_ref, og_ref, if_ref, cw_ref, cb_ref, gb_ref, ng_ref, out_ref,
                  tail_ref, c_ref, m_ref, *, chunk):
    L = chunk
    dh = HEAD_DIM

    @pl.when(pl.program_id(1) == 0)
    def _():
        tail_ref[...] = jnp.zeros_like(tail_ref)
        c_ref[...] = jnp.zeros_like(c_ref)
        m_ref[...] = jnp.zeros_like(m_ref)

    tail = tail_ref.shape[0]
    cur = qk_ref[0]
    ext = jnp.concatenate([tail_ref[...], cur], axis=0)
    tail_ref[...] = cur[L - tail:, :]
    srow = lax.broadcasted_iota(jnp.int32, (L, tail + L), 0)
    scol = lax.broadcasted_iota(jnp.int32, (L, tail + L), 1)
    acc = cur.astype(F32) * cw_ref[CONV_WIDTH - 1:CONV_WIDTH, :] + cb_ref[...]
    for s in range(1, CONV_WIDTH):
        shift = (scol == srow + (tail - s)).astype(BF16)
        acc = acc + (jnp.dot(shift, ext, preferred_element_type=F32)
                     * cw_ref[CONV_WIDTH - 1 - s:CONV_WIDTH - s, :])
    qk = acc * _sigmoid(acc)
    q_all = qk[:, :MLSTM_WIDTH]
    k_all = qk[:, MLSTM_WIDTH:] * (dh ** -0.5)

    head_lane = lax.broadcasted_iota(jnp.int32, (L, LANES), 1) < MLSTM_HEADS
    gi = jnp.where(head_lane, if_ref[0][:, :LANES] + gb_ref[:, :LANES], 0.0)
    lf = jnp.where(head_lane, _log_sigmoid(if_ref[0][:, LANES:] + gb_ref[:, LANES:]), 0.0)
    row = lax.broadcasted_iota(jnp.int32, (L, L), 0)
    col = lax.broadcasted_iota(jnp.int32, (L, L), 1)
    causal = col <= row
    cum = jnp.dot(causal.astype(F32), lf, preferred_element_type=F32,
                  precision=lax.Precision.HIGHEST)
    rmat = gi - cum
    cmax = rmat
    trow = lax.broadcasted_iota(jnp.int32, (L, LANES), 0)
    step = 1
    while step < L:
        cmax = jnp.maximum(cmax, jnp.where(trow >= step, pltpu.roll(cmax, step, axis=0), -jnp.inf))
        step *= 2
    m_prev = m_ref[...]
    gmax = jnp.maximum(m_prev, cmax)
    s_inter = jnp.exp(m_prev - gmax)
    inv_scale = jnp.exp(-(cum + gmax))
    b_last = cum[L - 1:L, :]
    m_new = b_last + gmax[L - 1:L, :]
    w_state = jnp.exp(b_last + rmat - m_new)
    decay = jnp.exp(b_last + m_prev - m_new)
    m_ref[...] = m_new
    rmat_t = rmat.T
    ones_blk = jnp.ones((L, dh), BF16)

    def column(mat, h):
        return jnp.broadcast_to(mat[:, h:h + 1], (L, dh))

    for h in range(MLSTM_HEADS):
        hs = slice(h * dh, (h + 1) * dh)
        p = jnp.exp(jnp.where(causal, rmat_t[h:h + 1, :] - column(gmax, h), NEG))
        qh = q_all[:, hs]
        kh = k_all[:, hs]
        s = lax.dot_general(qh.astype(BF16), kh.astype(BF16), (((1,), (1,)), ((), ())),
                            preferred_element_type=F32)
        vaug = jnp.concatenate([v_ref[0][:, hs], ones_blk], axis=1)
        c_prev = c_ref[h]
        lhs = jnp.concatenate([(p * s).astype(BF16), (column(s_inter, h) * qh).astype(BF16)],
                              axis=1)
        rhs = jnp.concatenate([vaug, c_prev.astype(BF16)], axis=0)
        nd = jnp.dot(lhs, rhs, preferred_element_type=F32)
        hh = nd[:, :dh] / jnp.maximum(jnp.abs(nd[:, dh:]), column(inv_scale, h))

        kw = (column(w_state, h) * kh).astype(BF16)
        kv = lax.dot_general(kw, vaug, (((0,), (0,)), ((), ())), preferred_element_type=F32)
        c_ref[h] = decay[:, h:h + 1] * c_prev + kv

        y = hh * lax.rsqrt(jnp.mean(hh * hh, axis=-1, keepdims=True) + RMS_EPS) * ng_ref[:, hs]
        y = y * _sigmoid(og_ref[0][:, hs].astype(F32))
        out_ref[0, :, hs] = y.astype(BF16)


def _mlstm(proj3, ifg3, conv_w, conv_b, gate_b, norm_g, *, chunk=MLSTM_CHUNK):
    bsz, s, _ = proj3.shape
    w = MLSTM_WIDTH
    kern = functools.partial(_mlstm_kernel, chunk=chunk)
    return pl.pallas_call(
        kern,
        grid=(bsz, s // chunk),
        in_specs=[
            pl.BlockSpec((1, chunk, 2 * w), lambda b, c: (b, c, COL_QK // (2 * w))),
            pl.BlockSpec((1, chunk, w), lambda b, c: (b, c, COL_V // w)),
            pl.BlockSpec((1, chunk, w), lambda b, c: (b, c, COL_O // w)),
            pl.BlockSpec((1, chunk, IF_WIDTH), lambda b, c: (b, c, 0)),
            pl.BlockSpec((CONV_WIDTH, 2 * w), lambda b, c: (0, 0)),
            pl.BlockSpec((1, 2 * w), lambda b, c: (0, 0)),
            pl.BlockSpec((1, IF_WIDTH), lambda b, c: (0, 0)),
            pl.BlockSpec((1, w), lambda b, c: (0, 0)),
        ],
        out_specs=pl.BlockSpec((1, chunk, w), lambda b, c: (b, c, 0)),
        out_shape=jax.ShapeDtypeStruct((bsz, s, w), BF16),
        scratch_shapes=[
            pltpu.VMEM((16, 2 * w), BF16),
            pltpu.VMEM((MLSTM_HEADS, HEAD_DIM, 2 * HEAD_DIM), F32),
            pltpu.VMEM((1, LANES), F32),
        ],
        compiler_params=_params(("arbitrary", "arbitrary")),
        name="mlstm",
    )(proj3, proj3, proj3, ifg3, conv_w, conv_b, gate_b, norm_g)


def _attn_kernel(q_ref, kp_ref, kc_ref, vp_ref, vc_ref, bias_ref, o_ref, lse_ref):
    qb = ATTN_BLOCK
    dh = HEAD_DIM
    n_q = q_ref.shape[2] // qb
    first = pl.program_id(2) == 0
    kcol = lax.broadcasted_iota(jnp.int32, (qb, 2 * qb), 1)
    dead = jnp.logical_and(first, kcol < qb)
    lane = lax.broadcasted_iota(jnp.int32, (qb, LANES), 1)
    for jq in range(n_q):
        rows = slice(jq * qb, (jq + 1) * qb)
        lse_all = jnp.zeros((qb, LANES), F32)
        for j in range(HEADS_PER_GROUP):
            hs = slice(j * dh, (j + 1) * dh)
            q = q_ref[0, 0][rows, hs]
            if jq == 0:
                k = jnp.concatenate([kp_ref[0, 0][:, hs], kc_ref[0, 0][0:qb, hs]], axis=0)
                v = jnp.concatenate([vp_ref[0, 0][:, hs], vc_ref[0, 0][0:qb, hs]], axis=0)
            else:
                k = kc_ref[0, 0][(jq - 1) * qb:(jq + 1) * qb, hs]
                v = vc_ref[0, 0][(jq - 1) * qb:(jq + 1) * qb, hs]
            s = lax.dot_general(q, k, (((1,), (1,)), ((), ())), preferred_element_type=F32)
            s = s * (dh ** -0.5) + bias_ref[j]
            if jq == 0:
                s = jnp.where(dead, NEG, s)
            mx = jnp.max(s, axis=1, keepdims=True)
            p = jnp.exp(s - mx)
            den = jnp.sum(p, axis=1, keepdims=True)
            o = jnp.dot(p.astype(BF16), v, preferred_element_type=F32) / den
            o_ref[0, 0, rows, hs] = o.astype(BF16)
            lse_all = jnp.where(lane == j, mx + jnp.log(den), lse_all)
        lse_ref[0, 0, rows, :] = lse_all


def _attn_group(qkv, bias, g, *, q_blocks=2):
    bsz, dilation, n, _ = qkv.shape
    gw = GROUP_WIDTH
    q_blocks = min(q_blocks, n // ATTN_BLOCK)
    run = q_blocks * ATTN_BLOCK

    def own(part, width=gw):
        return pl.BlockSpec((1, 1, run, width), lambda b, r, i: (b, r, i, part))

    def previous(part):
        return pl.BlockSpec((1, 1, ATTN_BLOCK, gw),
                            lambda b, r, i: (b, r, jnp.maximum(i * q_blocks - 1, 0), part))

    return pl.pallas_call(
        _attn_kernel,
        grid=(bsz, dilation, n // run),
        in_specs=[
            own(0),
            previous(1), own(1),
            previous(2), own(2),
            pl.BlockSpec((HEADS_PER_GROUP, ATTN_BLOCK, 2 * ATTN_BLOCK), lambda b, r, i: (0, 0, 0)),
        ],
        out_specs=[own(0), own(0, LANES)],
        out_shape=[
            jax.ShapeDtypeStruct((bsz, dilation, n, gw), BF16),
            jax.ShapeDtypeStruct((bsz, dilation, n, LANES), F32),
        ],
        compiler_params=_params(("arbitrary", "arbitrary", "arbitrary")),
        name=f"attn_g{g}",
    )(qkv, qkv, qkv, qkv, qkv, bias)


def _t5_bucket(dist):
    max_exact = REL_BUCKETS // 2
    d_f = jnp.maximum(dist, 1).astype(F32)
    large = max_exact + (jnp.log(d_f / max_exact) / math.log(REL_MAX_DIST / max_exact)
                         * (REL_BUCKETS - max_exact)).astype(jnp.int32)
    large = jnp.minimum(large, REL_BUCKETS - 1)
    return jnp.where(dist < max_exact, dist, large)


def _attn_bias(rel_bias, g, dilation):
    span = ATTN_SPAN
    buckets = _t5_bucket(jnp.arange(span + 1, dtype=jnp.int32) * dilation)
    vec = rel_bias[buckets][:, g * HEADS_PER_GROUP:(g + 1) * HEADS_PER_GROUP].T.astype(F32)
    qpos = jnp.arange(ATTN_BLOCK)[:, None]
    kpos = jnp.arange(2 * ATTN_BLOCK)[None, :]
    dist = qpos + ATTN_BLOCK - kpos
    valid = (dist >= 0) & (dist <= span)
    onehot = (dist[:, :, None] == jnp.arange(span + 1)[None, None, :]).astype(F32)
    table = jnp.einsum('qkj,hj->hqk', onehot, vec, precision=lax.Precision.HIGHEST)
    return jnp.where(valid[None], table, NEG)


def _merge_kernel(hm_ref, o0_ref, o1_ref, o2_ref, l0_ref, l1_ref, l2_ref, gm_ref, ga_ref, x_ref,
                  wbm_ref, wba_ref, wo_ref, ng_ref, rw_ref, rb_ref,
                  x1_ref, h2_ref, idx_ref, gate_ref):
    dh = HEAD_DIM
    bm = x_ref.shape[0]

    def token_order(ref):
        dilation = ref.shape[1]
        if dilation == 1:
            return ref[0, 0]
        perm = _residue_perm(bm, dilation, True).astype(BF16)
        blk = ref[0].reshape(bm, ref.shape[3])
        parts = [blk] if blk.dtype == BF16 else _bf16_parts(blk, 3)
        out = None
        for p in parts:
            moved = jnp.dot(perm, p, preferred_element_type=F32)
            out = moved if out is None else out + moved
        return out

    l0, l1, l2 = token_order(l0_ref), token_order(l1_ref), token_order(l2_ref)
    o0, o1, o2 = token_order(o0_ref), token_order(o1_ref), token_order(o2_ref)
    mx = jnp.maximum(jnp.maximum(l0, l1), l2)
    e0, e1, e2 = jnp.exp(l0 - mx), jnp.exp(l1 - mx), jnp.exp(l2 - mx)
    den = e0 + e1 + e2
    w0, w1, w2 = e0 / den, e1 / den, e2 / den
    parts = []
    for j in range(HEADS_PER_GROUP):
        hs = slice(j * dh, (j + 1) * dh)
        parts.append(w0[:, j:j + 1] * o0[:, hs] + w1[:, j:j + 1] * o1[:, hs]
                     + w2[:, j:j + 1] * o2[:, hs])
    ha = jnp.concatenate(parts, axis=1).astype(BF16)
    ym = jnp.dot(hm_ref[...], wbm_ref[...], preferred_element_type=F32)
    ya = jnp.dot(ha, wba_ref[...], preferred_element_type=F32)
    merged = (_sigmoid(gm_ref[...].astype(F32)) * ym + _sigmoid(ga_ref[...].astype(F32)) * ya)
    x1 = x_ref[...] + jnp.dot(merged.astype(BF16), wo_ref[...], preferred_element_type=F32)
    x1_ref[...] = x1
    h2 = x1 * lax.rsqrt(jnp.mean(x1 * x1, axis=-1, keepdims=True) + RMS_EPS) * ng_ref[...]
    _store_token_tiles(h2_ref, _pack_pairs(h2))
    h_hi, h_lo = _bf16_parts(h2, 2)
    logits = jnp.dot(jnp.concatenate([h_hi, h_lo, h_hi], axis=1), rw_ref[...],
                     preferred_element_type=F32) + rb_ref[...]
    lane = lax.broadcasted_iota(jnp.int32, logits.shape, 1)
    idx_all = jnp.zeros(logits.shape, jnp.int32)
    val_all = jnp.zeros(logits.shape, F32)
    top0 = None
    esum = None
    for k in range(TOP_K):
        m = jnp.max(logits, axis=1, keepdims=True)
        sel = jnp.min(jnp.where(logits == m, lane, LANES), axis=1, keepdims=True)
        if k == 0:
            top0 = m
        e = jnp.exp(m - top0)
        esum = e if k == 0 else esum + e
        idx_all = jnp.where(lane == k, sel, idx_all)
        val_all = jnp.where(lane == k, e, val_all)
        logits = jnp.where(lane == sel, -jnp.inf, logits)
    idx_ref[...] = idx_all
    gate_ref[...] = val_all / esum


def _merge(hm, outs, lses, proj, x2, wbm, wba, wo, ng, rw, rb, *, bm=256):
    n = x2.shape[0]
    d = D_MODEL
    gcol = COL_GATES // d
    n_blk = outs[0].shape[2] // bm

    def rows(width):
        return pl.BlockSpec((bm, width), lambda i: (i, 0))

    def full(a, b):
        return pl.BlockSpec((a, b), lambda i: (0, 0), pipeline_mode=pl.Buffered(1))

    def residue(arr):
        dilation, width = arr.shape[1], arr.shape[3]
        return pl.BlockSpec((1, dilation, bm // dilation, width),
                            lambda i: (i // n_blk, 0, i % n_blk, 0))

    return pl.pallas_call(
        _merge_kernel,
        grid=(n // bm,),
        in_specs=[
            rows(MLSTM_WIDTH),
            residue(outs[0]), residue(outs[1]), residue(outs[2]),
            residue(lses[0]), residue(lses[1]), residue(lses[2]),
            pl.BlockSpec((bm, d), lambda i: (i, gcol)),
            pl.BlockSpec((bm, d), lambda i: (i, gcol + 1)),
            rows(d),
            full(MLSTM_WIDTH, d), full(GROUP_WIDTH, d), full(d, d),
            full(1, d), full(3 * d, LANES), full(1, LANES),
        ],
        out_specs=[rows(d), pl.BlockSpec((bm * PACK_ROWS, LANES), lambda i: (i, 0)),
                   rows(LANES), rows(LANES)],
        out_shape=[
            jax.ShapeDtypeStruct((n, d), F32),
            jax.ShapeDtypeStruct((n * PACK_ROWS, LANES), jnp.uint32),
            jax.ShapeDtypeStruct((n, LANES), jnp.int32),
            jax.ShapeDtypeStruct((n, LANES), F32),
        ],
        compiler_params=_params(("arbitrary",)),
        name="merge",
    )(hm, outs[0], outs[1], outs[2], lses[0], lses[1], lses[2], proj, proj, x2,
      wbm, wba, wo, ng, rw, rb)


PACK_ROWS = 8


def _pack_pairs(x):
    w = x.shape[1] // 2
    lo = lax.bitcast_convert_type(x[:, :w].astype(BF16).astype(F32), jnp.uint32) >> 16
    hi = lax.bitcast_convert_type(x[:, w:].astype(BF16).astype(F32), jnp.uint32)
    return (hi & jnp.uint32(0xFFFF0000)) | lo


def _unpack_pairs(words):
    lo = lax.bitcast_convert_type(words << 16, F32)
    hi = lax.bitcast_convert_type(words & jnp.uint32(0xFFFF0000), F32)
    return lo, hi


def _store_token_tiles(ref, words):
    rows = words.shape[0]
    for s in range(PACK_ROWS):
        ref[pl.ds(s, rows, stride=PACK_ROWS), :] = words[:, s * LANES:(s + 1) * LANES]


def _dispatch_kernel(fs_ref, fl_ref, nu_ref, dest_ref, hp_ref, xb_hbm, zero_ref, sem, zsem, *,
                     tokens, tm, n_blk):
    pr = PACK_ROWS
    fill_sizes = [1 << b for b in reversed(range((tm - 1).bit_length()))]

    def fill_copies(e):
        off = fs_ref[e]
        for p in fill_sizes:
            take = (fl_ref[e] & p) != 0
            dst = xb_hbm.at[pl.ds(pl.multiple_of(off * pr, pr), p * pr), :]
            yield take, pltpu.make_async_copy(zero_ref.at[pl.ds(0, p * pr), :], dst, zsem)
            off = off + jnp.where(take, p, 0)

    def block_copy(b):
        dst = xb_hbm.at[pl.ds(pl.multiple_of(b * (tm * pr), tm * pr), tm * pr), :]
        return pltpu.make_async_copy(zero_ref, dst, zsem)

    @pl.when(pl.program_id(0) == 0)
    def _():
        zero_ref[...] = jnp.zeros_like(zero_ref)
        for start in (True, False):
            def per_expert(e, carry):
                for take, cp in fill_copies(e):
                    @pl.when(take)
                    def _():
                        cp.start() if start else cp.wait()
                return carry

            def per_block(b, carry):
                block_copy(b).start() if start else block_copy(b).wait()
                return carry

            lax.fori_loop(0, N_EXPERTS, per_expert, 0)
            lax.fori_loop(nu_ref[0], n_blk, per_block, 0)

    def issue(t, carry):
        src = hp_ref.at[pl.ds(pl.multiple_of(t * pr, pr), pr), :]
        for k in range(TOP_K):
            slot = dest_ref[t * TOP_K + k]
            pltpu.make_async_copy(src, xb_hbm.at[pl.ds(pl.multiple_of(slot * pr, pr), pr), :],
                                  sem).start(priority=k % 2)
        return carry

    lax.fori_loop(0, tokens, issue, 0)
    for k in range(TOP_K):
        pltpu.make_async_copy(hp_ref, xb_hbm.at[pl.ds(0, tokens * pr), :], sem).wait()


def _dispatch(fill_start, fill_len, n_used, dest, h2p, n_pad, *, tokens=256, tm=MOE_TILE):
    pr = PACK_ROWS
    n_tok = h2p.shape[0] // pr
    kern = functools.partial(_dispatch_kernel, tokens=tokens, tm=tm, n_blk=n_pad // tm)
    return pl.pallas_call(
        kern,
        grid_spec=pltpu.PrefetchScalarGridSpec(
            num_scalar_prefetch=3,
            grid=(n_tok // tokens,),
            in_specs=[
                pl.BlockSpec((tokens * TOP_K,), lambda i, fs, fl, nu: (i,),
                             memory_space=pltpu.SMEM),
                pl.BlockSpec((tokens * pr, LANES), lambda i, fs, fl, nu: (i, 0)),
            ],
            out_specs=pl.BlockSpec(memory_space=pl.ANY),
            scratch_shapes=[pltpu.VMEM((tm * pr, LANES), jnp.uint32),
                            pltpu.SemaphoreType.DMA(()), pltpu.SemaphoreType.DMA(())],
        ),
        out_shape=jax.ShapeDtypeStruct((n_pad * pr, LANES), jnp.uint32),
        compiler_params=_params(("arbitrary",)),
        name="dispatch",
    )(fill_start, fill_len, n_used, dest, h2p)


def _cast_rows(src_ref, dst_ref, rows):
    total = src_ref.shape[0]

    def body(i, carry):
        r = pl.multiple_of(i * rows, rows)
        dst_ref[pl.ds(r, rows), :] = src_ref[pl.ds(r, rows), :].astype(BF16)
        return carry

    lax.fori_loop(0, total // rows, body, 0)


def _stream_expert_weights(be_ref, nu_ref, ne_ref, copies, convert):
    k, j = pl.program_id(0), pl.program_id(1)
    expert = be_ref[j]
    used = j < nu_ref[0]
    run_start = jnp.logical_or(j == 0, expert != be_ref[jnp.maximum(j - 1, 0)])

    @pl.when(jnp.logical_and(k == 0, j == 0))
    def _():
        for cp in copies(expert, k):
            cp.start()

    @pl.when(jnp.logical_and(used, run_start))
    def _():
        for cp in copies(expert, k):
            cp.wait()
        convert()
        last_run = ne_ref[j] < 0
        nxt_e = jnp.where(last_run, be_ref[0], ne_ref[j])
        nxt_k = jnp.where(last_run, k + 1, k)

        @pl.when(nxt_k < pl.num_programs(0))
        def _():
            for cp in copies(nxt_e, nxt_k):
                cp.start()

    return used


def _per_block_rows(used, bv_ref, o_ref, compute, store=None, rows_per_token=1):
    tm = o_ref.shape[0] // rows_per_token
    few = bv_ref[pl.program_id(1)] <= tm // 2

    def run(rows):
        res = compute(rows)
        head = o_ref.at[pl.ds(0, rows * rows_per_token), :]
        if store is None:
            head[...] = res
        else:
            store(head, res)
        if rows < tm:
            tail = o_ref.at[pl.ds(rows * rows_per_token, (tm - rows) * rows_per_token), :]
            tail[...] = jnp.zeros_like(tail)

    @pl.when(jnp.logical_and(used, few))
    def _():
        run(tm // 2)

    @pl.when(jnp.logical_and(used, jnp.logical_not(few)))
    def _():
        run(tm)

    @pl.when(jnp.logical_not(used))
    def _():
        o_ref[...] = jnp.zeros_like(o_ref)


def _gate_up_kernel(be_ref, nu_ref, ne_ref, bv_ref, x_ref, w_hbm, bg_ref, bu_ref, o_ref,
                    stage_ref, w_bf, sem):
    tf = o_ref.shape[1]
    kt = pl.num_programs(0)

    def copies(expert, k):
        return [pltpu.make_async_copy(
            w_hbm.at[expert, :, pl.ds(pl.multiple_of((half * kt + k) * tf, tf), tf)],
            stage_ref.at[half], sem.at[half]) for half in range(2)]

    def convert():
        for half in range(2):
            _cast_rows(stage_ref.at[half], w_bf.at[half], 256)

    used = _stream_expert_weights(be_ref, nu_ref, ne_ref, copies, convert)

    def compute(rows):
        halves = [_unpack_pairs(x_ref[pl.ds(s, rows, stride=PACK_ROWS), :])
                  for s in range(PACK_ROWS)]
        x = jnp.concatenate([h[0].astype(BF16) for h in halves]
                            + [h[1].astype(BF16) for h in halves], axis=1)
        gate = jnp.dot(x, w_bf[0], preferred_element_type=F32) + bg_ref[0]
        up = jnp.dot(x, w_bf[1], preferred_element_type=F32) + bu_ref[0]
        gate = jnp.minimum(gate, SWIGLU_LIMIT)
        up = jnp.clip(up, -SWIGLU_LIMIT, SWIGLU_LIMIT)
        glu = gate * _sigmoid(SWIGLU_ALPHA * gate)
        return ((up + 1.0) * glu).astype(BF16)

    _per_block_rows(used, bv_ref, o_ref, compute)


def _used_block(j, nu):
    return jnp.minimum(j, nu[0] - 1)


def _gate_up(sched, xb, w_gate_up, b_gate_up3, *, tm=MOE_TILE, tf=1024):
    n_pad, d = xb.shape[0] // PACK_ROWS, D_MODEL
    kt = D_FF // tf

    def bspec(off):
        return pl.BlockSpec((1, 1, tf),
                            lambda k, j, be, nu, ne, bv: (be[_used_block(j, nu)], 0, off + k))

    return pl.pallas_call(
        _gate_up_kernel,
        grid_spec=pltpu.PrefetchScalarGridSpec(
            num_scalar_prefetch=4,
            grid=(kt, n_pad // tm),
            in_specs=[
                pl.BlockSpec((tm * PACK_ROWS, LANES),
                             lambda k, j, be, nu, ne, bv: (_used_block(j, nu), 0)),
                pl.BlockSpec(memory_space=pl.ANY),
                bspec(0), bspec(kt),
            ],
            out_specs=pl.BlockSpec((tm, tf), lambda k, j, be, nu, ne, bv: (j, k)),
            scratch_shapes=[pltpu.VMEM((2, d, tf), F32), pltpu.VMEM((2, d, tf), BF16),
                            pltpu.SemaphoreType.DMA((2,))],
        ),
        out_shape=jax.ShapeDtypeStruct((n_pad, D_FF), BF16),
        compiler_params=_params(("arbitrary", "arbitrary")),
        name="gate_up",
    )(*sched, xb, w_gate_up, b_gate_up3, b_gate_up3)


def _down_kernel(be_ref, nu_ref, ne_ref, bv_ref, h_ref, w_hbm, bd_ref, o_ref, stage_ref, w_bf,
                 sem):
    def copies(expert, k):
        del k
        return [pltpu.make_async_copy(w_hbm.at[expert], stage_ref, sem.at[0])]

    def convert():
        _cast_rows(stage_ref, w_bf, 256)

    used = _stream_expert_weights(be_ref, nu_ref, ne_ref, copies, convert)

    def compute(rows):
        y = jnp.dot(h_ref[0:rows, :], w_bf[...], preferred_element_type=F32) + bd_ref[0]
        return _pack_pairs(y)

    _per_block_rows(used, bv_ref, o_ref, compute, store=_store_token_tiles,
                    rows_per_token=PACK_ROWS)


def _down(sched, hb, w_down, b_down3, *, tm=MOE_TILE):
    n_pad, f = hb.shape
    d = D_MODEL
    return pl.pallas_call(
        _down_kernel,
        grid_spec=pltpu.PrefetchScalarGridSpec(
            num_scalar_prefetch=4,
            grid=(1, n_pad // tm),
            in_specs=[
                pl.BlockSpec((tm, f), lambda k, j, be, nu, ne, bv: (_used_block(j, nu), 0)),
                pl.BlockSpec(memory_space=pl.ANY),
                pl.BlockSpec((1, 1, d),
                             lambda k, j, be, nu, ne, bv: (be[_used_block(j, nu)], 0, 0)),
            ],
            out_specs=pl.BlockSpec((tm * PACK_ROWS, LANES), lambda k, j, be, nu, ne, bv: (j, 0)),
            scratch_shapes=[pltpu.VMEM((f, d), F32), pltpu.VMEM((f, d), BF16),
                            pltpu.SemaphoreType.DMA((1,))],
        ),
        out_shape=jax.ShapeDtypeStruct((n_pad * PACK_ROWS, LANES), jnp.uint32),
        compiler_params=_params(("arbitrary", "arbitrary")),
        name="down",
    )(*sched, hb, w_down, b_down3)


def _combine_kernel(dest_ref, dest_next_ref, y_hbm, gate_ref, x1_ref, ng_ref, o_ref, buf_ref, sem,
                    *, rows):
    pr = PACK_ROWS
    half = x1_ref.shape[1] // 2
    i = pl.program_id(0)
    slot_rows = TOP_K * rows * pr

    def gather(idx_ref, buf):
        def issue(t, carry):
            for k in range(TOP_K):
                slot = idx_ref[t * TOP_K + k]
                pltpu.make_async_copy(
                    y_hbm.at[pl.ds(pl.multiple_of(slot * pr, pr), pr), :],
                    buf_ref.at[pl.ds(pl.multiple_of(buf * slot_rows + (k * rows + t) * pr, pr), pr), :],
                    sem.at[buf]).start(priority=k % 2)
            return carry

        lax.fori_loop(0, rows, issue, 0)

    @pl.when(i == 0)
    def _():
        gather(dest_ref, 0)

    @pl.when(i + 1 < pl.num_programs(0))
    def _():
        gather(dest_next_ref, (i + 1) % 2)

    cur = i % 2
    base = pl.multiple_of(cur * slot_rows, slot_rows)
    pltpu.make_async_copy(y_hbm.at[pl.ds(0, slot_rows), :],
                          buf_ref.at[pl.ds(base, slot_rows), :], sem.at[cur]).wait()
    gates = [gate_ref[:, k:k + 1] for k in range(TOP_K)]
    lows, highs = [], []
    for s in range(pr):
        lo = x1_ref[:, s * LANES:(s + 1) * LANES]
        hi = x1_ref[:, half + s * LANES:half + (s + 1) * LANES]
        for k in range(TOP_K):
            y_lo, y_hi = _unpack_pairs(
                buf_ref[pl.ds(base + (k * rows * pr + s), rows, stride=pr), :])
            lo = lo + gates[k] * y_lo
            hi = hi + gates[k] * y_hi
        lows.append(lo)
        highs.append(hi)
    acc = jnp.concatenate(lows + highs, axis=1)
    o_ref[...] = acc * lax.rsqrt(jnp.mean(acc * acc, axis=-1, keepdims=True) + RMS_EPS) * ng_ref[...]


def _combine(dest, yb, gates, x1, ng, *, rows=256):
    n, d = x1.shape
    kern = functools.partial(_combine_kernel, rows=rows)
    n_steps = n // rows
    return pl.pallas_call(
        kern,
        grid=(n_steps,),
        in_specs=[
            pl.BlockSpec((rows * TOP_K,), lambda i: (i,), memory_space=pltpu.SMEM),
            pl.BlockSpec((rows * TOP_K,), lambda i: (jnp.minimum(i + 1, n_steps - 1),),
                         memory_space=pltpu.SMEM),
            pl.BlockSpec(memory_space=pl.ANY),
            pl.BlockSpec((rows, LANES), lambda i: (i, 0)),
            pl.BlockSpec((rows, d), lambda i: (i, 0)),
            pl.BlockSpec((1, d), lambda i: (0, 0)),
        ],
        out_specs=pl.BlockSpec((rows, d), lambda i: (i, 0)),
        out_shape=jax.ShapeDtypeStruct((n, d), F32),
        scratch_shapes=[pltpu.VMEM((2 * TOP_K * rows * PACK_ROWS, LANES), jnp.uint32),
                        pltpu.SemaphoreType.DMA((2,))],
        compiler_params=_params(("arbitrary",)),
        name="combine",
    )(dest, dest, yb, gates, x1, ng)


def _routing(top_idx, tm):
    n_tok = top_idx.shape[0]
    n_asg = n_tok * TOP_K
    e_flat = top_idx.reshape(n_asg)
    onehot = (e_flat[:, None] == jnp.arange(N_EXPERTS, dtype=jnp.int32)[None, :]).astype(jnp.int32)
    cum = jnp.cumsum(onehot, axis=0)
    rank = jnp.sum(onehot * cum, axis=1) - 1
    counts = cum[-1]
    padded = (counts + tm - 1) // tm * tm
    pend = jnp.cumsum(padded)
    pstart = pend - padded
    dest = (pstart[e_flat] + rank).astype(jnp.int32)
    n_blk = -(-(n_asg + N_EXPERTS * (tm - 1)) // tm)
    n_pad = n_blk * tm
    fill_start = (pstart + counts).astype(jnp.int32)
    fill_len = (padded - counts).astype(jnp.int32)
    block_start = jnp.arange(n_blk, dtype=jnp.int32) * tm
    block_e = jnp.minimum(
        jnp.sum((pend[None, :] <= block_start[:, None]).astype(jnp.int32), axis=1),
        N_EXPERTS - 1).astype(jnp.int32)
    n_used = (pend[-1:] // tm).astype(jnp.int32)
    run_end = pend[block_e] // tm
    next_e = jnp.where(run_end < n_used[0], block_e[jnp.minimum(run_end, n_blk - 1)], -1)
    block_rows = jnp.clip(fill_start[block_e] - block_start, 0, tm)
    sched = (block_e, n_used, next_e.astype(jnp.int32), block_rows.astype(jnp.int32))
    return dest, fill_start, fill_len, sched, n_pad


def _layer(x, norm_mix_g, w_in, conv_w, conv_b, igate_b, fgate_b, mlstm_norm_g, rel_bias,
           w_branch_mlstm, w_branch_attn, w_out, norm_moe_g, router_w, router_b,
           w_gate_up, b_gate_up, w_down, b_down, out_norm_g):
    bsz, s, d = x.shape
    n = bsz * s
    x2 = x.reshape(n, d)
    w2, w1 = 2 * MLSTM_WIDTH, MLSTM_WIDTH
    o_qk, o_v, o_o = 0, w2, w2 + w1
    o_i = o_o + w1
    o_f = o_i + MLSTM_HEADS
    o_qa = o_f + MLSTM_HEADS
    o_ka, o_va = o_qa + ATTN_WIDTH, o_qa + 2 * ATTN_WIDTH
    o_g = o_va + ATTN_WIDTH
    w_main = jnp.concatenate([w_in[:, o_qk:o_i], w_in[:, o_g:]], axis=1).astype(BF16)
    zpad = jnp.zeros((d, LANES - MLSTM_HEADS), w_in.dtype)
    w_if = jnp.concatenate([w_in[:, o_i:o_f], zpad, w_in[:, o_f:o_qa], zpad], axis=1).astype(BF16)
    bpad = jnp.zeros((LANES - MLSTM_HEADS,), F32)
    gate_b = jnp.concatenate([igate_b, bpad, fgate_b, bpad]).reshape(1, IF_WIDTH)

    norm_g = norm_mix_g.reshape(1, d)
    proj, ifg = _in_proj(x2, norm_g, w_main, w_if)
    proj3 = proj.reshape(bsz, s, D_MAIN)
    hm = _mlstm(proj3, ifg.reshape(bsz, s, IF_WIDTH), conv_w, conv_b.reshape(1, -1), gate_b,
                mlstm_norm_g.reshape(1, -1)).reshape(n, MLSTM_WIDTH)

    outs, lses = [], []
    gw = GROUP_WIDTH
    for g, (_, dilation) in enumerate(ATTN_GROUPS):
        w_qkv = jnp.concatenate(
            [w_in[:, o + g * gw:o + (g + 1) * gw] for o in (o_qa, o_ka, o_va)], axis=1).astype(BF16)
        qkv = _attn_proj(x2, norm_g, w_qkv, bsz, s, dilation)
        o_g_, lse_g = _attn_group(qkv, _attn_bias(rel_bias, g, dilation), g)
        outs.append(o_g_)
        lses.append(lse_g)

    rw = jnp.concatenate([router_w, jnp.zeros((d, LANES - N_EXPERTS), F32)], axis=1)
    rw_hi = rw.astype(BF16)
    rw_lo = (rw - rw_hi.astype(F32)).astype(BF16)
    rw = jnp.concatenate([rw_hi, rw_hi, rw_lo], axis=0)
    rb = jnp.concatenate([router_b, jnp.full((LANES - N_EXPERTS,), NEG, F32)]).reshape(1, LANES)
    x1, h2, idx, gates = _merge(
        hm, outs, lses, proj, x2, w_branch_mlstm.astype(BF16), w_branch_attn.astype(BF16),
        w_out.astype(BF16), norm_moe_g.reshape(1, d), rw, rb)

    dest, fill_start, fill_len, sched, n_pad = _routing(idx[:, :TOP_K], MOE_TILE)
    xb = _dispatch(fill_start, fill_len, sched[1], dest, h2, n_pad)
    hb = _gate_up(sched, xb, w_gate_up, b_gate_up.reshape(N_EXPERTS, 1, 2 * D_FF))
    yb = _down(sched, hb, w_down, b_down.reshape(N_EXPERTS, 1, d))
    out = _combine(dest, yb, gates, x1, out_norm_g.reshape(1, d))
    return out.reshape(bsz, s, d)


def kernel(x, norm_mix_g, w_in, conv_w, conv_b, igate_b, fgate_b, mlstm_norm_g, rel_bias,
           w_branch_mlstm, w_branch_attn, w_out, norm_moe_g, router_w, router_b,
           w_gate_up, b_gate_up, w_down, b_down, norm_final_g):
    assert w_in.shape[0] == 1, "single-layer block"
    return _layer(x, norm_mix_g[0], w_in[0], conv_w[0], conv_b[0], igate_b[0], fgate_b[0],
                  mlstm_norm_g[0], rel_bias, w_branch_mlstm[0], w_branch_attn[0], w_out[0],
                  norm_moe_g[0], router_w[0], router_b[0], w_gate_up[0], b_gate_up[0],
                  w_down[0], b_down[0], norm_final_g)
```

```python
import functools
import math

import jax
import jax.numpy as jnp
from jax import lax
from jax.experimental import pallas as pl
from jax.experimental.pallas import tpu as pltpu

F32 = jnp.float32
BF16 = jnp.bfloat16

D_MODEL = 2048
MLSTM_HEADS = 8
HEAD_DIM = 128
MLSTM_WIDTH = MLSTM_HEADS * HEAD_DIM
CONV_WIDTH = 4
ATTN_GROUPS = ((128, 1), (512, 4), (2048, 16))
N_GROUPS = 3
HEADS_PER_GROUP = 4
ATTN_HEADS = HEADS_PER_GROUP * N_GROUPS
ATTN_WIDTH = ATTN_HEADS * HEAD_DIM
GROUP_WIDTH = HEADS_PER_GROUP * HEAD_DIM
ATTN_BLOCK = 128
ATTN_SPAN = 128
REL_BUCKETS = 32
REL_MAX_DIST = 2048
N_EXPERTS = 32
TOP_K = 4
D_FF = D_MODEL
SWIGLU_LIMIT = 7.0
SWIGLU_ALPHA = 1.702
RMS_EPS = 1e-6
NEG = -1e30

COL_QK = 0
COL_V = 2 * MLSTM_WIDTH
COL_O = COL_V + MLSTM_WIDTH
COL_GATES = COL_O + MLSTM_WIDTH
D_MAIN = COL_GATES + 2 * D_MODEL
QKV_WIDTH = 3 * GROUP_WIDTH
LANES = 128
IF_WIDTH = 2 * LANES

MLSTM_CHUNK = 128
MOE_TILE = 512
VMEM_LIMIT = 56 * 1024 * 1024


def _sigmoid(x):
    return 0.5 * jnp.tanh(0.5 * x) + 0.5


def _bf16_parts(x, n_parts):
    parts = []
    for _ in range(n_parts):
        p = x.astype(BF16)
        parts.append(p)
        x = x - p.astype(F32)
    return parts


def _log_sigmoid(x):
    return -(jnp.maximum(-x, 0.0) + jnp.log1p(jnp.exp(-jnp.abs(x))))


def _params(sem):
    return pltpu.CompilerParams(dimension_semantics=sem, vmem_limit_bytes=VMEM_LIMIT)


def _in_proj_kernel(x_ref, g_ref, w_ref, wif_ref, o_ref, oif_ref, h_ref, *, bm, rows):
    @pl.when(pl.program_id(1) == 0)
    def _():
        for r in range(0, bm, rows):
            x = x_ref[r:r + rows, :]
            ms = jnp.mean(x * x, axis=-1, keepdims=True)
            h = (x * lax.rsqrt(ms + RMS_EPS) * g_ref[...]).astype(BF16)
            h_ref[r:r + rows, :] = h
            oif_ref[r:r + rows, :] = jnp.dot(h, wif_ref[...], preferred_element_type=F32)

    o_ref[...] = jnp.dot(h_ref[...], w_ref[...], preferred_element_type=F32).astype(BF16)


def _residue_perm(size, dilation, inverse):
    per = size // dilation
    i = lax.broadcasted_iota(jnp.int32, (size, size), 0)
    j = lax.broadcasted_iota(jnp.int32, (size, size), 1)
    if inverse:
        src = (i & (dilation - 1)) * per + (i >> (dilation.bit_length() - 1))
    else:
        src = (i & (per - 1)) * dilation + (i >> (per.bit_length() - 1))
    return j == src


def _attn_proj_kernel(x_ref, g_ref, w_ref, o_ref, h_ref, *, bm, dilation, rows):
    per = rows // dilation
    if dilation > 1:
        perm = _residue_perm(rows, dilation, False).astype(BF16)
    for c in range(0, bm, rows):
        x = x_ref[c:c + rows, :]
        ms = jnp.mean(x * x, axis=-1, keepdims=True)
        h = (x * lax.rsqrt(ms + RMS_EPS) * g_ref[...]).astype(BF16)
        if dilation > 1:
            h = jnp.dot(perm, h, preferred_element_type=F32).astype(BF16)
        h_ref[c:c + rows, :] = h
    res = jnp.dot(h_ref[...], w_ref[...], preferred_element_type=F32).astype(BF16)
    for c in range(bm // rows):
        for r in range(dilation):
            o_ref[0, r, c * per:(c + 1) * per, :] = res[c * rows + r * per:c * rows + (r + 1) * per, :]


def _attn_proj(x2, g, w_qkv, bsz, s, dilation, *, bm=1024):
    n_blk = s // bm
    per = bm // dilation
    kern = functools.partial(_attn_proj_kernel, bm=bm, dilation=dilation, rows=256)
    return pl.pallas_call(
        kern,
        grid=(bsz, n_blk),
        in_specs=[
            pl.BlockSpec((bm, D_MODEL), lambda b, i: (b * n_blk + i, 0)),
            pl.BlockSpec((1, D_MODEL), lambda b, i: (0, 0)),
            pl.BlockSpec((D_MODEL, QKV_WIDTH), lambda b, i: (0, 0),
                         pipeline_mode=pl.Buffered(1)),
        ],
        out_specs=pl.BlockSpec((1, dilation, per, QKV_WIDTH), lambda b, i: (b, 0, i, 0)),
        out_shape=jax.ShapeDtypeStruct((bsz, dilation, s // dilation, QKV_WIDTH), BF16),
        scratch_shapes=[pltpu.VMEM((bm, D_MODEL), BF16)],
        compiler_params=_params(("arbitrary", "arbitrary")),
        name=f"attn_proj_d{dilation}",
    )(x2, g, w_qkv)


def _in_proj(x2, g, w_main, w_if, *, bm=1024, bn=1024):
    n = x2.shape[0]
    bm = min(bm, n)
    kern = functools.partial(_in_proj_kernel, bm=bm, rows=256)
    return pl.pallas_call(
        kern,
        grid=(n // bm, D_MAIN // bn),
        in_specs=[
            pl.BlockSpec((bm, D_MODEL), lambda i, j: (i, 0)),
            pl.BlockSpec((1, D_MODEL), lambda i, j: (0, 0)),
            pl.BlockSpec((D_MODEL, bn), lambda i, j: (0, j)),
            pl.BlockSpec((D_MODEL, IF_WIDTH), lambda i, j: (0, 0)),
        ],
        out_specs=[
            pl.BlockSpec((bm, bn), lambda i, j: (i, j)),
            pl.BlockSpec((bm, IF_WIDTH), lambda i, j: (i, 0)),
        ],
        out_shape=[
            jax.ShapeDtypeStruct((n, D_MAIN), BF16),
            jax.ShapeDtypeStruct((n, IF_WIDTH), F32),
        ],
        scratch_shapes=[pltpu.VMEM((bm, D_MODEL), BF16)],
        compiler_params=_params(("arbitrary", "arbitrary")),
        name="in_proj",
    )(x2, g, w_main, w_if)


def _mlstm_kernel(qk_ref, v_ref, og_ref, if_ref, cw_ref, cb_ref, gb_ref, ng_ref, out_ref,
                  tail_ref, c_ref, m_ref, *, chunk):
    L = chunk
    dh = HEAD_DIM

    @pl.when(pl.program_id(1) == 0)
    def _():
        tail_ref[...] = jnp.zeros_like(tail_ref)
        c_ref[...] = jnp.zeros_like(c_ref)
        m_ref[...] = jnp.zeros_like(m_ref)

    tail = tail_ref.shape[0]
    cur = qk_ref[0]
    ext = jnp.concatenate([tail_ref[...], cur], axis=0)
    tail_ref[...] = cur[L - tail:, :]
    srow = lax.broadcasted_iota(jnp.int32, (L, tail + L), 0)
    scol = lax.broadcasted_iota(jnp.int32, (L, tail + L), 1)
    acc = cur.astype(F32) * cw_ref[CONV_WIDTH - 1:CONV_WIDTH, :] + cb_ref[...]
    for s in range(1, CONV_WIDTH):
        shift = (scol == srow + (tail - s)).astype(BF16)
        acc = acc + (jnp.dot(shift, ext, preferred_element_type=F32)
                     * cw_ref[CONV_WIDTH - 1 - s:CONV_WIDTH - s, :])
    qk = acc * _sigmoid(acc)
    q_all = qk[:, :MLSTM_WIDTH]
    k_all = qk[:, MLSTM_WIDTH:] * (dh ** -0.5)

    head_lane = lax.broadcasted_iota(jnp.int32, (L, LANES), 1) < MLSTM_HEADS
    gi = jnp.where(head_lane, if_ref[0][:, :LANES] + gb_ref[:, :LANES], 0.0)
    lf = jnp.where(head_lane, _log_sigmoid(if_ref[0][:, LANES:] + gb_ref[:, LANES:]), 0.0)
    row = lax.broadcasted_iota(jnp.int32, (L, L), 0)
    col = lax.broadcasted_iota(jnp.int32, (L, L), 1)
    causal = col <= row
    cum = jnp.dot(causal.astype(F32), lf, preferred_element_type=F32,
                  precision=lax.Precision.HIGHEST)
    rmat = gi - cum
    cmax = rmat
    trow = lax.broadcasted_iota(jnp.int32, (L, LANES), 0)
    step = 1
    while step < L:
        cmax = jnp.maximum(cmax, jnp.where(trow >= step, pltpu.roll(cmax, step, axis=0), -jnp.inf))
        step *= 2
    m_prev = m_ref[...]
    gmax = jnp.maximum(m_prev, cmax)
    s_inter = jnp.exp(m_prev - gmax)
    inv_scale = jnp.exp(-(cum + gmax))
    b_last = cum[L - 1:L, :]
    m_new = b_last + gmax[L - 1:L, :]
    w_state = jnp.exp(b_last + rmat - m_new)
    decay = jnp.exp(b_last + m_prev - m_new)
    m_ref[...] = m_new
    rmat_t = rmat.T
    ones_blk = jnp.ones((L, dh), BF16)

    def column(mat, h):
        return jnp.broadcast_to(mat[:, h:h + 1], (L, dh))

    for h in range(MLSTM_HEADS):
        hs = slice(h * dh, (h + 1) * dh)
        p = jnp.exp(jnp.where(causal, rmat_t[h:h + 1, :] - column(gmax, h), NEG))
        qh = q_all[:, hs]
        kh = k_all[:, hs]
        s = lax.dot_general(qh.astype(BF16), kh.astype(BF16), (((1,), (1,)), ((), ())),
                            preferred_element_type=F32)
        vaug = jnp.concatenate([v_ref[0][:, hs], ones_blk], axis=1)
        c_prev = c_ref[h]
        lhs = jnp.concatenate([(p * s).astype(BF16), (column(s_inter, h) * qh).astype(BF16)],
                              axis=1)
        rhs = jnp.concatenate([vaug, c_prev.astype(BF16)], axis=0)
        nd = jnp.dot(lhs, rhs, preferred_element_type=F32)
        hh = nd[:, :dh] / jnp.maximum(jnp.abs(nd[:, dh:]), column(inv_scale, h))

        kw = (column(w_state, h) * kh).astype(BF16)
        kv = lax.dot_general(kw, vaug, (((0,), (0,)), ((), ())), preferred_element_type=F32)
        c_ref[h] = decay[:, h:h + 1] * c_prev + kv

        y = hh * lax.rsqrt(jnp.mean(hh * hh, axis=-1, keepdims=True) + RMS_EPS) * ng_ref[:, hs]
        y = y * _sigmoid(og_ref[0][:, hs].astype(F32))
        out_ref[0, :, hs] = y.astype(BF16)


def _mlstm(proj3, ifg3, conv_w, conv_b, gate_b, norm_g, *, chunk=MLSTM_CHUNK):
    bsz, s, _ = proj3.shape
    w = MLSTM_WIDTH
    kern = functools.partial(_mlstm_kernel, chunk=chunk)
    return pl.pallas_call(
        kern,
        grid=(bsz, s // chunk),
        in_specs=[
            pl.BlockSpec((1, chunk, 2 * w), lambda b, c: (b, c, COL_QK // (2 * w))),
            pl.BlockSpec((1, chunk, w), lambda b, c: (b, c, COL_V // w)),
            pl.BlockSpec((1, chunk, w), lambda b, c: (b, c, COL_O // w)),
            pl.BlockSpec((1, chunk, IF_WIDTH), lambda b, c: (b, c, 0)),
            pl.BlockSpec((CONV_WIDTH, 2 * w), lambda b, c: (0, 0)),
            pl.BlockSpec((1, 2 * w), lambda b, c: (0, 0)),
            pl.BlockSpec((1, IF_WIDTH), lambda b, c: (0, 0)),
            pl.BlockSpec((1, w), lambda b, c: (0, 0)),
        ],
        out_specs=pl.BlockSpec((1, chunk, w), lambda b, c: (b, c, 0)),
        out_shape=jax.ShapeDtypeStruct((bsz, s, w), BF16),
        scratch_shapes=[
            pltpu.VMEM((16, 2 * w), BF16),
            pltpu.VMEM((MLSTM_HEADS, HEAD_DIM, 2 * HEAD_DIM), F32),
            pltpu.VMEM((1, LANES), F32),
        ],
        compiler_params=_params(("arbitrary", "arbitrary")),
        name="mlstm",
    )(proj3, proj3, proj3, ifg3, conv_w, conv_b, gate_b, norm_g)


def _attn_kernel(q_ref, kp_ref, kc_ref, vp_ref, vc_ref, bias_ref, o_ref, lse_ref):
    qb = ATTN_BLOCK
    dh = HEAD_DIM
    n_q = q_ref.shape[2] // qb
    first = pl.program_id(2) == 0
    kcol = lax.broadcasted_iota(jnp.int32, (qb, 2 * qb), 1)
    dead = jnp.logical_and(first, kcol < qb)
    lane = lax.broadcasted_iota(jnp.int32, (qb, LANES), 1)
    for jq in range(n_q):
        rows = slice(jq * qb, (jq + 1) * qb)
        lse_all = jnp.zeros((qb, LANES), F32)
        for j in range(HEADS_PER_GROUP):
            hs = slice(j * dh, (j + 1) * dh)
            q = q_ref[0, 0][rows, hs]
            if jq == 0:
                k = jnp.concatenate([kp_ref[0, 0][:, hs], kc_ref[0, 0][0:qb, hs]], axis=0)
                v = jnp.concatenate([vp_ref[0, 0][:, hs], vc_ref[0, 0][0:qb, hs]], axis=0)
            else:
                k = kc_ref[0, 0][(jq - 1) * qb:(jq + 1) * qb, hs]
                v = vc_ref[0, 0][(jq - 1) * qb:(jq + 1) * qb, hs]
            s = lax.dot_general(q, k, (((1,), (1,)), ((), ())), preferred_element_type=F32)
            s = s * (dh ** -0.5) + bias_ref[j]
            if jq == 0:
                s = jnp.where(dead, NEG, s)
            mx = jnp.max(s, axis=1, keepdims=True)
            p = jnp.exp(s - mx)
            den = jnp.sum(p, axis=1, keepdims=True)
            o = jnp.dot(p.astype(BF16), v, preferred_element_type=F32) / den
            o_ref[0, 0, rows, hs] = o.astype(BF16)
            lse_all = jnp.where(lane == j, mx + jnp.log(den), lse_all)
        lse_ref[0, 0, rows, :] = lse_all


def _attn_group(qkv, bias, g, *, q_blocks=2):
    bsz, dilation, n, _ = qkv.shape
    gw = GROUP_WIDTH
    q_blocks = min(q_blocks, n // ATTN_BLOCK)
    run = q_blocks * ATTN_BLOCK

    def own(part, width=gw):
        return pl.BlockSpec((1, 1, run, width), lambda b, r, i: (b, r, i, part))

    def previous(part):
        return pl.BlockSpec((1, 1, ATTN_BLOCK, gw),
                            lambda b, r, i: (b, r, jnp.maximum(i * q_blocks - 1, 0), part))

    return pl.pallas_call(
        _attn_kernel,
        grid=(bsz, dilation, n // run),
        in_specs=[
            own(0),
            previous(1), own(1),
            previous(2), own(2),
            pl.BlockSpec((HEADS_PER_GROUP, ATTN_BLOCK, 2 * ATTN_BLOCK), lambda b, r, i: (0, 0, 0)),
        ],
        out_specs=[own(0), own(0, LANES)],
        out_shape=[
            jax.ShapeDtypeStruct((bsz, dilation, n, gw), BF16),
            jax.ShapeDtypeStruct((bsz, dilation, n, LANES), F32),
        ],
        compiler_params=_params(("arbitrary", "arbitrary", "arbitrary")),
        name=f"attn_g{g}",
    )(qkv, qkv, qkv, qkv, qkv, bias)


def _t5_bucket(dist):
    max_exact = REL_BUCKETS // 2
    d_f = jnp.maximum(dist, 1).astype(F32)
    large = max_exact + (jnp.log(d_f / max_exact) / math.log(REL_MAX_DIST / max_exact)
                         * (REL_BUCKETS - max_exact)).astype(jnp.int32)
    large = jnp.minimum(large, REL_BUCKETS - 1)
    return jnp.where(dist < max_exact, dist, large)


def _attn_bias(rel_bias, g, dilation):
    span = ATTN_SPAN
    buckets = _t5_bucket(jnp.arange(span + 1, dtype=jnp.int32) * dilation)
    vec = rel_bias[buckets][:, g * HEADS_PER_GROUP:(g + 1) * HEADS_PER_GROUP].T.astype(F32)
    qpos = jnp.arange(ATTN_BLOCK)[:, None]
    kpos = jnp.arange(2 * ATTN_BLOCK)[None, :]
    dist = qpos + ATTN_BLOCK - kpos
    valid = (dist >= 0) & (dist <= span)
    onehot = (dist[:, :, None] == jnp.arange(span + 1)[None, None, :]).astype(F32)
    table = jnp.einsum('qkj,hj->hqk', onehot, vec, precision=lax.Precision.HIGHEST)
    return jnp.where(valid[None], table, NEG)


def _merge_kernel(hm_ref, o0_ref, o1_ref, o2_ref, l0_ref, l1_ref, l2_ref, gm_ref, ga_ref, x_ref,
                  wbm_ref, wba_ref, wo_ref, ng_ref, rw_ref, rb_ref,
                  x1_ref, h2_ref, idx_ref, gate_ref, rank_ref, count_ref, cnt_ref):
    dh = HEAD_DIM
    bm = x_ref.shape[0]

    def token_order(ref):
        dilation = ref.shape[1]
        if dilation == 1:
            return ref[0, 0]
        perm = _residue_perm(bm, dilation, True).astype(BF16)
        blk = ref[0].reshape(bm, ref.shape[3])
        parts = [blk] if blk.dtype == BF16 else _bf16_parts(blk, 3)
        out = None
        for p in parts:
            moved = jnp.dot(perm, p, preferred_element_type=F32)
            out = moved if out is None else out + moved
        return out

    l0, l1, l2 = token_order(l0_ref), token_order(l1_ref), token_order(l2_ref)
    o0, o1, o2 = token_order(o0_ref), token_order(o1_ref), token_order(o2_ref)
    mx = jnp.maximum(jnp.maximum(l0, l1), l2)
    e0, e1, e2 = jnp.exp(l0 - mx), jnp.exp(l1 - mx), jnp.exp(l2 - mx)
    den = e0 + e1 + e2
    w0, w1, w2 = e0 / den, e1 / den, e2 / den
    parts = []
    for j in range(HEADS_PER_GROUP):
        hs = slice(j * dh, (j + 1) * dh)
        parts.append(w0[:, j:j + 1] * o0[:, hs] + w1[:, j:j + 1] * o1[:, hs]
                     + w2[:, j:j + 1] * o2[:, hs])
    ha = jnp.concatenate(parts, axis=1).astype(BF16)
    ym = jnp.dot(hm_ref[...], wbm_ref[...], preferred_element_type=F32)
    ya = jnp.dot(ha, wba_ref[...], preferred_element_type=F32)
    merged = (_sigmoid(gm_ref[...].astype(F32)) * ym + _sigmoid(ga_ref[...].astype(F32)) * ya)
    x1 = x_ref[...] + jnp.dot(merged.astype(BF16), wo_ref[...], preferred_element_type=F32)
    x1_ref[...] = x1
    h2 = x1 * lax.rsqrt(jnp.mean(x1 * x1, axis=-1, keepdims=True) + RMS_EPS) * ng_ref[...]
    _store_token_tiles(h2_ref, _pack_pairs(h2))
    h_hi, h_lo = _bf16_parts(h2, 2)
    logits = jnp.dot(jnp.concatenate([h_hi, h_lo, h_hi], axis=1), rw_ref[...],
                     preferred_element_type=F32) + rb_ref[...]
    lane = lax.broadcasted_iota(jnp.int32, logits.shape, 1)
    idx_all = jnp.zeros(logits.shape, jnp.int32)
    val_all = jnp.zeros(logits.shape, F32)
    top0 = None
    esum = None
    picks = []
    chosen = jnp.zeros(logits.shape, F32)
    for k in range(TOP_K):
        m = jnp.max(logits, axis=1, keepdims=True)
        sel = jnp.min(jnp.where(logits == m, lane, LANES), axis=1, keepdims=True)
        if k == 0:
            top0 = m
        e = jnp.exp(m - top0)
        esum = e if k == 0 else esum + e
        idx_all = jnp.where(lane == k, sel, idx_all)
        val_all = jnp.where(lane == k, e, val_all)
        pick = lane == sel
        picks.append(pick)
        chosen = chosen + pick.astype(F32)
        logits = jnp.where(pick, -jnp.inf, logits)
    idx_ref[...] = idx_all
    gate_ref[...] = val_all / esum

    @pl.when(pl.program_id(0) == 0)
    def _():
        cnt_ref[...] = jnp.zeros_like(cnt_ref)

    trow = lax.broadcasted_iota(jnp.int32, (bm, bm), 0)
    tcol = lax.broadcasted_iota(jnp.int32, (bm, bm), 1)
    earlier = jnp.dot((tcol < trow).astype(BF16), chosen.astype(BF16),
                      preferred_element_type=F32) + cnt_ref[...]
    rank_all = jnp.zeros(logits.shape, jnp.int32)
    for k in range(TOP_K):
        r = jnp.sum(jnp.where(picks[k], earlier, 0.0), axis=1, keepdims=True)
        rank_all = jnp.where(lane == k, r.astype(jnp.int32), rank_all)
    rank_ref[...] = rank_all
    cnt_ref[...] = cnt_ref[...] + jnp.sum(chosen, axis=0, keepdims=True)
    count_ref[...] = jnp.broadcast_to(cnt_ref[...], count_ref.shape)


def _merge(hm, outs, lses, proj, x2, wbm, wba, wo, ng, rw, rb, *, bm=256):
    n = x2.shape[0]
    d = D_MODEL
    gcol = COL_GATES // d
    n_blk = outs[0].shape[2] // bm

    def rows(width):
        return pl.BlockSpec((bm, width), lambda i: (i, 0))

    def full(a, b):
        return pl.BlockSpec((a, b), lambda i: (0, 0), pipeline_mode=pl.Buffered(1))

    def residue(arr):
        dilation, width = arr.shape[1], arr.shape[3]
        return pl.BlockSpec((1, dilation, bm // dilation, width),
                            lambda i: (i // n_blk, 0, i % n_blk, 0))

    return pl.pallas_call(
        _merge_kernel,
        grid=(n // bm,),
        in_specs=[
            rows(MLSTM_WIDTH),
            residue(outs[0]), residue(outs[1]), residue(outs[2]),
            residue(lses[0]), residue(lses[1]), residue(lses[2]),
            pl.BlockSpec((bm, d), lambda i: (i, gcol)),
            pl.BlockSpec((bm, d), lambda i: (i, gcol + 1)),
            rows(d),
            full(MLSTM_WIDTH, d), full(GROUP_WIDTH, d), full(d, d),
            full(1, d), full(3 * d, LANES), full(1, LANES),
        ],
        out_specs=[rows(d), pl.BlockSpec((bm * PACK_ROWS, LANES), lambda i: (i, 0)),
                   rows(LANES), rows(LANES), rows(LANES),
                   pl.BlockSpec((8, LANES), lambda i: (0, 0))],
        out_shape=[
            jax.ShapeDtypeStruct((n, d), F32),
            jax.ShapeDtypeStruct((n * PACK_ROWS, LANES), jnp.uint32),
            jax.ShapeDtypeStruct((n, LANES), jnp.int32),
            jax.ShapeDtypeStruct((n, LANES), F32),
            jax.ShapeDtypeStruct((n, LANES), jnp.int32),
            jax.ShapeDtypeStruct((8, LANES), F32),
        ],
        scratch_shapes=[pltpu.VMEM((1, LANES), F32)],
        compiler_params=_params(("arbitrary",)),
        name="merge",
    )(hm, outs[0], outs[1], outs[2], lses[0], lses[1], lses[2], proj, proj, x2,
      wbm, wba, wo, ng, rw, rb)


PACK_ROWS = 8


def _pack_pairs(x):
    w = x.shape[1] // 2
    lo = lax.bitcast_convert_type(x[:, :w].astype(BF16).astype(F32), jnp.uint32) >> 16
    hi = lax.bitcast_convert_type(x[:, w:].astype(BF16).astype(F32), jnp.uint32)
    return (hi & jnp.uint32(0xFFFF0000)) | lo


def _unpack_pairs(words):
    lo = lax.bitcast_convert_type(words << 16, F32)
    hi = lax.bitcast_convert_type(words & jnp.uint32(0xFFFF0000), F32)
    return lo, hi


def _store_token_tiles(ref, words):
    rows = words.shape[0]
    for s in range(PACK_ROWS):
        ref[pl.ds(s, rows, stride=PACK_ROWS), :] = words[:, s * LANES:(s + 1) * LANES]


def _dispatch_kernel(fs_ref, fl_ref, nu_ref, dest_ref, hp_ref, xb_hbm, zero_ref, sem, zsem, *,
                     tokens, tm, n_blk):
    pr = PACK_ROWS
    fill_sizes = [1 << b for b in reversed(range((tm - 1).bit_length()))]

    def fill_copies(e):
        off = fs_ref[e]
        for p in fill_sizes:
            take = (fl_ref[e] & p) != 0
            dst = xb_hbm.at[pl.ds(pl.multiple_of(off * pr, pr), p * pr), :]
            yield take, pltpu.make_async_copy(zero_ref.at[pl.ds(0, p * pr), :], dst, zsem)
            off = off + jnp.where(take, p, 0)

    def block_copy(b):
        dst = xb_hbm.at[pl.ds(pl.multiple_of(b * (tm * pr), tm * pr), tm * pr), :]
        return pltpu.make_async_copy(zero_ref, dst, zsem)

    @pl.when(pl.program_id(0) == 0)
    def _():
        zero_ref[...] = jnp.zeros_like(zero_ref)
        for start in (True, False):
            def per_expert(e, carry):
                for take, cp in fill_copies(e):
                    @pl.when(take)
                    def _():
                        cp.start() if start else cp.wait()
                return carry

            def per_block(b, carry):
                block_copy(b).start() if start else block_copy(b).wait()
                return carry

            lax.fori_loop(0, N_EXPERTS, per_expert, 0)
            lax.fori_loop(nu_ref[0], n_blk, per_block, 0)

    def issue(t, carry):
        src = hp_ref.at[pl.ds(pl.multiple_of(t * pr, pr), pr), :]
        for k in range(TOP_K):
            slot = dest_ref[t * TOP_K + k]
            pltpu.make_async_copy(src, xb_hbm.at[pl.ds(pl.multiple_of(slot * pr, pr), pr), :],
                                  sem).start(priority=k % 2)
        return carry

    lax.fori_loop(0, tokens, issue, 0)
    for k in range(TOP_K):
        pltpu.make_async_copy(hp_ref, xb_hbm.at[pl.ds(0, tokens * pr), :], sem).wait()


def _dispatch(fill_start, fill_len, n_used, dest, h2p, n_pad, *, tokens=256, tm=MOE_TILE):
    pr = PACK_ROWS
    n_tok = h2p.shape[0] // pr
    kern = functools.partial(_dispatch_kernel, tokens=tokens, tm=tm, n_blk=n_pad // tm)
    return pl.pallas_call(
        kern,
        grid_spec=pltpu.PrefetchScalarGridSpec(
            num_scalar_prefetch=3,
            grid=(n_tok // tokens,),
            in_specs=[
                pl.BlockSpec((tokens * TOP_K,), lambda i, fs, fl, nu: (i,),
                             memory_space=pltpu.SMEM),
                pl.BlockSpec((tokens * pr, LANES), lambda i, fs, fl, nu: (i, 0)),
            ],
            out_specs=pl.BlockSpec(memory_space=pl.ANY),
            scratch_shapes=[pltpu.VMEM((tm * pr, LANES), jnp.uint32),
                            pltpu.SemaphoreType.DMA(()), pltpu.SemaphoreType.DMA(())],
        ),
        out_shape=jax.ShapeDtypeStruct((n_pad * pr, LANES), jnp.uint32),
        compiler_params=_params(("arbitrary",)),
        name="dispatch",
    )(fill_start, fill_len, n_used, dest, h2p)


def _cast_rows(src_ref, dst_ref, rows):
    total = src_ref.shape[0]

    def body(i, carry):
        r = pl.multiple_of(i * rows, rows)
        dst_ref[pl.ds(r, rows), :] = src_ref[pl.ds(r, rows), :].astype(BF16)
        return carry

    lax.fori_loop(0, total // rows, body, 0)


def _stream_expert_weights(be_ref, nu_ref, ne_ref, copies, convert):
    k, j = pl.program_id(0), pl.program_id(1)
    expert = be_ref[j]
    used = j < nu_ref[0]
    run_start = jnp.logical_or(j == 0, expert != be_ref[jnp.maximum(j - 1, 0)])

    @pl.when(jnp.logical_and(k == 0, j == 0))
    def _():
        for cp in copies(expert, k):
            cp.start()

    @pl.when(jnp.logical_and(used, run_start))
    def _():
        for cp in copies(expert, k):
            cp.wait()
        convert()
        last_run = ne_ref[j] < 0
        nxt_e = jnp.where(last_run, be_ref[0], ne_ref[j])
        nxt_k = jnp.where(last_run, k + 1, k)

        @pl.when(nxt_k < pl.num_programs(0))
        def _():
            for cp in copies(nxt_e, nxt_k):
                cp.start()

    return used


def _per_block_rows(used, bv_ref, o_ref, compute, store=None, rows_per_token=1):
    tm = o_ref.shape[0] // rows_per_token
    few = bv_ref[pl.program_id(1)] <= tm // 2

    def run(rows):
        res = compute(rows)
        head = o_ref.at[pl.ds(0, rows * rows_per_token), :]
        if store is None:
            head[...] = res
        else:
            store(head, res)
        if rows < tm:
            tail = o_ref.at[pl.ds(rows * rows_per_token, (tm - rows) * rows_per_token), :]
            tail[...] = jnp.zeros_like(tail)

    @pl.when(jnp.logical_and(used, few))
    def _():
        run(tm // 2)

    @pl.when(jnp.logical_and(used, jnp.logical_not(few)))
    def _():
        run(tm)

    @pl.when(jnp.logical_not(used))
    def _():
        o_ref[...] = jnp.zeros_like(o_ref)


def _gate_up_kernel(be_ref, nu_ref, ne_ref, bv_ref, x_ref, w_hbm, bg_ref, bu_ref, o_ref,
                    stage_ref, w_bf, sem):
    tf = o_ref.shape[1]
    kt = pl.num_programs(0)

    def copies(expert, k):
        return [pltpu.make_async_copy(
            w_hbm.at[expert, :, pl.ds(pl.multiple_of((half * kt + k) * tf, tf), tf)],
            stage_ref.at[half], sem.at[half]) for half in range(2)]

    def convert():
        for half in range(2):
            _cast_rows(stage_ref.at[half], w_bf.at[half], 256)

    used = _stream_expert_weights(be_ref, nu_ref, ne_ref, copies, convert)

    def compute(rows):
        halves = [_unpack_pairs(x_ref[pl.ds(s, rows, stride=PACK_ROWS), :])
                  for s in range(PACK_ROWS)]
        x = jnp.concatenate([h[0].astype(BF16) for h in halves]
                            + [h[1].astype(BF16) for h in halves], axis=1)
        gate = jnp.dot(x, w_bf[0], preferred_element_type=F32) + bg_ref[0]
        up = jnp.dot(x, w_bf[1], preferred_element_type=F32) + bu_ref[0]
        gate = jnp.minimum(gate, SWIGLU_LIMIT)
        up = jnp.clip(up, -SWIGLU_LIMIT, SWIGLU_LIMIT)
        glu = gate * _sigmoid(SWIGLU_ALPHA * gate)
        return ((up + 1.0) * glu).astype(BF16)

    _per_block_rows(used, bv_ref, o_ref, compute)


def _used_block(j, nu):
    return jnp.minimum(j, nu[0] - 1)


def _gate_up(sched, xb, w_gate_up, b_gate_up3, *, tm=MOE_TILE, tf=1024):
    n_pad, d = xb.shape[0] // PACK_ROWS, D_MODEL
    kt = D_FF // tf

    def bspec(off):
        return pl.BlockSpec((1, 1, tf),
                            lambda k, j, be, nu, ne, bv: (be[_used_block(j, nu)], 0, off + k))

    return pl.pallas_call(
        _gate_up_kernel,
        grid_spec=pltpu.PrefetchScalarGridSpec(
            num_scalar_prefetch=4,
            grid=(kt, n_pad // tm),
            in_specs=[
                pl.BlockSpec((tm * PACK_ROWS, LANES),
                             lambda k, j, be, nu, ne, bv: (_used_block(j, nu), 0)),
                pl.BlockSpec(memory_space=pl.ANY),
                bspec(0), bspec(kt),
            ],
            out_specs=pl.BlockSpec((tm, tf), lambda k, j, be, nu, ne, bv: (j, k)),
            scratch_shapes=[pltpu.VMEM((2, d, tf), F32), pltpu.VMEM((2, d, tf), BF16),
                            pltpu.SemaphoreType.DMA((2,))],
        ),
        out_shape=jax.ShapeDtypeStruct((n_pad, D_FF), BF16),
        compiler_params=_params(("arbitrary", "arbitrary")),
        name="gate_up",
    )(*sched, xb, w_gate_up, b_gate_up3, b_gate_up3)


def _down_kernel(be_ref, nu_ref, ne_ref, bv_ref, h_ref, w_hbm, bd_ref, o_ref, stage_ref, w_bf,
                 sem):
    def copies(expert, k):
        del k
        return [pltpu.make_async_copy(w_hbm.at[expert], stage_ref, sem.at[0])]

    def convert():
        _cast_rows(stage_ref, w_bf, 256)

    used = _stream_expert_weights(be_ref, nu_ref, ne_ref, copies, convert)

    def compute(rows):
        y = jnp.dot(h_ref[0:rows, :], w_bf[...], preferred_element_type=F32) + bd_ref[0]
        return _pack_pairs(y)

    _per_block_rows(used, bv_ref, o_ref, compute, store=_store_token_tiles,
                    rows_per_token=PACK_ROWS)


def _down(sched, hb, w_down, b_down3, *, tm=MOE_TILE):
    n_pad, f = hb.shape
    d = D_MODEL
    return pl.pallas_call(
        _down_kernel,
        grid_spec=pltpu.PrefetchScalarGridSpec(
            num_scalar_prefetch=4,
            grid=(1, n_pad // tm),
            in_specs=[
                pl.BlockSpec((tm, f), lambda k, j, be, nu, ne, bv: (_used_block(j, nu), 0)),
                pl.BlockSpec(memory_space=pl.ANY),
                pl.BlockSpec((1, 1, d),
                             lambda k, j, be, nu, ne, bv: (be[_used_block(j, nu)], 0, 0)),
            ],
            out_specs=pl.BlockSpec((tm * PACK_ROWS, LANES), lambda k, j, be, nu, ne, bv: (j, 0)),
            scratch_shapes=[pltpu.VMEM((f, d), F32), pltpu.VMEM((f, d), BF16),
                            pltpu.SemaphoreType.DMA((1,))],
        ),
        out_shape=jax.ShapeDtypeStruct((n_pad * PACK_ROWS, LANES), jnp.uint32),
        compiler_params=_params(("arbitrary", "arbitrary")),
        name="down",
    )(*sched, hb, w_down, b_down3)


def _combine_kernel(dest_ref, dest_next_ref, y_hbm, gate_ref, x1_ref, ng_ref, o_ref, buf_ref, sem,
                    *, rows):
    pr = PACK_ROWS
    slab = 64
    half = x1_ref.shape[1] // 2
    i = pl.program_id(0)
    slot_rows = TOP_K * rows * pr
    cur = i % 2
    base = pl.multiple_of(cur * slot_rows, slot_rows)
    nxt_base = pl.multiple_of((1 - cur) * slot_rows, slot_rows)

    def gather(idx_ref, dst_base, buf, t):
        for k in range(TOP_K):
            slot = idx_ref[t * TOP_K + k]
            pltpu.make_async_copy(
                y_hbm.at[pl.ds(pl.multiple_of(slot * pr, pr), pr), :],
                buf_ref.at[pl.ds(pl.multiple_of(dst_base + (k * rows + t) * pr, pr), pr), :],
                sem.at[buf]).start(priority=k % 2)

    @pl.when(i == 0)
    def _():
        def first(t, carry):
            gather(dest_ref, 0, 0, t)
            return carry

        lax.fori_loop(0, rows, first, 0)

    pltpu.make_async_copy(y_hbm.at[pl.ds(0, slot_rows), :],
                          buf_ref.at[pl.ds(base, slot_rows), :], sem.at[cur]).wait()

    def reduce_slab(t0):
        gates = [gate_ref[pl.ds(t0, slab), k:k + 1] for k in range(TOP_K)]
        lows, highs = [], []
        for s in range(pr):
            lo = x1_ref[pl.ds(t0, slab), s * LANES:(s + 1) * LANES]
            hi = x1_ref[pl.ds(t0, slab), half + s * LANES:half + (s + 1) * LANES]
            for k in range(TOP_K):
                y_lo, y_hi = _unpack_pairs(
                    buf_ref[pl.ds(base + (k * rows + t0) * pr + s, slab, stride=pr), :])
                lo = lo + gates[k] * y_lo
                hi = hi + gates[k] * y_hi
            lows.append(lo)
            highs.append(hi)
        acc = jnp.concatenate(lows + highs, axis=1)
        o_ref[pl.ds(t0, slab), :] = (
            acc * lax.rsqrt(jnp.mean(acc * acc, axis=-1, keepdims=True) + RMS_EPS) * ng_ref[...])

    def trip_with_prefetch(j, carry):
        t0 = pl.multiple_of(j * slab, slab)
        for tt in range(slab):
            gather(dest_next_ref, nxt_base, 1 - cur, t0 + tt)
        reduce_slab(t0)
        return carry

    def trip(j, carry):
        reduce_slab(pl.multiple_of(j * slab, slab))
        return carry

    has_next = i + 1 < pl.num_programs(0)

    @pl.when(has_next)
    def _():
        lax.fori_loop(0, rows // slab, trip_with_prefetch, 0)

    @pl.when(jnp.logical_not(has_next))
    def _():
        lax.fori_loop(0, rows // slab, trip, 0)


def _combine(dest, yb, gates, x1, ng, *, rows=256):
    n, d = x1.shape
    kern = functools.partial(_combine_kernel, rows=rows)
    n_steps = n // rows
    return pl.pallas_call(
        kern,
        grid=(n_steps,),
        in_specs=[
            pl.BlockSpec((rows * TOP_K,), lambda i: (i,), memory_space=pltpu.SMEM),
            pl.BlockSpec((rows * TOP_K,), lambda i: (jnp.minimum(i + 1, n_steps - 1),),
                         memory_space=pltpu.SMEM),
            pl.BlockSpec(memory_space=pl.ANY),
            pl.BlockSpec((rows, LANES), lambda i: (i, 0)),
            pl.BlockSpec((rows, d), lambda i: (i, 0)),
            pl.BlockSpec((1, d), lambda i: (0, 0)),
        ],
        out_specs=pl.BlockSpec((rows, d), lambda i: (i, 0)),
        out_shape=jax.ShapeDtypeStruct((n, d), F32),
        scratch_shapes=[pltpu.VMEM((2 * TOP_K * rows * PACK_ROWS, LANES), jnp.uint32),
                        pltpu.SemaphoreType.DMA((2,))],
        compiler_params=_params(("arbitrary",)),
        name="combine",
    )(dest, dest, yb, gates, x1, ng)


def _routing(top_idx, rank, counts, tm):
    n_tok = top_idx.shape[0]
    n_asg = n_tok * TOP_K
    e_flat = top_idx.reshape(n_asg)
    padded = (counts + tm - 1) // tm * tm
    pend = jnp.cumsum(padded)
    pstart = pend - padded
    experts = jnp.arange(N_EXPERTS, dtype=jnp.int32)
    start_of = jnp.sum(jnp.where(e_flat[:, None] == experts[None, :], pstart[None, :], 0), axis=1)
    dest = (start_of + rank.reshape(n_asg)).astype(jnp.int32)
    n_blk = -(-(n_asg + N_EXPERTS * (tm - 1)) // tm)
    n_pad = n_blk * tm
    fill_start = (pstart + counts).astype(jnp.int32)
    fill_len = (padded - counts).astype(jnp.int32)
    block_start = jnp.arange(n_blk, dtype=jnp.int32) * tm
    block_e = jnp.minimum(
        jnp.sum((pend[None, :] <= block_start[:, None]).astype(jnp.int32), axis=1),
        N_EXPERTS - 1).astype(jnp.int32)
    n_used = (pend[-1:] // tm).astype(jnp.int32)
    run_end = pend[block_e] // tm
    next_e = jnp.where(run_end < n_used[0], block_e[jnp.minimum(run_end, n_blk - 1)], -1)
    block_rows = jnp.clip(fill_start[block_e] - block_start, 0, tm)
    sched = (block_e, n_used, next_e.astype(jnp.int32), block_rows.astype(jnp.int32))
    return dest, fill_start, fill_len, sched, n_pad


def _layer(x, norm_mix_g, w_in, conv_w, conv_b, igate_b, fgate_b, mlstm_norm_g, rel_bias,
           w_branch_mlstm, w_branch_attn, w_out, norm_moe_g, router_w, router_b,
           w_gate_up, b_gate_up, w_down, b_down, out_norm_g):
    bsz, s, d = x.shape
    n = bsz * s
    x2 = x.reshape(n, d)
    w2, w1 = 2 * MLSTM_WIDTH, MLSTM_WIDTH
    o_qk, o_v, o_o = 0, w2, w2 + w1
    o_i = o_o + w1
    o_f = o_i + MLSTM_HEADS
    o_qa = o_f + MLSTM_HEADS
    o_ka, o_va = o_qa + ATTN_WIDTH, o_qa + 2 * ATTN_WIDTH
    o_g = o_va + ATTN_WIDTH
    w_main = jnp.concatenate([w_in[:, o_qk:o_i], w_in[:, o_g:]], axis=1).astype(BF16)
    zpad = jnp.zeros((d, LANES - MLSTM_HEADS), w_in.dtype)
    w_if = jnp.concatenate([w_in[:, o_i:o_f], zpad, w_in[:, o_f:o_qa], zpad], axis=1).astype(BF16)
    bpad = jnp.zeros((LANES - MLSTM_HEADS,), F32)
    gate_b = jnp.concatenate([igate_b, bpad, fgate_b, bpad]).reshape(1, IF_WIDTH)

    norm_g = norm_mix_g.reshape(1, d)
    proj, ifg = _in_proj(x2, norm_g, w_main, w_if)
    proj3 = proj.reshape(bsz, s, D_MAIN)
    hm = _mlstm(proj3, ifg.reshape(bsz, s, IF_WIDTH), conv_w, conv_b.reshape(1, -1), gate_b,
                mlstm_norm_g.reshape(1, -1)).reshape(n, MLSTM_WIDTH)

    outs, lses = [], []
    gw = GROUP_WIDTH
    for g, (_, dilation) in enumerate(ATTN_GROUPS):
        w_qkv = jnp.concatenate(
            [w_in[:, o + g * gw:o + (g + 1) * gw] for o in (o_qa, o_ka, o_va)], axis=1).astype(BF16)
        qkv = _attn_proj(x2, norm_g, w_qkv, bsz, s, dilation)
        o_g_, lse_g = _attn_group(qkv, _attn_bias(rel_bias, g, dilation), g)
        outs.append(o_g_)
        lses.append(lse_g)

    rw = jnp.concatenate([router_w, jnp.zeros((d, LANES - N_EXPERTS), F32)], axis=1)
    rw_hi = rw.astype(BF16)
    rw_lo = (rw - rw_hi.astype(F32)).astype(BF16)
    rw = jnp.concatenate([rw_hi, rw_hi, rw_lo], axis=0)
    rb = jnp.concatenate([router_b, jnp.full((LANES - N_EXPERTS,), NEG, F32)]).reshape(1, LANES)
    x1, h2, idx, gates, rank, counts = _merge(
        hm, outs, lses, proj, x2, w_branch_mlstm.astype(BF16), w_branch_attn.astype(BF16),
        w_out.astype(BF16), norm_moe_g.reshape(1, d), rw, rb)

    dest, fill_start, fill_len, sched, n_pad = _routing(
        idx[:, :TOP_K], rank[:, :TOP_K], counts[0, :N_EXPERTS].astype(jnp.int32), MOE_TILE)
    xb = _dispatch(fill_start, fill_len, sched[1], dest, h2, n_pad)
    hb = _gate_up(sched, xb, w_gate_up, b_gate_up.reshape(N_EXPERTS, 1, 2 * D_FF))
    yb = _down(sched, hb, w_down, b_down.reshape(N_EXPERTS, 1, d))
    out = _combine(dest, yb, gates, x1, out_norm_g.reshape(1, d))
    return out.reshape(bsz, s, d)


def kernel(x, norm_mix_g, w_in, conv_w, conv_b, igate_b, fgate_b, mlstm_norm_g, rel_bias,
           w_branch_mlstm, w_branch_attn, w_out, norm_moe_g, router_w, router_b,
           w_gate_up, b_gate_up, w_down, b_down, norm_final_g):
    assert w_in.shape[0] == 1, "single-layer block"
    return _layer(x, norm_mix_g[0], w_in[0], conv_w[0], conv_b[0], igate_b[0], fgate_b[0],
                  mlstm_norm_g[0], rel_bias, w_branch_mlstm[0], w_branch_attn[0], w_out[0],
                  norm_moe_g[0], router_w[0], router_b[0], w_gate_up[0], b_gate_up[0],
                  w_down[0], b_down[0], norm_final_g)
```

```python
import functools
import math

import jax
import jax.numpy as jnp
from jax import lax
from jax.experimental import pallas as pl
from jax.experimental.pallas import tpu as pltpu

F32 = jnp.float32
BF16 = jnp.bfloat16

D_MODEL = 2048
MLSTM_HEADS = 8
HEAD_DIM = 128
MLSTM_WIDTH = MLSTM_HEADS * HEAD_DIM
CONV_WIDTH = 4
ATTN_GROUPS = ((128, 1), (512, 4), (2048, 16))
N_GROUPS = 3
HEADS_PER_GROUP = 4
ATTN_HEADS = HEADS_PER_GROUP * N_GROUPS
ATTN_WIDTH = ATTN_HEADS * HEAD_DIM
GROUP_WIDTH = HEADS_PER_GROUP * HEAD_DIM
ATTN_BLOCK = 128
ATTN_SPAN = 128
REL_BUCKETS = 32
REL_MAX_DIST = 2048
N_EXPERTS = 32
TOP_K = 4
D_FF = D_MODEL
SWIGLU_LIMIT = 7.0
SWIGLU_ALPHA = 1.702
RMS_EPS = 1e-6
NEG = -1e30

COL_QK = 0
COL_V = 2 * MLSTM_WIDTH
COL_O = COL_V + MLSTM_WIDTH
COL_GATES = COL_O + MLSTM_WIDTH
D_MAIN = COL_GATES + 2 * D_MODEL
QKV_WIDTH = 3 * GROUP_WIDTH
LANES = 128
IF_WIDTH = 2 * LANES

MLSTM_CHUNK = 128
MOE_TILE = 1024
MOE_ROW_STEP = 256
VMEM_LIMIT = 56 * 1024 * 1024


def _sigmoid(x):
    return 0.5 * jnp.tanh(0.5 * x) + 0.5


def _bf16_parts(x, n_parts):
    parts = []
    for _ in range(n_parts):
        p = x.astype(BF16)
        parts.append(p)
        x = x - p.astype(F32)
    return parts


def _log_sigmoid(x):
    return -(jnp.maximum(-x, 0.0) + jnp.log1p(jnp.exp(-jnp.abs(x))))


def _params(sem):
    return pltpu.CompilerParams(dimension_semantics=sem, vmem_limit_bytes=VMEM_LIMIT)


def _in_proj_kernel(x_ref, g_ref, w_ref, wif_ref, o_ref, oif_ref, h_ref, *, bm, rows):
    @pl.when(pl.program_id(1) == 0)
    def _():
        for r in range(0, bm, rows):
            x = x_ref[r:r + rows, :]
            ms = jnp.mean(x * x, axis=-1, keepdims=True)
            h = (x * lax.rsqrt(ms + RMS_EPS) * g_ref[...]).astype(BF16)
            h_ref[r:r + rows, :] = h
            oif_ref[r:r + rows, :] = jnp.dot(h, wif_ref[...], preferred_element_type=F32)

    o_ref[...] = jnp.dot(h_ref[...], w_ref[...], preferred_element_type=F32).astype(BF16)


def _residue_perm(size, dilation, inverse):
    per = size // dilation
    i = lax.broadcasted_iota(jnp.int32, (size, size), 0)
    j = lax.broadcasted_iota(jnp.int32, (size, size), 1)
    if inverse:
        src = (i & (dilation - 1)) * per + (i >> (dilation.bit_length() - 1))
    else:
        src = (i & (per - 1)) * dilation + (i >> (per.bit_length() - 1))
    return j == src


def _attn_proj_kernel(x_ref, g_ref, w_ref, o_ref, h_ref, *, bm, dilation, rows):
    per = rows // dilation
    if dilation > 1:
        perm = _residue_perm(rows, dilation, False).astype(BF16)
    for c in range(0, bm, rows):
        x = x_ref[c:c + rows, :]
        ms = jnp.mean(x * x, axis=-1, keepdims=True)
        h = (x * lax.rsqrt(ms + RMS_EPS) * g_ref[...]).astype(BF16)
        if dilation > 1:
            h = jnp.dot(perm, h, preferred_element_type=F32).astype(BF16)
        h_ref[c:c + rows, :] = h
    res = jnp.dot(h_ref[...], w_ref[...], preferred_element_type=F32).astype(BF16)
    for c in range(bm // rows):
        for r in range(dilation):
            o_ref[0, r, c * per:(c + 1) * per, :] = res[c * rows + r * per:c * rows + (r + 1) * per, :]


def _attn_proj(x2, g, w_qkv, bsz, s, dilation, *, bm=1024):
    n_blk = s // bm
    per = bm // dilation
    kern = functools.partial(_attn_proj_kernel, bm=bm, dilation=dilation, rows=256)
    return pl.pallas_call(
        kern,
        grid=(bsz, n_blk),
        in_specs=[
            pl.BlockSpec((bm, D_MODEL), lambda b, i: (b * n_blk + i, 0)),
            pl.BlockSpec((1, D_MODEL), lambda b, i: (0, 0)),
            pl.BlockSpec((D_MODEL, QKV_WIDTH), lambda b, i: (0, 0),
                         pipeline_mode=pl.Buffered(1)),
        ],
        out_specs=pl.BlockSpec((1, dilation, per, QKV_WIDTH), lambda b, i: (b, 0, i, 0)),
        out_shape=jax.ShapeDtypeStruct((bsz, dilation, s // dilation, QKV_WIDTH), BF16),
        scratch_shapes=[pltpu.VMEM((bm, D_MODEL), BF16)],
        compiler_params=_params(("arbitrary", "arbitrary")),
        name=f"attn_proj_d{dilation}",
    )(x2, g, w_qkv)


def _in_proj(x2, g, w_main, w_if, *, bm=1024, bn=1024):
    n = x2.shape[0]
    bm = min(bm, n)
    kern = functools.partial(_in_proj_kernel, bm=bm, rows=256)
    return pl.pallas_call(
        kern,
        grid=(n // bm, D_MAIN // bn),
        in_specs=[
            pl.BlockSpec((bm, D_MODEL), lambda i, j: (i, 0)),
            pl.BlockSpec((1, D_MODEL), lambda i, j: (0, 0)),
            pl.BlockSpec((D_MODEL, bn), lambda i, j: (0, j)),
            pl.BlockSpec((D_MODEL, IF_WIDTH), lambda i, j: (0, 0)),
        ],
        out_specs=[
            pl.BlockSpec((bm, bn), lambda i, j: (i, j)),
            pl.BlockSpec((bm, IF_WIDTH), lambda i, j: (i, 0)),
        ],
        out_shape=[
            jax.ShapeDtypeStruct((n, D_MAIN), BF16),
            jax.ShapeDtypeStruct((n, IF_WIDTH), F32),
        ],
        scratch_shapes=[pltpu.VMEM((bm, D_MODEL), BF16)],
        compiler_params=_params(("arbitrary", "arbitrary")),
        name="in_proj",
    )(x2, g, w_main, w_if)


def _mlstm_kernel(qk_ref, v_ref, og_ref, if_ref, cw_ref, cb_ref, gb_ref, ng_ref, out_ref,
                  tail_ref, c_ref, m_ref, *, chunk):
    L = chunk
    dh = HEAD_DIM

    @pl.when(pl.program_id(1) == 0)
    def _():
        tail_ref[...] = jnp.zeros_like(tail_ref)
        c_ref[...] = jnp.zeros_like(c_ref)
        m_ref[...] = jnp.zeros_like(m_ref)

    tail = tail_ref.shape[0]
    cur = qk_ref[0]
    ext = jnp.concatenate([tail_ref[...], cur], axis=0)
    tail_ref[...] = cur[L - tail:, :]
    srow = lax.broadcasted_iota(jnp.int32, (L, tail + L), 0)
    scol = lax.broadcasted_iota(jnp.int32, (L, tail + L), 1)
    acc = cur.astype(F32) * cw_ref[CONV_WIDTH - 1:CONV_WIDTH, :] + cb_ref[...]
    for s in range(1, CONV_WIDTH):
        shift = (scol == srow + (tail - s)).astype(BF16)
        acc = acc + (jnp.dot(shift, ext, preferred_element_type=F32)
                     * cw_ref[CONV_WIDTH - 1 - s:CONV_WIDTH - s, :])
    qk = acc * _sigmoid(acc)
    q_all = qk[:, :MLSTM_WIDTH]
    k_all = qk[:, MLSTM_WIDTH:] * (dh ** -0.5)

    head_lane = lax.broadcasted_iota(jnp.int32, (L, LANES), 1) < MLSTM_HEADS
    gi = jnp.where(head_lane, if_ref[0][:, :LANES] + gb_ref[:, :LANES], 0.0)
    lf = jnp.where(head_lane, _log_sigmoid(if_ref[0][:, LANES:] + gb_ref[:, LANES:]), 0.0)
    row = lax.broadcasted_iota(jnp.int32, (L, L), 0)
    col = lax.broadcasted_iota(jnp.int32, (L, L), 1)
    causal = col <= row
    cum = jnp.dot(causal.astype(F32), lf, preferred_element_type=F32,
                  precision=lax.Precision.HIGHEST)
    rmat = gi - cum
    cmax = rmat
    trow = lax.broadcasted_iota(jnp.int32, (L, LANES), 0)
    step = 1
    while step < L:
        cmax = jnp.maximum(cmax, jnp.where(trow >= step, pltpu.roll(cmax, step, axis=0), -jnp.inf))
        step *= 2
    m_prev = m_ref[...]
    gmax = jnp.maximum(m_prev, cmax)
    s_inter = jnp.exp(m_prev - gmax)
    inv_scale = jnp.exp(-(cum + gmax))
    b_last = cum[L - 1:L, :]
    m_new = b_last + gmax[L - 1:L, :]
    w_state = jnp.exp(b_last + rmat - m_new)
    decay = jnp.exp(b_last + m_prev - m_new)
    m_ref[...] = m_new
    rmat_t = rmat.T
    ones_blk = jnp.ones((L, dh), BF16)

    def column(mat, h):
        return jnp.broadcast_to(mat[:, h:h + 1], (L, dh))

    for h in range(MLSTM_HEADS):
        hs = slice(h * dh, (h + 1) * dh)
        p = jnp.exp(jnp.where(causal, rmat_t[h:h + 1, :] - column(gmax, h), NEG))
        qh = q_all[:, hs]
        kh = k_all[:, hs]
        s = lax.dot_general(qh.astype(BF16), kh.astype(BF16), (((1,), (1,)), ((), ())),
                            preferred_element_type=F32)
        vaug = jnp.concatenate([v_ref[0][:, hs], ones_blk], axis=1)
        c_prev = c_ref[h]
        lhs = jnp.concatenate([(p * s).astype(BF16), (column(s_inter, h) * qh).astype(BF16)],
                              axis=1)
        rhs = jnp.concatenate([vaug, c_prev.astype(BF16)], axis=0)
        nd = jnp.dot(lhs, rhs, preferred_element_type=F32)
        hh = nd[:, :dh] / jnp.maximum(jnp.abs(nd[:, dh:]), column(inv_scale, h))

        kw = (column(w_state, h) * kh).astype(BF16)
        kv = lax.dot_general(kw, vaug, (((0,), (0,)), ((), ())), preferred_element_type=F32)
        c_ref[h] = decay[:, h:h + 1] * c_prev + kv

        y = hh * lax.rsqrt(jnp.mean(hh * hh, axis=-1, keepdims=True) + RMS_EPS) * ng_ref[:, hs]
        y = y * _sigmoid(og_ref[0][:, hs].astype(F32))
        out_ref[0, :, hs] = y.astype(BF16)


def _mlstm(proj3, ifg3, conv_w, conv_b, gate_b, norm_g, *, chunk=MLSTM_CHUNK):
    bsz, s, _ = proj3.shape
    w = MLSTM_WIDTH
    kern = functools.partial(_mlstm_kernel, chunk=chunk)
    return pl.pallas_call(
        kern,
        grid=(bsz, s // chunk),
        in_specs=[
            pl.BlockSpec((1, chunk, 2 * w), lambda b, c: (b, c, COL_QK // (2 * w))),
            pl.BlockSpec((1, chunk, w), lambda b, c: (b, c, COL_V // w)),
            pl.BlockSpec((1, chunk, w), lambda b, c: (b, c, COL_O // w)),
            pl.BlockSpec((1, chunk, IF_WIDTH), lambda b, c: (b, c, 0)),
            pl.BlockSpec((CONV_WIDTH, 2 * w), lambda b, c: (0, 0)),
            pl.BlockSpec((1, 2 * w), lambda b, c: (0, 0)),
            pl.BlockSpec((1, IF_WIDTH), lambda b, c: (0, 0)),
            pl.BlockSpec((1, w), lambda b, c: (0, 0)),
        ],
        out_specs=pl.BlockSpec((1, chunk, w), lambda b, c: (b, c, 0)),
        out_shape=jax.ShapeDtypeStruct((bsz, s, w), BF16),
        scratch_shapes=[
            pltpu.VMEM((16, 2 * w), BF16),
            pltpu.VMEM((MLSTM_HEADS, HEAD_DIM, 2 * HEAD_DIM), F32),
            pltpu.VMEM((1, LANES), F32),
        ],
        compiler_params=_params(("arbitrary", "arbitrary")),
        name="mlstm",
    )(proj3, proj3, proj3, ifg3, conv_w, conv_b, gate_b, norm_g)


def _attn_kernel(q_ref, kp_ref, kc_ref, vp_ref, vc_ref, bias_ref, o_ref, lse_ref):
    qb = ATTN_BLOCK
    dh = HEAD_DIM
    n_q = q_ref.shape[2] // qb
    first = pl.program_id(2) == 0
    kcol = lax.broadcasted_iota(jnp.int32, (qb, 2 * qb), 1)
    dead = jnp.logical_and(first, kcol < qb)
    lane = lax.broadcasted_iota(jnp.int32, (qb, LANES), 1)
    for jq in range(n_q):
        rows = slice(jq * qb, (jq + 1) * qb)
        lse_all = jnp.zeros((qb, LANES), F32)
        for j in range(HEADS_PER_GROUP):
            hs = slice(j * dh, (j + 1) * dh)
            q = q_ref[0, 0][rows, hs]
            if jq == 0:
                k = jnp.concatenate([kp_ref[0, 0][:, hs], kc_ref[0, 0][0:qb, hs]], axis=0)
                v = jnp.concatenate([vp_ref[0, 0][:, hs], vc_ref[0, 0][0:qb, hs]], axis=0)
            else:
                k = kc_ref[0, 0][(jq - 1) * qb:(jq + 1) * qb, hs]
                v = vc_ref[0, 0][(jq - 1) * qb:(jq + 1) * qb, hs]
            s = lax.dot_general(q, k, (((1,), (1,)), ((), ())), preferred_element_type=F32)
            s = s * (dh ** -0.5) + bias_ref[j]
            if jq == 0:
                s = jnp.where(dead, NEG, s)
            mx = jnp.max(s, axis=1, keepdims=True)
            p = jnp.exp(s - mx)
            den = jnp.sum(p, axis=1, keepdims=True)
            o = jnp.dot(p.astype(BF16), v, preferred_element_type=F32) / den
            o_ref[0, 0, rows, hs] = o.astype(BF16)
            lse_all = jnp.where(lane == j, mx + jnp.log(den), lse_all)
        lse_ref[0, 0, rows, :] = lse_all


def _attn_group(qkv, bias, g, *, q_blocks=2):
    bsz, dilation, n, _ = qkv.shape
    gw = GROUP_WIDTH
    q_blocks = min(q_blocks, n // ATTN_BLOCK)
    run = q_blocks * ATTN_BLOCK

    def own(part, width=gw):
        return pl.BlockSpec((1, 1, run, width), lambda b, r, i: (b, r, i, part))

    def previous(part):
        return pl.BlockSpec((1, 1, ATTN_BLOCK, gw),
                            lambda b, r, i: (b, r, jnp.maximum(i * q_blocks - 1, 0), part))

    return pl.pallas_call(
        _attn_kernel,
        grid=(bsz, dilation, n // run),
        in_specs=[
            own(0),
            previous(1), own(1),
            previous(2), own(2),
            pl.BlockSpec((HEADS_PER_GROUP, ATTN_BLOCK, 2 * ATTN_BLOCK), lambda b, r, i: (0, 0, 0)),
        ],
        out_specs=[own(0), own(0, LANES)],
        out_shape=[
            jax.ShapeDtypeStruct((bsz, dilation, n, gw), BF16),
            jax.ShapeDtypeStruct((bsz, dilation, n, LANES), F32),
        ],
        compiler_params=_params(("arbitrary", "arbitrary", "arbitrary")),
        name=f"attn_g{g}",
    )(qkv, qkv, qkv, qkv, qkv, bias)


def _t5_bucket(dist):
    max_exact = REL_BUCKETS // 2
    d_f = jnp.maximum(dist, 1).astype(F32)
    large = max_exact + (jnp.log(d_f / max_exact) / math.log(REL_MAX_DIST / max_exact)
                         * (REL_BUCKETS - max_exact)).astype(jnp.int32)
    large = jnp.minimum(large, REL_BUCKETS - 1)
    return jnp.where(dist < max_exact, dist, large)


def _attn_bias(rel_bias, g, dilation):
    span = ATTN_SPAN
    buckets = _t5_bucket(jnp.arange(span + 1, dtype=jnp.int32) * dilation)
    vec = rel_bias[buckets][:, g * HEADS_PER_GROUP:(g + 1) * HEADS_PER_GROUP].T.astype(F32)
    qpos = jnp.arange(ATTN_BLOCK)[:, None]
    kpos = jnp.arange(2 * ATTN_BLOCK)[None, :]
    dist = qpos + ATTN_BLOCK - kpos
    valid = (dist >= 0) & (dist <= span)
    onehot = (dist[:, :, None] == jnp.arange(span + 1)[None, None, :]).astype(F32)
    table = jnp.einsum('qkj,hj->hqk', onehot, vec, precision=lax.Precision.HIGHEST)
    return jnp.where(valid[None], table, NEG)


def _merge_kernel(hm_ref, o0_ref, o1_ref, o2_ref, l0_ref, l1_ref, l2_ref, gm_ref, ga_ref, x_ref,
                  wbm_ref, wba_ref, wo_ref, ng_ref, rw_ref, rb_ref,
                  x1_ref, h2_ref, idx_ref, gate_ref, rank_ref, count_ref, cnt_ref):
    dh = HEAD_DIM
    bm = x_ref.shape[0]

    def token_order(ref):
        dilation = ref.shape[1]
        if dilation == 1:
            return ref[0, 0]
        perm = _residue_perm(bm, dilation, True).astype(BF16)
        blk = ref[0].reshape(bm, ref.shape[3])
        parts = [blk] if blk.dtype == BF16 else _bf16_parts(blk, 3)
        out = None
        for p in parts:
            moved = jnp.dot(perm, p, preferred_element_type=F32)
            out = moved if out is None else out + moved
        return out

    l0, l1, l2 = token_order(l0_ref), token_order(l1_ref), token_order(l2_ref)
    o0, o1, o2 = token_order(o0_ref), token_order(o1_ref), token_order(o2_ref)
    mx = jnp.maximum(jnp.maximum(l0, l1), l2)
    e0, e1, e2 = jnp.exp(l0 - mx), jnp.exp(l1 - mx), jnp.exp(l2 - mx)
    den = e0 + e1 + e2
    w0, w1, w2 = e0 / den, e1 / den, e2 / den
    parts = []
    for j in range(HEADS_PER_GROUP):
        hs = slice(j * dh, (j + 1) * dh)
        parts.append(w0[:, j:j + 1] * o0[:, hs] + w1[:, j:j + 1] * o1[:, hs]
                     + w2[:, j:j + 1] * o2[:, hs])
    ha = jnp.concatenate(parts, axis=1).astype(BF16)
    ym = jnp.dot(hm_ref[...], wbm_ref[...], preferred_element_type=F32)
    ya = jnp.dot(ha, wba_ref[...], preferred_element_type=F32)
    merged = (_sigmoid(gm_ref[...].astype(F32)) * ym + _sigmoid(ga_ref[...].astype(F32)) * ya)
    x1 = x_ref[...] + jnp.dot(merged.astype(BF16), wo_ref[...], preferred_element_type=F32)
    x1_ref[...] = x1
    h2 = x1 * lax.rsqrt(jnp.mean(x1 * x1, axis=-1, keepdims=True) + RMS_EPS) * ng_ref[...]
    _store_token_tiles(h2_ref, _pack_pairs(h2))
    h_hi, h_lo = _bf16_parts(h2, 2)
    logits = jnp.dot(jnp.concatenate([h_hi, h_lo, h_hi], axis=1), rw_ref[...],
                     preferred_element_type=F32) + rb_ref[...]
    lane = lax.broadcasted_iota(jnp.int32, logits.shape, 1)
    idx_all = jnp.zeros(logits.shape, jnp.int32)
    val_all = jnp.zeros(logits.shape, F32)
    top0 = None
    esum = None
    picks = []
    chosen = jnp.zeros(logits.shape, F32)
    for k in range(TOP_K):
        m = jnp.max(logits, axis=1, keepdims=True)
        sel = jnp.min(jnp.where(logits == m, lane, LANES), axis=1, keepdims=True)
        if k == 0:
            top0 = m
        e = jnp.exp(m - top0)
        esum = e if k == 0 else esum + e
        idx_all = jnp.where(lane == k, sel, idx_all)
        val_all = jnp.where(lane == k, e, val_all)
        pick = lane == sel
        picks.append(pick)
        chosen = chosen + pick.astype(F32)
        logits = jnp.where(pick, -jnp.inf, logits)
    idx_ref[...] = idx_all
    gate_ref[...] = val_all / esum

    @pl.when(pl.program_id(0) == 0)
    def _():
        cnt_ref[...] = jnp.zeros_like(cnt_ref)

    trow = lax.broadcasted_iota(jnp.int32, (bm, bm), 0)
    tcol = lax.broadcasted_iota(jnp.int32, (bm, bm), 1)
    earlier = jnp.dot((tcol < trow).astype(BF16), chosen.astype(BF16),
                      preferred_element_type=F32) + cnt_ref[...]
    rank_all = jnp.zeros(logits.shape, jnp.int32)
    for k in range(TOP_K):
        r = jnp.sum(jnp.where(picks[k], earlier, 0.0), axis=1, keepdims=True)
        rank_all = jnp.where(lane == k, r.astype(jnp.int32), rank_all)
    rank_ref[...] = rank_all
    cnt_ref[...] = cnt_ref[...] + jnp.sum(chosen, axis=0, keepdims=True)
    count_ref[...] = jnp.broadcast_to(cnt_ref[...], count_ref.shape)


def _merge(hm, outs, lses, proj, x2, wbm, wba, wo, ng, rw, rb, *, bm=256):
    n = x2.shape[0]
    d = D_MODEL
    gcol = COL_GATES // d
    n_blk = outs[0].shape[2] // bm

    def rows(width):
        return pl.BlockSpec((bm, width), lambda i: (i, 0))

    def full(a, b):
        return pl.BlockSpec((a, b), lambda i: (0, 0), pipeline_mode=pl.Buffered(1))

    def residue(arr):
        dilation, width = arr.shape[1], arr.shape[3]
        return pl.BlockSpec((1, dilation, bm // dilation, width),
                            lambda i: (i // n_blk, 0, i % n_blk, 0))

    return pl.pallas_call(
        _merge_kernel,
        grid=(n // bm,),
        in_specs=[
            rows(MLSTM_WIDTH),
            residue(outs[0]), residue(outs[1]), residue(outs[2]),
            residue(lses[0]), residue(lses[1]), residue(lses[2]),
            pl.BlockSpec((bm, d), lambda i: (i, gcol)),
            pl.BlockSpec((bm, d), lambda i: (i, gcol + 1)),
            rows(d),
            full(MLSTM_WIDTH, d), full(GROUP_WIDTH, d), full(d, d),
            full(1, d), full(3 * d, LANES), full(1, LANES),
        ],
        out_specs=[rows(d), pl.BlockSpec((bm * PACK_ROWS, LANES), lambda i: (i, 0)),
                   rows(LANES), rows(LANES), rows(LANES),
                   pl.BlockSpec((8, LANES), lambda i: (0, 0))],
        out_shape=[
            jax.ShapeDtypeStruct((n, d), F32),
            jax.ShapeDtypeStruct((n * PACK_ROWS, LANES), jnp.uint32),
            jax.ShapeDtypeStruct((n, LANES), jnp.int32),
            jax.ShapeDtypeStruct((n, LANES), F32),
            jax.ShapeDtypeStruct((n, LANES), jnp.int32),
            jax.ShapeDtypeStruct((8, LANES), F32),
        ],
        scratch_shapes=[pltpu.VMEM((1, LANES), F32)],
        compiler_params=_params(("arbitrary",)),
        name="merge",
    )(hm, outs[0], outs[1], outs[2], lses[0], lses[1], lses[2], proj, proj, x2,
      wbm, wba, wo, ng, rw, rb)


PACK_ROWS = 8


def _pack_pairs(x):
    w = x.shape[1] // 2
    lo = lax.bitcast_convert_type(x[:, :w].astype(BF16).astype(F32), jnp.uint32) >> 16
    hi = lax.bitcast_convert_type(x[:, w:].astype(BF16).astype(F32), jnp.uint32)
    return (hi & jnp.uint32(0xFFFF0000)) | lo


def _unpack_pairs(words):
    lo = lax.bitcast_convert_type(words << 16, F32)
    hi = lax.bitcast_convert_type(words & jnp.uint32(0xFFFF0000), F32)
    return lo, hi


def _store_token_tiles(ref, words):
    rows = words.shape[0]
    for s in range(PACK_ROWS):
        ref[pl.ds(s, rows, stride=PACK_ROWS), :] = words[:, s * LANES:(s + 1) * LANES]


def _dispatch_kernel(fs_ref, fl_ref, nu_ref, dest_ref, hp_ref, xb_hbm, zero_ref, sem, zsem, *,
                     tokens, tm, n_blk):
    pr = PACK_ROWS
    fill_sizes = [1 << b for b in reversed(range((tm - 1).bit_length()))]

    def fill_copies(e):
        off = fs_ref[e]
        for p in fill_sizes:
            take = (fl_ref[e] & p) != 0
            dst = xb_hbm.at[pl.ds(pl.multiple_of(off * pr, pr), p * pr), :]
            yield take, pltpu.make_async_copy(zero_ref.at[pl.ds(0, p * pr), :], dst, zsem)
            off = off + jnp.where(take, p, 0)

    def block_copy(b):
        dst = xb_hbm.at[pl.ds(pl.multiple_of(b * (tm * pr), tm * pr), tm * pr), :]
        return pltpu.make_async_copy(zero_ref, dst, zsem)

    @pl.when(pl.program_id(0) == 0)
    def _():
        zero_ref[...] = jnp.zeros_like(zero_ref)
        for start in (True, False):
            def per_expert(e, carry):
                for take, cp in fill_copies(e):
                    @pl.when(take)
                    def _():
                        cp.start() if start else cp.wait()
                return carry

            def per_block(b, carry):
                block_copy(b).start() if start else block_copy(b).wait()
                return carry

            lax.fori_loop(0, N_EXPERTS, per_expert, 0)
            lax.fori_loop(nu_ref[0], n_blk, per_block, 0)

    def issue(t, carry):
        src = hp_ref.at[pl.ds(pl.multiple_of(t * pr, pr), pr), :]
        for k in range(TOP_K):
            slot = dest_ref[t * TOP_K + k]
            pltpu.make_async_copy(src, xb_hbm.at[pl.ds(pl.multiple_of(slot * pr, pr), pr), :],
                                  sem).start(priority=k % 2)
        return carry

    lax.fori_loop(0, tokens, issue, 0)
    for k in range(TOP_K):
        pltpu.make_async_copy(hp_ref, xb_hbm.at[pl.ds(0, tokens * pr), :], sem).wait()


def _dispatch(fill_start, fill_len, n_used, dest, h2p, n_pad, *, tokens=256, tm=MOE_TILE):
    pr = PACK_ROWS
    n_tok = h2p.shape[0] // pr
    kern = functools.partial(_dispatch_kernel, tokens=tokens, tm=tm, n_blk=n_pad // tm)
    return pl.pallas_call(
        kern,
        grid_spec=pltpu.PrefetchScalarGridSpec(
            num_scalar_prefetch=3,
            grid=(n_tok // tokens,),
            in_specs=[
                pl.BlockSpec((tokens * TOP_K,), lambda i, fs, fl, nu: (i,),
                             memory_space=pltpu.SMEM),
                pl.BlockSpec((tokens * pr, LANES), lambda i, fs, fl, nu: (i, 0)),
            ],
            out_specs=pl.BlockSpec(memory_space=pl.ANY),
            scratch_shapes=[pltpu.VMEM((tm * pr, LANES), jnp.uint32),
                            pltpu.SemaphoreType.DMA(()), pltpu.SemaphoreType.DMA(())],
        ),
        out_shape=jax.ShapeDtypeStruct((n_pad * pr, LANES), jnp.uint32),
        compiler_params=_params(("arbitrary",)),
        name="dispatch",
    )(fill_start, fill_len, n_used, dest, h2p)


def _cast_rows(src_ref, dst_ref, rows):
    total = src_ref.shape[0]

    def body(i, carry):
        r = pl.multiple_of(i * rows, rows)
        dst_ref[pl.ds(r, rows), :] = src_ref[pl.ds(r, rows), :].astype(BF16)
        return carry

    lax.fori_loop(0, total // rows, body, 0)


def _stream_expert_weights(be_ref, nu_ref, ne_ref, copies, convert):
    k, j = pl.program_id(0), pl.program_id(1)
    expert = be_ref[j]
    used = j < nu_ref[0]
    run_start = jnp.logical_or(j == 0, expert != be_ref[jnp.maximum(j - 1, 0)])

    @pl.when(jnp.logical_and(k == 0, j == 0))
    def _():
        for cp in copies(expert, k):
            cp.start()

    @pl.when(jnp.logical_and(used, run_start))
    def _():
        for cp in copies(expert, k):
            cp.wait()
        convert()
        last_run = ne_ref[j] < 0
        nxt_e = jnp.where(last_run, be_ref[0], ne_ref[j])
        nxt_k = jnp.where(last_run, k + 1, k)

        @pl.when(nxt_k < pl.num_programs(0))
        def _():
            for cp in copies(nxt_e, nxt_k):
                cp.start()

    return used


def _per_block_rows(used, bv_ref, o_ref, compute, store=None, rows_per_token=1):
    tm = o_ref.shape[0] // rows_per_token
    need = (bv_ref[pl.program_id(1)] + (MOE_ROW_STEP - 1)) // MOE_ROW_STEP

    def run(rows):
        res = compute(rows)
        head = o_ref.at[pl.ds(0, rows * rows_per_token), :]
        if store is None:
            head[...] = res
        else:
            store(head, res)
        if rows < tm:
            tail = o_ref.at[pl.ds(rows * rows_per_token, (tm - rows) * rows_per_token), :]
            tail[...] = jnp.zeros_like(tail)

    for q in range(1, tm // MOE_ROW_STEP + 1):
        @pl.when(jnp.logical_and(used, need == q))
        def _():
            run(q * MOE_ROW_STEP)

    @pl.when(jnp.logical_not(used))
    def _():
        o_ref[...] = jnp.zeros_like(o_ref)


def _gate_up_kernel(be_ref, nu_ref, ne_ref, bv_ref, x_ref, w_hbm, bg_ref, bu_ref, o_ref,
                    stage_ref, w_bf, sem):
    tf = o_ref.shape[1]
    kt = pl.num_programs(0)

    def copies(expert, k):
        return [pltpu.make_async_copy(
            w_hbm.at[expert, :, pl.ds(pl.multiple_of((half * kt + k) * tf, tf), tf)],
            stage_ref.at[half], sem.at[half]) for half in range(2)]

    def convert():
        for half in range(2):
            _cast_rows(stage_ref.at[half], w_bf.at[half], 256)

    used = _stream_expert_weights(be_ref, nu_ref, ne_ref, copies, convert)

    def compute(rows):
        halves = [_unpack_pairs(x_ref[pl.ds(s, rows, stride=PACK_ROWS), :])
                  for s in range(PACK_ROWS)]
        x = jnp.concatenate([h[0].astype(BF16) for h in halves]
                            + [h[1].astype(BF16) for h in halves], axis=1)
        gate = jnp.dot(x, w_bf[0], preferred_element_type=F32) + bg_ref[0]
        up = jnp.dot(x, w_bf[1], preferred_element_type=F32) + bu_ref[0]
        gate = jnp.minimum(gate, SWIGLU_LIMIT)
        up = jnp.clip(up, -SWIGLU_LIMIT, SWIGLU_LIMIT)
        glu = gate * _sigmoid(SWIGLU_ALPHA * gate)
        return ((up + 1.0) * glu).astype(BF16)

    _per_block_rows(used, bv_ref, o_ref, compute)


def _used_block(j, nu):
    return jnp.minimum(j, nu[0] - 1)


def _gate_up(sched, xb, w_gate_up, b_gate_up3, *, tm=MOE_TILE, tf=1024):
    n_pad, d = xb.shape[0] // PACK_ROWS, D_MODEL
    kt = D_FF // tf

    def bspec(off):
        return pl.BlockSpec((1, 1, tf),
                            lambda k, j, be, nu, ne, bv: (be[_used_block(j, nu)], 0, off + k))

    return pl.pallas_call(
        _gate_up_kernel,
        grid_spec=pltpu.PrefetchScalarGridSpec(
            num_scalar_prefetch=4,
            grid=(kt, n_pad // tm),
            in_specs=[
                pl.BlockSpec((tm * PACK_ROWS, LANES),
                             lambda k, j, be, nu, ne, bv: (_used_block(j, nu), 0)),
                pl.BlockSpec(memory_space=pl.ANY),
                bspec(0), bspec(kt),
            ],
            out_specs=pl.BlockSpec((tm, tf), lambda k, j, be, nu, ne, bv: (j, k)),
            scratch_shapes=[pltpu.VMEM((2, d, tf), F32), pltpu.VMEM((2, d, tf), BF16),
                            pltpu.SemaphoreType.DMA((2,))],
        ),
        out_shape=jax.ShapeDtypeStruct((n_pad, D_FF), BF16),
        compiler_params=_params(("arbitrary", "arbitrary")),
        name="gate_up",
    )(*sched, xb, w_gate_up, b_gate_up3, b_gate_up3)


def _down_kernel(be_ref, nu_ref, ne_ref, bv_ref, h_ref, w_hbm, bd_ref, o_ref, stage_ref, w_bf,
                 sem):
    def copies(expert, k):
        del k
        return [pltpu.make_async_copy(w_hbm.at[expert], stage_ref, sem.at[0])]

    def convert():
        _cast_rows(stage_ref, w_bf, 256)

    used = _stream_expert_weights(be_ref, nu_ref, ne_ref, copies, convert)

    def compute(rows):
        y = jnp.dot(h_ref[0:rows, :], w_bf[...], preferred_element_type=F32) + bd_ref[0]
        return _pack_pairs(y)

    _per_block_rows(used, bv_ref, o_ref, compute, store=_store_token_tiles,
                    rows_per_token=PACK_ROWS)


def _down(sched, hb, w_down, b_down3, *, tm=MOE_TILE):
    n_pad, f = hb.shape
    d = D_MODEL
    return pl.pallas_call(
        _down_kernel,
        grid_spec=pltpu.PrefetchScalarGridSpec(
            num_scalar_prefetch=4,
            grid=(1, n_pad // tm),
            in_specs=[
                pl.BlockSpec((tm, f), lambda k, j, be, nu, ne, bv: (_used_block(j, nu), 0)),
                pl.BlockSpec(memory_space=pl.ANY),
                pl.BlockSpec((1, 1, d),
                             lambda k, j, be, nu, ne, bv: (be[_used_block(j, nu)], 0, 0)),
            ],
            out_specs=pl.BlockSpec((tm * PACK_ROWS, LANES), lambda k, j, be, nu, ne, bv: (j, 0)),
            scratch_shapes=[pltpu.VMEM((f, d), F32), pltpu.VMEM((f, d), BF16),
                            pltpu.SemaphoreType.DMA((1,))],
        ),
        out_shape=jax.ShapeDtypeStruct((n_pad * PACK_ROWS, LANES), jnp.uint32),
        compiler_params=_params(("arbitrary", "arbitrary")),
        name="down",
    )(*sched, hb, w_down, b_down3)


def _combine_kernel(dest_ref, dest_next_ref, y_hbm, gate_ref, x1_ref, ng_ref, o_ref, buf_ref, sem,
                    *, rows):
    pr = PACK_ROWS
    slab = 64
    half = x1_ref.shape[1] // 2
    i = pl.program_id(0)
    slot_rows = TOP_K * rows * pr
    cur = i % 2
    base = pl.multiple_of(cur * slot_rows, slot_rows)
    nxt_base = pl.multiple_of((1 - cur) * slot_rows, slot_rows)

    def gather(idx_ref, dst_base, buf, t):
        for k in range(TOP_K):
            slot = idx_ref[t * TOP_K + k]
            pltpu.make_async_copy(
                y_hbm.at[pl.ds(pl.multiple_of(slot * pr, pr), pr), :],
                buf_ref.at[pl.ds(pl.multiple_of(dst_base + (k * rows + t) * pr, pr), pr), :],
                sem.at[buf]).start(priority=k % 2)

    @pl.when(i == 0)
    def _():
        def first(t, carry):
            gather(dest_ref, 0, 0, t)
            return carry

        lax.fori_loop(0, rows, first, 0)

    pltpu.make_async_copy(y_hbm.at[pl.ds(0, slot_rows), :],
                          buf_ref.at[pl.ds(base, slot_rows), :], sem.at[cur]).wait()

    def reduce_slab(t0):
        gates = [gate_ref[pl.ds(t0, slab), k:k + 1] for k in range(TOP_K)]
        lows, highs = [], []
        for s in range(pr):
            lo = x1_ref[pl.ds(t0, slab), s * LANES:(s + 1) * LANES]
            hi = x1_ref[pl.ds(t0, slab), half + s * LANES:half + (s + 1) * LANES]
            for k in range(TOP_K):
                y_lo, y_hi = _unpack_pairs(
                    buf_ref[pl.ds(base + (k * rows + t0) * pr + s, slab, stride=pr), :])
                lo = lo + gates[k] * y_lo
                hi = hi + gates[k] * y_hi
            lows.append(lo)
            highs.append(hi)
        acc = jnp.concatenate(lows + highs, axis=1)
        o_ref[pl.ds(t0, slab), :] = (
            acc * lax.rsqrt(jnp.mean(acc * acc, axis=-1, keepdims=True) + RMS_EPS) * ng_ref[...])

    def trip_with_prefetch(j, carry):
        t0 = pl.multiple_of(j * slab, slab)
        for tt in range(slab):
            gather(dest_next_ref, nxt_base, 1 - cur, t0 + tt)
        reduce_slab(t0)
        return carry

    def trip(j, carry):
        reduce_slab(pl.multiple_of(j * slab, slab))
        return carry

    has_next = i + 1 < pl.num_programs(0)

    @pl.when(has_next)
    def _():
        lax.fori_loop(0, rows // slab, trip_with_prefetch, 0)

    @pl.when(jnp.logical_not(has_next))
    def _():
        lax.fori_loop(0, rows // slab, trip, 0)


def _combine(dest, yb, gates, x1, ng, *, rows=256):
    n, d = x1.shape
    kern = functools.partial(_combine_kernel, rows=rows)
    n_steps = n // rows
    return pl.pallas_call(
        kern,
        grid=(n_steps,),
        in_specs=[
            pl.BlockSpec((rows * TOP_K,), lambda i: (i,), memory_space=pltpu.SMEM),
            pl.BlockSpec((rows * TOP_K,), lambda i: (jnp.minimum(i + 1, n_steps - 1),),
                         memory_space=pltpu.SMEM),
            pl.BlockSpec(memory_space=pl.ANY),
            pl.BlockSpec((rows, LANES), lambda i: (i, 0)),
            pl.BlockSpec((rows, d), lambda i: (i, 0)),
            pl.BlockSpec((1, d), lambda i: (0, 0)),
        ],
        out_specs=pl.BlockSpec((rows, d), lambda i: (i, 0)),
        out_shape=jax.ShapeDtypeStruct((n, d), F32),
        scratch_shapes=[pltpu.VMEM((2 * TOP_K * rows * PACK_ROWS, LANES), jnp.uint32),
                        pltpu.SemaphoreType.DMA((2,))],
        compiler_params=_params(("arbitrary",)),
        name="combine",
    )(dest, dest, yb, gates, x1, ng)


def _routing(top_idx, rank, counts, tm):
    n_tok = top_idx.shape[0]
    n_asg = n_tok * TOP_K
    e_flat = top_idx.reshape(n_asg)
    padded = (counts + tm - 1) // tm * tm
    pend = jnp.cumsum(padded)
    pstart = pend - padded
    experts = jnp.arange(N_EXPERTS, dtype=jnp.int32)
    start_of = jnp.sum(jnp.where(e_flat[:, None] == experts[None, :], pstart[None, :], 0), axis=1)
    dest = (start_of + rank.reshape(n_asg)).astype(jnp.int32)
    n_blk = -(-(n_asg + N_EXPERTS * (tm - 1)) // tm)
    n_pad = n_blk * tm
    fill_start = (pstart + counts).astype(jnp.int32)
    fill_len = (padded - counts).astype(jnp.int32)
    block_start = jnp.arange(n_blk, dtype=jnp.int32) * tm
    block_e = jnp.minimum(
        jnp.sum((pend[None, :] <= block_start[:, None]).astype(jnp.int32), axis=1),
        N_EXPERTS - 1).astype(jnp.int32)
    n_used = (pend[-1:] // tm).astype(jnp.int32)
    run_end = pend[block_e] // tm
    next_e = jnp.where(run_end < n_used[0], block_e[jnp.minimum(run_end, n_blk - 1)], -1)
    block_rows = jnp.clip(fill_start[block_e] - block_start, 0, tm)
    sched = (block_e, n_used, next_e.astype(jnp.int32), block_rows.astype(jnp.int32))
    return dest, fill_start, fill_len, sched, n_pad


def _layer(x, norm_mix_g, w_in, conv_w, conv_b, igate_b, fgate_b, mlstm_norm_g, rel_bias,
           w_branch_mlstm, w_branch_attn, w_out, norm_moe_g, router_w, router_b,
           w_gate_up, b_gate_up, w_down, b_down, out_norm_g):
    bsz, s, d = x.shape
    n = bsz * s
    x2 = x.reshape(n, d)
    w2, w1 = 2 * MLSTM_WIDTH, MLSTM_WIDTH
    o_qk, o_v, o_o = 0, w2, w2 + w1
    o_i = o_o + w1
    o_f = o_i + MLSTM_HEADS
    o_qa = o_f + MLSTM_HEADS
    o_ka, o_va = o_qa + ATTN_WIDTH, o_qa + 2 * ATTN_WIDTH
    o_g = o_va + ATTN_WIDTH
    w_main = jnp.concatenate([w_in[:, o_qk:o_i], w_in[:, o_g:]], axis=1).astype(BF16)
    zpad = jnp.zeros((d, LANES - MLSTM_HEADS), w_in.dtype)
    w_if = jnp.concatenate([w_in[:, o_i:o_f], zpad, w_in[:, o_f:o_qa], zpad], axis=1).astype(BF16)
    bpad = jnp.zeros((LANES - MLSTM_HEADS,), F32)
    gate_b = jnp.concatenate([igate_b, bpad, fgate_b, bpad]).reshape(1, IF_WIDTH)

    norm_g = norm_mix_g.reshape(1, d)
    proj, ifg = _in_proj(x2, norm_g, w_main, w_if)
    proj3 = proj.reshape(bsz, s, D_MAIN)
    hm = _mlstm(proj3, ifg.reshape(bsz, s, IF_WIDTH), conv_w, conv_b.reshape(1, -1), gate_b,
                mlstm_norm_g.reshape(1, -1)).reshape(n, MLSTM_WIDTH)

    outs, lses = [], []
    gw = GROUP_WIDTH
    for g, (_, dilation) in enumerate(ATTN_GROUPS):
        w_qkv = jnp.concatenate(
            [w_in[:, o + g * gw:o + (g + 1) * gw] for o in (o_qa, o_ka, o_va)], axis=1).astype(BF16)
        qkv = _attn_proj(x2, norm_g, w_qkv, bsz, s, dilation)
        o_g_, lse_g = _attn_group(qkv, _attn_bias(rel_bias, g, dilation), g)
        outs.append(o_g_)
        lses.append(lse_g)

    rw = jnp.concatenate([router_w, jnp.zeros((d, LANES - N_EXPERTS), F32)], axis=1)
    rw_hi = rw.astype(BF16)
    rw_lo = (rw - rw_hi.astype(F32)).astype(BF16)
    rw = jnp.concatenate([rw_hi, rw_hi, rw_lo], axis=0)
    rb = jnp.concatenate([router_b, jnp.full((LANES - N_EXPERTS,), NEG, F32)]).reshape(1, LANES)
    x1, h2, idx, gates, rank, counts = _merge(
        hm, outs, lses, proj, x2, w_branch_mlstm.astype(BF16), w_branch_attn.astype(BF16),
        w_out.astype(BF16), norm_moe_g.reshape(1, d), rw, rb)

    dest, fill_start, fill_len, sched, n_pad = _routing(
        idx[:, :TOP_K], rank[:, :TOP_K], counts[0, :N_EXPERTS].astype(jnp.int32), MOE_TILE)
    xb = _dispatch(fill_start, fill_len, sched[1], dest, h2, n_pad)
    hb = _gate_up(sched, xb, w_gate_up, b_gate_up.reshape(N_EXPERTS, 1, 2 * D_FF))
    yb = _down(sched, hb, w_down, b_down.reshape(N_EXPERTS, 1, d))
    out = _combine(dest, yb, gates, x1, out_norm_g.reshape(1, d))
    return out.reshape(bsz, s, d)


def kernel(x, norm_mix_g, w_in, conv_w, conv_b, igate_b, fgate_b, mlstm_norm_g, rel_bias,
           w_branch_mlstm, w_branch_attn, w_out, norm_moe_g, router_w, router_b,
           w_gate_up, b_gate_up, w_down, b_down, norm_final_g):
    assert w_in.shape[0] == 1, "single-layer block"
    return _layer(x, norm_mix_g[0], w_in[0], conv_w[0], conv_b[0], igate_b[0], fgate_b[0],
                  mlstm_norm_g[0], rel_bias, w_branch_mlstm[0], w_branch_attn[0], w_out[0],
                  norm_moe_g[0], router_w[0], router_b[0], w_gate_up[0], b_gate_up[0],
                  w_down[0], b_down[0], norm_final_g)
```

```python
import functools
import math

import jax
import jax.numpy as jnp
from jax import lax
from jax.experimental import pallas as pl
from jax.experimental.pallas import tpu as pltpu

F32 = jnp.float32
BF16 = jnp.bfloat16

D_MODEL = 2048
MLSTM_HEADS = 8
HEAD_DIM = 128
MLSTM_WIDTH = MLSTM_HEADS * HEAD_DIM
CONV_WIDTH = 4
ATTN_GROUPS = ((128, 1), (512, 4), (2048, 16))
N_GROUPS = 3
HEADS_PER_GROUP = 4
ATTN_HEADS = HEADS_PER_GROUP * N_GROUPS
ATTN_WIDTH = ATTN_HEADS * HEAD_DIM
GROUP_WIDTH = HEADS_PER_GROUP * HEAD_DIM
ATTN_BLOCK = 128
ATTN_SPAN = 128
REL_BUCKETS = 32
REL_MAX_DIST = 2048
N_EXPERTS = 32
TOP_K = 4
D_FF = D_MODEL
SWIGLU_LIMIT = 7.0
SWIGLU_ALPHA = 1.702
RMS_EPS = 1e-6
NEG = -1e30

COL_QK = 0
COL_V = 2 * MLSTM_WIDTH
COL_O = COL_V + MLSTM_WIDTH
COL_GATES = COL_O + MLSTM_WIDTH
D_MAIN = COL_GATES + 2 * D_MODEL
QKV_WIDTH = 3 * GROUP_WIDTH
LANES = 128
IF_WIDTH = 2 * LANES

MLSTM_CHUNK = 128
MOE_TILE = 1024
MOE_ROW_STEP = 256
VMEM_LIMIT = 56 * 1024 * 1024


def _sigmoid(x):
    return 0.5 * jnp.tanh(0.5 * x) + 0.5


def _bf16_parts(x, n_parts):
    parts = []
    for _ in range(n_parts):
        p = x.astype(BF16)
        parts.append(p)
        x = x - p.astype(F32)
    return parts


def _log_sigmoid(x):
    return -(jnp.maximum(-x, 0.0) + jnp.log1p(jnp.exp(-jnp.abs(x))))


def _params(sem):
    return pltpu.CompilerParams(dimension_semantics=sem, vmem_limit_bytes=VMEM_LIMIT)


def _in_proj_kernel(x_ref, g_ref, w_ref, wif_ref, o_ref, oif_ref, h_ref, *, bm, rows):
    @pl.when(pl.program_id(1) == 0)
    def _():
        for r in range(0, bm, rows):
            x = x_ref[r:r + rows, :]
            ms = jnp.mean(x * x, axis=-1, keepdims=True)
            h = (x * lax.rsqrt(ms + RMS_EPS) * g_ref[...]).astype(BF16)
            h_ref[r:r + rows, :] = h
            oif_ref[r:r + rows, :] = jnp.dot(h, wif_ref[...], preferred_element_type=F32)

    o_ref[...] = jnp.dot(h_ref[...], w_ref[...], preferred_element_type=F32).astype(BF16)


def _residue_perm(size, dilation, inverse):
    per = size // dilation
    i = lax.broadcasted_iota(jnp.int32, (size, size), 0)
    j = lax.broadcasted_iota(jnp.int32, (size, size), 1)
    if inverse:
        src = (i & (dilation - 1)) * per + (i >> (dilation.bit_length() - 1))
    else:
        src = (i & (per - 1)) * dilation + (i >> (per.bit_length() - 1))
    return j == src


def _attn_proj_block(h, w_ref, o_ref):
    dilation = o_ref.shape[1]
    per = o_ref.shape[2]
    if dilation > 1:
        perm = _residue_perm(h.shape[0], dilation, False).astype(BF16)
        h = jnp.dot(perm, h, preferred_element_type=F32).astype(BF16)
    res = jnp.dot(h, w_ref[...], preferred_element_type=F32).astype(BF16)
    for r in range(dilation):
        o_ref[0, r] = res[r * per:(r + 1) * per, :]


def _in_proj(x2, g, w_main, w_if, *, bm=1024, bn=1024):
    n = x2.shape[0]
    bm = min(bm, n)
    kern = functools.partial(_in_proj_kernel, bm=bm, rows=256)
    return pl.pallas_call(
        kern,
        grid=(n // bm, D_MAIN // bn),
        in_specs=[
            pl.BlockSpec((bm, D_MODEL), lambda i, j: (i, 0)),
            pl.BlockSpec((1, D_MODEL), lambda i, j: (0, 0)),
            pl.BlockSpec((D_MODEL, bn), lambda i, j: (0, j)),
            pl.BlockSpec((D_MODEL, IF_WIDTH), lambda i, j: (0, 0)),
        ],
        out_specs=[
            pl.BlockSpec((bm, bn), lambda i, j: (i, j)),
            pl.BlockSpec((bm, IF_WIDTH), lambda i, j: (i, 0)),
        ],
        out_shape=[
            jax.ShapeDtypeStruct((n, D_MAIN), BF16),
            jax.ShapeDtypeStruct((n, IF_WIDTH), F32),
        ],
        scratch_shapes=[pltpu.VMEM((bm, D_MODEL), BF16)],
        compiler_params=_params(("arbitrary", "arbitrary")),
        name="in_proj",
    )(x2, g, w_main, w_if)


def _mlstm_chunk(rows, qk_ref, v_ref, og_ref, if_ref, cw_ref, cb_ref, gb_ref, ng_ref, out_ref,
                 tail_ref, c_ref, m_ref):
    L = rows.stop - rows.start
    dh = HEAD_DIM

    tail = tail_ref.shape[0]
    cur = qk_ref[0, rows, :]
    ext = jnp.concatenate([tail_ref[...], cur], axis=0)
    tail_ref[...] = cur[L - tail:, :]
    srow = lax.broadcasted_iota(jnp.int32, (L, tail + L), 0)
    scol = lax.broadcasted_iota(jnp.int32, (L, tail + L), 1)
    acc = cur.astype(F32) * cw_ref[CONV_WIDTH - 1:CONV_WIDTH, :] + cb_ref[...]
    for s in range(1, CONV_WIDTH):
        shift = (scol == srow + (tail - s)).astype(BF16)
        acc = acc + (jnp.dot(shift, ext, preferred_element_type=F32)
                     * cw_ref[CONV_WIDTH - 1 - s:CONV_WIDTH - s, :])
    qk = acc * _sigmoid(acc)
    q_all = qk[:, :MLSTM_WIDTH]
    k_all = qk[:, MLSTM_WIDTH:] * (dh ** -0.5)

    head_lane = lax.broadcasted_iota(jnp.int32, (L, LANES), 1) < MLSTM_HEADS
    gi = jnp.where(head_lane, if_ref[0, rows, :LANES] + gb_ref[:, :LANES], 0.0)
    lf = jnp.where(head_lane, _log_sigmoid(if_ref[0, rows, LANES:] + gb_ref[:, LANES:]), 0.0)
    row = lax.broadcasted_iota(jnp.int32, (L, L), 0)
    col = lax.broadcasted_iota(jnp.int32, (L, L), 1)
    causal = col <= row
    cum = jnp.dot(causal.astype(F32), lf, preferred_element_type=F32,
                  precision=lax.Precision.HIGHEST)
    rmat = gi - cum
    cmax = rmat
    trow = lax.broadcasted_iota(jnp.int32, (L, LANES), 0)
    step = 1
    while step < L:
        cmax = jnp.maximum(cmax, jnp.where(trow >= step, pltpu.roll(cmax, step, axis=0), -jnp.inf))
        step *= 2
    m_prev = m_ref[...]
    gmax = jnp.maximum(m_prev, cmax)
    s_inter = jnp.exp(m_prev - gmax)
    inv_scale = jnp.exp(-(cum + gmax))
    b_last = cum[L - 1:L, :]
    m_new = b_last + gmax[L - 1:L, :]
    w_state = jnp.exp(b_last + rmat - m_new)
    decay = jnp.exp(b_last + m_prev - m_new)
    m_ref[...] = m_new
    rmat_t = rmat.T
    ones_blk = jnp.ones((L, dh), BF16)

    def column(mat, h):
        return jnp.broadcast_to(mat[:, h:h + 1], (L, dh))

    for h in range(MLSTM_HEADS):
        hs = slice(h * dh, (h + 1) * dh)
        p = jnp.exp(jnp.where(causal, rmat_t[h:h + 1, :] - column(gmax, h), NEG))
        qh = q_all[:, hs]
        kh = k_all[:, hs]
        s = lax.dot_general(qh.astype(BF16), kh.astype(BF16), (((1,), (1,)), ((), ())),
                            preferred_element_type=F32)
        vaug = jnp.concatenate([v_ref[0, rows, hs], ones_blk], axis=1)
        c_prev = c_ref[h]
        lhs = jnp.concatenate([(p * s).astype(BF16), (column(s_inter, h) * qh).astype(BF16)],
                              axis=1)
        rhs = jnp.concatenate([vaug, c_prev.astype(BF16)], axis=0)
        nd = jnp.dot(lhs, rhs, preferred_element_type=F32)
        hh = nd[:, :dh] / jnp.maximum(jnp.abs(nd[:, dh:]), column(inv_scale, h))

        kw = (column(w_state, h) * kh).astype(BF16)
        kv = lax.dot_general(kw, vaug, (((0,), (0,)), ((), ())), preferred_element_type=F32)
        c_ref[h] = decay[:, h:h + 1] * c_prev + kv

        y = hh * lax.rsqrt(jnp.mean(hh * hh, axis=-1, keepdims=True) + RMS_EPS) * ng_ref[:, hs]
        y = y * _sigmoid(og_ref[0, rows, hs].astype(F32))
        out_ref[0, rows, hs] = y.astype(BF16)


def _mixer_kernel(x_ref, xg_ref, w0_ref, w1_ref, w2_ref,
                  qk_ref, v_ref, og_ref, if_ref, cw_ref, cb_ref, gb_ref, ng_ref,
                  hm_ref, a0_ref, a1_ref, a2_ref, tail_ref, c_ref, m_ref, *, chunk):
    @pl.when(pl.program_id(1) == 0)
    def _():
        tail_ref[...] = jnp.zeros_like(tail_ref)
        c_ref[...] = jnp.zeros_like(c_ref)
        m_ref[...] = jnp.zeros_like(m_ref)

    x = x_ref[...]
    h = (x * lax.rsqrt(jnp.mean(x * x, axis=-1, keepdims=True) + RMS_EPS) * xg_ref[...]).astype(BF16)
    projections = [(w0_ref, a0_ref), (w1_ref, a1_ref), (w2_ref, a2_ref)]
    chunks = list(range(0, x_ref.shape[0], chunk))
    while projections or chunks:
        if projections:
            _attn_proj_block(h, *projections.pop(0))
        if chunks:
            r0 = chunks.pop(0)
            _mlstm_chunk(slice(r0, r0 + chunk), qk_ref, v_ref, og_ref, if_ref, cw_ref, cb_ref,
                         gb_ref, ng_ref, hm_ref, tail_ref, c_ref, m_ref)


def _mixer_inputs(x2, norm_g, w_qkv, proj3, ifg3, conv_w, conv_b, gate_b, mnorm_g, *,
                  rows=256, chunk=MLSTM_CHUNK):
    bsz, s, _ = proj3.shape
    w = MLSTM_WIDTH
    n_blk = s // rows
    dilations = [d for _, d in ATTN_GROUPS]

    def const(shape):
        return pl.BlockSpec(shape, lambda b, i: (0,) * len(shape))

    def weight():
        return pl.BlockSpec((D_MODEL, QKV_WIDTH), lambda b, i: (0, 0), pipeline_mode=pl.Buffered(1))

    kern = functools.partial(_mixer_kernel, chunk=chunk)
    outs = pl.pallas_call(
        kern,
        grid=(bsz, n_blk),
        in_specs=[
            pl.BlockSpec((rows, D_MODEL), lambda b, i: (b * n_blk + i, 0)),
            const((1, D_MODEL)),
            weight(), weight(), weight(),
            pl.BlockSpec((1, rows, 2 * w), lambda b, i: (b, i, COL_QK // (2 * w))),
            pl.BlockSpec((1, rows, w), lambda b, i: (b, i, COL_V // w)),
            pl.BlockSpec((1, rows, w), lambda b, i: (b, i, COL_O // w)),
            pl.BlockSpec((1, rows, IF_WIDTH), lambda b, i: (b, i, 0)),
            const((CONV_WIDTH, 2 * w)), const((1, 2 * w)), const((1, IF_WIDTH)), const((1, w)),
        ],
        out_specs=[pl.BlockSpec((1, rows, w), lambda b, i: (b, i, 0))] + [
            pl.BlockSpec((1, d, rows // d, QKV_WIDTH), lambda b, i: (b, 0, i, 0)) for d in dilations],
        out_shape=[jax.ShapeDtypeStruct((bsz, s, w), BF16)] + [
            jax.ShapeDtypeStruct((bsz, d, s // d, QKV_WIDTH), BF16) for d in dilations],
        scratch_shapes=[
            pltpu.VMEM((16, 2 * w), BF16),
            pltpu.VMEM((MLSTM_HEADS, HEAD_DIM, 2 * HEAD_DIM), F32),
            pltpu.VMEM((1, LANES), F32),
        ],
        compiler_params=_params(("arbitrary", "arbitrary")),
        name="mixer_inputs",
    )(x2, norm_g, *w_qkv, proj3, proj3, proj3, ifg3, conv_w, conv_b, gate_b, mnorm_g)
    return outs[0], outs[1:]


def _attn_kernel(q_ref, kp_ref, kc_ref, vp_ref, vc_ref, bias_ref, o_ref, lse_ref):
    qb = ATTN_BLOCK
    dh = HEAD_DIM
    n_q = q_ref.shape[2] // qb
    first = pl.program_id(2) == 0
    kcol = lax.broadcasted_iota(jnp.int32, (qb, 2 * qb), 1)
    dead = jnp.logical_and(first, kcol < qb)
    lane = lax.broadcasted_iota(jnp.int32, (qb, LANES), 1)
    for jq in range(n_q):
        rows = slice(jq * qb, (jq + 1) * qb)
        lse_all = jnp.zeros((qb, LANES), F32)
        for j in range(HEADS_PER_GROUP):
            hs = slice(j * dh, (j + 1) * dh)
            q = q_ref[0, 0][rows, hs]
            if jq == 0:
                k = jnp.concatenate([kp_ref[0, 0][:, hs], kc_ref[0, 0][0:qb, hs]], axis=0)
                v = jnp.concatenate([vp_ref[0, 0][:, hs], vc_ref[0, 0][0:qb, hs]], axis=0)
            else:
                k = kc_ref[0, 0][(jq - 1) * qb:(jq + 1) * qb, hs]
                v = vc_ref[0, 0][(jq - 1) * qb:(jq + 1) * qb, hs]
            s = lax.dot_general(q, k, (((1,), (1,)), ((), ())), preferred_element_type=F32)
            s = s * (dh ** -0.5) + bias_ref[j]
            if jq == 0:
                s = jnp.where(dead, NEG, s)
            mx = jnp.max(s, axis=1, keepdims=True)
            p = jnp.exp(s - mx)
            den = jnp.sum(p, axis=1, keepdims=True)
            o = jnp.dot(p.astype(BF16), v, preferred_element_type=F32) / den
            o_ref[0, 0, rows, hs] = o.astype(BF16)
            lse_all = jnp.where(lane == j, mx + jnp.log(den), lse_all)
        lse_ref[0, 0, rows, :] = lse_all


def _attn_group(qkv, bias, g, *, q_blocks=2):
    bsz, dilation, n, _ = qkv.shape
    gw = GROUP_WIDTH
    q_blocks = min(q_blocks, n // ATTN_BLOCK)
    run = q_blocks * ATTN_BLOCK

    def own(part, width=gw):
        return pl.BlockSpec((1, 1, run, width), lambda b, r, i: (b, r, i, part))

    def previous(part):
        return pl.BlockSpec((1, 1, ATTN_BLOCK, gw),
                            lambda b, r, i: (b, r, jnp.maximum(i * q_blocks - 1, 0), part))

    return pl.pallas_call(
        _attn_kernel,
        grid=(bsz, dilation, n // run),
        in_specs=[
            own(0),
            previous(1), own(1),
            previous(2), own(2),
            pl.BlockSpec((HEADS_PER_GROUP, ATTN_BLOCK, 2 * ATTN_BLOCK), lambda b, r, i: (0, 0, 0)),
        ],
        out_specs=[own(0), own(0, LANES)],
        out_shape=[
            jax.ShapeDtypeStruct((bsz, dilation, n, gw), BF16),
            jax.ShapeDtypeStruct((bsz, dilation, n, LANES), F32),
        ],
        compiler_params=_params(("arbitrary", "arbitrary", "arbitrary")),
        name=f"attn_g{g}",
    )(qkv, qkv, qkv, qkv, qkv, bias)


def _t5_bucket(dist):
    max_exact = REL_BUCKETS // 2
    d_f = jnp.maximum(dist, 1).astype(F32)
    large = max_exact + (jnp.log(d_f / max_exact) / math.log(REL_MAX_DIST / max_exact)
                         * (REL_BUCKETS - max_exact)).astype(jnp.int32)
    large = jnp.minimum(large, REL_BUCKETS - 1)
    return jnp.where(dist < max_exact, dist, large)


def _attn_bias(rel_bias, g, dilation):
    span = ATTN_SPAN
    buckets = _t5_bucket(jnp.arange(span + 1, dtype=jnp.int32) * dilation)
    vec = rel_bias[buckets][:, g * HEADS_PER_GROUP:(g + 1) * HEADS_PER_GROUP].T.astype(F32)
    qpos = jnp.arange(ATTN_BLOCK)[:, None]
    kpos = jnp.arange(2 * ATTN_BLOCK)[None, :]
    dist = qpos + ATTN_BLOCK - kpos
    valid = (dist >= 0) & (dist <= span)
    onehot = (dist[:, :, None] == jnp.arange(span + 1)[None, None, :]).astype(F32)
    table = jnp.einsum('qkj,hj->hqk', onehot, vec, precision=lax.Precision.HIGHEST)
    return jnp.where(valid[None], table, NEG)


def _merge_kernel(hm_ref, o0_ref, o1_ref, o2_ref, l0_ref, l1_ref, l2_ref, gm_ref, ga_ref, x_ref,
                  wbm_ref, wba_ref, wo_ref, ng_ref, rw_ref, rb_ref,
                  x1_ref, h2_ref, idx_ref, gate_ref, rank_ref, count_ref, cnt_ref):
    dh = HEAD_DIM
    bm = x_ref.shape[0]

    def token_order(ref):
        dilation = ref.shape[1]
        if dilation == 1:
            return ref[0, 0]
        perm = _residue_perm(bm, dilation, True).astype(BF16)
        blk = ref[0].reshape(bm, ref.shape[3])
        parts = [blk] if blk.dtype == BF16 else _bf16_parts(blk, 3)
        out = None
        for p in parts:
            moved = jnp.dot(perm, p, preferred_element_type=F32)
            out = moved if out is None else out + moved
        return out

    l0, l1, l2 = token_order(l0_ref), token_order(l1_ref), token_order(l2_ref)
    o0, o1, o2 = token_order(o0_ref), token_order(o1_ref), token_order(o2_ref)
    mx = jnp.maximum(jnp.maximum(l0, l1), l2)
    e0, e1, e2 = jnp.exp(l0 - mx), jnp.exp(l1 - mx), jnp.exp(l2 - mx)
    den = e0 + e1 + e2
    w0, w1, w2 = e0 / den, e1 / den, e2 / den
    parts = []
    for j in range(HEADS_PER_GROUP):
        hs = slice(j * dh, (j + 1) * dh)
        parts.append(w0[:, j:j + 1] * o0[:, hs] + w1[:, j:j + 1] * o1[:, hs]
                     + w2[:, j:j + 1] * o2[:, hs])
    ha = jnp.concatenate(parts, axis=1).astype(BF16)
    ym = jnp.dot(hm_ref[...], wbm_ref[...], preferred_element_type=F32)
    ya = jnp.dot(ha, wba_ref[...], preferred_element_type=F32)
    merged = (_sigmoid(gm_ref[...].astype(F32)) * ym + _sigmoid(ga_ref[...].astype(F32)) * ya)
    x1 = x_ref[...] + jnp.dot(merged.astype(BF16), wo_ref[...], preferred_element_type=F32)
    x1_ref[...] = x1
    h2 = x1 * lax.rsqrt(jnp.mean(x1 * x1, axis=-1, keepdims=True) + RMS_EPS) * ng_ref[...]
    _store_token_tiles(h2_ref, _pack_pairs(h2))
    h_hi, h_lo = _bf16_parts(h2, 2)
    logits = jnp.dot(jnp.concatenate([h_hi, h_lo, h_hi], axis=1), rw_ref[...],
                     preferred_element_type=F32) + rb_ref[...]
    lane = lax.broadcasted_iota(jnp.int32, logits.shape, 1)
    idx_all = jnp.zeros(logits.shape, jnp.int32)
    val_all = jnp.zeros(logits.shape, F32)
    top0 = None
    esum = None
    picks = []
    chosen = jnp.zeros(logits.shape, F32)
    for k in range(TOP_K):
        m = jnp.max(logits, axis=1, keepdims=True)
        sel = jnp.min(jnp.where(logits == m, lane, LANES), axis=1, keepdims=True)
        if k == 0:
            top0 = m
        e = jnp.exp(m - top0)
        esum = e if k == 0 else esum + e
        idx_all = jnp.where(lane == k, sel, idx_all)
        val_all = jnp.where(lane == k, e, val_all)
        pick = lane == sel
        picks.append(pick)
        chosen = chosen + pick.astype(F32)
        logits = jnp.where(pick, -jnp.inf, logits)
    idx_ref[...] = idx_all
    gate_ref[...] = val_all / esum

    @pl.when(pl.program_id(0) == 0)
    def _():
        cnt_ref[...] = jnp.zeros_like(cnt_ref)

    trow = lax.broadcasted_iota(jnp.int32, (bm, bm), 0)
    tcol = lax.broadcasted_iota(jnp.int32, (bm, bm), 1)
    earlier = jnp.dot((tcol < trow).astype(BF16), chosen.astype(BF16),
                      preferred_element_type=F32) + cnt_ref[...]
    rank_all = jnp.zeros(logits.shape, jnp.int32)
    for k in range(TOP_K):
        r = jnp.sum(jnp.where(picks[k], earlier, 0.0), axis=1, keepdims=True)
        rank_all = jnp.where(lane == k, r.astype(jnp.int32), rank_all)
    rank_ref[...] = rank_all
    cnt_ref[...] = cnt_ref[...] + jnp.sum(chosen, axis=0, keepdims=True)
    count_ref[...] = jnp.broadcast_to(cnt_ref[...], count_ref.shape)


def _merge(hm, outs, lses, proj, x2, wbm, wba, wo, ng, rw, rb, *, bm=256):
    n = x2.shape[0]
    d = D_MODEL
    gcol = COL_GATES // d
    n_blk = outs[0].shape[2] // bm

    def rows(width):
        return pl.BlockSpec((bm, width), lambda i: (i, 0))

    def full(a, b):
        return pl.BlockSpec((a, b), lambda i: (0, 0), pipeline_mode=pl.Buffered(1))

    def residue(arr):
        dilation, width = arr.shape[1], arr.shape[3]
        return pl.BlockSpec((1, dilation, bm // dilation, width),
                            lambda i: (i // n_blk, 0, i % n_blk, 0))

    return pl.pallas_call(
        _merge_kernel,
        grid=(n // bm,),
        in_specs=[
            rows(MLSTM_WIDTH),
            residue(outs[0]), residue(outs[1]), residue(outs[2]),
            residue(lses[0]), residue(lses[1]), residue(lses[2]),
            pl.BlockSpec((bm, d), lambda i: (i, gcol)),
            pl.BlockSpec((bm, d), lambda i: (i, gcol + 1)),
            rows(d),
            full(MLSTM_WIDTH, d), full(GROUP_WIDTH, d), full(d, d),
            full(1, d), full(3 * d, LANES), full(1, LANES),
        ],
        out_specs=[rows(d), pl.BlockSpec((bm * PACK_ROWS, LANES), lambda i: (i, 0)),
                   rows(LANES), rows(LANES), rows(LANES),
                   pl.BlockSpec((8, LANES), lambda i: (0, 0))],
        out_shape=[
            jax.ShapeDtypeStruct((n, d), F32),
            jax.ShapeDtypeStruct((n * PACK_ROWS, LANES), jnp.uint32),
            jax.ShapeDtypeStruct((n, LANES), jnp.int32),
            jax.ShapeDtypeStruct((n, LANES), F32),
            jax.ShapeDtypeStruct((n, LANES), jnp.int32),
            jax.ShapeDtypeStruct((8, LANES), F32),
        ],
        scratch_shapes=[pltpu.VMEM((1, LANES), F32)],
        compiler_params=_params(("arbitrary",)),
        name="merge",
    )(hm, outs[0], outs[1], outs[2], lses[0], lses[1], lses[2], proj, proj, x2,
      wbm, wba, wo, ng, rw, rb)


PACK_ROWS = 8


def _pack_pairs(x):
    w = x.shape[1] // 2
    lo = lax.bitcast_convert_type(x[:, :w].astype(BF16).astype(F32), jnp.uint32) >> 16
    hi = lax.bitcast_convert_type(x[:, w:].astype(BF16).astype(F32), jnp.uint32)
    return (hi & jnp.uint32(0xFFFF0000)) | lo


def _unpack_pairs(words):
    lo = lax.bitcast_convert_type(words << 16, F32)
    hi = lax.bitcast_convert_type(words & jnp.uint32(0xFFFF0000), F32)
    return lo, hi


def _store_token_tiles(ref, words):
    rows = words.shape[0]
    for s in range(PACK_ROWS):
        ref[pl.ds(s, rows, stride=PACK_ROWS), :] = words[:, s * LANES:(s + 1) * LANES]


def _dispatch_kernel(fs_ref, fl_ref, nu_ref, dest_ref, hp_ref, xb_hbm, zero_ref, sem, zsem, *,
                     tokens, tm, n_blk):
    pr = PACK_ROWS
    fill_sizes = [1 << b for b in reversed(range((tm - 1).bit_length()))]

    def fill_copies(e):
        off = fs_ref[e]
        for p in fill_sizes:
            take = (fl_ref[e] & p) != 0
            dst = xb_hbm.at[pl.ds(pl.multiple_of(off * pr, pr), p * pr), :]
            yield take, pltpu.make_async_copy(zero_ref.at[pl.ds(0, p * pr), :], dst, zsem)
            off = off + jnp.where(take, p, 0)

    def block_copy(b):
        dst = xb_hbm.at[pl.ds(pl.multiple_of(b * (tm * pr), tm * pr), tm * pr), :]
        return pltpu.make_async_copy(zero_ref, dst, zsem)

    @pl.when(pl.program_id(0) == 0)
    def _():
        zero_ref[...] = jnp.zeros_like(zero_ref)
        for start in (True, False):
            def per_expert(e, carry):
                for take, cp in fill_copies(e):
                    @pl.when(take)
                    def _():
                        cp.start() if start else cp.wait()
                return carry

            def per_block(b, carry):
                block_copy(b).start() if start else block_copy(b).wait()
                return carry

            lax.fori_loop(0, N_EXPERTS, per_expert, 0)
            lax.fori_loop(nu_ref[0], n_blk, per_block, 0)

    def issue(t, carry):
        src = hp_ref.at[pl.ds(pl.multiple_of(t * pr, pr), pr), :]
        for k in range(TOP_K):
            slot = dest_ref[t * TOP_K + k]
            pltpu.make_async_copy(src, xb_hbm.at[pl.ds(pl.multiple_of(slot * pr, pr), pr), :],
                                  sem).start(priority=k % 2)
        return carry

    lax.fori_loop(0, tokens, issue, 0)
    for k in range(TOP_K):
        pltpu.make_async_copy(hp_ref, xb_hbm.at[pl.ds(0, tokens * pr), :], sem).wait()


def _dispatch(fill_start, fill_len, n_used, dest, h2p, n_pad, *, tokens=256, tm=MOE_TILE):
    pr = PACK_ROWS
    n_tok = h2p.shape[0] // pr
    kern = functools.partial(_dispatch_kernel, tokens=tokens, tm=tm, n_blk=n_pad // tm)
    return pl.pallas_call(
        kern,
        grid_spec=pltpu.PrefetchScalarGridSpec(
            num_scalar_prefetch=3,
            grid=(n_tok // tokens,),
            in_specs=[
                pl.BlockSpec((tokens * TOP_K,), lambda i, fs, fl, nu: (i,),
                             memory_space=pltpu.SMEM),
                pl.BlockSpec((tokens * pr, LANES), lambda i, fs, fl, nu: (i, 0)),
            ],
            out_specs=pl.BlockSpec(memory_space=pl.ANY),
            scratch_shapes=[pltpu.VMEM((tm * pr, LANES), jnp.uint32),
                            pltpu.SemaphoreType.DMA(()), pltpu.SemaphoreType.DMA(())],
        ),
        out_shape=jax.ShapeDtypeStruct((n_pad * pr, LANES), jnp.uint32),
        compiler_params=_params(("arbitrary",)),
        name="dispatch",
    )(fill_start, fill_len, n_used, dest, h2p)


def _cast_rows(src_ref, dst_ref, rows):
    total = src_ref.shape[0]

    def body(i, carry):
        r = pl.multiple_of(i * rows, rows)
        dst_ref[pl.ds(r, rows), :] = src_ref[pl.ds(r, rows), :].astype(BF16)
        return carry

    lax.fori_loop(0, total // rows, body, 0)


def _stream_expert_weights(be_ref, nu_ref, ne_ref, copies, convert):
    k, j = pl.program_id(0), pl.program_id(1)
    expert = be_ref[j]
    used = j < nu_ref[0]
    run_start = jnp.logical_or(j == 0, expert != be_ref[jnp.maximum(j - 1, 0)])

    @pl.when(jnp.logical_and(k == 0, j == 0))
    def _():
        for cp in copies(expert, k):
            cp.start()

    @pl.when(jnp.logical_and(used, run_start))
    def _():
        for cp in copies(expert, k):
            cp.wait()
        convert()
        last_run = ne_ref[j] < 0
        nxt_e = jnp.where(last_run, be_ref[0], ne_ref[j])
        nxt_k = jnp.where(last_run, k + 1, k)

        @pl.when(nxt_k < pl.num_programs(0))
        def _():
            for cp in copies(nxt_e, nxt_k):
                cp.start()

    return used


def _per_block_rows(used, bv_ref, o_ref, compute, store=None, rows_per_token=1):
    tm = o_ref.shape[0] // rows_per_token
    need = (bv_ref[pl.program_id(1)] + (MOE_ROW_STEP - 1)) // MOE_ROW_STEP

    def run(rows):
        res = compute(rows)
        head = o_ref.at[pl.ds(0, rows * rows_per_token), :]
        if store is None:
            head[...] = res
        else:
            store(head, res)
        if rows < tm:
            tail = o_ref.at[pl.ds(rows * rows_per_token, (tm - rows) * rows_per_token), :]
            tail[...] = jnp.zeros_like(tail)

    for q in range(1, tm // MOE_ROW_STEP + 1):
        @pl.when(jnp.logical_and(used, need == q))
        def _():
            run(q * MOE_ROW_STEP)

    @pl.when(jnp.logical_not(used))
    def _():
        o_ref[...] = jnp.zeros_like(o_ref)


def _gate_up_kernel(be_ref, nu_ref, ne_ref, bv_ref, x_ref, w_hbm, bg_ref, bu_ref, o_ref,
                    stage_ref, w_bf, sem):
    tf = o_ref.shape[1]
    kt = pl.num_programs(0)

    def copies(expert, k):
        return [pltpu.make_async_copy(
            w_hbm.at[expert, :, pl.ds(pl.multiple_of((half * kt + k) * tf, tf), tf)],
            stage_ref.at[half], sem.at[half]) for half in range(2)]

    def convert():
        for half in range(2):
            _cast_rows(stage_ref.at[half], w_bf.at[half], 256)

    used = _stream_expert_weights(be_ref, nu_ref, ne_ref, copies, convert)

    def compute(rows):
        halves = [_unpack_pairs(x_ref[pl.ds(s, rows, stride=PACK_ROWS), :])
                  for s in range(PACK_ROWS)]
        x = jnp.concatenate([h[0].astype(BF16) for h in halves]
                            + [h[1].astype(BF16) for h in halves], axis=1)
        gate = jnp.dot(x, w_bf[0], preferred_element_type=F32) + bg_ref[0]
        up = jnp.dot(x, w_bf[1], preferred_element_type=F32) + bu_ref[0]
        gate = jnp.minimum(gate, SWIGLU_LIMIT)
        up = jnp.clip(up, -SWIGLU_LIMIT, SWIGLU_LIMIT)
        glu = gate * _sigmoid(SWIGLU_ALPHA * gate)
        return ((up + 1.0) * glu).astype(BF16)

    _per_block_rows(used, bv_ref, o_ref, compute)


def _used_block(j, nu):
    return jnp.minimum(j, nu[0] - 1)


def _gate_up(sched, xb, w_gate_up, b_gate_up3, *, tm=MOE_TILE, tf=1024):
    n_pad, d = xb.shape[0] // PACK_ROWS, D_MODEL
    kt = D_FF // tf

    def bspec(off):
        return pl.BlockSpec((1, 1, tf),
                            lambda k, j, be, nu, ne, bv: (be[_used_block(j, nu)], 0, off + k))

    return pl.pallas_call(
        _gate_up_kernel,
        grid_spec=pltpu.PrefetchScalarGridSpec(
            num_scalar_prefetch=4,
            grid=(kt, n_pad // tm),
            in_specs=[
                pl.BlockSpec((tm * PACK_ROWS, LANES),
                             lambda k, j, be, nu, ne, bv: (_used_block(j, nu), 0)),
                pl.BlockSpec(memory_space=pl.ANY),
                bspec(0), bspec(kt),
            ],
            out_specs=pl.BlockSpec((tm, tf), lambda k, j, be, nu, ne, bv: (j, k)),
            scratch_shapes=[pltpu.VMEM((2, d, tf), F32), pltpu.VMEM((2, d, tf), BF16),
                            pltpu.SemaphoreType.DMA((2,))],
        ),
        out_shape=jax.ShapeDtypeStruct((n_pad, D_FF), BF16),
        compiler_params=_params(("arbitrary", "arbitrary")),
        name="gate_up",
    )(*sched, xb, w_gate_up, b_gate_up3, b_gate_up3)


def _down_kernel(be_ref, nu_ref, ne_ref, bv_ref, h_ref, w_hbm, bd_ref, o_ref, stage_ref, w_bf,
                 sem):
    def copies(expert, k):
        del k
        return [pltpu.make_async_copy(w_hbm.at[expert], stage_ref, sem.at[0])]

    def convert():
        _cast_rows(stage_ref, w_bf, 256)

    used = _stream_expert_weights(be_ref, nu_ref, ne_ref, copies, convert)

    def compute(rows):
        y = jnp.dot(h_ref[0:rows, :], w_bf[...], preferred_element_type=F32) + bd_ref[0]
        return _pack_pairs(y)

    _per_block_rows(used, bv_ref, o_ref, compute, store=_store_token_tiles,
                    rows_per_token=PACK_ROWS)


def _down(sched, hb, w_down, b_down3, *, tm=MOE_TILE):
    n_pad, f = hb.shape
    d = D_MODEL
    return pl.pallas_call(
        _down_kernel,
        grid_spec=pltpu.PrefetchScalarGridSpec(
            num_scalar_prefetch=4,
            grid=(1, n_pad // tm),
            in_specs=[
                pl.BlockSpec((tm, f), lambda k, j, be, nu, ne, bv: (_used_block(j, nu), 0)),
                pl.BlockSpec(memory_space=pl.ANY),
                pl.BlockSpec((1, 1, d),
                             lambda k, j, be, nu, ne, bv: (be[_used_block(j, nu)], 0, 0)),
            ],
            out_specs=pl.BlockSpec((tm * PACK_ROWS, LANES), lambda k, j, be, nu, ne, bv: (j, 0)),
            scratch_shapes=[pltpu.VMEM((f, d), F32), pltpu.VMEM((f, d), BF16),
                            pltpu.SemaphoreType.DMA((1,))],
        ),
        out_shape=jax.ShapeDtypeStruct((n_pad * PACK_ROWS, LANES), jnp.uint32),
        compiler_params=_params(("arbitrary", "arbitrary")),
        name="down",
    )(*sched, hb, w_down, b_down3)


def _combine_kernel(dest_ref, dest_next_ref, y_hbm, gate_ref, x1_ref, ng_ref, o_ref, buf_ref, sem,
                    *, rows):
    pr = PACK_ROWS
    slab = 64
    half = x1_ref.shape[1] // 2
    i = pl.program_id(0)
    slot_rows = TOP_K * rows * pr
    cur = i % 2
    base = pl.multiple_of(cur * slot_rows, slot_rows)
    nxt_base = pl.multiple_of((1 - cur) * slot_rows, slot_rows)

    def gather(idx_ref, dst_base, buf, t):
        for k in range(TOP_K):
            slot = idx_ref[t * TOP_K + k]
            pltpu.make_async_copy(
                y_hbm.at[pl.ds(pl.multiple_of(slot * pr, pr), pr), :],
                buf_ref.at[pl.ds(pl.multiple_of(dst_base + (k * rows + t) * pr, pr), pr), :],
                sem.at[buf]).start(priority=k % 2)

    @pl.when(i == 0)
    def _():
        def first(t, carry):
            gather(dest_ref, 0, 0, t)
            return carry

        lax.fori_loop(0, rows, first, 0)

    pltpu.make_async_copy(y_hbm.at[pl.ds(0, slot_rows), :],
                          buf_ref.at[pl.ds(base, slot_rows), :], sem.at[cur]).wait()

    def reduce_slab(t0):
        gates = [gate_ref[pl.ds(t0, slab), k:k + 1] for k in range(TOP_K)]
        lows, highs = [], []
        for s in range(pr):
            lo = x1_ref[pl.ds(t0, slab), s * LANES:(s + 1) * LANES]
            hi = x1_ref[pl.ds(t0, slab), half + s * LANES:half + (s + 1) * LANES]
            for k in range(TOP_K):
                y_lo, y_hi = _unpack_pairs(
                    buf_ref[pl.ds(base + (k * rows + t0) * pr + s, slab, stride=pr), :])
                lo = lo + gates[k] * y_lo
                hi = hi + gates[k] * y_hi
            lows.append(lo)
            highs.append(hi)
        acc = jnp.concatenate(lows + highs, axis=1)
        o_ref[pl.ds(t0, slab), :] = (
            acc * lax.rsqrt(jnp.mean(acc * acc, axis=-1, keepdims=True) + RMS_EPS) * ng_ref[...])

    def trip_with_prefetch(j, carry):
        t0 = pl.multiple_of(j * slab, slab)
        for tt in range(slab):
            gather(dest_next_ref, nxt_base, 1 - cur, t0 + tt)
        reduce_slab(t0)
        return carry

    def trip(j, carry):
        reduce_slab(pl.multiple_of(j * slab, slab))
        return carry

    has_next = i + 1 < pl.num_programs(0)

    @pl.when(has_next)
    def _():
        lax.fori_loop(0, rows // slab, trip_with_prefetch, 0)

    @pl.when(jnp.logical_not(has_next))
    def _():
        lax.fori_loop(0, rows // slab, trip, 0)


def _combine(dest, yb, gates, x1, ng, *, rows=256):
    n, d = x1.shape
    kern = functools.partial(_combine_kernel, rows=rows)
    n_steps = n // rows
    return pl.pallas_call(
        kern,
        grid=(n_steps,),
        in_specs=[
            pl.BlockSpec((rows * TOP_K,), lambda i: (i,), memory_space=pltpu.SMEM),
            pl.BlockSpec((rows * TOP_K,), lambda i: (jnp.minimum(i + 1, n_steps - 1),),
                         memory_space=pltpu.SMEM),
            pl.BlockSpec(memory_space=pl.ANY),
            pl.BlockSpec((rows, LANES), lambda i: (i, 0)),
            pl.BlockSpec((rows, d), lambda i: (i, 0)),
            pl.BlockSpec((1, d), lambda i: (0, 0)),
        ],
        out_specs=pl.BlockSpec((rows, d), lambda i: (i, 0)),
        out_shape=jax.ShapeDtypeStruct((n, d), F32),
        scratch_shapes=[pltpu.VMEM((2 * TOP_K * rows * PACK_ROWS, LANES), jnp.uint32),
                        pltpu.SemaphoreType.DMA((2,))],
        compiler_params=_params(("arbitrary",)),
        name="combine",
    )(dest, dest, yb, gates, x1, ng)


def _routing(top_idx, rank, counts, tm):
    n_tok = top_idx.shape[0]
    n_asg = n_tok * TOP_K
    e_flat = top_idx.reshape(n_asg)
    padded = (counts + tm - 1) // tm * tm
    pend = jnp.cumsum(padded)
    pstart = pend - padded
    experts = jnp.arange(N_EXPERTS, dtype=jnp.int32)
    start_of = jnp.sum(jnp.where(e_flat[:, None] == experts[None, :], pstart[None, :], 0), axis=1)
    dest = (start_of + rank.reshape(n_asg)).astype(jnp.int32)
    n_blk = -(-(n_asg + N_EXPERTS * (tm - 1)) // tm)
    n_pad = n_blk * tm
    fill_start = (pstart + counts).astype(jnp.int32)
    fill_len = (padded - counts).astype(jnp.int32)
    block_start = jnp.arange(n_blk, dtype=jnp.int32) * tm
    block_e = jnp.minimum(
        jnp.sum((pend[None, :] <= block_start[:, None]).astype(jnp.int32), axis=1),
        N_EXPERTS - 1).astype(jnp.int32)
    n_used = (pend[-1:] // tm).astype(jnp.int32)
    run_end = pend[block_e] // tm
    next_e = jnp.where(run_end < n_used[0], block_e[jnp.minimum(run_end, n_blk - 1)], -1)
    block_rows = jnp.clip(fill_start[block_e] - block_start, 0, tm)
    sched = (block_e, n_used, next_e.astype(jnp.int32), block_rows.astype(jnp.int32))
    return dest, fill_start, fill_len, sched, n_pad


def _layer(x, norm_mix_g, w_in, conv_w, conv_b, igate_b, fgate_b, mlstm_norm_g, rel_bias,
           w_branch_mlstm, w_branch_attn, w_out, norm_moe_g, router_w, router_b,
           w_gate_up, b_gate_up, w_down, b_down, out_norm_g):
    bsz, s, d = x.shape
    n = bsz * s
    x2 = x.reshape(n, d)
    w2, w1 = 2 * MLSTM_WIDTH, MLSTM_WIDTH
    o_qk, o_v, o_o = 0, w2, w2 + w1
    o_i = o_o + w1
    o_f = o_i + MLSTM_HEADS
    o_qa = o_f + MLSTM_HEADS
    o_ka, o_va = o_qa + ATTN_WIDTH, o_qa + 2 * ATTN_WIDTH
    o_g = o_va + ATTN_WIDTH
    w_main = jnp.concatenate([w_in[:, o_qk:o_i], w_in[:, o_g:]], axis=1).astype(BF16)
    zpad = jnp.zeros((d, LANES - MLSTM_HEADS), w_in.dtype)
    w_if = jnp.concatenate([w_in[:, o_i:o_f], zpad, w_in[:, o_f:o_qa], zpad], axis=1).astype(BF16)
    bpad = jnp.zeros((LANES - MLSTM_HEADS,), F32)
    gate_b = jnp.concatenate([igate_b, bpad, fgate_b, bpad]).reshape(1, IF_WIDTH)

    norm_g = norm_mix_g.reshape(1, d)
    proj, ifg = _in_proj(x2, norm_g, w_main, w_if)
    proj3 = proj.reshape(bsz, s, D_MAIN)
    gw = GROUP_WIDTH
    w_qkv = [jnp.concatenate([w_in[:, o + g * gw:o + (g + 1) * gw] for o in (o_qa, o_ka, o_va)],
                             axis=1).astype(BF16) for g in range(N_GROUPS)]
    hm, qkvs = _mixer_inputs(x2, norm_g, w_qkv, proj3, ifg.reshape(bsz, s, IF_WIDTH), conv_w,
                             conv_b.reshape(1, -1), gate_b, mlstm_norm_g.reshape(1, -1))
    hm = hm.reshape(n, MLSTM_WIDTH)

    outs, lses = [], []
    for g, (_, dilation) in enumerate(ATTN_GROUPS):
        o_g_, lse_g = _attn_group(qkvs[g], _attn_bias(rel_bias, g, dilation), g)
        outs.append(o_g_)
        lses.append(lse_g)

    rw = jnp.concatenate([router_w, jnp.zeros((d, LANES - N_EXPERTS), F32)], axis=1)
    rw_hi = rw.astype(BF16)
    rw_lo = (rw - rw_hi.astype(F32)).astype(BF16)
    rw = jnp.concatenate([rw_hi, rw_hi, rw_lo], axis=0)
    rb = jnp.concatenate([router_b, jnp.full((LANES - N_EXPERTS,), NEG, F32)]).reshape(1, LANES)
    x1, h2, idx, gates, rank, counts = _merge(
        hm, outs, lses, proj, x2, w_branch_mlstm.astype(BF16), w_branch_attn.astype(BF16),
        w_out.astype(BF16), norm_moe_g.reshape(1, d), rw, rb)

    dest, fill_start, fill_len, sched, n_pad = _routing(
        idx[:, :TOP_K], rank[:, :TOP_K], counts[0, :N_EXPERTS].astype(jnp.int32), MOE_TILE)
    xb = _dispatch(fill_start, fill_len, sched[1], dest, h2, n_pad)
    hb = _gate_up(sched, xb, w_gate_up, b_gate_up.reshape(N_EXPERTS, 1, 2 * D_FF))
    yb = _down(sched, hb, w_down, b_down.reshape(N_EXPERTS, 1, d))
    out = _combine(dest, yb, gates, x1, out_norm_g.reshape(1, d))
    return out.reshape(bsz, s, d)


def kernel(x, norm_mix_g, w_in, conv_w, conv_b, igate_b, fgate_b, mlstm_norm_g, rel_bias,
           w_branch_mlstm, w_branch_attn, w_out, norm_moe_g, router_w, router_b,
           w_gate_up, b_gate_up, w_down, b_down, norm_final_g):
    assert w_in.shape[0] == 1, "single-layer block"
    return _layer(x, norm_mix_g[0], w_in[0], conv_w[0], conv_b[0], igate_b[0], fgate_b[0],
                  mlstm_norm_g[0], rel_bias, w_branch_mlstm[0], w_branch_attn[0], w_out[0],
                  norm_moe_g[0], router_w[0], router_b[0], w_gate_up[0], b_gate_up[0],
                  w_down[0], b_down[0], norm_final_g)
```

```python
import functools
import math

import jax
import jax.numpy as jnp
from jax import lax
from jax.experimental import pallas as pl
from jax.experimental.pallas import tpu as pltpu

F32 = jnp.float32
BF16 = jnp.bfloat16

D_MODEL = 2048
MLSTM_HEADS = 8
HEAD_DIM = 128
MLSTM_WIDTH = MLSTM_HEADS * HEAD_DIM
CONV_WIDTH = 4
ATTN_GROUPS = ((128, 1), (512, 4), (2048, 16))
N_GROUPS = 3
HEADS_PER_GROUP = 4
ATTN_HEADS = HEADS_PER_GROUP * N_GROUPS
ATTN_WIDTH = ATTN_HEADS * HEAD_DIM
GROUP_WIDTH = HEADS_PER_GROUP * HEAD_DIM
ATTN_BLOCK = 128
ATTN_SPAN = 128
REL_BUCKETS = 32
REL_MAX_DIST = 2048
N_EXPERTS = 32
TOP_K = 4
D_FF = D_MODEL
SWIGLU_LIMIT = 7.0
SWIGLU_ALPHA = 1.702
RMS_EPS = 1e-6
NEG = -1e30

COL_QK = 0
COL_V = 2 * MLSTM_WIDTH
COL_O = COL_V + MLSTM_WIDTH
COL_GATES = COL_O + MLSTM_WIDTH
D_MAIN = COL_GATES + 2 * D_MODEL
QKV_WIDTH = 3 * GROUP_WIDTH
LANES = 128
IF_WIDTH = 2 * LANES

MLSTM_CHUNK = 128
MOE_TILE = 512
MOE_ROW_STEP = 256
VMEM_LIMIT = 56 * 1024 * 1024


def _sigmoid(x):
    return 0.5 * jnp.tanh(0.5 * x) + 0.5


def _bf16_parts(x, n_parts):
    parts = []
    for _ in range(n_parts):
        p = x.astype(BF16)
        parts.append(p)
        x = x - p.astype(F32)
    return parts


def _log_sigmoid(x):
    return -(jnp.maximum(-x, 0.0) + jnp.log1p(jnp.exp(-jnp.abs(x))))


def _params(sem):
    return pltpu.CompilerParams(dimension_semantics=sem, vmem_limit_bytes=VMEM_LIMIT)


def _in_proj_kernel(x_ref, g_ref, w_ref, wif_ref, o_ref, oif_ref, h_ref, *, bm, rows):
    @pl.when(pl.program_id(1) == 0)
    def _():
        for r in range(0, bm, rows):
            x = x_ref[r:r + rows, :]
            ms = jnp.mean(x * x, axis=-1, keepdims=True)
            h = (x * lax.rsqrt(ms + RMS_EPS) * g_ref[...]).astype(BF16)
            h_ref[r:r + rows, :] = h
            oif_ref[r:r + rows, :] = jnp.dot(h, wif_ref[...], preferred_element_type=F32)

    o_ref[...] = jnp.dot(h_ref[...], w_ref[...], preferred_element_type=F32).astype(BF16)


def _residue_perm(size, dilation, inverse):
    per = size // dilation
    i = lax.broadcasted_iota(jnp.int32, (size, size), 0)
    j = lax.broadcasted_iota(jnp.int32, (size, size), 1)
    if inverse:
        src = (i & (dilation - 1)) * per + (i >> (dilation.bit_length() - 1))
    else:
        src = (i & (per - 1)) * dilation + (i >> (per.bit_length() - 1))
    return j == src


def _attn_proj_block(h, w_ref, o_ref):
    dilation = o_ref.shape[1]
    per = o_ref.shape[2]
    if dilation > 1:
        perm = _residue_perm(h.shape[0], dilation, False).astype(BF16)
        h = jnp.dot(perm, h, preferred_element_type=F32).astype(BF16)
    res = jnp.dot(h, w_ref[...], preferred_element_type=F32).astype(BF16)
    for r in range(dilation):
        o_ref[0, r] = res[r * per:(r + 1) * per, :]


_O_I = 4 * MLSTM_WIDTH
_O_QA = _O_I + 2 * MLSTM_HEADS
_O_GATES = _O_QA + 3 * ATTN_WIDTH
D_IN = _O_GATES + 2 * D_MODEL


def _split_w_in_kernel(w_ref, main_ref, if_ref, q0_ref, q1_ref, q2_ref):
    main_ref[:, :_O_I] = w_ref[:, :_O_I].astype(BF16)
    main_ref[:, _O_I:] = w_ref[:, _O_GATES:].astype(BF16)
    window = w_ref[:, _O_I:_O_I + LANES]
    lane = lax.broadcasted_iota(jnp.int32, window.shape, 1)
    i_part = jnp.where(lane < MLSTM_HEADS, window, 0.0)
    f_part = jnp.where(lane < MLSTM_HEADS, pltpu.roll(window, LANES - MLSTM_HEADS, axis=1), 0.0)
    if_ref[...] = jnp.concatenate([i_part, f_part], axis=1).astype(BF16)
    gw = GROUP_WIDTH
    for g, q_ref in enumerate((q0_ref, q1_ref, q2_ref)):
        for part in range(3):
            lo = _O_QA + part * ATTN_WIDTH + g * gw
            q_ref[:, part * gw:(part + 1) * gw] = w_ref[:, lo:lo + gw].astype(BF16)


def _split_w_in(w_in, *, rows=256):
    d = w_in.shape[0]
    assert w_in.shape[1] == D_IN

    def out(width):
        return pl.BlockSpec((rows, width), lambda i: (i, 0))

    outs = pl.pallas_call(
        _split_w_in_kernel,
        grid=(d // rows,),
        in_specs=[pl.BlockSpec((rows, D_IN), lambda i: (i, 0))],
        out_specs=[out(D_MAIN), out(IF_WIDTH), out(QKV_WIDTH), out(QKV_WIDTH), out(QKV_WIDTH)],
        out_shape=[jax.ShapeDtypeStruct((d, width), BF16)
                   for width in (D_MAIN, IF_WIDTH, QKV_WIDTH, QKV_WIDTH, QKV_WIDTH)],
        compiler_params=_params(("arbitrary",)),
        name="split_w_in",
    )(w_in)
    return outs[0], outs[1], outs[2:]


def _in_proj(x2, g, w_main, w_if, *, bm=1024, bn=1024):
    n = x2.shape[0]
    bm = min(bm, n)
    kern = functools.partial(_in_proj_kernel, bm=bm, rows=256)
    return pl.pallas_call(
        kern,
        grid=(n // bm, D_MAIN // bn),
        in_specs=[
            pl.BlockSpec((bm, D_MODEL), lambda i, j: (i, 0)),
            pl.BlockSpec((1, D_MODEL), lambda i, j: (0, 0)),
            pl.BlockSpec((D_MODEL, bn), lambda i, j: (0, j)),
            pl.BlockSpec((D_MODEL, IF_WIDTH), lambda i, j: (0, 0)),
        ],
        out_specs=[
            pl.BlockSpec((bm, bn), lambda i, j: (i, j)),
            pl.BlockSpec((bm, IF_WIDTH), lambda i, j: (i, 0)),
        ],
        out_shape=[
            jax.ShapeDtypeStruct((n, D_MAIN), BF16),
            jax.ShapeDtypeStruct((n, IF_WIDTH), F32),
        ],
        scratch_shapes=[pltpu.VMEM((bm, D_MODEL), BF16)],
        compiler_params=_params(("arbitrary", "arbitrary")),
        name="in_proj",
    )(x2, g, w_main, w_if)


def _mlstm_chunk(rows, qk_ref, v_ref, og_ref, if_ref, cw_ref, cb_ref, gb_ref, ng_ref, out_ref,
                 tail_ref, c_ref, m_ref):
    L = rows.stop - rows.start
    dh = HEAD_DIM

    tail = tail_ref.shape[0]
    cur = qk_ref[0, rows, :]
    ext = jnp.concatenate([tail_ref[...], cur], axis=0)
    tail_ref[...] = cur[L - tail:, :]
    srow = lax.broadcasted_iota(jnp.int32, (L, tail + L), 0)
    scol = lax.broadcasted_iota(jnp.int32, (L, tail + L), 1)
    acc = cur.astype(F32) * cw_ref[CONV_WIDTH - 1:CONV_WIDTH, :] + cb_ref[...]
    for s in range(1, CONV_WIDTH):
        shift = (scol == srow + (tail - s)).astype(BF16)
        acc = acc + (jnp.dot(shift, ext, preferred_element_type=F32)
                     * cw_ref[CONV_WIDTH - 1 - s:CONV_WIDTH - s, :])
    qk = acc * _sigmoid(acc)
    q_all = qk[:, :MLSTM_WIDTH]
    k_all = qk[:, MLSTM_WIDTH:] * (dh ** -0.5)

    head_lane = lax.broadcasted_iota(jnp.int32, (L, LANES), 1) < MLSTM_HEADS
    gi = jnp.where(head_lane, if_ref[0, rows, :LANES] + gb_ref[:, :LANES], 0.0)
    lf = jnp.where(head_lane, _log_sigmoid(if_ref[0, rows, LANES:] + gb_ref[:, LANES:]), 0.0)
    row = lax.broadcasted_iota(jnp.int32, (L, L), 0)
    col = lax.broadcasted_iota(jnp.int32, (L, L), 1)
    causal = col <= row
    cum = jnp.dot(causal.astype(F32), lf, preferred_element_type=F32,
                  precision=lax.Precision.HIGHEST)
    rmat = gi - cum
    cmax = rmat
    trow = lax.broadcasted_iota(jnp.int32, (L, LANES), 0)
    step = 1
    while step < L:
        cmax = jnp.maximum(cmax, jnp.where(trow >= step, pltpu.roll(cmax, step, axis=0), -jnp.inf))
        step *= 2
    m_prev = m_ref[...]
    gmax = jnp.maximum(m_prev, cmax)
    s_inter = jnp.exp(m_prev - gmax)
    inv_scale = jnp.exp(-(cum + gmax))
    b_last = cum[L - 1:L, :]
    m_new = b_last + gmax[L - 1:L, :]
    w_state = jnp.exp(b_last + rmat - m_new)
    decay = jnp.exp(b_last + m_prev - m_new)
    m_ref[...] = m_new
    rmat_t = rmat.T
    ones_blk = jnp.ones((L, dh), BF16)

    def column(mat, h):
        return jnp.broadcast_to(mat[:, h:h + 1], (L, dh))

    for h in range(MLSTM_HEADS):
        hs = slice(h * dh, (h + 1) * dh)
        p = jnp.exp(jnp.where(causal, rmat_t[h:h + 1, :] - column(gmax, h), NEG))
        qh = q_all[:, hs]
        kh = k_all[:, hs]
        s = lax.dot_general(qh.astype(BF16), kh.astype(BF16), (((1,), (1,)), ((), ())),
                            preferred_element_type=F32)
        vaug = jnp.concatenate([v_ref[0, rows, hs], ones_blk], axis=1)
        c_prev = c_ref[h]
        lhs = jnp.concatenate([(p * s).astype(BF16), (column(s_inter, h) * qh).astype(BF16)],
                              axis=1)
        rhs = jnp.concatenate([vaug, c_prev.astype(BF16)], axis=0)
        nd = jnp.dot(lhs, rhs, preferred_element_type=F32)
        hh = nd[:, :dh] / jnp.maximum(jnp.abs(nd[:, dh:]), column(inv_scale, h))

        kw = (column(w_state, h) * kh).astype(BF16)
        kv = lax.dot_general(kw, vaug, (((0,), (0,)), ((), ())), preferred_element_type=F32)
        c_ref[h] = decay[:, h:h + 1] * c_prev + kv

        y = hh * lax.rsqrt(jnp.mean(hh * hh, axis=-1, keepdims=True) + RMS_EPS) * ng_ref[:, hs]
        y = y * _sigmoid(og_ref[0, rows, hs].astype(F32))
        out_ref[0, rows, hs] = y.astype(BF16)


def _mixer_kernel(x_ref, xg_ref, w0_ref, w1_ref, w2_ref,
                  qk_ref, v_ref, og_ref, if_ref, cw_ref, cb_ref, gb_ref, ng_ref,
                  hm_ref, a0_ref, a1_ref, a2_ref, tail_ref, c_ref, m_ref, *, chunk):
    @pl.when(pl.program_id(1) == 0)
    def _():
        tail_ref[...] = jnp.zeros_like(tail_ref)
        c_ref[...] = jnp.zeros_like(c_ref)
        m_ref[...] = jnp.zeros_like(m_ref)

    x = x_ref[...]
    h = (x * lax.rsqrt(jnp.mean(x * x, axis=-1, keepdims=True) + RMS_EPS) * xg_ref[...]).astype(BF16)
    projections = [(w0_ref, a0_ref), (w1_ref, a1_ref), (w2_ref, a2_ref)]
    chunks = list(range(0, x_ref.shape[0], chunk))
    while projections or chunks:
        if projections:
            _attn_proj_block(h, *projections.pop(0))
        if chunks:
            r0 = chunks.pop(0)
            _mlstm_chunk(slice(r0, r0 + chunk), qk_ref, v_ref, og_ref, if_ref, cw_ref, cb_ref,
                         gb_ref, ng_ref, hm_ref, tail_ref, c_ref, m_ref)


def _mixer_inputs(x2, norm_g, w_qkv, proj3, ifg3, conv_w, conv_b, gate_b, mnorm_g, *,
                  rows=256, chunk=MLSTM_CHUNK):
    bsz, s, _ = proj3.shape
    w = MLSTM_WIDTH
    n_blk = s // rows
    dilations = [d for _, d in ATTN_GROUPS]

    def const(shape):
        return pl.BlockSpec(shape, lambda b, i: (0,) * len(shape))

    def weight():
        return pl.BlockSpec((D_MODEL, QKV_WIDTH), lambda b, i: (0, 0), pipeline_mode=pl.Buffered(1))

    kern = functools.partial(_mixer_kernel, chunk=chunk)
    outs = pl.pallas_call(
        kern,
        grid=(bsz, n_blk),
        in_specs=[
            pl.BlockSpec((rows, D_MODEL), lambda b, i: (b * n_blk + i, 0)),
            const((1, D_MODEL)),
            weight(), weight(), weight(),
            pl.BlockSpec((1, rows, 2 * w), lambda b, i: (b, i, COL_QK // (2 * w))),
            pl.BlockSpec((1, rows, w), lambda b, i: (b, i, COL_V // w)),
            pl.BlockSpec((1, rows, w), lambda b, i: (b, i, COL_O // w)),
            pl.BlockSpec((1, rows, IF_WIDTH), lambda b, i: (b, i, 0)),
            const((CONV_WIDTH, 2 * w)), const((1, 2 * w)), const((1, IF_WIDTH)), const((1, w)),
        ],
        out_specs=[pl.BlockSpec((1, rows, w), lambda b, i: (b, i, 0))] + [
            pl.BlockSpec((1, d, rows // d, QKV_WIDTH), lambda b, i: (b, 0, i, 0)) for d in dilations],
        out_shape=[jax.ShapeDtypeStruct((bsz, s, w), BF16)] + [
            jax.ShapeDtypeStruct((bsz, d, s // d, QKV_WIDTH), BF16) for d in dilations],
        scratch_shapes=[
            pltpu.VMEM((16, 2 * w), BF16),
            pltpu.VMEM((MLSTM_HEADS, HEAD_DIM, 2 * HEAD_DIM), F32),
            pltpu.VMEM((1, LANES), F32),
        ],
        compiler_params=_params(("arbitrary", "arbitrary")),
        name="mixer_inputs",
    )(x2, norm_g, *w_qkv, proj3, proj3, proj3, ifg3, conv_w, conv_b, gate_b, mnorm_g)
    return outs[0], outs[1:]


def _attn_kernel(q_ref, kp_ref, kc_ref, vp_ref, vc_ref, bias_ref, o_ref, lse_ref):
    qb = ATTN_BLOCK
    dh = HEAD_DIM
    n_q = q_ref.shape[2] // qb
    first = pl.program_id(2) == 0
    kcol = lax.broadcasted_iota(jnp.int32, (qb, 2 * qb), 1)
    dead = jnp.logical_and(first, kcol < qb)
    lane = lax.broadcasted_iota(jnp.int32, (qb, LANES), 1)
    for jq in range(n_q):
        rows = slice(jq * qb, (jq + 1) * qb)
        lse_all = jnp.zeros((qb, LANES), F32)
        for j in range(HEADS_PER_GROUP):
            hs = slice(j * dh, (j + 1) * dh)
            q = q_ref[0, 0][rows, hs]
            if jq == 0:
                k = jnp.concatenate([kp_ref[0, 0][:, hs], kc_ref[0, 0][0:qb, hs]], axis=0)
                v = jnp.concatenate([vp_ref[0, 0][:, hs], vc_ref[0, 0][0:qb, hs]], axis=0)
            else:
                k = kc_ref[0, 0][(jq - 1) * qb:(jq + 1) * qb, hs]
                v = vc_ref[0, 0][(jq - 1) * qb:(jq + 1) * qb, hs]
            s = lax.dot_general(q, k, (((1,), (1,)), ((), ())), preferred_element_type=F32)
            s = s * (dh ** -0.5) + bias_ref[j]
            if jq == 0:
                s = jnp.where(dead, NEG, s)
            mx = jnp.max(s, axis=1, keepdims=True)
            p = jnp.exp(s - mx)
            den = jnp.sum(p, axis=1, keepdims=True)
            o = jnp.dot(p.astype(BF16), v, preferred_element_type=F32) / den
            o_ref[0, 0, rows, hs] = o.astype(BF16)
            lse_all = jnp.where(lane == j, mx + jnp.log(den), lse_all)
        lse_ref[0, 0, rows, :] = lse_all


def _attn_group(qkv, bias, g, *, q_blocks=2):
    bsz, dilation, n, _ = qkv.shape
    gw = GROUP_WIDTH
    q_blocks = min(q_blocks, n // ATTN_BLOCK)
    run = q_blocks * ATTN_BLOCK

    def own(part, width=gw):
        return pl.BlockSpec((1, 1, run, width), lambda b, r, i: (b, r, i, part))

    def previous(part):
        return pl.BlockSpec((1, 1, ATTN_BLOCK, gw),
                            lambda b, r, i: (b, r, jnp.maximum(i * q_blocks - 1, 0), part))

    return pl.pallas_call(
        _attn_kernel,
        grid=(bsz, dilation, n // run),
        in_specs=[
            own(0),
            previous(1), own(1),
            previous(2), own(2),
            pl.BlockSpec((HEADS_PER_GROUP, ATTN_BLOCK, 2 * ATTN_BLOCK), lambda b, r, i: (0, 0, 0)),
        ],
        out_specs=[own(0), own(0, LANES)],
        out_shape=[
            jax.ShapeDtypeStruct((bsz, dilation, n, gw), BF16),
            jax.ShapeDtypeStruct((bsz, dilation, n, LANES), F32),
        ],
        compiler_params=_params(("arbitrary", "arbitrary", "arbitrary")),
        name=f"attn_g{g}",
    )(qkv, qkv, qkv, qkv, qkv, bias)


def _t5_bucket(dist):
    max_exact = REL_BUCKETS // 2
    d_f = jnp.maximum(dist, 1).astype(F32)
    large = max_exact + (jnp.log(d_f / max_exact) / math.log(REL_MAX_DIST / max_exact)
                         * (REL_BUCKETS - max_exact)).astype(jnp.int32)
    large = jnp.minimum(large, REL_BUCKETS - 1)
    return jnp.where(dist < max_exact, dist, large)


def _attn_bias(rel_bias, g, dilation):
    span = ATTN_SPAN
    buckets = _t5_bucket(jnp.arange(span + 1, dtype=jnp.int32) * dilation)
    vec = rel_bias[buckets][:, g * HEADS_PER_GROUP:(g + 1) * HEADS_PER_GROUP].T.astype(F32)
    qpos = jnp.arange(ATTN_BLOCK)[:, None]
    kpos = jnp.arange(2 * ATTN_BLOCK)[None, :]
    dist = qpos + ATTN_BLOCK - kpos
    valid = (dist >= 0) & (dist <= span)
    onehot = (dist[:, :, None] == jnp.arange(span + 1)[None, None, :]).astype(F32)
    table = jnp.einsum('qkj,hj->hqk', onehot, vec, precision=lax.Precision.HIGHEST)
    return jnp.where(valid[None], table, NEG)


def _merge_kernel(hm_ref, o0_ref, o1_ref, o2_ref, l0_ref, l1_ref, l2_ref, gm_ref, ga_ref, x_ref,
                  wbm_ref, wba_ref, wo_ref, ng_ref, rw_ref, rb_ref,
                  x1_ref, h2_ref, idx_ref, gate_ref, rank_ref, count_ref, cnt_ref):
    dh = HEAD_DIM
    bm = x_ref.shape[0]

    def token_order(ref):
        dilation = ref.shape[1]
        if dilation == 1:
            return ref[0, 0]
        perm = _residue_perm(bm, dilation, True).astype(BF16)
        blk = ref[0].reshape(bm, ref.shape[3])
        parts = [blk] if blk.dtype == BF16 else _bf16_parts(blk, 3)
        out = None
        for p in parts:
            moved = jnp.dot(perm, p, preferred_element_type=F32)
            out = moved if out is None else out + moved
        return out

    l0, l1, l2 = token_order(l0_ref), token_order(l1_ref), token_order(l2_ref)
    o0, o1, o2 = token_order(o0_ref), token_order(o1_ref), token_order(o2_ref)
    mx = jnp.maximum(jnp.maximum(l0, l1), l2)
    e0, e1, e2 = jnp.exp(l0 - mx), jnp.exp(l1 - mx), jnp.exp(l2 - mx)
    den = e0 + e1 + e2
    w0, w1, w2 = e0 / den, e1 / den, e2 / den
    parts = []
    for j in range(HEADS_PER_GROUP):
        hs = slice(j * dh, (j + 1) * dh)
        parts.append(w0[:, j:j + 1] * o0[:, hs] + w1[:, j:j + 1] * o1[:, hs]
                     + w2[:, j:j + 1] * o2[:, hs])
    ha = jnp.concatenate(parts, axis=1).astype(BF16)
    ym = jnp.dot(hm_ref[...], wbm_ref[...], preferred_element_type=F32)
    ya = jnp.dot(ha, wba_ref[...], preferred_element_type=F32)
    merged = (_sigmoid(gm_ref[...].astype(F32)) * ym + _sigmoid(ga_ref[...].astype(F32)) * ya)
    x1 = x_ref[...] + jnp.dot(merged.astype(BF16), wo_ref[...], preferred_element_type=F32)
    x1_ref[...] = x1
    h2 = x1 * lax.rsqrt(jnp.mean(x1 * x1, axis=-1, keepdims=True) + RMS_EPS) * ng_ref[...]
    _store_token_tiles(h2_ref, _pack_pairs(h2))
    h_hi, h_lo = _bf16_parts(h2, 2)
    logits = jnp.dot(jnp.concatenate([h_hi, h_lo, h_hi], axis=1), rw_ref[...],
                     preferred_element_type=F32) + rb_ref[...]
    lane = lax.broadcasted_iota(jnp.int32, logits.shape, 1)
    idx_all = jnp.zeros(logits.shape, jnp.int32)
    val_all = jnp.zeros(logits.shape, F32)
    top0 = None
    esum = None
    picks = []
    chosen = jnp.zeros(logits.shape, F32)
    for k in range(TOP_K):
        m = jnp.max(logits, axis=1, keepdims=True)
        sel = jnp.min(jnp.where(logits == m, lane, LANES), axis=1, keepdims=True)
        if k == 0:
            top0 = m
        e = jnp.exp(m - top0)
        esum = e if k == 0 else esum + e
        idx_all = jnp.where(lane == k, sel, idx_all)
        val_all = jnp.where(lane == k, e, val_all)
        pick = lane == sel
        picks.append(pick)
        chosen = chosen + pick.astype(F32)
        logits = jnp.where(pick, -jnp.inf, logits)
    idx_ref[...] = idx_all
    gate_ref[...] = val_all / esum

    @pl.when(pl.program_id(0) == 0)
    def _():
        cnt_ref[...] = jnp.zeros_like(cnt_ref)

    trow = lax.broadcasted_iota(jnp.int32, (bm, bm), 0)
    tcol = lax.broadcasted_iota(jnp.int32, (bm, bm), 1)
    earlier = jnp.dot((tcol < trow).astype(BF16), chosen.astype(BF16),
                      preferred_element_type=F32) + cnt_ref[...]
    rank_all = jnp.zeros(logits.shape, jnp.int32)
    for k in range(TOP_K):
        r = jnp.sum(jnp.where(picks[k], earlier, 0.0), axis=1, keepdims=True)
        rank_all = jnp.where(lane == k, r.astype(jnp.int32), rank_all)
    rank_ref[...] = rank_all
    cnt_ref[...] = cnt_ref[...] + jnp.sum(chosen, axis=0, keepdims=True)
    count_ref[...] = jnp.broadcast_to(cnt_ref[...], count_ref.shape)


def _merge(hm, outs, lses, proj, x2, wbm, wba, wo, ng, rw, rb, *, bm=256):
    n = x2.shape[0]
    d = D_MODEL
    gcol = COL_GATES // d
    n_blk = outs[0].shape[2] // bm

    def rows(width):
        return pl.BlockSpec((bm, width), lambda i: (i, 0))

    def full(a, b):
        return pl.BlockSpec((a, b), lambda i: (0, 0), pipeline_mode=pl.Buffered(1))

    def residue(arr):
        dilation, width = arr.shape[1], arr.shape[3]
        return pl.BlockSpec((1, dilation, bm // dilation, width),
                            lambda i: (i // n_blk, 0, i % n_blk, 0))

    return pl.pallas_call(
        _merge_kernel,
        grid=(n // bm,),
        in_specs=[
            rows(MLSTM_WIDTH),
            residue(outs[0]), residue(outs[1]), residue(outs[2]),
            residue(lses[0]), residue(lses[1]), residue(lses[2]),
            pl.BlockSpec((bm, d), lambda i: (i, gcol)),
            pl.BlockSpec((bm, d), lambda i: (i, gcol + 1)),
            rows(d),
            full(MLSTM_WIDTH, d), full(GROUP_WIDTH, d), full(d, d),
            full(1, d), full(3 * d, LANES), full(1, LANES),
        ],
        out_specs=[rows(d), pl.BlockSpec((bm * PACK_ROWS, LANES), lambda i: (i, 0)),
                   rows(LANES), rows(LANES), rows(LANES),
                   pl.BlockSpec((8, LANES), lambda i: (0, 0))],
        out_shape=[
            jax.ShapeDtypeStruct((n, d), F32),
            jax.ShapeDtypeStruct((n * PACK_ROWS, LANES), jnp.uint32),
            jax.ShapeDtypeStruct((n, LANES), jnp.int32),
            jax.ShapeDtypeStruct((n, LANES), F32),
            jax.ShapeDtypeStruct((n, LANES), jnp.int32),
            jax.ShapeDtypeStruct((8, LANES), F32),
        ],
        scratch_shapes=[pltpu.VMEM((1, LANES), F32)],
        compiler_params=_params(("arbitrary",)),
        name="merge",
    )(hm, outs[0], outs[1], outs[2], lses[0], lses[1], lses[2], proj, proj, x2,
      wbm, wba, wo, ng, rw, rb)


PACK_ROWS = 8


def _pack_pairs(x):
    w = x.shape[1] // 2
    lo = lax.bitcast_convert_type(x[:, :w].astype(BF16).astype(F32), jnp.uint32) >> 16
    hi = lax.bitcast_convert_type(x[:, w:].astype(BF16).astype(F32), jnp.uint32)
    return (hi & jnp.uint32(0xFFFF0000)) | lo


def _unpack_pairs(words):
    lo = lax.bitcast_convert_type(words << 16, F32)
    hi = lax.bitcast_convert_type(words & jnp.uint32(0xFFFF0000), F32)
    return lo, hi


def _store_token_tiles(ref, words):
    rows = words.shape[0]
    for s in range(PACK_ROWS):
        ref[pl.ds(s, rows, stride=PACK_ROWS), :] = words[:, s * LANES:(s + 1) * LANES]


def _dispatch_kernel(fs_ref, fl_ref, nu_ref, dest_ref, hp_ref, xb_hbm, zero_ref, sem, zsem, *,
                     tokens, tm, n_blk):
    pr = PACK_ROWS
    fill_sizes = [1 << b for b in reversed(range((tm - 1).bit_length()))]

    def fill_copies(e):
        off = fs_ref[e]
        for p in fill_sizes:
            take = (fl_ref[e] & p) != 0
            dst = xb_hbm.at[pl.ds(pl.multiple_of(off * pr, pr), p * pr), :]
            yield take, pltpu.make_async_copy(zero_ref.at[pl.ds(0, p * pr), :], dst, zsem)
            off = off + jnp.where(take, p, 0)

    def block_copy(b):
        dst = xb_hbm.at[pl.ds(pl.multiple_of(b * (tm * pr), tm * pr), tm * pr), :]
        return pltpu.make_async_copy(zero_ref, dst, zsem)

    @pl.when(pl.program_id(0) == 0)
    def _():
        zero_ref[...] = jnp.zeros_like(zero_ref)
        for start in (True, False):
            def per_expert(e, carry):
                for take, cp in fill_copies(e):
                    @pl.when(take)
                    def _():
                        cp.start() if start else cp.wait()
                return carry

            def per_block(b, carry):
                block_copy(b).start() if start else block_copy(b).wait()
                return carry

            lax.fori_loop(0, N_EXPERTS, per_expert, 0)
            lax.fori_loop(nu_ref[0], n_blk, per_block, 0)

    def issue(t, carry):
        src = hp_ref.at[pl.ds(pl.multiple_of(t * pr, pr), pr), :]
        for k in range(TOP_K):
            slot = dest_ref[t * TOP_K + k]
            pltpu.make_async_copy(src, xb_hbm.at[pl.ds(pl.multiple_of(slot * pr, pr), pr), :],
                                  sem).start(priority=k % 2)
        return carry

    lax.fori_loop(0, tokens, issue, 0)
    for k in range(TOP_K):
        pltpu.make_async_copy(hp_ref, xb_hbm.at[pl.ds(0, tokens * pr), :], sem).wait()


def _dispatch(fill_start, fill_len, n_used, dest, h2p, n_pad, *, tokens=256, tm=MOE_TILE):
    pr = PACK_ROWS
    n_tok = h2p.shape[0] // pr
    kern = functools.partial(_dispatch_kernel, tokens=tokens, tm=tm, n_blk=n_pad // tm)
    return pl.pallas_call(
        kern,
        grid_spec=pltpu.PrefetchScalarGridSpec(
            num_scalar_prefetch=3,
            grid=(n_tok // tokens,),
            in_specs=[
                pl.BlockSpec((tokens * TOP_K,), lambda i, fs, fl, nu: (i,),
                             memory_space=pltpu.SMEM),
                pl.BlockSpec((tokens * pr, LANES), lambda i, fs, fl, nu: (i, 0)),
            ],
            out_specs=pl.BlockSpec(memory_space=pl.ANY),
            scratch_shapes=[pltpu.VMEM((tm * pr, LANES), jnp.uint32),
                            pltpu.SemaphoreType.DMA(()), pltpu.SemaphoreType.DMA(())],
        ),
        out_shape=jax.ShapeDtypeStruct((n_pad * pr, LANES), jnp.uint32),
        compiler_params=_params(("arbitrary",)),
        name="dispatch",
    )(fill_start, fill_len, n_used, dest, h2p)


def _cast_rows(src_ref, dst_ref, rows):
    total = src_ref.shape[0]

    def body(i, carry):
        r = pl.multiple_of(i * rows, rows)
        dst_ref[pl.ds(r, rows), :] = src_ref[pl.ds(r, rows), :].astype(BF16)
        return carry

    lax.fori_loop(0, total // rows, body, 0)


def _stream_expert_weights(be_ref, nu_ref, ne_ref, copies, convert):
    k, j = pl.program_id(0), pl.program_id(1)
    expert = be_ref[j]
    used = j < nu_ref[0]
    run_start = jnp.logical_or(j == 0, expert != be_ref[jnp.maximum(j - 1, 0)])

    @pl.when(jnp.logical_and(k == 0, j == 0))
    def _():
        for cp in copies(expert, k):
            cp.start()

    @pl.when(jnp.logical_and(used, run_start))
    def _():
        for cp in copies(expert, k):
            cp.wait()
        convert()
        last_run = ne_ref[j] < 0
        nxt_e = jnp.where(last_run, be_ref[0], ne_ref[j])
        nxt_k = jnp.where(last_run, k + 1, k)

        @pl.when(nxt_k < pl.num_programs(0))
        def _():
            for cp in copies(nxt_e, nxt_k):
                cp.start()

    return used


def _per_block_rows(used, bv_ref, o_ref, compute, store=None, rows_per_token=1):
    tm = o_ref.shape[0] // rows_per_token
    need = (bv_ref[pl.program_id(1)] + (MOE_ROW_STEP - 1)) // MOE_ROW_STEP

    def run(rows):
        res = compute(rows)
        head = o_ref.at[pl.ds(0, rows * rows_per_token), :]
        if store is None:
            head[...] = res
        else:
            store(head, res)
        if rows < tm:
            tail = o_ref.at[pl.ds(rows * rows_per_token, (tm - rows) * rows_per_token), :]
            tail[...] = jnp.zeros_like(tail)

    for q in range(1, tm // MOE_ROW_STEP + 1):
        @pl.when(jnp.logical_and(used, need == q))
        def _():
            run(q * MOE_ROW_STEP)

    @pl.when(jnp.logical_not(used))
    def _():
        o_ref[...] = jnp.zeros_like(o_ref)


def _gate_up_kernel(be_ref, nu_ref, ne_ref, bv_ref, x_ref, w_hbm, bg_ref, bu_ref, o_ref,
                    stage_ref, w_bf, sem):
    tf = o_ref.shape[1]
    kt = pl.num_programs(0)

    def copies(expert, k):
        return [pltpu.make_async_copy(
            w_hbm.at[expert, :, pl.ds(pl.multiple_of((half * kt + k) * tf, tf), tf)],
            stage_ref.at[half], sem.at[half]) for half in range(2)]

    def convert():
        for half in range(2):
            _cast_rows(stage_ref.at[half], w_bf.at[half], 256)

    used = _stream_expert_weights(be_ref, nu_ref, ne_ref, copies, convert)

    def compute(rows):
        halves = [_unpack_pairs(x_ref[pl.ds(s, rows, stride=PACK_ROWS), :])
                  for s in range(PACK_ROWS)]
        x = jnp.concatenate([h[0].astype(BF16) for h in halves]
                            + [h[1].astype(BF16) for h in halves], axis=1)
        gate = jnp.dot(x, w_bf[0], preferred_element_type=F32) + bg_ref[0]
        up = jnp.dot(x, w_bf[1], preferred_element_type=F32) + bu_ref[0]
        gate = jnp.minimum(gate, SWIGLU_LIMIT)
        up = jnp.clip(up, -SWIGLU_LIMIT, SWIGLU_LIMIT)
        glu = gate * _sigmoid(SWIGLU_ALPHA * gate)
        return ((up + 1.0) * glu).astype(BF16)

    _per_block_rows(used, bv_ref, o_ref, compute)


def _used_block(j, nu):
    return jnp.minimum(j, nu[0] - 1)


def _gate_up(sched, xb, w_gate_up, b_gate_up3, *, tm=MOE_TILE, tf=1024):
    n_pad, d = xb.shape[0] // PACK_ROWS, D_MODEL
    kt = D_FF // tf

    def bspec(off):
        return pl.BlockSpec((1, 1, tf),
                            lambda k, j, be, nu, ne, bv: (be[_used_block(j, nu)], 0, off + k))

    return pl.pallas_call(
        _gate_up_kernel,
        grid_spec=pltpu.PrefetchScalarGridSpec(
            num_scalar_prefetch=4,
            grid=(kt, n_pad // tm),
            in_specs=[
                pl.BlockSpec((tm * PACK_ROWS, LANES),
                             lambda k, j, be, nu, ne, bv: (_used_block(j, nu), 0)),
                pl.BlockSpec(memory_space=pl.ANY),
                bspec(0), bspec(kt),
            ],
            out_specs=pl.BlockSpec((tm, tf), lambda k, j, be, nu, ne, bv: (j, k)),
            scratch_shapes=[pltpu.VMEM((2, d, tf), F32), pltpu.VMEM((2, d, tf), BF16),
                            pltpu.SemaphoreType.DMA((2,))],
        ),
        out_shape=jax.ShapeDtypeStruct((n_pad, D_FF), BF16),
        compiler_params=_params(("arbitrary", "arbitrary")),
        name="gate_up",
    )(*sched, xb, w_gate_up, b_gate_up3, b_gate_up3)


def _down_kernel(be_ref, nu_ref, ne_ref, bv_ref, h_ref, w_hbm, bd_ref, o_ref, stage_ref, w_bf,
                 sem):
    def copies(expert, k):
        del k
        return [pltpu.make_async_copy(w_hbm.at[expert], stage_ref, sem.at[0])]

    def convert():
        _cast_rows(stage_ref, w_bf, 256)

    used = _stream_expert_weights(be_ref, nu_ref, ne_ref, copies, convert)

    def compute(rows):
        y = jnp.dot(h_ref[0:rows, :], w_bf[...], preferred_element_type=F32) + bd_ref[0]
        return _pack_pairs(y)

    _per_block_rows(used, bv_ref, o_ref, compute, store=_store_token_tiles,
                    rows_per_token=PACK_ROWS)


def _down(sched, hb, w_down, b_down3, *, tm=MOE_TILE):
    n_pad, f = hb.shape
    d = D_MODEL
    return pl.pallas_call(
        _down_kernel,
        grid_spec=pltpu.PrefetchScalarGridSpec(
            num_scalar_prefetch=4,
            grid=(1, n_pad // tm),
            in_specs=[
                pl.BlockSpec((tm, f), lambda k, j, be, nu, ne, bv: (_used_block(j, nu), 0)),
                pl.BlockSpec(memory_space=pl.ANY),
                pl.BlockSpec((1, 1, d),
                             lambda k, j, be, nu, ne, bv: (be[_used_block(j, nu)], 0, 0)),
            ],
            out_specs=pl.BlockSpec((tm * PACK_ROWS, LANES), lambda k, j, be, nu, ne, bv: (j, 0)),
            scratch_shapes=[pltpu.VMEM((f, d), F32), pltpu.VMEM((f, d), BF16),
                            pltpu.SemaphoreType.DMA((1,))],
        ),
        out_shape=jax.ShapeDtypeStruct((n_pad * PACK_ROWS, LANES), jnp.uint32),
        compiler_params=_params(("arbitrary", "arbitrary")),
        name="down",
    )(*sched, hb, w_down, b_down3)


def _combine_kernel(dest_ref, dest_next_ref, y_hbm, gate_ref, x1_ref, ng_ref, o_ref, buf_ref, sem,
                    *, rows):
    pr = PACK_ROWS
    slab = 64
    half = x1_ref.shape[1] // 2
    i = pl.program_id(0)
    slot_rows = TOP_K * rows * pr
    cur = i % 2
    base = pl.multiple_of(cur * slot_rows, slot_rows)
    nxt_base = pl.multiple_of((1 - cur) * slot_rows, slot_rows)

    def gather(idx_ref, dst_base, buf, t):
        for k in range(TOP_K):
            slot = idx_ref[t * TOP_K + k]
            pltpu.make_async_copy(
                y_hbm.at[pl.ds(pl.multiple_of(slot * pr, pr), pr), :],
                buf_ref.at[pl.ds(pl.multiple_of(dst_base + (k * rows + t) * pr, pr), pr), :],
                sem.at[buf]).start(priority=k % 2)

    @pl.when(i == 0)
    def _():
        def first(t, carry):
            gather(dest_ref, 0, 0, t)
            return carry

        lax.fori_loop(0, rows, first, 0)

    pltpu.make_async_copy(y_hbm.at[pl.ds(0, slot_rows), :],
                          buf_ref.at[pl.ds(base, slot_rows), :], sem.at[cur]).wait()

    def reduce_slab(t0):
        gates = [gate_ref[pl.ds(t0, slab), k:k + 1] for k in range(TOP_K)]
        lows, highs = [], []
        for s in range(pr):
            lo = x1_ref[pl.ds(t0, slab), s * LANES:(s + 1) * LANES]
            hi = x1_ref[pl.ds(t0, slab), half + s * LANES:half + (s + 1) * LANES]
            for k in range(TOP_K):
                y_lo, y_hi = _unpack_pairs(
                    buf_ref[pl.ds(base + (k * rows + t0) * pr + s, slab, stride=pr), :])
                lo = lo + gates[k] * y_lo
                hi = hi + gates[k] * y_hi
            lows.append(lo)
            highs.append(hi)
        acc = jnp.concatenate(lows + highs, axis=1)
        o_ref[pl.ds(t0, slab), :] = (
            acc * lax.rsqrt(jnp.mean(acc * acc, axis=-1, keepdims=True) + RMS_EPS) * ng_ref[...])

    def trip_with_prefetch(j, carry):
        t0 = pl.multiple_of(j * slab, slab)
        for tt in range(slab):
            gather(dest_next_ref, nxt_base, 1 - cur, t0 + tt)
        reduce_slab(t0)
        return carry

    def trip(j, carry):
        reduce_slab(pl.multiple_of(j * slab, slab))
        return carry

    has_next = i + 1 < pl.num_programs(0)

    @pl.when(has_next)
    def _():
        lax.fori_loop(0, rows // slab, trip_with_prefetch, 0)

    @pl.when(jnp.logical_not(has_next))
    def _():
        lax.fori_loop(0, rows // slab, trip, 0)


def _combine(dest, yb, gates, x1, ng, *, rows=256):
    n, d = x1.shape
    kern = functools.partial(_combine_kernel, rows=rows)
    n_steps = n // rows
    return pl.pallas_call(
        kern,
        grid=(n_steps,),
        in_specs=[
            pl.BlockSpec((rows * TOP_K,), lambda i: (i,), memory_space=pltpu.SMEM),
            pl.BlockSpec((rows * TOP_K,), lambda i: (jnp.minimum(i + 1, n_steps - 1),),
                         memory_space=pltpu.SMEM),
            pl.BlockSpec(memory_space=pl.ANY),
            pl.BlockSpec((rows, LANES), lambda i: (i, 0)),
            pl.BlockSpec((rows, d), lambda i: (i, 0)),
            pl.BlockSpec((1, d), lambda i: (0, 0)),
        ],
        out_specs=pl.BlockSpec((rows, d), lambda i: (i, 0)),
        out_shape=jax.ShapeDtypeStruct((n, d), F32),
        scratch_shapes=[pltpu.VMEM((2 * TOP_K * rows * PACK_ROWS, LANES), jnp.uint32),
                        pltpu.SemaphoreType.DMA((2,))],
        compiler_params=_params(("arbitrary",)),
        name="combine",
    )(dest, dest, yb, gates, x1, ng)


def _routing(top_idx, rank, counts, tm):
    n_tok = top_idx.shape[0]
    n_asg = n_tok * TOP_K
    e_flat = top_idx.reshape(n_asg)
    padded = (counts + tm - 1) // tm * tm
    pend = jnp.cumsum(padded)
    pstart = pend - padded
    experts = jnp.arange(N_EXPERTS, dtype=jnp.int32)
    start_of = jnp.sum(jnp.where(e_flat[:, None] == experts[None, :], pstart[None, :], 0), axis=1)
    dest = (start_of + rank.reshape(n_asg)).astype(jnp.int32)
    n_blk = -(-(n_asg + N_EXPERTS * (tm - 1)) // tm)
    n_pad = n_blk * tm
    fill_start = (pstart + counts).astype(jnp.int32)
    fill_len = (padded - counts).astype(jnp.int32)
    block_start = jnp.arange(n_blk, dtype=jnp.int32) * tm
    block_e = jnp.minimum(
        jnp.sum((pend[None, :] <= block_start[:, None]).astype(jnp.int32), axis=1),
        N_EXPERTS - 1).astype(jnp.int32)
    n_used = (pend[-1:] // tm).astype(jnp.int32)
    run_end = pend[block_e] // tm
    next_e = jnp.where(run_end < n_used[0], block_e[jnp.minimum(run_end, n_blk - 1)], -1)
    block_rows = jnp.clip(fill_start[block_e] - block_start, 0, tm)
    sched = (block_e, n_used, next_e.astype(jnp.int32), block_rows.astype(jnp.int32))
    return dest, fill_start, fill_len, sched, n_pad


def _layer(x, norm_mix_g, w_in, conv_w, conv_b, igate_b, fgate_b, mlstm_norm_g, rel_bias,
           w_branch_mlstm, w_branch_attn, w_out, norm_moe_g, router_w, router_b,
           w_gate_up, b_gate_up, w_down, b_down, out_norm_g):
    bsz, s, d = x.shape
    n = bsz * s
    x2 = x.reshape(n, d)
    w_main, w_if, w_qkv = _split_w_in(w_in)
    bpad = jnp.zeros((LANES - MLSTM_HEADS,), F32)
    gate_b = jnp.concatenate([igate_b, bpad, fgate_b, bpad]).reshape(1, IF_WIDTH)

    norm_g = norm_mix_g.reshape(1, d)
    proj, ifg = _in_proj(x2, norm_g, w_main, w_if)
    proj3 = proj.reshape(bsz, s, D_MAIN)
    hm, qkvs = _mixer_inputs(x2, norm_g, w_qkv, proj3, ifg.reshape(bsz, s, IF_WIDTH), conv_w,
                             conv_b.reshape(1, -1), gate_b, mlstm_norm_g.reshape(1, -1))
    hm = hm.reshape(n, MLSTM_WIDTH)

    outs, lses = [], []
    for g, (_, dilation) in enumerate(ATTN_GROUPS):
        o_g_, lse_g = _attn_group(qkvs[g], _attn_bias(rel_bias, g, dilation), g)
        outs.append(o_g_)
        lses.append(lse_g)

    rw = jnp.concatenate([router_w, jnp.zeros((d, LANES - N_EXPERTS), F32)], axis=1)
    rw_hi = rw.astype(BF16)
    rw_lo = (rw - rw_hi.astype(F32)).astype(BF16)
    rw = jnp.concatenate([rw_hi, rw_hi, rw_lo], axis=0)
    rb = jnp.concatenate([router_b, jnp.full((LANES - N_EXPERTS,), NEG, F32)]).reshape(1, LANES)
    x1, h2, idx, gates, rank, counts = _merge(
        hm, outs, lses, proj, x2, w_branch_mlstm.astype(BF16), w_branch_attn.astype(BF16),
        w_out.astype(BF16), norm_moe_g.reshape(1, d), rw, rb)

    dest, fill_start, fill_len, sched, n_pad = _routing(
        idx[:, :TOP_K], rank[:, :TOP_K], counts[0, :N_EXPERTS].astype(jnp.int32), MOE_TILE)
    xb = _dispatch(fill_start, fill_len, sched[1], dest, h2, n_pad)
    hb = _gate_up(sched, xb, w_gate_up, b_gate_up.reshape(N_EXPERTS, 1, 2 * D_FF))
    yb = _down(sched, hb, w_down, b_down.reshape(N_EXPERTS, 1, d))
    out = _combine(dest, yb, gates, x1, out_norm_g.reshape(1, d))
    return out.reshape(bsz, s, d)


def kernel(x, norm_mix_g, w_in, conv_w, conv_b, igate_b, fgate_b, mlstm_norm_g, rel_bias,
           w_branch_mlstm, w_branch_attn, w_out, norm_moe_g, router_w, router_b,
           w_gate_up, b_gate_up, w_down, b_down, norm_final_g):
    assert w_in.shape[0] == 1, "single-layer block"
    return _layer(x, norm_mix_g[0], w_in[0], conv_w[0], conv_b[0], igate_b[0], fgate_b[0],
                  mlstm_norm_g[0], rel_bias, w_branch_mlstm[0], w_branch_attn[0], w_out[0],
                  norm_moe_g[0], router_w[0], router_b[0], w_gate_up[0], b_gate_up[0],
                  w_down[0], b_down[0], norm_final_g)
```

```python
import functools
import math

import jax
import jax.numpy as jnp
from jax import lax
from jax.experimental import pallas as pl
from jax.experimental.pallas import tpu as pltpu

F32 = jnp.float32
BF16 = jnp.bfloat16

D_MODEL = 2048
MLSTM_HEADS = 8
HEAD_DIM = 128
MLSTM_WIDTH = MLSTM_HEADS * HEAD_DIM
CONV_WIDTH = 4
ATTN_GROUPS = ((128, 1), (512, 4), (2048, 16))
N_GROUPS = 3
HEADS_PER_GROUP = 4
ATTN_HEADS = HEADS_PER_GROUP * N_GROUPS
ATTN_WIDTH = ATTN_HEADS * HEAD_DIM
GROUP_WIDTH = HEADS_PER_GROUP * HEAD_DIM
ATTN_BLOCK = 128
ATTN_SPAN = 128
REL_BUCKETS = 32
REL_MAX_DIST = 2048
N_EXPERTS = 32
TOP_K = 4
D_FF = D_MODEL
SWIGLU_LIMIT = 7.0
SWIGLU_ALPHA = 1.702
RMS_EPS = 1e-6
NEG = -1e30

COL_QK = 0
COL_V = 2 * MLSTM_WIDTH
COL_O = COL_V + MLSTM_WIDTH
COL_GATES = COL_O + MLSTM_WIDTH
D_MAIN = COL_GATES + 2 * D_MODEL
QKV_WIDTH = 3 * GROUP_WIDTH
LANES = 128
IF_WIDTH = 2 * LANES

MLSTM_CHUNK = 128
MOE_TILE = 512
MOE_ROW_STEP = 256
VMEM_LIMIT = 56 * 1024 * 1024


def _sigmoid(x):
    return 0.5 * jnp.tanh(0.5 * x) + 0.5


def _bf16_parts(x, n_parts):
    parts = []
    for _ in range(n_parts):
        p = x.astype(BF16)
        parts.append(p)
        x = x - p.astype(F32)
    return parts


def _log_sigmoid(x):
    return -(jnp.maximum(-x, 0.0) + jnp.log1p(jnp.exp(-jnp.abs(x))))


def _params(sem):
    return pltpu.CompilerParams(dimension_semantics=sem, vmem_limit_bytes=VMEM_LIMIT)


def _in_proj_kernel(x_ref, g_ref, w_ref, wif_ref, o_ref, oif_ref, h_ref, *, bm, rows):
    @pl.when(pl.program_id(1) == 0)
    def _():
        for r in range(0, bm, rows):
            x = x_ref[r:r + rows, :]
            ms = jnp.mean(x * x, axis=-1, keepdims=True)
            h = (x * lax.rsqrt(ms + RMS_EPS) * g_ref[...]).astype(BF16)
            h_ref[r:r + rows, :] = h
            oif_ref[r:r + rows, :] = jnp.dot(h, wif_ref[...], preferred_element_type=F32)

    o_ref[...] = jnp.dot(h_ref[...], w_ref[...], preferred_element_type=F32).astype(BF16)


def _residue_perm(size, dilation, inverse):
    per = size // dilation
    i = lax.broadcasted_iota(jnp.int32, (size, size), 0)
    j = lax.broadcasted_iota(jnp.int32, (size, size), 1)
    if inverse:
        src = (i & (dilation - 1)) * per + (i >> (dilation.bit_length() - 1))
    else:
        src = (i & (per - 1)) * dilation + (i >> (per.bit_length() - 1))
    return j == src


def _attn_proj_block(h, w_ref, o_ref):
    dilation = o_ref.shape[1]
    per = o_ref.shape[2]
    if dilation > 1:
        perm = _residue_perm(h.shape[0], dilation, False).astype(BF16)
        h = jnp.dot(perm, h, preferred_element_type=F32).astype(BF16)
    res = jnp.dot(h, w_ref[...], preferred_element_type=F32).astype(BF16)
    for r in range(dilation):
        o_ref[0, r] = res[r * per:(r + 1) * per, :]


_O_I = 4 * MLSTM_WIDTH
_O_QA = _O_I + 2 * MLSTM_HEADS
_O_GATES = _O_QA + 3 * ATTN_WIDTH
D_IN = _O_GATES + 2 * D_MODEL


SPLIT_CHUNK = 512


def _split_w_in_kernel(off_ref, wt_hbm, main_ref, if_ref, q0_ref, q1_ref, q2_ref, buf_ref, sem):
    c = pl.program_id(0)
    n_main = D_MAIN // SPLIT_CHUNK
    gw = GROUP_WIDTH

    def fetch(step):
        slot = step % 2
        rows = pl.ds(pl.multiple_of(off_ref[step], 8), SPLIT_CHUNK)
        return pltpu.make_async_copy(wt_hbm.at[rows, :], buf_ref.at[slot], sem.at[slot])

    @pl.when(c == 0)
    def _():
        fetch(c).start()

    @pl.when(c + 1 < pl.num_programs(0))
    def _():
        fetch(c + 1).start()

    fetch(c).wait()
    piece = buf_ref[c % 2].T

    @pl.when(c < n_main)
    def _():
        main_ref[...] = piece.astype(BF16)

    for g, q_ref in enumerate((q0_ref, q1_ref, q2_ref)):
        for part in range(3):
            @pl.when(c == n_main + 3 * g + part)
            def _():
                q_ref[:, part * gw:(part + 1) * gw] = piece.astype(BF16)

    @pl.when(c == n_main + 3 * N_GROUPS)
    def _():
        window = piece[:, :LANES]
        lane = lax.broadcasted_iota(jnp.int32, window.shape, 1)
        i_part = jnp.where(lane < MLSTM_HEADS, window, 0.0)
        f_part = jnp.where(lane < MLSTM_HEADS,
                           pltpu.roll(window, LANES - MLSTM_HEADS, axis=1), 0.0)
        if_ref[...] = jnp.concatenate([i_part, f_part], axis=1).astype(BF16)


def _split_w_in(w_in_t):
    d = w_in_t.shape[1]
    assert w_in_t.shape[0] == D_IN
    ch = SPLIT_CHUNK
    n_main = D_MAIN // ch
    offsets = ([k * ch for k in range(_O_I // ch)]
               + [_O_GATES + k * ch for k in range(2 * D_MODEL // ch)]
               + [_O_QA + part * ATTN_WIDTH + g * GROUP_WIDTH
                  for g in range(N_GROUPS) for part in range(3)]
               + [_O_I])
    assert len(offsets) == n_main + 3 * N_GROUPS + 1 and GROUP_WIDTH == ch

    def whole(width):
        return pl.BlockSpec((d, width), lambda i, off: (0, 0))

    outs = pl.pallas_call(
        _split_w_in_kernel,
        grid_spec=pltpu.PrefetchScalarGridSpec(
            num_scalar_prefetch=1,
            grid=(len(offsets),),
            in_specs=[pl.BlockSpec(memory_space=pl.ANY)],
            out_specs=[pl.BlockSpec((d, ch), lambda i, off: (0, jnp.minimum(i, n_main - 1))),
                       whole(IF_WIDTH), whole(QKV_WIDTH), whole(QKV_WIDTH), whole(QKV_WIDTH)],
            scratch_shapes=[pltpu.VMEM((2, ch, d), F32), pltpu.SemaphoreType.DMA((2,))],
        ),
        out_shape=[jax.ShapeDtypeStruct((d, width), BF16)
                   for width in (D_MAIN, IF_WIDTH, QKV_WIDTH, QKV_WIDTH, QKV_WIDTH)],
        compiler_params=_params(("arbitrary",)),
        name="split_w_in",
    )(jnp.asarray(offsets, jnp.int32), w_in_t)
    return outs[0], outs[1], outs[2:]


def _in_proj(x2, g, w_main, w_if, *, bm=1024, bn=1024):
    n = x2.shape[0]
    bm = min(bm, n)
    kern = functools.partial(_in_proj_kernel, bm=bm, rows=256)
    return pl.pallas_call(
        kern,
        grid=(n // bm, D_MAIN // bn),
        in_specs=[
            pl.BlockSpec((bm, D_MODEL), lambda i, j: (i, 0)),
            pl.BlockSpec((1, D_MODEL), lambda i, j: (0, 0)),
            pl.BlockSpec((D_MODEL, bn), lambda i, j: (0, j)),
            pl.BlockSpec((D_MODEL, IF_WIDTH), lambda i, j: (0, 0)),
        ],
        out_specs=[
            pl.BlockSpec((bm, bn), lambda i, j: (i, j)),
            pl.BlockSpec((bm, IF_WIDTH), lambda i, j: (i, 0)),
        ],
        out_shape=[
            jax.ShapeDtypeStruct((n, D_MAIN), BF16),
            jax.ShapeDtypeStruct((n, IF_WIDTH), F32),
        ],
        scratch_shapes=[pltpu.VMEM((bm, D_MODEL), BF16)],
        compiler_params=_params(("arbitrary", "arbitrary")),
        name="in_proj",
    )(x2, g, w_main, w_if)


def _mlstm_chunk(rows, qk_ref, v_ref, og_ref, if_ref, cw_ref, cb_ref, gb_ref, ng_ref, out_ref,
                 tail_ref, c_ref, m_ref):
    L = rows.stop - rows.start
    dh = HEAD_DIM

    tail = tail_ref.shape[0]
    cur = qk_ref[0, rows, :]
    ext = jnp.concatenate([tail_ref[...], cur], axis=0)
    tail_ref[...] = cur[L - tail:, :]
    srow = lax.broadcasted_iota(jnp.int32, (L, tail + L), 0)
    scol = lax.broadcasted_iota(jnp.int32, (L, tail + L), 1)
    acc = cur.astype(F32) * cw_ref[CONV_WIDTH - 1:CONV_WIDTH, :] + cb_ref[...]
    for s in range(1, CONV_WIDTH):
        shift = (scol == srow + (tail - s)).astype(BF16)
        acc = acc + (jnp.dot(shift, ext, preferred_element_type=F32)
                     * cw_ref[CONV_WIDTH - 1 - s:CONV_WIDTH - s, :])
    qk = acc * _sigmoid(acc)
    q_all = qk[:, :MLSTM_WIDTH]
    k_all = qk[:, MLSTM_WIDTH:] * (dh ** -0.5)

    head_lane = lax.broadcasted_iota(jnp.int32, (L, LANES), 1) < MLSTM_HEADS
    gi = jnp.where(head_lane, if_ref[0, rows, :LANES] + gb_ref[:, :LANES], 0.0)
    lf = jnp.where(head_lane, _log_sigmoid(if_ref[0, rows, LANES:] + gb_ref[:, LANES:]), 0.0)
    row = lax.broadcasted_iota(jnp.int32, (L, L), 0)
    col = lax.broadcasted_iota(jnp.int32, (L, L), 1)
    causal = col <= row
    cum = jnp.dot(causal.astype(F32), lf, preferred_element_type=F32,
                  precision=lax.Precision.HIGHEST)
    rmat = gi - cum
    cmax = rmat
    trow = lax.broadcasted_iota(jnp.int32, (L, LANES), 0)
    step = 1
    while step < L:
        cmax = jnp.maximum(cmax, jnp.where(trow >= step, pltpu.roll(cmax, step, axis=0), -jnp.inf))
        step *= 2
    m_prev = m_ref[...]
    gmax = jnp.maximum(m_prev, cmax)
    s_inter = jnp.exp(m_prev - gmax)
    inv_scale = jnp.exp(-(cum + gmax))
    b_last = cum[L - 1:L, :]
    m_new = b_last + gmax[L - 1:L, :]
    w_state = jnp.exp(b_last + rmat - m_new)
    decay = jnp.exp(b_last + m_prev - m_new)
    m_ref[...] = m_new
    rmat_t = rmat.T
    ones_blk = jnp.ones((L, dh), BF16)

    def column(mat, h):
        return jnp.broadcast_to(mat[:, h:h + 1], (L, dh))

    for h in range(MLSTM_HEADS):
        hs = slice(h * dh, (h + 1) * dh)
        p = jnp.exp(jnp.where(causal, rmat_t[h:h + 1, :] - column(gmax, h), NEG))
        qh = q_all[:, hs]
        kh = k_all[:, hs]
        s = lax.dot_general(qh.astype(BF16), kh.astype(BF16), (((1,), (1,)), ((), ())),
                            preferred_element_type=F32)
        vaug = jnp.concatenate([v_ref[0, rows, hs], ones_blk], axis=1)
        c_prev = c_ref[h]
        lhs = jnp.concatenate([(p * s).astype(BF16), (column(s_inter, h) * qh).astype(BF16)],
                              axis=1)
        rhs = jnp.concatenate([vaug, c_prev.astype(BF16)], axis=0)
        nd = jnp.dot(lhs, rhs, preferred_element_type=F32)
        hh = nd[:, :dh] / jnp.maximum(jnp.abs(nd[:, dh:]), column(inv_scale, h))

        kw = (column(w_state, h) * kh).astype(BF16)
        kv = lax.dot_general(kw, vaug, (((0,), (0,)), ((), ())), preferred_element_type=F32)
        c_ref[h] = decay[:, h:h + 1] * c_prev + kv

        y = hh * lax.rsqrt(jnp.mean(hh * hh, axis=-1, keepdims=True) + RMS_EPS) * ng_ref[:, hs]
        y = y * _sigmoid(og_ref[0, rows, hs].astype(F32))
        out_ref[0, rows, hs] = y.astype(BF16)


def _mixer_kernel(x_ref, xg_ref, w0_ref, w1_ref, w2_ref,
                  qk_ref, v_ref, og_ref, if_ref, cw_ref, cb_ref, gb_ref, ng_ref,
                  hm_ref, a0_ref, a1_ref, a2_ref, tail_ref, c_ref, m_ref, *, chunk):
    @pl.when(pl.program_id(1) == 0)
    def _():
        tail_ref[...] = jnp.zeros_like(tail_ref)
        c_ref[...] = jnp.zeros_like(c_ref)
        m_ref[...] = jnp.zeros_like(m_ref)

    x = x_ref[...]
    h = (x * lax.rsqrt(jnp.mean(x * x, axis=-1, keepdims=True) + RMS_EPS) * xg_ref[...]).astype(BF16)
    projections = [(w0_ref, a0_ref), (w1_ref, a1_ref), (w2_ref, a2_ref)]
    chunks = list(range(0, x_ref.shape[0], chunk))
    while projections or chunks:
        if projections:
            _attn_proj_block(h, *projections.pop(0))
        if chunks:
            r0 = chunks.pop(0)
            _mlstm_chunk(slice(r0, r0 + chunk), qk_ref, v_ref, og_ref, if_ref, cw_ref, cb_ref,
                         gb_ref, ng_ref, hm_ref, tail_ref, c_ref, m_ref)


def _mixer_inputs(x2, norm_g, w_qkv, proj3, ifg3, conv_w, conv_b, gate_b, mnorm_g, *,
                  rows=256, chunk=MLSTM_CHUNK):
    bsz, s, _ = proj3.shape
    w = MLSTM_WIDTH
    n_blk = s // rows
    dilations = [d for _, d in ATTN_GROUPS]

    def const(shape):
        return pl.BlockSpec(shape, lambda b, i: (0,) * len(shape))

    def weight():
        return pl.BlockSpec((D_MODEL, QKV_WIDTH), lambda b, i: (0, 0), pipeline_mode=pl.Buffered(1))

    kern = functools.partial(_mixer_kernel, chunk=chunk)
    outs = pl.pallas_call(
        kern,
        grid=(bsz, n_blk),
        in_specs=[
            pl.BlockSpec((rows, D_MODEL), lambda b, i: (b * n_blk + i, 0)),
            const((1, D_MODEL)),
            weight(), weight(), weight(),
            pl.BlockSpec((1, rows, 2 * w), lambda b, i: (b, i, COL_QK // (2 * w))),
            pl.BlockSpec((1, rows, w), lambda b, i: (b, i, COL_V // w)),
            pl.BlockSpec((1, rows, w), lambda b, i: (b, i, COL_O // w)),
            pl.BlockSpec((1, rows, IF_WIDTH), lambda b, i: (b, i, 0)),
            const((CONV_WIDTH, 2 * w)), const((1, 2 * w)), const((1, IF_WIDTH)), const((1, w)),
        ],
        out_specs=[pl.BlockSpec((1, rows, w), lambda b, i: (b, i, 0))] + [
            pl.BlockSpec((1, d, rows // d, QKV_WIDTH), lambda b, i: (b, 0, i, 0)) for d in dilations],
        out_shape=[jax.ShapeDtypeStruct((bsz, s, w), BF16)] + [
            jax.ShapeDtypeStruct((bsz, d, s // d, QKV_WIDTH), BF16) for d in dilations],
        scratch_shapes=[
            pltpu.VMEM((16, 2 * w), BF16),
            pltpu.VMEM((MLSTM_HEADS, HEAD_DIM, 2 * HEAD_DIM), F32),
            pltpu.VMEM((1, LANES), F32),
        ],
        compiler_params=_params(("arbitrary", "arbitrary")),
        name="mixer_inputs",
    )(x2, norm_g, *w_qkv, proj3, proj3, proj3, ifg3, conv_w, conv_b, gate_b, mnorm_g)
    return outs[0], outs[1:]


def _attn_kernel(q_ref, kp_ref, kc_ref, vp_ref, vc_ref, bias_ref, o_ref, lse_ref):
    qb = ATTN_BLOCK
    dh = HEAD_DIM
    n_q = q_ref.shape[2] // qb
    first = pl.program_id(2) == 0
    kcol = lax.broadcasted_iota(jnp.int32, (qb, 2 * qb), 1)
    dead = jnp.logical_and(first, kcol < qb)
    lane = lax.broadcasted_iota(jnp.int32, (qb, LANES), 1)
    for jq in range(n_q):
        rows = slice(jq * qb, (jq + 1) * qb)
        lse_all = jnp.zeros((qb, LANES), F32)
        for j in range(HEADS_PER_GROUP):
            hs = slice(j * dh, (j + 1) * dh)
            q = q_ref[0, 0][rows, hs]
            if jq == 0:
                k = jnp.concatenate([kp_ref[0, 0][:, hs], kc_ref[0, 0][0:qb, hs]], axis=0)
                v = jnp.concatenate([vp_ref[0, 0][:, hs], vc_ref[0, 0][0:qb, hs]], axis=0)
            else:
                k = kc_ref[0, 0][(jq - 1) * qb:(jq + 1) * qb, hs]
                v = vc_ref[0, 0][(jq - 1) * qb:(jq + 1) * qb, hs]
            s = lax.dot_general(q, k, (((1,), (1,)), ((), ())), preferred_element_type=F32)
            s = s * (dh ** -0.5) + bias_ref[j]
            if jq == 0:
                s = jnp.where(dead, NEG, s)
            mx = jnp.max(s, axis=1, keepdims=True)
            p = jnp.exp(s - mx)
            den = jnp.sum(p, axis=1, keepdims=True)
            o = jnp.dot(p.astype(BF16), v, preferred_element_type=F32) / den
            o_ref[0, 0, rows, hs] = o.astype(BF16)
            lse_all = jnp.where(lane == j, mx + jnp.log(den), lse_all)
        lse_ref[0, 0, rows, :] = lse_all


def _attn_group(qkv, bias, g, *, q_blocks=2):
    bsz, dilation, n, _ = qkv.shape
    gw = GROUP_WIDTH
    q_blocks = min(q_blocks, n // ATTN_BLOCK)
    run = q_blocks * ATTN_BLOCK

    def own(part, width=gw):
        return pl.BlockSpec((1, 1, run, width), lambda b, r, i: (b, r, i, part))

    def previous(part):
        return pl.BlockSpec((1, 1, ATTN_BLOCK, gw),
                            lambda b, r, i: (b, r, jnp.maximum(i * q_blocks - 1, 0), part))

    return pl.pallas_call(
        _attn_kernel,
        grid=(bsz, dilation, n // run),
        in_specs=[
            own(0),
            previous(1), own(1),
            previous(2), own(2),
            pl.BlockSpec((HEADS_PER_GROUP, ATTN_BLOCK, 2 * ATTN_BLOCK), lambda b, r, i: (0, 0, 0)),
        ],
        out_specs=[own(0), own(0, LANES)],
        out_shape=[
            jax.ShapeDtypeStruct((bsz, dilation, n, gw), BF16),
            jax.ShapeDtypeStruct((bsz, dilation, n, LANES), F32),
        ],
        compiler_params=_params(("arbitrary", "arbitrary", "arbitrary")),
        name=f"attn_g{g}",
    )(qkv, qkv, qkv, qkv, qkv, bias)


def _t5_bucket(dist):
    max_exact = REL_BUCKETS // 2
    d_f = jnp.maximum(dist, 1).astype(F32)
    large = max_exact + (jnp.log(d_f / max_exact) / math.log(REL_MAX_DIST / max_exact)
                         * (REL_BUCKETS - max_exact)).astype(jnp.int32)
    large = jnp.minimum(large, REL_BUCKETS - 1)
    return jnp.where(dist < max_exact, dist, large)


def _attn_bias(rel_bias, g, dilation):
    span = ATTN_SPAN
    buckets = _t5_bucket(jnp.arange(span + 1, dtype=jnp.int32) * dilation)
    vec = rel_bias[buckets][:, g * HEADS_PER_GROUP:(g + 1) * HEADS_PER_GROUP].T.astype(F32)
    qpos = jnp.arange(ATTN_BLOCK)[:, None]
    kpos = jnp.arange(2 * ATTN_BLOCK)[None, :]
    dist = qpos + ATTN_BLOCK - kpos
    valid = (dist >= 0) & (dist <= span)
    onehot = (dist[:, :, None] == jnp.arange(span + 1)[None, None, :]).astype(F32)
    table = jnp.einsum('qkj,hj->hqk', onehot, vec, precision=lax.Precision.HIGHEST)
    return jnp.where(valid[None], table, NEG)


def _merge_kernel(hm_ref, o0_ref, o1_ref, o2_ref, l0_ref, l1_ref, l2_ref, gm_ref, ga_ref, x_ref,
                  wbm_ref, wba_ref, wo_ref, ng_ref, rw_ref, rb_ref,
                  x1_ref, h2_ref, idx_ref, gate_ref, rank_ref, count_ref, cnt_ref):
    dh = HEAD_DIM
    bm = x_ref.shape[0]

    def token_order(ref):
        dilation = ref.shape[1]
        if dilation == 1:
            return ref[0, 0]
        perm = _residue_perm(bm, dilation, True).astype(BF16)
        blk = ref[0].reshape(bm, ref.shape[3])
        parts = [blk] if blk.dtype == BF16 else _bf16_parts(blk, 3)
        out = None
        for p in parts:
            moved = jnp.dot(perm, p, preferred_element_type=F32)
            out = moved if out is None else out + moved
        return out

    l0, l1, l2 = token_order(l0_ref), token_order(l1_ref), token_order(l2_ref)
    o0, o1, o2 = token_order(o0_ref), token_order(o1_ref), token_order(o2_ref)
    mx = jnp.maximum(jnp.maximum(l0, l1), l2)
    e0, e1, e2 = jnp.exp(l0 - mx), jnp.exp(l1 - mx), jnp.exp(l2 - mx)
    den = e0 + e1 + e2
    w0, w1, w2 = e0 / den, e1 / den, e2 / den
    parts = []
    for j in range(HEADS_PER_GROUP):
        hs = slice(j * dh, (j + 1) * dh)
        parts.append(w0[:, j:j + 1] * o0[:, hs] + w1[:, j:j + 1] * o1[:, hs]
                     + w2[:, j:j + 1] * o2[:, hs])
    ha = jnp.concatenate(parts, axis=1).astype(BF16)
    ym = jnp.dot(hm_ref[...], wbm_ref[...], preferred_element_type=F32)
    ya = jnp.dot(ha, wba_ref[...], preferred_element_type=F32)
    merged = (_sigmoid(gm_ref[...].astype(F32)) * ym + _sigmoid(ga_ref[...].astype(F32)) * ya)
    x1 = x_ref[...] + jnp.dot(merged.astype(BF16), wo_ref[...], preferred_element_type=F32)
    x1_ref[...] = x1
    h2 = x1 * lax.rsqrt(jnp.mean(x1 * x1, axis=-1, keepdims=True) + RMS_EPS) * ng_ref[...]
    _store_token_tiles(h2_ref, _pack_pairs(h2))
    h_hi, h_lo = _bf16_parts(h2, 2)
    logits = jnp.dot(jnp.concatenate([h_hi, h_lo, h_hi], axis=1), rw_ref[...],
                     preferred_element_type=F32) + rb_ref[...]
    lane = lax.broadcasted_iota(jnp.int32, logits.shape, 1)
    idx_all = jnp.zeros(logits.shape, jnp.int32)
    val_all = jnp.zeros(logits.shape, F32)
    top0 = None
    esum = None
    picks = []
    chosen = jnp.zeros(logits.shape, F32)
    for k in range(TOP_K):
        m = jnp.max(logits, axis=1, keepdims=True)
        sel = jnp.min(jnp.where(logits == m, lane, LANES), axis=1, keepdims=True)
        if k == 0:
            top0 = m
        e = jnp.exp(m - top0)
        esum = e if k == 0 else esum + e
        idx_all = jnp.where(lane == k, sel, idx_all)
        val_all = jnp.where(lane == k, e, val_all)
        pick = lane == sel
        picks.append(pick)
        chosen = chosen + pick.astype(F32)
        logits = jnp.where(pick, -jnp.inf, logits)
    idx_ref[...] = idx_all
    gate_ref[...] = val_all / esum

    @pl.when(pl.program_id(0) == 0)
    def _():
        cnt_ref[...] = jnp.zeros_like(cnt_ref)

    trow = lax.broadcasted_iota(jnp.int32, (bm, bm), 0)
    tcol = lax.broadcasted_iota(jnp.int32, (bm, bm), 1)
    earlier = jnp.dot((tcol < trow).astype(BF16), chosen.astype(BF16),
                      preferred_element_type=F32) + cnt_ref[...]
    rank_all = jnp.zeros(logits.shape, jnp.int32)
    for k in range(TOP_K):
        r = jnp.sum(jnp.where(picks[k], earlier, 0.0), axis=1, keepdims=True)
        rank_all = jnp.where(lane == k, r.astype(jnp.int32), rank_all)
    rank_ref[...] = rank_all
    cnt_ref[...] = cnt_ref[...] + jnp.sum(chosen, axis=0, keepdims=True)
    count_ref[...] = jnp.broadcast_to(cnt_ref[...], count_ref.shape)


def _merge(hm, outs, lses, proj, x2, wbm, wba, wo, ng, rw, rb, *, bm=256):
    n = x2.shape[0]
    d = D_MODEL
    gcol = COL_GATES // d
    n_blk = outs[0].shape[2] // bm

    def rows(width):
        return pl.BlockSpec((bm, width), lambda i: (i, 0))

    def full(a, b):
        return pl.BlockSpec((a, b), lambda i: (0, 0), pipeline_mode=pl.Buffered(1))

    def residue(arr):
        dilation, width = arr.shape[1], arr.shape[3]
        return pl.BlockSpec((1, dilation, bm // dilation, width),
                            lambda i: (i // n_blk, 0, i % n_blk, 0))

    return pl.pallas_call(
        _merge_kernel,
        grid=(n // bm,),
        in_specs=[
            rows(MLSTM_WIDTH),
            residue(outs[0]), residue(outs[1]), residue(outs[2]),
            residue(lses[0]), residue(lses[1]), residue(lses[2]),
            pl.BlockSpec((bm, d), lambda i: (i, gcol)),
            pl.BlockSpec((bm, d), lambda i: (i, gcol + 1)),
            rows(d),
            full(MLSTM_WIDTH, d), full(GROUP_WIDTH, d), full(d, d),
            full(1, d), full(3 * d, LANES), full(1, LANES),
        ],
        out_specs=[rows(d), pl.BlockSpec((bm * PACK_ROWS, LANES), lambda i: (i, 0)),
                   rows(LANES), rows(LANES), rows(LANES),
                   pl.BlockSpec((8, LANES), lambda i: (0, 0))],
        out_shape=[
            jax.ShapeDtypeStruct((n, d), F32),
            jax.ShapeDtypeStruct((n * PACK_ROWS, LANES), jnp.uint32),
            jax.ShapeDtypeStruct((n, LANES), jnp.int32),
            jax.ShapeDtypeStruct((n, LANES), F32),
            jax.ShapeDtypeStruct((n, LANES), jnp.int32),
            jax.ShapeDtypeStruct((8, LANES), F32),
        ],
        scratch_shapes=[pltpu.VMEM((1, LANES), F32)],
        compiler_params=_params(("arbitrary",)),
        name="merge",
    )(hm, outs[0], outs[1], outs[2], lses[0], lses[1], lses[2], proj, proj, x2,
      wbm, wba, wo, ng, rw, rb)


PACK_ROWS = 8


def _pack_pairs(x):
    w = x.shape[1] // 2
    lo = lax.bitcast_convert_type(x[:, :w].astype(BF16).astype(F32), jnp.uint32) >> 16
    hi = lax.bitcast_convert_type(x[:, w:].astype(BF16).astype(F32), jnp.uint32)
    return (hi & jnp.uint32(0xFFFF0000)) | lo


def _unpack_pairs(words):
    lo = lax.bitcast_convert_type(words << 16, F32)
    hi = lax.bitcast_convert_type(words & jnp.uint32(0xFFFF0000), F32)
    return lo, hi


def _store_token_tiles(ref, words):
    rows = words.shape[0]
    for s in range(PACK_ROWS):
        ref[pl.ds(s, rows, stride=PACK_ROWS), :] = words[:, s * LANES:(s + 1) * LANES]


def _dispatch_kernel(fs_ref, fl_ref, nu_ref, dest_ref, hp_ref, xb_hbm, zero_ref, sem, zsem, *,
                     tokens, tm, n_blk):
    pr = PACK_ROWS
    fill_sizes = [1 << b for b in reversed(range((tm - 1).bit_length()))]

    def fill_copies(e):
        off = fs_ref[e]
        for p in fill_sizes:
            take = (fl_ref[e] & p) != 0
            dst = xb_hbm.at[pl.ds(pl.multiple_of(off * pr, pr), p * pr), :]
            yield take, pltpu.make_async_copy(zero_ref.at[pl.ds(0, p * pr), :], dst, zsem)
            off = off + jnp.where(take, p, 0)

    def block_copy(b):
        dst = xb_hbm.at[pl.ds(pl.multiple_of(b * (tm * pr), tm * pr), tm * pr), :]
        return pltpu.make_async_copy(zero_ref, dst, zsem)

    @pl.when(pl.program_id(0) == 0)
    def _():
        zero_ref[...] = jnp.zeros_like(zero_ref)
        for start in (True, False):
            def per_expert(e, carry):
                for take, cp in fill_copies(e):
                    @pl.when(take)
                    def _():
                        cp.start() if start else cp.wait()
                return carry

            def per_block(b, carry):
                block_copy(b).start() if start else block_copy(b).wait()
                return carry

            lax.fori_loop(0, N_EXPERTS, per_expert, 0)
            lax.fori_loop(nu_ref[0], n_blk, per_block, 0)

    def issue(t, carry):
        src = hp_ref.at[pl.ds(pl.multiple_of(t * pr, pr), pr), :]
        for k in range(TOP_K):
            slot = dest_ref[t * TOP_K + k]
            pltpu.make_async_copy(src, xb_hbm.at[pl.ds(pl.multiple_of(slot * pr, pr), pr), :],
                                  sem).start(priority=k % 2)
        return carry

    lax.fori_loop(0, tokens, issue, 0)
    for k in range(TOP_K):
        pltpu.make_async_copy(hp_ref, xb_hbm.at[pl.ds(0, tokens * pr), :], sem).wait()


def _dispatch(fill_start, fill_len, n_used, dest, h2p, n_pad, *, tokens=256, tm=MOE_TILE):
    pr = PACK_ROWS
    n_tok = h2p.shape[0] // pr
    kern = functools.partial(_dispatch_kernel, tokens=tokens, tm=tm, n_blk=n_pad // tm)
    return pl.pallas_call(
        kern,
        grid_spec=pltpu.PrefetchScalarGridSpec(
            num_scalar_prefetch=3,
            grid=(n_tok // tokens,),
            in_specs=[
                pl.BlockSpec((tokens * TOP_K,), lambda i, fs, fl, nu: (i,),
                             memory_space=pltpu.SMEM),
                pl.BlockSpec((tokens * pr, LANES), lambda i, fs, fl, nu: (i, 0)),
            ],
            out_specs=pl.BlockSpec(memory_space=pl.ANY),
            scratch_shapes=[pltpu.VMEM((tm * pr, LANES), jnp.uint32),
                            pltpu.SemaphoreType.DMA(()), pltpu.SemaphoreType.DMA(())],
        ),
        out_shape=jax.ShapeDtypeStruct((n_pad * pr, LANES), jnp.uint32),
        compiler_params=_params(("arbitrary",)),
        name="dispatch",
    )(fill_start, fill_len, n_used, dest, h2p)


def _cast_rows(src_ref, dst_ref, rows):
    total = src_ref.shape[0]

    def body(i, carry):
        r = pl.multiple_of(i * rows, rows)
        dst_ref[pl.ds(r, rows), :] = src_ref[pl.ds(r, rows), :].astype(BF16)
        return carry

    lax.fori_loop(0, total // rows, body, 0)


def _stream_expert_weights(be_ref, nu_ref, ne_ref, copies, convert):
    k, j = pl.program_id(0), pl.program_id(1)
    expert = be_ref[j]
    used = j < nu_ref[0]
    run_start = jnp.logical_or(j == 0, expert != be_ref[jnp.maximum(j - 1, 0)])

    @pl.when(jnp.logical_and(k == 0, j == 0))
    def _():
        for cp in copies(expert, k):
            cp.start()

    @pl.when(jnp.logical_and(used, run_start))
    def _():
        for cp in copies(expert, k):
            cp.wait()
        convert()
        last_run = ne_ref[j] < 0
        nxt_e = jnp.where(last_run, be_ref[0], ne_ref[j])
        nxt_k = jnp.where(last_run, k + 1, k)

        @pl.when(nxt_k < pl.num_programs(0))
        def _():
            for cp in copies(nxt_e, nxt_k):
                cp.start()

    return used


def _per_block_rows(used, bv_ref, o_ref, compute, store=None, rows_per_token=1):
    tm = o_ref.shape[0] // rows_per_token
    need = (bv_ref[pl.program_id(1)] + (MOE_ROW_STEP - 1)) // MOE_ROW_STEP

    def run(rows):
        res = compute(rows)
        head = o_ref.at[pl.ds(0, rows * rows_per_token), :]
        if store is None:
            head[...] = res
        else:
            store(head, res)
        if rows < tm:
            tail = o_ref.at[pl.ds(rows * rows_per_token, (tm - rows) * rows_per_token), :]
            tail[...] = jnp.zeros_like(tail)

    for q in range(1, tm // MOE_ROW_STEP + 1):
        @pl.when(jnp.logical_and(used, need == q))
        def _():
            run(q * MOE_ROW_STEP)

    @pl.when(jnp.logical_not(used))
    def _():
        o_ref[...] = jnp.zeros_like(o_ref)


def _gate_up_kernel(be_ref, nu_ref, ne_ref, bv_ref, x_ref, w_hbm, bg_ref, bu_ref, o_ref,
                    stage_ref, w_bf, sem):
    tf = o_ref.shape[1]
    kt = pl.num_programs(0)

    def copies(expert, k):
        return [pltpu.make_async_copy(
            w_hbm.at[expert, :, pl.ds(pl.multiple_of((half * kt + k) * tf, tf), tf)],
            stage_ref.at[half], sem.at[half]) for half in range(2)]

    def convert():
        for half in range(2):
            _cast_rows(stage_ref.at[half], w_bf.at[half], 256)

    used = _stream_expert_weights(be_ref, nu_ref, ne_ref, copies, convert)

    def compute(rows):
        halves = [_unpack_pairs(x_ref[pl.ds(s, rows, stride=PACK_ROWS), :])
                  for s in range(PACK_ROWS)]
        x = jnp.concatenate([h[0].astype(BF16) for h in halves]
                            + [h[1].astype(BF16) for h in halves], axis=1)
        gate = jnp.dot(x, w_bf[0], preferred_element_type=F32) + bg_ref[0]
        up = jnp.dot(x, w_bf[1], preferred_element_type=F32) + bu_ref[0]
        gate = jnp.minimum(gate, SWIGLU_LIMIT)
        up = jnp.clip(up, -SWIGLU_LIMIT, SWIGLU_LIMIT)
        glu = gate * _sigmoid(SWIGLU_ALPHA * gate)
        return ((up + 1.0) * glu).astype(BF16)

    _per_block_rows(used, bv_ref, o_ref, compute)


def _used_block(j, nu):
    return jnp.minimum(j, nu[0] - 1)


def _gate_up(sched, xb, w_gate_up, b_gate_up3, *, tm=MOE_TILE, tf=1024):
    n_pad, d = xb.shape[0] // PACK_ROWS, D_MODEL
    kt = D_FF // tf

    def bspec(off):
        return pl.BlockSpec((1, 1, tf),
                            lambda k, j, be, nu, ne, bv: (be[_used_block(j, nu)], 0, off + k))

    return pl.pallas_call(
        _gate_up_kernel,
        grid_spec=pltpu.PrefetchScalarGridSpec(
            num_scalar_prefetch=4,
            grid=(kt, n_pad // tm),
            in_specs=[
                pl.BlockSpec((tm * PACK_ROWS, LANES),
                             lambda k, j, be, nu, ne, bv: (_used_block(j, nu), 0)),
                pl.BlockSpec(memory_space=pl.ANY),
                bspec(0), bspec(kt),
            ],
            out_specs=pl.BlockSpec((tm, tf), lambda k, j, be, nu, ne, bv: (j, k)),
            scratch_shapes=[pltpu.VMEM((2, d, tf), F32), pltpu.VMEM((2, d, tf), BF16),
                            pltpu.SemaphoreType.DMA((2,))],
        ),
        out_shape=jax.ShapeDtypeStruct((n_pad, D_FF), BF16),
        compiler_params=_params(("arbitrary", "arbitrary")),
        name="gate_up",
    )(*sched, xb, w_gate_up, b_gate_up3, b_gate_up3)


def _down_kernel(be_ref, nu_ref, ne_ref, bv_ref, h_ref, w_hbm, bd_ref, o_ref, stage_ref, w_bf,
                 sem):
    def copies(expert, k):
        del k
        return [pltpu.make_async_copy(w_hbm.at[expert], stage_ref, sem.at[0])]

    def convert():
        _cast_rows(stage_ref, w_bf, 256)

    used = _stream_expert_weights(be_ref, nu_ref, ne_ref, copies, convert)

    def compute(rows):
        y = jnp.dot(h_ref[0:rows, :], w_bf[...], preferred_element_type=F32) + bd_ref[0]
        return _pack_pairs(y)

    _per_block_rows(used, bv_ref, o_ref, compute, store=_store_token_tiles,
                    rows_per_token=PACK_ROWS)


def _down(sched, hb, w_down, b_down3, *, tm=MOE_TILE):
    n_pad, f = hb.shape
    d = D_MODEL
    return pl.pallas_call(
        _down_kernel,
        grid_spec=pltpu.PrefetchScalarGridSpec(
            num_scalar_prefetch=4,
            grid=(1, n_pad // tm),
            in_specs=[
                pl.BlockSpec((tm, f), lambda k, j, be, nu, ne, bv: (_used_block(j, nu), 0)),
                pl.BlockSpec(memory_space=pl.ANY),
                pl.BlockSpec((1, 1, d),
                             lambda k, j, be, nu, ne, bv: (be[_used_block(j, nu)], 0, 0)),
            ],
            out_specs=pl.BlockSpec((tm * PACK_ROWS, LANES), lambda k, j, be, nu, ne, bv: (j, 0)),
            scratch_shapes=[pltpu.VMEM((f, d), F32), pltpu.VMEM((f, d), BF16),
                            pltpu.SemaphoreType.DMA((1,))],
        ),
        out_shape=jax.ShapeDtypeStruct((n_pad * PACK_ROWS, LANES), jnp.uint32),
        compiler_params=_params(("arbitrary", "arbitrary")),
        name="down",
    )(*sched, hb, w_down, b_down3)


def _combine_kernel(dest_ref, dest_next_ref, y_hbm, gate_ref, x1_ref, ng_ref, o_ref, buf_ref, sem,
                    *, rows):
    pr = PACK_ROWS
    slab = 64
    half = x1_ref.shape[1] // 2
    i = pl.program_id(0)
    slot_rows = TOP_K * rows * pr
    cur = i % 2
    base = pl.multiple_of(cur * slot_rows, slot_rows)
    nxt_base = pl.multiple_of((1 - cur) * slot_rows, slot_rows)

    def gather(idx_ref, dst_base, buf, t):
        for k in range(TOP_K):
            slot = idx_ref[t * TOP_K + k]
            pltpu.make_async_copy(
                y_hbm.at[pl.ds(pl.multiple_of(slot * pr, pr), pr), :],
                buf_ref.at[pl.ds(pl.multiple_of(dst_base + (k * rows + t) * pr, pr), pr), :],
                sem.at[buf]).start(priority=k % 2)

    @pl.when(i == 0)
    def _():
        def first(t, carry):
            gather(dest_ref, 0, 0, t)
            return carry

        lax.fori_loop(0, rows, first, 0)

    pltpu.make_async_copy(y_hbm.at[pl.ds(0, slot_rows), :],
                          buf_ref.at[pl.ds(base, slot_rows), :], sem.at[cur]).wait()

    def reduce_slab(t0):
        gates = [gate_ref[pl.ds(t0, slab), k:k + 1] for k in range(TOP_K)]
        lows, highs = [], []
        for s in range(pr):
            lo = x1_ref[pl.ds(t0, slab), s * LANES:(s + 1) * LANES]
            hi = x1_ref[pl.ds(t0, slab), half + s * LANES:half + (s + 1) * LANES]
            for k in range(TOP_K):
                y_lo, y_hi = _unpack_pairs(
                    buf_ref[pl.ds(base + (k * rows + t0) * pr + s, slab, stride=pr), :])
                lo = lo + gates[k] * y_lo
                hi = hi + gates[k] * y_hi
            lows.append(lo)
            highs.append(hi)
        acc = jnp.concatenate(lows + highs, axis=1)
        o_ref[pl.ds(t0, slab), :] = (
            acc * lax.rsqrt(jnp.mean(acc * acc, axis=-1, keepdims=True) + RMS_EPS) * ng_ref[...])

    def trip_with_prefetch(j, carry):
        t0 = pl.multiple_of(j * slab, slab)
        for tt in range(slab):
            gather(dest_next_ref, nxt_base, 1 - cur, t0 + tt)
        reduce_slab(t0)
        return carry

    def trip(j, carry):
        reduce_slab(pl.multiple_of(j * slab, slab))
        return carry

    has_next = i + 1 < pl.num_programs(0)

    @pl.when(has_next)
    def _():
        lax.fori_loop(0, rows // slab, trip_with_prefetch, 0)

    @pl.when(jnp.logical_not(has_next))
    def _():
        lax.fori_loop(0, rows // slab, trip, 0)


def _combine(dest, yb, gates, x1, ng, *, rows=256):
    n, d = x1.shape
    kern = functools.partial(_combine_kernel, rows=rows)
    n_steps = n // rows
    return pl.pallas_call(
        kern,
        grid=(n_steps,),
        in_specs=[
            pl.BlockSpec((rows * TOP_K,), lambda i: (i,), memory_space=pltpu.SMEM),
            pl.BlockSpec((rows * TOP_K,), lambda i: (jnp.minimum(i + 1, n_steps - 1),),
                         memory_space=pltpu.SMEM),
            pl.BlockSpec(memory_space=pl.ANY),
            pl.BlockSpec((rows, LANES), lambda i: (i, 0)),
            pl.BlockSpec((rows, d), lambda i: (i, 0)),
            pl.BlockSpec((1, d), lambda i: (0, 0)),
        ],
        out_specs=pl.BlockSpec((rows, d), lambda i: (i, 0)),
        out_shape=jax.ShapeDtypeStruct((n, d), F32),
        scratch_shapes=[pltpu.VMEM((2 * TOP_K * rows * PACK_ROWS, LANES), jnp.uint32),
                        pltpu.SemaphoreType.DMA((2,))],
        compiler_params=_params(("arbitrary",)),
        name="combine",
    )(dest, dest, yb, gates, x1, ng)


def _routing(top_idx, rank, counts, tm):
    n_tok = top_idx.shape[0]
    n_asg = n_tok * TOP_K
    e_flat = top_idx.reshape(n_asg)
    padded = (counts + tm - 1) // tm * tm
    pend = jnp.cumsum(padded)
    pstart = pend - padded
    experts = jnp.arange(N_EXPERTS, dtype=jnp.int32)
    start_of = jnp.sum(jnp.where(e_flat[:, None] == experts[None, :], pstart[None, :], 0), axis=1)
    dest = (start_of + rank.reshape(n_asg)).astype(jnp.int32)
    n_blk = -(-(n_asg + N_EXPERTS * (tm - 1)) // tm)
    n_pad = n_blk * tm
    fill_start = (pstart + counts).astype(jnp.int32)
    fill_len = (padded - counts).astype(jnp.int32)
    block_start = jnp.arange(n_blk, dtype=jnp.int32) * tm
    block_e = jnp.minimum(
        jnp.sum((pend[None, :] <= block_start[:, None]).astype(jnp.int32), axis=1),
        N_EXPERTS - 1).astype(jnp.int32)
    n_used = (pend[-1:] // tm).astype(jnp.int32)
    run_end = pend[block_e] // tm
    next_e = jnp.where(run_end < n_used[0], block_e[jnp.minimum(run_end, n_blk - 1)], -1)
    block_rows = jnp.clip(fill_start[block_e] - block_start, 0, tm)
    sched = (block_e, n_used, next_e.astype(jnp.int32), block_rows.astype(jnp.int32))
    return dest, fill_start, fill_len, sched, n_pad


def _layer(x, norm_mix_g, w_in, conv_w, conv_b, igate_b, fgate_b, mlstm_norm_g, rel_bias,
           w_branch_mlstm, w_branch_attn, w_out, norm_moe_g, router_w, router_b,
           w_gate_up, b_gate_up, w_down, b_down, out_norm_g):
    bsz, s, d = x.shape
    n = bsz * s
    x2 = x.reshape(n, d)
    w_main, w_if, w_qkv = _split_w_in(jnp.swapaxes(w_in, 0, 1))
    bpad = jnp.zeros((LANES - MLSTM_HEADS,), F32)
    gate_b = jnp.concatenate([igate_b, bpad, fgate_b, bpad]).reshape(1, IF_WIDTH)

    norm_g = norm_mix_g.reshape(1, d)
    proj, ifg = _in_proj(x2, norm_g, w_main, w_if)
    proj3 = proj.reshape(bsz, s, D_MAIN)
    hm, qkvs = _mixer_inputs(x2, norm_g, w_qkv, proj3, ifg.reshape(bsz, s, IF_WIDTH), conv_w,
                             conv_b.reshape(1, -1), gate_b, mlstm_norm_g.reshape(1, -1))
    hm = hm.reshape(n, MLSTM_WIDTH)

    outs, lses = [], []
    for g, (_, dilation) in enumerate(ATTN_GROUPS):
        o_g_, lse_g = _attn_group(qkvs[g], _attn_bias(rel_bias, g, dilation), g)
        outs.append(o_g_)
        lses.append(lse_g)

    rw = jnp.concatenate([router_w, jnp.zeros((d, LANES - N_EXPERTS), F32)], axis=1)
    rw_hi = rw.astype(BF16)
    rw_lo = (rw - rw_hi.astype(F32)).astype(BF16)
    rw = jnp.concatenate([rw_hi, rw_hi, rw_lo], axis=0)
    rb = jnp.concatenate([router_b, jnp.full((LANES - N_EXPERTS,), NEG, F32)]).reshape(1, LANES)
    x1, h2, idx, gates, rank, counts = _merge(
        hm, outs, lses, proj, x2, w_branch_mlstm.astype(BF16), w_branch_attn.astype(BF16),
        w_out.astype(BF16), norm_moe_g.reshape(1, d), rw, rb)

    dest, fill_start, fill_len, sched, n_pad = _routing(
        idx[:, :TOP_K], rank[:, :TOP_K], counts[0, :N_EXPERTS].astype(jnp.int32), MOE_TILE)
    xb = _dispatch(fill_start, fill_len, sched[1], dest, h2, n_pad)
    hb = _gate_up(sched, xb, w_gate_up, b_gate_up.reshape(N_EXPERTS, 1, 2 * D_FF))
    yb = _down(sched, hb, w_down, b_down.reshape(N_EXPERTS, 1, d))
    out = _combine(dest, yb, gates, x1, out_norm_g.reshape(1, d))
    return out.reshape(bsz, s, d)


def kernel(x, norm_mix_g, w_in, conv_w, conv_b, igate_b, fgate_b, mlstm_norm_g, rel_bias,
           w_branch_mlstm, w_branch_attn, w_out, norm_moe_g, router_w, router_b,
           w_gate_up, b_gate_up, w_down, b_down, norm_final_g):
    assert w_in.shape[0] == 1, "single-layer block"
    return _layer(x, norm_mix_g[0], w_in[0], conv_w[0], conv_b[0], igate_b[0], fgate_b[0],
                  mlstm_norm_g[0], rel_bias, w_branch_mlstm[0], w_branch_attn[0], w_out[0],
                  norm_moe_g[0], router_w[0], router_b[0], w_gate_up[0], b_gate_up[0],
                  w_down[0], b_down[0], norm_final_g)
```

```python
import functools
import itertools
import math

import jax
import jax.numpy as jnp
from jax import lax
from jax.experimental import pallas as pl
from jax.experimental.pallas import tpu as pltpu

F32 = jnp.float32
BF16 = jnp.bfloat16

D_MODEL = 2048
MLSTM_HEADS = 8
HEAD_DIM = 128
MLSTM_WIDTH = MLSTM_HEADS * HEAD_DIM
CONV_WIDTH = 4
ATTN_GROUPS = ((128, 1), (512, 4), (2048, 16))
N_GROUPS = 3
HEADS_PER_GROUP = 4
ATTN_HEADS = HEADS_PER_GROUP * N_GROUPS
ATTN_WIDTH = ATTN_HEADS * HEAD_DIM
GROUP_WIDTH = HEADS_PER_GROUP * HEAD_DIM
ATTN_BLOCK = 128
ATTN_SPAN = 128
REL_BUCKETS = 32
REL_MAX_DIST = 2048
N_EXPERTS = 32
TOP_K = 4
D_FF = D_MODEL
SWIGLU_LIMIT = 7.0
SWIGLU_ALPHA = 1.702
RMS_EPS = 1e-6
NEG = -1e30

COL_QK = 0
COL_V = 2 * MLSTM_WIDTH
COL_O = COL_V + MLSTM_WIDTH
COL_GATES = COL_O + MLSTM_WIDTH
D_MAIN = COL_GATES + 2 * D_MODEL
QKV_WIDTH = 3 * GROUP_WIDTH
LANES = 128
IF_WIDTH = 2 * LANES

MLSTM_CHUNK = 128
MOE_TILE = 512
MOE_ROW_STEP = 256
VMEM_LIMIT = 56 * 1024 * 1024


def _sigmoid(x):
    return 0.5 * jnp.tanh(0.5 * x) + 0.5


def _bf16_parts(x, n_parts):
    parts = []
    for _ in range(n_parts):
        p = x.astype(BF16)
        parts.append(p)
        x = x - p.astype(F32)
    return parts


def _log_sigmoid(x):
    return -(jnp.maximum(-x, 0.0) + jnp.log1p(jnp.exp(-jnp.abs(x))))


def _params(sem):
    return pltpu.CompilerParams(dimension_semantics=sem, vmem_limit_bytes=VMEM_LIMIT)


def _in_proj_kernel(x_ref, g_ref, w_ref, wif_ref, o_ref, oif_ref, h_ref, *, bm, rows):
    @pl.when(pl.program_id(1) == 0)
    def _():
        for r in range(0, bm, rows):
            x = x_ref[r:r + rows, :]
            ms = jnp.mean(x * x, axis=-1, keepdims=True)
            h = (x * lax.rsqrt(ms + RMS_EPS) * g_ref[...]).astype(BF16)
            h_ref[r:r + rows, :] = h
            oif_ref[r:r + rows, :] = jnp.dot(h, wif_ref[...], preferred_element_type=F32)

    o_ref[...] = jnp.dot(h_ref[...], w_ref[...], preferred_element_type=F32).astype(BF16)


def _residue_perm(size, dilation, inverse):
    per = size // dilation
    i = lax.broadcasted_iota(jnp.int32, (size, size), 0)
    j = lax.broadcasted_iota(jnp.int32, (size, size), 1)
    if inverse:
        src = (i & (dilation - 1)) * per + (i >> (dilation.bit_length() - 1))
    else:
        src = (i & (per - 1)) * dilation + (i >> (per.bit_length() - 1))
    return j == src


def _attn_proj_pieces(h, w_ref, o_ref, width=256):
    dilation = o_ref.shape[1]
    per = o_ref.shape[2]
    if dilation > 1:
        perm = _residue_perm(h.shape[0], dilation, False).astype(BF16)
        h = jnp.dot(perm, h, preferred_element_type=F32).astype(BF16)
        yield
    for c in range(0, o_ref.shape[3], width):
        res = jnp.dot(h, w_ref[:, c:c + width], preferred_element_type=F32).astype(BF16)
        for r in range(dilation):
            o_ref[0, r, :, c:c + width] = res[r * per:(r + 1) * per, :]
        yield


_O_I = 4 * MLSTM_WIDTH
_O_QA = _O_I + 2 * MLSTM_HEADS
_O_GATES = _O_QA + 3 * ATTN_WIDTH
D_IN = _O_GATES + 2 * D_MODEL


SPLIT_CHUNK = 512


def _split_w_in_kernel(off_ref, wt_hbm, main_ref, if_ref, q0_ref, q1_ref, q2_ref, buf_ref, sem):
    c = pl.program_id(0)
    n_main = D_MAIN // SPLIT_CHUNK
    gw = GROUP_WIDTH

    def fetch(step):
        slot = step % 2
        rows = pl.ds(pl.multiple_of(off_ref[step], 8), SPLIT_CHUNK)
        return pltpu.make_async_copy(wt_hbm.at[rows, :], buf_ref.at[slot], sem.at[slot])

    @pl.when(c == 0)
    def _():
        fetch(c).start()

    @pl.when(c + 1 < pl.num_programs(0))
    def _():
        fetch(c + 1).start()

    fetch(c).wait()
    piece = buf_ref[c % 2].T

    @pl.when(c < n_main)
    def _():
        main_ref[...] = piece.astype(BF16)

    for g, q_ref in enumerate((q0_ref, q1_ref, q2_ref)):
        for part in range(3):
            @pl.when(c == n_main + 3 * g + part)
            def _():
                q_ref[:, part * gw:(part + 1) * gw] = piece.astype(BF16)

    @pl.when(c == n_main + 3 * N_GROUPS)
    def _():
        window = piece[:, :LANES]
        lane = lax.broadcasted_iota(jnp.int32, window.shape, 1)
        i_part = jnp.where(lane < MLSTM_HEADS, window, 0.0)
        f_part = jnp.where(lane < MLSTM_HEADS,
                           pltpu.roll(window, LANES - MLSTM_HEADS, axis=1), 0.0)
        if_ref[...] = jnp.concatenate([i_part, f_part], axis=1).astype(BF16)


def _split_w_in(w_in_t):
    d = w_in_t.shape[1]
    assert w_in_t.shape[0] == D_IN
    ch = SPLIT_CHUNK
    n_main = D_MAIN // ch
    offsets = ([k * ch for k in range(_O_I // ch)]
               + [_O_GATES + k * ch for k in range(2 * D_MODEL // ch)]
               + [_O_QA + part * ATTN_WIDTH + g * GROUP_WIDTH
                  for g in range(N_GROUPS) for part in range(3)]
               + [_O_I])
    assert len(offsets) == n_main + 3 * N_GROUPS + 1 and GROUP_WIDTH == ch

    def whole(width):
        return pl.BlockSpec((d, width), lambda i, off: (0, 0))

    outs = pl.pallas_call(
        _split_w_in_kernel,
        grid_spec=pltpu.PrefetchScalarGridSpec(
            num_scalar_prefetch=1,
            grid=(len(offsets),),
            in_specs=[pl.BlockSpec(memory_space=pl.ANY)],
            out_specs=[pl.BlockSpec((d, ch), lambda i, off: (0, jnp.minimum(i, n_main - 1))),
                       whole(IF_WIDTH), whole(QKV_WIDTH), whole(QKV_WIDTH), whole(QKV_WIDTH)],
            scratch_shapes=[pltpu.VMEM((2, ch, d), F32), pltpu.SemaphoreType.DMA((2,))],
        ),
        out_shape=[jax.ShapeDtypeStruct((d, width), BF16)
                   for width in (D_MAIN, IF_WIDTH, QKV_WIDTH, QKV_WIDTH, QKV_WIDTH)],
        compiler_params=_params(("arbitrary",)),
        name="split_w_in",
    )(jnp.asarray(offsets, jnp.int32), w_in_t)
    return outs[0], outs[1], outs[2:]


def _in_proj(x2, g, w_main, w_if, *, bm=1024, bn=1024):
    n = x2.shape[0]
    bm = min(bm, n)
    kern = functools.partial(_in_proj_kernel, bm=bm, rows=256)
    return pl.pallas_call(
        kern,
        grid=(n // bm, D_MAIN // bn),
        in_specs=[
            pl.BlockSpec((bm, D_MODEL), lambda i, j: (i, 0)),
            pl.BlockSpec((1, D_MODEL), lambda i, j: (0, 0)),
            pl.BlockSpec((D_MODEL, bn), lambda i, j: (0, j)),
            pl.BlockSpec((D_MODEL, IF_WIDTH), lambda i, j: (0, 0)),
        ],
        out_specs=[
            pl.BlockSpec((bm, bn), lambda i, j: (i, j)),
            pl.BlockSpec((bm, IF_WIDTH), lambda i, j: (i, 0)),
        ],
        out_shape=[
            jax.ShapeDtypeStruct((n, D_MAIN), BF16),
            jax.ShapeDtypeStruct((n, IF_WIDTH), F32),
        ],
        scratch_shapes=[pltpu.VMEM((bm, D_MODEL), BF16)],
        compiler_params=_params(("arbitrary", "arbitrary")),
        name="in_proj",
    )(x2, g, w_main, w_if)


def _mlstm_chunk(rows, qk_ref, v_ref, og_ref, if_ref, cw_ref, cb_ref, gb_ref, ng_ref, out_ref,
                 tail_ref, c_ref, m_ref, other_work):
    L = rows.stop - rows.start
    dh = HEAD_DIM

    tail = tail_ref.shape[0]
    cur = qk_ref[0, rows, :]
    ext = jnp.concatenate([tail_ref[...], cur], axis=0)
    tail_ref[...] = cur[L - tail:, :]
    srow = lax.broadcasted_iota(jnp.int32, (L, tail + L), 0)
    scol = lax.broadcasted_iota(jnp.int32, (L, tail + L), 1)
    acc = cur.astype(F32) * cw_ref[CONV_WIDTH - 1:CONV_WIDTH, :] + cb_ref[...]
    for s in range(1, CONV_WIDTH):
        shift = (scol == srow + (tail - s)).astype(BF16)
        acc = acc + (jnp.dot(shift, ext, preferred_element_type=F32)
                     * cw_ref[CONV_WIDTH - 1 - s:CONV_WIDTH - s, :])
    qk = acc * _sigmoid(acc)
    q_all = qk[:, :MLSTM_WIDTH]
    k_all = qk[:, MLSTM_WIDTH:] * (dh ** -0.5)

    head_lane = lax.broadcasted_iota(jnp.int32, (L, LANES), 1) < MLSTM_HEADS
    gi = jnp.where(head_lane, if_ref[0, rows, :LANES] + gb_ref[:, :LANES], 0.0)
    lf = jnp.where(head_lane, _log_sigmoid(if_ref[0, rows, LANES:] + gb_ref[:, LANES:]), 0.0)
    row = lax.broadcasted_iota(jnp.int32, (L, L), 0)
    col = lax.broadcasted_iota(jnp.int32, (L, L), 1)
    causal = col <= row
    cum = jnp.dot(causal.astype(F32), lf, preferred_element_type=F32,
                  precision=lax.Precision.HIGHEST)
    rmat = gi - cum
    cmax = rmat
    trow = lax.broadcasted_iota(jnp.int32, (L, LANES), 0)
    step = 1
    while step < L:
        cmax = jnp.maximum(cmax, jnp.where(trow >= step, pltpu.roll(cmax, step, axis=0), -jnp.inf))
        step *= 2
    m_prev = m_ref[...]
    gmax = jnp.maximum(m_prev, cmax)
    s_inter = jnp.exp(m_prev - gmax)
    inv_scale = jnp.exp(-(cum + gmax))
    b_last = cum[L - 1:L, :]
    m_new = b_last + gmax[L - 1:L, :]
    w_state = jnp.exp(b_last + rmat - m_new)
    decay = jnp.exp(b_last + m_prev - m_new)
    m_ref[...] = m_new
    rmat_t = rmat.T
    ones_blk = jnp.ones((L, dh), BF16)

    def column(mat, h):
        return jnp.broadcast_to(mat[:, h:h + 1], (L, dh))

    for h in range(MLSTM_HEADS):
        hs = slice(h * dh, (h + 1) * dh)
        p = jnp.exp(jnp.where(causal, rmat_t[h:h + 1, :] - column(gmax, h), NEG))
        qh = q_all[:, hs]
        kh = k_all[:, hs]
        s = lax.dot_general(qh.astype(BF16), kh.astype(BF16), (((1,), (1,)), ((), ())),
                            preferred_element_type=F32)
        vaug = jnp.concatenate([v_ref[0, rows, hs], ones_blk], axis=1)
        c_prev = c_ref[h]
        lhs = jnp.concatenate([(p * s).astype(BF16), (column(s_inter, h) * qh).astype(BF16)],
                              axis=1)
        rhs = jnp.concatenate([vaug, c_prev.astype(BF16)], axis=0)
        nd = jnp.dot(lhs, rhs, preferred_element_type=F32)
        hh = nd[:, :dh] / jnp.maximum(jnp.abs(nd[:, dh:]), column(inv_scale, h))

        kw = (column(w_state, h) * kh).astype(BF16)
        kv = lax.dot_general(kw, vaug, (((0,), (0,)), ((), ())), preferred_element_type=F32)
        c_ref[h] = decay[:, h:h + 1] * c_prev + kv

        y = hh * lax.rsqrt(jnp.mean(hh * hh, axis=-1, keepdims=True) + RMS_EPS) * ng_ref[:, hs]
        y = y * _sigmoid(og_ref[0, rows, hs].astype(F32))
        out_ref[0, rows, hs] = y.astype(BF16)
        next(other_work, None)


def _mixer_kernel(x_ref, xg_ref, w0_ref, w1_ref, w2_ref,
                  qk_ref, v_ref, og_ref, if_ref, cw_ref, cb_ref, gb_ref, ng_ref,
                  hm_ref, a0_ref, a1_ref, a2_ref, tail_ref, c_ref, m_ref, *, chunk):
    @pl.when(pl.program_id(1) == 0)
    def _():
        tail_ref[...] = jnp.zeros_like(tail_ref)
        c_ref[...] = jnp.zeros_like(c_ref)
        m_ref[...] = jnp.zeros_like(m_ref)

    x = x_ref[...]
    h = (x * lax.rsqrt(jnp.mean(x * x, axis=-1, keepdims=True) + RMS_EPS) * xg_ref[...]).astype(BF16)
    pieces = itertools.chain(_attn_proj_pieces(h, w0_ref, a0_ref), _attn_proj_pieces(h, w1_ref, a1_ref),
                             _attn_proj_pieces(h, w2_ref, a2_ref))
    n_pieces = 3 * (QKV_WIDTH // 256) + 2
    n_heads = (x_ref.shape[0] // chunk) * MLSTM_HEADS
    def some_pieces():
        done = 0
        for head in range(1, n_heads + 1):
            while done * n_heads < head * n_pieces:
                next(pieces, None)
                done += 1
            yield

    other_work = some_pieces()
    for r0 in range(0, x_ref.shape[0], chunk):
        _mlstm_chunk(slice(r0, r0 + chunk), qk_ref, v_ref, og_ref, if_ref, cw_ref, cb_ref, gb_ref,
                     ng_ref, hm_ref, tail_ref, c_ref, m_ref, other_work)
    for _ in pieces:
        pass


def _mixer_inputs(x2, norm_g, w_qkv, proj3, ifg3, conv_w, conv_b, gate_b, mnorm_g, *,
                  rows=256, chunk=MLSTM_CHUNK):
    bsz, s, _ = proj3.shape
    w = MLSTM_WIDTH
    n_blk = s // rows
    dilations = [d for _, d in ATTN_GROUPS]

    def const(shape):
        return pl.BlockSpec(shape, lambda b, i: (0,) * len(shape))

    def weight():
        return pl.BlockSpec((D_MODEL, QKV_WIDTH), lambda b, i: (0, 0), pipeline_mode=pl.Buffered(1))

    kern = functools.partial(_mixer_kernel, chunk=chunk)
    outs = pl.pallas_call(
        kern,
        grid=(bsz, n_blk),
        in_specs=[
            pl.BlockSpec((rows, D_MODEL), lambda b, i: (b * n_blk + i, 0)),
            const((1, D_MODEL)),
            weight(), weight(), weight(),
            pl.BlockSpec((1, rows, 2 * w), lambda b, i: (b, i, COL_QK // (2 * w))),
            pl.BlockSpec((1, rows, w), lambda b, i: (b, i, COL_V // w)),
            pl.BlockSpec((1, rows, w), lambda b, i: (b, i, COL_O // w)),
            pl.BlockSpec((1, rows, IF_WIDTH), lambda b, i: (b, i, 0)),
            const((CONV_WIDTH, 2 * w)), const((1, 2 * w)), const((1, IF_WIDTH)), const((1, w)),
        ],
        out_specs=[pl.BlockSpec((1, rows, w), lambda b, i: (b, i, 0))] + [
            pl.BlockSpec((1, d, rows // d, QKV_WIDTH), lambda b, i: (b, 0, i, 0)) for d in dilations],
        out_shape=[jax.ShapeDtypeStruct((bsz, s, w), BF16)] + [
            jax.ShapeDtypeStruct((bsz, d, s // d, QKV_WIDTH), BF16) for d in dilations],
        scratch_shapes=[
            pltpu.VMEM((16, 2 * w), BF16),
            pltpu.VMEM((MLSTM_HEADS, HEAD_DIM, 2 * HEAD_DIM), F32),
            pltpu.VMEM((1, LANES), F32),
        ],
        compiler_params=_params(("arbitrary", "arbitrary")),
        name="mixer_inputs",
    )(x2, norm_g, *w_qkv, proj3, proj3, proj3, ifg3, conv_w, conv_b, gate_b, mnorm_g)
    return outs[0], outs[1:]


def _attn_kernel(q_ref, kp_ref, kc_ref, vp_ref, vc_ref, bias_ref, o_ref, lse_ref):
    qb = ATTN_BLOCK
    dh = HEAD_DIM
    n_q = q_ref.shape[2] // qb
    first = pl.program_id(2) == 0
    kcol = lax.broadcasted_iota(jnp.int32, (qb, 2 * qb), 1)
    dead = jnp.logical_and(first, kcol < qb)
    lane = lax.broadcasted_iota(jnp.int32, (qb, LANES), 1)
    for jq in range(n_q):
        rows = slice(jq * qb, (jq + 1) * qb)
        lse_all = jnp.zeros((qb, LANES), F32)
        for j in range(HEADS_PER_GROUP):
            hs = slice(j * dh, (j + 1) * dh)
            q = q_ref[0, 0][rows, hs]
            if jq == 0:
                k = jnp.concatenate([kp_ref[0, 0][:, hs], kc_ref[0, 0][0:qb, hs]], axis=0)
                v = jnp.concatenate([vp_ref[0, 0][:, hs], vc_ref[0, 0][0:qb, hs]], axis=0)
            else:
                k = kc_ref[0, 0][(jq - 1) * qb:(jq + 1) * qb, hs]
                v = vc_ref[0, 0][(jq - 1) * qb:(jq + 1) * qb, hs]
            s = lax.dot_general(q, k, (((1,), (1,)), ((), ())), preferred_element_type=F32)
            s = s * (dh ** -0.5) + bias_ref[j]
            if jq == 0:
                s = jnp.where(dead, NEG, s)
            mx = jnp.max(s, axis=1, keepdims=True)
            p = jnp.exp(s - mx)
            den = jnp.sum(p, axis=1, keepdims=True)
            o = jnp.dot(p.astype(BF16), v, preferred_element_type=F32) / den
            o_ref[0, 0, rows, hs] = o.astype(BF16)
            lse_all = jnp.where(lane == j, mx + jnp.log(den), lse_all)
        lse_ref[0, 0, rows, :] = lse_all


def _attn_group(qkv, bias, g, *, q_blocks=2):
    bsz, dilation, n, _ = qkv.shape
    gw = GROUP_WIDTH
    q_blocks = min(q_blocks, n // ATTN_BLOCK)
    run = q_blocks * ATTN_BLOCK

    def own(part, width=gw):
        return pl.BlockSpec((1, 1, run, width), lambda b, r, i: (b, r, i, part))

    def previous(part):
        return pl.BlockSpec((1, 1, ATTN_BLOCK, gw),
                            lambda b, r, i: (b, r, jnp.maximum(i * q_blocks - 1, 0), part))

    return pl.pallas_call(
        _attn_kernel,
        grid=(bsz, dilation, n // run),
        in_specs=[
            own(0),
            previous(1), own(1),
            previous(2), own(2),
            pl.BlockSpec((HEADS_PER_GROUP, ATTN_BLOCK, 2 * ATTN_BLOCK), lambda b, r, i: (0, 0, 0)),
        ],
        out_specs=[own(0), own(0, LANES)],
        out_shape=[
            jax.ShapeDtypeStruct((bsz, dilation, n, gw), BF16),
            jax.ShapeDtypeStruct((bsz, dilation, n, LANES), F32),
        ],
        compiler_params=_params(("arbitrary", "arbitrary", "arbitrary")),
        name=f"attn_g{g}",
    )(qkv, qkv, qkv, qkv, qkv, bias)


def _t5_bucket(dist):
    max_exact = REL_BUCKETS // 2
    d_f = jnp.maximum(dist, 1).astype(F32)
    large = max_exact + (jnp.log(d_f / max_exact) / math.log(REL_MAX_DIST / max_exact)
                         * (REL_BUCKETS - max_exact)).astype(jnp.int32)
    large = jnp.minimum(large, REL_BUCKETS - 1)
    return jnp.where(dist < max_exact, dist, large)


def _attn_bias(rel_bias, g, dilation):
    span = ATTN_SPAN
    buckets = _t5_bucket(jnp.arange(span + 1, dtype=jnp.int32) * dilation)
    vec = rel_bias[buckets][:, g * HEADS_PER_GROUP:(g + 1) * HEADS_PER_GROUP].T.astype(F32)
    qpos = jnp.arange(ATTN_BLOCK)[:, None]
    kpos = jnp.arange(2 * ATTN_BLOCK)[None, :]
    dist = qpos + ATTN_BLOCK - kpos
    valid = (dist >= 0) & (dist <= span)
    onehot = (dist[:, :, None] == jnp.arange(span + 1)[None, None, :]).astype(F32)
    table = jnp.einsum('qkj,hj->hqk', onehot, vec, precision=lax.Precision.HIGHEST)
    return jnp.where(valid[None], table, NEG)


def _merge_kernel(hm_ref, o0_ref, o1_ref, o2_ref, l0_ref, l1_ref, l2_ref, gm_ref, ga_ref, x_ref,
                  wbm_ref, wba_ref, wo_ref, ng_ref, rw_ref, rb_ref,
                  x1_ref, h2_ref, idx_ref, gate_ref, rank_ref, count_ref, cnt_ref):
    dh = HEAD_DIM
    bm = x_ref.shape[0]

    def token_order(ref):
        dilation = ref.shape[1]
        if dilation == 1:
            return ref[0, 0]
        perm = _residue_perm(bm, dilation, True).astype(BF16)
        blk = ref[0].reshape(bm, ref.shape[3])
        parts = [blk] if blk.dtype == BF16 else _bf16_parts(blk, 3)
        out = None
        for p in parts:
            moved = jnp.dot(perm, p, preferred_element_type=F32)
            out = moved if out is None else out + moved
        return out

    l0, l1, l2 = token_order(l0_ref), token_order(l1_ref), token_order(l2_ref)
    o0, o1, o2 = token_order(o0_ref), token_order(o1_ref), token_order(o2_ref)
    mx = jnp.maximum(jnp.maximum(l0, l1), l2)
    e0, e1, e2 = jnp.exp(l0 - mx), jnp.exp(l1 - mx), jnp.exp(l2 - mx)
    den = e0 + e1 + e2
    w0, w1, w2 = e0 / den, e1 / den, e2 / den
    parts = []
    for j in range(HEADS_PER_GROUP):
        hs = slice(j * dh, (j + 1) * dh)
        parts.append(w0[:, j:j + 1] * o0[:, hs] + w1[:, j:j + 1] * o1[:, hs]
                     + w2[:, j:j + 1] * o2[:, hs])
    ha = jnp.concatenate(parts, axis=1).astype(BF16)
    ym = jnp.dot(hm_ref[...], wbm_ref[...], preferred_element_type=F32)
    ya = jnp.dot(ha, wba_ref[...], preferred_element_type=F32)
    merged = (_sigmoid(gm_ref[...].astype(F32)) * ym + _sigmoid(ga_ref[...].astype(F32)) * ya)
    x1 = x_ref[...] + jnp.dot(merged.astype(BF16), wo_ref[...], preferred_element_type=F32)
    x1_ref[...] = x1
    h2 = x1 * lax.rsqrt(jnp.mean(x1 * x1, axis=-1, keepdims=True) + RMS_EPS) * ng_ref[...]
    _store_token_tiles(h2_ref, _pack_pairs(h2))
    h_hi, h_lo = _bf16_parts(h2, 2)
    logits = jnp.dot(jnp.concatenate([h_hi, h_lo, h_hi], axis=1), rw_ref[...],
                     preferred_element_type=F32) + rb_ref[...]
    lane = lax.broadcasted_iota(jnp.int32, logits.shape, 1)
    idx_all = jnp.zeros(logits.shape, jnp.int32)
    val_all = jnp.zeros(logits.shape, F32)
    top0 = None
    esum = None
    picks = []
    chosen = jnp.zeros(logits.shape, F32)
    for k in range(TOP_K):
        m = jnp.max(logits, axis=1, keepdims=True)
        sel = jnp.min(jnp.where(logits == m, lane, LANES), axis=1, keepdims=True)
        if k == 0:
            top0 = m
        e = jnp.exp(m - top0)
        esum = e if k == 0 else esum + e
        idx_all = jnp.where(lane == k, sel, idx_all)
        val_all = jnp.where(lane == k, e, val_all)
        pick = lane == sel
        picks.append(pick)
        chosen = chosen + pick.astype(F32)
        logits = jnp.where(pick, -jnp.inf, logits)
    idx_ref[...] = idx_all
    gate_ref[...] = val_all / esum

    @pl.when(pl.program_id(0) == 0)
    def _():
        cnt_ref[...] = jnp.zeros_like(cnt_ref)

    trow = lax.broadcasted_iota(jnp.int32, (bm, bm), 0)
    tcol = lax.broadcasted_iota(jnp.int32, (bm, bm), 1)
    earlier = jnp.dot((tcol < trow).astype(BF16), chosen.astype(BF16),
                      preferred_element_type=F32) + cnt_ref[...]
    rank_all = jnp.zeros(logits.shape, jnp.int32)
    for k in range(TOP_K):
        r = jnp.sum(jnp.where(picks[k], earlier, 0.0), axis=1, keepdims=True)
        rank_all = jnp.where(lane == k, r.astype(jnp.int32), rank_all)
    rank_ref[...] = rank_all
    cnt_ref[...] = cnt_ref[...] + jnp.sum(chosen, axis=0, keepdims=True)
    count_ref[...] = jnp.broadcast_to(cnt_ref[...], count_ref.shape)


def _merge(hm, outs, lses, proj, x2, wbm, wba, wo, ng, rw, rb, *, bm=256):
    n = x2.shape[0]
    d = D_MODEL
    gcol = COL_GATES // d
    n_blk = outs[0].shape[2] // bm

    def rows(width):
        return pl.BlockSpec((bm, width), lambda i: (i, 0))

    def full(a, b):
        return pl.BlockSpec((a, b), lambda i: (0, 0), pipeline_mode=pl.Buffered(1))

    def residue(arr):
        dilation, width = arr.shape[1], arr.shape[3]
        return pl.BlockSpec((1, dilation, bm // dilation, width),
                            lambda i: (i // n_blk, 0, i % n_blk, 0))

    return pl.pallas_call(
        _merge_kernel,
        grid=(n // bm,),
        in_specs=[
            rows(MLSTM_WIDTH),
            residue(outs[0]), residue(outs[1]), residue(outs[2]),
            residue(lses[0]), residue(lses[1]), residue(lses[2]),
            pl.BlockSpec((bm, d), lambda i: (i, gcol)),
            pl.BlockSpec((bm, d), lambda i: (i, gcol + 1)),
            rows(d),
            full(MLSTM_WIDTH, d), full(GROUP_WIDTH, d), full(d, d),
            full(1, d), full(3 * d, LANES), full(1, LANES),
        ],
        out_specs=[rows(d), pl.BlockSpec((bm * PACK_ROWS, LANES), lambda i: (i, 0)),
                   rows(LANES), rows(LANES), rows(LANES),
                   pl.BlockSpec((8, LANES), lambda i: (0, 0))],
        out_shape=[
            jax.ShapeDtypeStruct((n, d), F32),
            jax.ShapeDtypeStruct((n * PACK_ROWS, LANES), jnp.uint32),
            jax.ShapeDtypeStruct((n, LANES), jnp.int32),
            jax.ShapeDtypeStruct((n, LANES), F32),
            jax.ShapeDtypeStruct((n, LANES), jnp.int32),
            jax.ShapeDtypeStruct((8, LANES), F32),
        ],
        scratch_shapes=[pltpu.VMEM((1, LANES), F32)],
        compiler_params=_params(("arbitrary",)),
        name="merge",
    )(hm, outs[0], outs[1], outs[2], lses[0], lses[1], lses[2], proj, proj, x2,
      wbm, wba, wo, ng, rw, rb)


PACK_ROWS = 8


def _pack_pairs(x):
    w = x.shape[1] // 2
    lo = lax.bitcast_convert_type(x[:, :w].astype(BF16).astype(F32), jnp.uint32) >> 16
    hi = lax.bitcast_convert_type(x[:, w:].astype(BF16).astype(F32), jnp.uint32)
    return (hi & jnp.uint32(0xFFFF0000)) | lo


def _unpack_pairs(words):
    lo = lax.bitcast_convert_type(words << 16, F32)
    hi = lax.bitcast_convert_type(words & jnp.uint32(0xFFFF0000), F32)
    return lo, hi


def _store_token_tiles(ref, words):
    rows = words.shape[0]
    for s in range(PACK_ROWS):
        ref[pl.ds(s, rows, stride=PACK_ROWS), :] = words[:, s * LANES:(s + 1) * LANES]


def _dispatch_kernel(fs_ref, fl_ref, nu_ref, dest_ref, hp_ref, xb_hbm, zero_ref, sem, zsem, *,
                     tokens, tm, n_blk):
    pr = PACK_ROWS
    fill_sizes = [1 << b for b in reversed(range((tm - 1).bit_length()))]

    def fill_copies(e):
        off = fs_ref[e]
        for p in fill_sizes:
            take = (fl_ref[e] & p) != 0
            dst = xb_hbm.at[pl.ds(pl.multiple_of(off * pr, pr), p * pr), :]
            yield take, pltpu.make_async_copy(zero_ref.at[pl.ds(0, p * pr), :], dst, zsem)
            off = off + jnp.where(take, p, 0)

    def block_copy(b):
        dst = xb_hbm.at[pl.ds(pl.multiple_of(b * (tm * pr), tm * pr), tm * pr), :]
        return pltpu.make_async_copy(zero_ref, dst, zsem)

    @pl.when(pl.program_id(0) == 0)
    def _():
        zero_ref[...] = jnp.zeros_like(zero_ref)
        for start in (True, False):
            def per_expert(e, carry):
                for take, cp in fill_copies(e):
                    @pl.when(take)
                    def _():
                        cp.start() if start else cp.wait()
                return carry

            def per_block(b, carry):
                block_copy(b).start() if start else block_copy(b).wait()
                return carry

            lax.fori_loop(0, N_EXPERTS, per_expert, 0)
            lax.fori_loop(nu_ref[0], n_blk, per_block, 0)

    def issue(t, carry):
        src = hp_ref.at[pl.ds(pl.multiple_of(t * pr, pr), pr), :]
        for k in range(TOP_K):
            slot = dest_ref[t * TOP_K + k]
            pltpu.make_async_copy(src, xb_hbm.at[pl.ds(pl.multiple_of(slot * pr, pr), pr), :],
                                  sem).start(priority=k % 2)
        return carry

    lax.fori_loop(0, tokens, issue, 0)
    for k in range(TOP_K):
        pltpu.make_async_copy(hp_ref, xb_hbm.at[pl.ds(0, tokens * pr), :], sem).wait()


def _dispatch(fill_start, fill_len, n_used, dest, h2p, n_pad, *, tokens=256, tm=MOE_TILE):
    pr = PACK_ROWS
    n_tok = h2p.shape[0] // pr
    kern = functools.partial(_dispatch_kernel, tokens=tokens, tm=tm, n_blk=n_pad // tm)
    return pl.pallas_call(
        kern,
        grid_spec=pltpu.PrefetchScalarGridSpec(
            num_scalar_prefetch=3,
            grid=(n_tok // tokens,),
            in_specs=[
                pl.BlockSpec((tokens * TOP_K,), lambda i, fs, fl, nu: (i,),
                             memory_space=pltpu.SMEM),
                pl.BlockSpec((tokens * pr, LANES), lambda i, fs, fl, nu: (i, 0)),
            ],
            out_specs=pl.BlockSpec(memory_space=pl.ANY),
            scratch_shapes=[pltpu.VMEM((tm * pr, LANES), jnp.uint32),
                            pltpu.SemaphoreType.DMA(()), pltpu.SemaphoreType.DMA(())],
        ),
        out_shape=jax.ShapeDtypeStruct((n_pad * pr, LANES), jnp.uint32),
        compiler_params=_params(("arbitrary",)),
        name="dispatch",
    )(fill_start, fill_len, n_used, dest, h2p)


def _cast_rows(src_ref, dst_ref, rows):
    total = src_ref.shape[0]

    def body(i, carry):
        r = pl.multiple_of(i * rows, rows)
        dst_ref[pl.ds(r, rows), :] = src_ref[pl.ds(r, rows), :].astype(BF16)
        return carry

    lax.fori_loop(0, total // rows, body, 0)


def _stream_expert_weights(be_ref, nu_ref, ne_ref, copies, convert):
    k, j = pl.program_id(0), pl.program_id(1)
    expert = be_ref[j]
    used = j < nu_ref[0]
    run_start = jnp.logical_or(j == 0, expert != be_ref[jnp.maximum(j - 1, 0)])

    @pl.when(jnp.logical_and(k == 0, j == 0))
    def _():
        for cp in copies(expert, k):
            cp.start()

    @pl.when(jnp.logical_and(used, run_start))
    def _():
        for cp in copies(expert, k):
            cp.wait()
        convert()
        last_run = ne_ref[j] < 0
        nxt_e = jnp.where(last_run, be_ref[0], ne_ref[j])
        nxt_k = jnp.where(last_run, k + 1, k)

        @pl.when(nxt_k < pl.num_programs(0))
        def _():
            for cp in copies(nxt_e, nxt_k):
                cp.start()

    return used


def _per_block_rows(used, bv_ref, o_ref, compute, store=None, rows_per_token=1):
    tm = o_ref.shape[0] // rows_per_token
    need = (bv_ref[pl.program_id(1)] + (MOE_ROW_STEP - 1)) // MOE_ROW_STEP

    def run(rows):
        res = compute(rows)
        head = o_ref.at[pl.ds(0, rows * rows_per_token), :]
        if store is None:
            head[...] = res
        else:
            store(head, res)
        if rows < tm:
            tail = o_ref.at[pl.ds(rows * rows_per_token, (tm - rows) * rows_per_token), :]
            tail[...] = jnp.zeros_like(tail)

    for q in range(1, tm // MOE_ROW_STEP + 1):
        @pl.when(jnp.logical_and(used, need == q))
        def _():
            run(q * MOE_ROW_STEP)

    @pl.when(jnp.logical_not(used))
    def _():
        o_ref[...] = jnp.zeros_like(o_ref)


def _gate_up_kernel(be_ref, nu_ref, ne_ref, bv_ref, x_ref, w_hbm, bg_ref, bu_ref, o_ref,
                    stage_ref, w_bf, sem):
    tf = o_ref.shape[1]
    kt = pl.num_programs(0)

    def copies(expert, k):
        return [pltpu.make_async_copy(
            w_hbm.at[expert, :, pl.ds(pl.multiple_of((half * kt + k) * tf, tf), tf)],
            stage_ref.at[half], sem.at[half]) for half in range(2)]

    def convert():
        for half in range(2):
            _cast_rows(stage_ref.at[half], w_bf.at[half], 256)

    used = _stream_expert_weights(be_ref, nu_ref, ne_ref, copies, convert)

    def compute(rows):
        halves = [_unpack_pairs(x_ref[pl.ds(s, rows, stride=PACK_ROWS), :])
                  for s in range(PACK_ROWS)]
        x = jnp.concatenate([h[0].astype(BF16) for h in halves]
                            + [h[1].astype(BF16) for h in halves], axis=1)
        gate = jnp.dot(x, w_bf[0], preferred_element_type=F32) + bg_ref[0]
        up = jnp.dot(x, w_bf[1], preferred_element_type=F32) + bu_ref[0]
        gate = jnp.minimum(gate, SWIGLU_LIMIT)
        up = jnp.clip(up, -SWIGLU_LIMIT, SWIGLU_LIMIT)
        glu = gate * _sigmoid(SWIGLU_ALPHA * gate)
        return ((up + 1.0) * glu).astype(BF16)

    _per_block_rows(used, bv_ref, o_ref, compute)


def _used_block(j, nu):
    return jnp.minimum(j, nu[0] - 1)


def _gate_up(sched, xb, w_gate_up, b_gate_up3, *, tm=MOE_TILE, tf=1024):
    n_pad, d = xb.shape[0] // PACK_ROWS, D_MODEL
    kt = D_FF // tf

    def bspec(off):
        return pl.BlockSpec((1, 1, tf),
                            lambda k, j, be, nu, ne, bv: (be[_used_block(j, nu)], 0, off + k))

    return pl.pallas_call(
        _gate_up_kernel,
        grid_spec=pltpu.PrefetchScalarGridSpec(
            num_scalar_prefetch=4,
            grid=(kt, n_pad // tm),
            in_specs=[
                pl.BlockSpec((tm * PACK_ROWS, LANES),
                             lambda k, j, be, nu, ne, bv: (_used_block(j, nu), 0)),
                pl.BlockSpec(memory_space=pl.ANY),
                bspec(0), bspec(kt),
            ],
            out_specs=pl.BlockSpec((tm, tf), lambda k, j, be, nu, ne, bv: (j, k)),
            scratch_shapes=[pltpu.VMEM((2, d, tf), F32), pltpu.VMEM((2, d, tf), BF16),
                            pltpu.SemaphoreType.DMA((2,))],
        ),
        out_shape=jax.ShapeDtypeStruct((n_pad, D_FF), BF16),
        compiler_params=_params(("arbitrary", "arbitrary")),
        name="gate_up",
    )(*sched, xb, w_gate_up, b_gate_up3, b_gate_up3)


def _down_kernel(be_ref, nu_ref, ne_ref, bv_ref, h_ref, w_hbm, bd_ref, o_ref, stage_ref, w_bf,
                 sem):
    def copies(expert, k):
        del k
        return [pltpu.make_async_copy(w_hbm.at[expert], stage_ref, sem.at[0])]

    def convert():
        _cast_rows(stage_ref, w_bf, 256)

    used = _stream_expert_weights(be_ref, nu_ref, ne_ref, copies, convert)

    def compute(rows):
        y = jnp.dot(h_ref[0:rows, :], w_bf[...], preferred_element_type=F32) + bd_ref[0]
        return _pack_pairs(y)

    _per_block_rows(used, bv_ref, o_ref, compute, store=_store_token_tiles,
                    rows_per_token=PACK_ROWS)


def _down(sched, hb, w_down, b_down3, *, tm=MOE_TILE):
    n_pad, f = hb.shape
    d = D_MODEL
    return pl.pallas_call(
        _down_kernel,
        grid_spec=pltpu.PrefetchScalarGridSpec(
            num_scalar_prefetch=4,
            grid=(1, n_pad // tm),
            in_specs=[
                pl.BlockSpec((tm, f), lambda k, j, be, nu, ne, bv: (_used_block(j, nu), 0)),
                pl.BlockSpec(memory_space=pl.ANY),
                pl.BlockSpec((1, 1, d),
                             lambda k, j, be, nu, ne, bv: (be[_used_block(j, nu)], 0, 0)),
            ],
            out_specs=pl.BlockSpec((tm * PACK_ROWS, LANES), lambda k, j, be, nu, ne, bv: (j, 0)),
            scratch_shapes=[pltpu.VMEM((f, d), F32), pltpu.VMEM((f, d), BF16),
                            pltpu.SemaphoreType.DMA((1,))],
        ),
        out_shape=jax.ShapeDtypeStruct((n_pad * PACK_ROWS, LANES), jnp.uint32),
        compiler_params=_params(("arbitrary", "arbitrary")),
        name="down",
    )(*sched, hb, w_down, b_down3)


def _combine_kernel(dest_ref, dest_next_ref, y_hbm, gate_ref, x1_ref, ng_ref, o_ref, buf_ref, sem,
                    *, rows):
    pr = PACK_ROWS
    slab = 64
    half = x1_ref.shape[1] // 2
    i = pl.program_id(0)
    slot_rows = TOP_K * rows * pr
    cur = i % 2
    base = pl.multiple_of(cur * slot_rows, slot_rows)
    nxt_base = pl.multiple_of((1 - cur) * slot_rows, slot_rows)

    def gather(idx_ref, dst_base, buf, t):
        for k in range(TOP_K):
            slot = idx_ref[t * TOP_K + k]
            pltpu.make_async_copy(
                y_hbm.at[pl.ds(pl.multiple_of(slot * pr, pr), pr), :],
                buf_ref.at[pl.ds(pl.multiple_of(dst_base + (k * rows + t) * pr, pr), pr), :],
                sem.at[buf]).start(priority=k % 2)

    @pl.when(i == 0)
    def _():
        def first(t, carry):
            gather(dest_ref, 0, 0, t)
            return carry

        lax.fori_loop(0, rows, first, 0)

    pltpu.make_async_copy(y_hbm.at[pl.ds(0, slot_rows), :],
                          buf_ref.at[pl.ds(base, slot_rows), :], sem.at[cur]).wait()

    def reduce_slab(t0):
        gates = [gate_ref[pl.ds(t0, slab), k:k + 1] for k in range(TOP_K)]
        lows, highs = [], []
        for s in range(pr):
            lo = x1_ref[pl.ds(t0, slab), s * LANES:(s + 1) * LANES]
            hi = x1_ref[pl.ds(t0, slab), half + s * LANES:half + (s + 1) * LANES]
            for k in range(TOP_K):
                y_lo, y_hi = _unpack_pairs(
                    buf_ref[pl.ds(base + (k * rows + t0) * pr + s, slab, stride=pr), :])
                lo = lo + gates[k] * y_lo
                hi = hi + gates[k] * y_hi
            lows.append(lo)
            highs.append(hi)
        acc = jnp.concatenate(lows + highs, axis=1)
        o_ref[pl.ds(t0, slab), :] = (
            acc * lax.rsqrt(jnp.mean(acc * acc, axis=-1, keepdims=True) + RMS_EPS) * ng_ref[...])

    def trip_with_prefetch(j, carry):
        t0 = pl.multiple_of(j * slab, slab)
        for tt in range(slab):
            gather(dest_next_ref, nxt_base, 1 - cur, t0 + tt)
        reduce_slab(t0)
        return carry

    def trip(j, carry):
        reduce_slab(pl.multiple_of(j * slab, slab))
        return carry

    has_next = i + 1 < pl.num_programs(0)

    @pl.when(has_next)
    def _():
        lax.fori_loop(0, rows // slab, trip_with_prefetch, 0)

    @pl.when(jnp.logical_not(has_next))
    def _():
        lax.fori_loop(0, rows // slab, trip, 0)


def _combine(dest, yb, gates, x1, ng, *, rows=256):
    n, d = x1.shape
    kern = functools.partial(_combine_kernel, rows=rows)
    n_steps = n // rows
    return pl.pallas_call(
        kern,
        grid=(n_steps,),
        in_specs=[
            pl.BlockSpec((rows * TOP_K,), lambda i: (i,), memory_space=pltpu.SMEM),
            pl.BlockSpec((rows * TOP_K,), lambda i: (jnp.minimum(i + 1, n_steps - 1),),
                         memory_space=pltpu.SMEM),
            pl.BlockSpec(memory_space=pl.ANY),
            pl.BlockSpec((rows, LANES), lambda i: (i, 0)),
            pl.BlockSpec((rows, d), lambda i: (i, 0)),
            pl.BlockSpec((1, d), lambda i: (0, 0)),
        ],
        out_specs=pl.BlockSpec((rows, d), lambda i: (i, 0)),
        out_shape=jax.ShapeDtypeStruct((n, d), F32),
        scratch_shapes=[pltpu.VMEM((2 * TOP_K * rows * PACK_ROWS, LANES), jnp.uint32),
                        pltpu.SemaphoreType.DMA((2,))],
        compiler_params=_params(("arbitrary",)),
        name="combine",
    )(dest, dest, yb, gates, x1, ng)


def _routing(top_idx, rank, counts, tm):
    n_tok = top_idx.shape[0]
    n_asg = n_tok * TOP_K
    e_flat = top_idx.reshape(n_asg)
    padded = (counts + tm - 1) // tm * tm
    pend = jnp.cumsum(padded)
    pstart = pend - padded
    experts = jnp.arange(N_EXPERTS, dtype=jnp.int32)
    start_of = jnp.sum(jnp.where(e_flat[:, None] == experts[None, :], pstart[None, :], 0), axis=1)
    dest = (start_of + rank.reshape(n_asg)).astype(jnp.int32)
    n_blk = -(-(n_asg + N_EXPERTS * (tm - 1)) // tm)
    n_pad = n_blk * tm
    fill_start = (pstart + counts).astype(jnp.int32)
    fill_len = (padded - counts).astype(jnp.int32)
    block_start = jnp.arange(n_blk, dtype=jnp.int32) * tm
    block_e = jnp.minimum(
        jnp.sum((pend[None, :] <= block_start[:, None]).astype(jnp.int32), axis=1),
        N_EXPERTS - 1).astype(jnp.int32)
    n_used = (pend[-1:] // tm).astype(jnp.int32)
    run_end = pend[block_e] // tm
    next_e = jnp.where(run_end < n_used[0], block_e[jnp.minimum(run_end, n_blk - 1)], -1)
    block_rows = jnp.clip(fill_start[block_e] - block_start, 0, tm)
    sched = (block_e, n_used, next_e.astype(jnp.int32), block_rows.astype(jnp.int32))
    return dest, fill_start, fill_len, sched, n_pad


def _layer(x, norm_mix_g, w_in, conv_w, conv_b, igate_b, fgate_b, mlstm_norm_g, rel_bias,
           w_branch_mlstm, w_branch_attn, w_out, norm_moe_g, router_w, router_b,
           w_gate_up, b_gate_up, w_down, b_down, out_norm_g):
    bsz, s, d = x.shape
    n = bsz * s
    x2 = x.reshape(n, d)
    w_main, w_if, w_qkv = _split_w_in(jnp.swapaxes(w_in, 0, 1))
    bpad = jnp.zeros((LANES - MLSTM_HEADS,), F32)
    gate_b = jnp.concatenate([igate_b, bpad, fgate_b, bpad]).reshape(1, IF_WIDTH)

    norm_g = norm_mix_g.reshape(1, d)
    proj, ifg = _in_proj(x2, norm_g, w_main, w_if)
    proj3 = proj.reshape(bsz, s, D_MAIN)
    hm, qkvs = _mixer_inputs(x2, norm_g, w_qkv, proj3, ifg.reshape(bsz, s, IF_WIDTH), conv_w,
                             conv_b.reshape(1, -1), gate_b, mlstm_norm_g.reshape(1, -1))
    hm = hm.reshape(n, MLSTM_WIDTH)

    outs, lses = [], []
    for g, (_, dilation) in enumerate(ATTN_GROUPS):
        o_g_, lse_g = _attn_group(qkvs[g], _attn_bias(rel_bias, g, dilation), g)
        outs.append(o_g_)
        lses.append(lse_g)

    rw = jnp.concatenate([router_w, jnp.zeros((d, LANES - N_EXPERTS), F32)], axis=1)
    rw_hi = rw.astype(BF16)
    rw_lo = (rw - rw_hi.astype(F32)).astype(BF16)
    rw = jnp.concatenate([rw_hi, rw_hi, rw_lo], axis=0)
    rb = jnp.concatenate([router_b, jnp.full((LANES - N_EXPERTS,), NEG, F32)]).reshape(1, LANES)
    x1, h2, idx, gates, rank, counts = _merge(
        hm, outs, lses, proj, x2, w_branch_mlstm.astype(BF16), w_branch_attn.astype(BF16),
        w_out.astype(BF16), norm_moe_g.reshape(1, d), rw, rb)

    dest, fill_start, fill_len, sched, n_pad = _routing(
        idx[:, :TOP_K], rank[:, :TOP_K], counts[0, :N_EXPERTS].astype(jnp.int32), MOE_TILE)
    xb = _dispatch(fill_start, fill_len, sched[1], dest, h2, n_pad)
    hb = _gate_up(sched, xb, w_gate_up, b_gate_up.reshape(N_EXPERTS, 1, 2 * D_FF))
    yb = _down(sched, hb, w_down, b_down.reshape(N_EXPERTS, 1, d))
    out = _combine(dest, yb, gates, x1, out_norm_g.reshape(1, d))
    return out.reshape(bsz, s, d)


def kernel(x, norm_mix_g, w_in, conv_w, conv_b, igate_b, fgate_b, mlstm_norm_g, rel_bias,
           w_branch_mlstm, w_branch_attn, w_out, norm_moe_g, router_w, router_b,
           w_gate_up, b_gate_up, w_down, b_down, norm_final_g):
    assert w_in.shape[0] == 1, "single-layer block"
    return _layer(x, norm_mix_g[0], w_in[0], conv_w[0], conv_b[0], igate_b[0], fgate_b[0],
                  mlstm_norm_g[0], rel_bias, w_branch_mlstm[0], w_branch_attn[0], w_out[0],
                  norm_moe_g[0], router_w[0], router_b[0], w_gate_up[0], b_gate_up[0],
                  w_down[0], b_down[0], norm_final_g)
```

```python
import functools
import itertools
import math

import jax
import jax.numpy as jnp
from jax import lax
from jax.experimental import pallas as pl
from jax.experimental.pallas import tpu as pltpu

F32 = jnp.float32
BF16 = jnp.bfloat16

D_MODEL = 2048
MLSTM_HEADS = 8
HEAD_DIM = 128
MLSTM_WIDTH = MLSTM_HEADS * HEAD_DIM
CONV_WIDTH = 4
ATTN_GROUPS = ((128, 1), (512, 4), (2048, 16))
N_GROUPS = 3
HEADS_PER_GROUP = 4
ATTN_HEADS = HEADS_PER_GROUP * N_GROUPS
ATTN_WIDTH = ATTN_HEADS * HEAD_DIM
GROUP_WIDTH = HEADS_PER_GROUP * HEAD_DIM
ATTN_BLOCK = 128
ATTN_SPAN = 128
REL_BUCKETS = 32
REL_MAX_DIST = 2048
N_EXPERTS = 32
TOP_K = 4
D_FF = D_MODEL
SWIGLU_LIMIT = 7.0
SWIGLU_ALPHA = 1.702
RMS_EPS = 1e-6
NEG = -1e30

COL_QK = 0
COL_V = 2 * MLSTM_WIDTH
COL_O = COL_V + MLSTM_WIDTH
COL_GATES = COL_O + MLSTM_WIDTH
D_MAIN = COL_GATES + 2 * D_MODEL
QKV_WIDTH = 3 * GROUP_WIDTH
LANES = 128
IF_WIDTH = 2 * LANES

MLSTM_CHUNK = 128
MOE_TILE = 512
MOE_ROW_STEP = 128
VMEM_LIMIT = 56 * 1024 * 1024


def _sigmoid(x):
    return 0.5 * jnp.tanh(0.5 * x) + 0.5


def _bf16_parts(x, n_parts):
    parts = []
    for _ in range(n_parts):
        p = x.astype(BF16)
        parts.append(p)
        x = x - p.astype(F32)
    return parts


def _log_sigmoid(x):
    return -(jnp.maximum(-x, 0.0) + jnp.log1p(jnp.exp(-jnp.abs(x))))


def _params(sem):
    return pltpu.CompilerParams(dimension_semantics=sem, vmem_limit_bytes=VMEM_LIMIT)


def _in_proj_kernel(x_ref, g_ref, w_ref, wif_ref, o_ref, oif_ref, h_ref, *, bm, rows):
    @pl.when(pl.program_id(1) == 0)
    def _():
        for r in range(0, bm, rows):
            x = x_ref[r:r + rows, :]
            ms = jnp.mean(x * x, axis=-1, keepdims=True)
            h = (x * lax.rsqrt(ms + RMS_EPS) * g_ref[...]).astype(BF16)
            h_ref[r:r + rows, :] = h
            oif_ref[r:r + rows, :] = jnp.dot(h, wif_ref[...], preferred_element_type=F32)

    o_ref[...] = jnp.dot(h_ref[...], w_ref[...], preferred_element_type=F32).astype(BF16)


def _residue_perm(size, dilation, inverse):
    per = size // dilation
    i = lax.broadcasted_iota(jnp.int32, (size, size), 0)
    j = lax.broadcasted_iota(jnp.int32, (size, size), 1)
    if inverse:
        src = (i & (dilation - 1)) * per + (i >> (dilation.bit_length() - 1))
    else:
        src = (i & (per - 1)) * dilation + (i >> (per.bit_length() - 1))
    return j == src


def _attn_proj_pieces(h, w_ref, o_ref, width=256):
    dilation = o_ref.shape[1]
    per = o_ref.shape[2]
    if dilation > 1:
        perm = _residue_perm(h.shape[0], dilation, False).astype(BF16)
        h = jnp.dot(perm, h, preferred_element_type=F32).astype(BF16)
        yield
    for c in range(0, o_ref.shape[3], width):
        res = jnp.dot(h, w_ref[:, c:c + width], preferred_element_type=F32).astype(BF16)
        for r in range(dilation):
            o_ref[0, r, :, c:c + width] = res[r * per:(r + 1) * per, :]
        yield


_O_I = 4 * MLSTM_WIDTH
_O_QA = _O_I + 2 * MLSTM_HEADS
_O_GATES = _O_QA + 3 * ATTN_WIDTH
D_IN = _O_GATES + 2 * D_MODEL


SPLIT_CHUNK = 512


def _split_w_in_kernel(off_ref, wt_hbm, main_ref, if_ref, q0_ref, q1_ref, q2_ref, buf_ref, sem):
    c = pl.program_id(0)
    n_main = D_MAIN // SPLIT_CHUNK
    gw = GROUP_WIDTH

    def fetch(step):
        slot = step % 2
        rows = pl.ds(pl.multiple_of(off_ref[step], 8), SPLIT_CHUNK)
        return pltpu.make_async_copy(wt_hbm.at[rows, :], buf_ref.at[slot], sem.at[slot])

    @pl.when(c == 0)
    def _():
        fetch(c).start()

    @pl.when(c + 1 < pl.num_programs(0))
    def _():
        fetch(c + 1).start()

    fetch(c).wait()
    piece = buf_ref[c % 2].T

    @pl.when(c < n_main)
    def _():
        main_ref[...] = piece.astype(BF16)

    for g, q_ref in enumerate((q0_ref, q1_ref, q2_ref)):
        for part in range(3):
            @pl.when(c == n_main + 3 * g + part)
            def _():
                q_ref[:, part * gw:(part + 1) * gw] = piece.astype(BF16)

    @pl.when(c == n_main + 3 * N_GROUPS)
    def _():
        window = piece[:, :LANES]
        lane = lax.broadcasted_iota(jnp.int32, window.shape, 1)
        i_part = jnp.where(lane < MLSTM_HEADS, window, 0.0)
        f_part = jnp.where(lane < MLSTM_HEADS,
                           pltpu.roll(window, LANES - MLSTM_HEADS, axis=1), 0.0)
        if_ref[...] = jnp.concatenate([i_part, f_part], axis=1).astype(BF16)


def _split_w_in(w_in_t):
    d = w_in_t.shape[1]
    assert w_in_t.shape[0] == D_IN
    ch = SPLIT_CHUNK
    n_main = D_MAIN // ch
    offsets = ([k * ch for k in range(_O_I // ch)]
               + [_O_GATES + k * ch for k in range(2 * D_MODEL // ch)]
               + [_O_QA + part * ATTN_WIDTH + g * GROUP_WIDTH
                  for g in range(N_GROUPS) for part in range(3)]
               + [_O_I])
    assert len(offsets) == n_main + 3 * N_GROUPS + 1 and GROUP_WIDTH == ch

    def whole(width):
        return pl.BlockSpec((d, width), lambda i, off: (0, 0))

    outs = pl.pallas_call(
        _split_w_in_kernel,
        grid_spec=pltpu.PrefetchScalarGridSpec(
            num_scalar_prefetch=1,
            grid=(len(offsets),),
            in_specs=[pl.BlockSpec(memory_space=pl.ANY)],
            out_specs=[pl.BlockSpec((d, ch), lambda i, off: (0, jnp.minimum(i, n_main - 1))),
                       whole(IF_WIDTH), whole(QKV_WIDTH), whole(QKV_WIDTH), whole(QKV_WIDTH)],
            scratch_shapes=[pltpu.VMEM((2, ch, d), F32), pltpu.SemaphoreType.DMA((2,))],
        ),
        out_shape=[jax.ShapeDtypeStruct((d, width), BF16)
                   for width in (D_MAIN, IF_WIDTH, QKV_WIDTH, QKV_WIDTH, QKV_WIDTH)],
        compiler_params=_params(("arbitrary",)),
        name="split_w_in",
    )(jnp.asarray(offsets, jnp.int32), w_in_t)
    return outs[0], outs[1], outs[2:]


def _in_proj(x2, g, w_main, w_if, *, bm=1024, bn=1024):
    n = x2.shape[0]
    bm = min(bm, n)
    kern = functools.partial(_in_proj_kernel, bm=bm, rows=256)
    return pl.pallas_call(
        kern,
        grid=(n // bm, D_MAIN // bn),
        in_specs=[
            pl.BlockSpec((bm, D_MODEL), lambda i, j: (i, 0)),
            pl.BlockSpec((1, D_MODEL), lambda i, j: (0, 0)),
            pl.BlockSpec((D_MODEL, bn), lambda i, j: (0, j)),
            pl.BlockSpec((D_MODEL, IF_WIDTH), lambda i, j: (0, 0)),
        ],
        out_specs=[
            pl.BlockSpec((bm, bn), lambda i, j: (i, j)),
            pl.BlockSpec((bm, IF_WIDTH), lambda i, j: (i, 0)),
        ],
        out_shape=[
            jax.ShapeDtypeStruct((n, D_MAIN), BF16),
            jax.ShapeDtypeStruct((n, IF_WIDTH), F32),
        ],
        scratch_shapes=[pltpu.VMEM((bm, D_MODEL), BF16)],
        compiler_params=_params(("arbitrary", "arbitrary")),
        name="in_proj",
    )(x2, g, w_main, w_if)


def _mlstm_chunk(rows, qk_ref, v_ref, og_ref, if_ref, cw_ref, cb_ref, gb_ref, ng_ref, out_ref,
                 tail_ref, c_ref, m_ref, other_work):
    L = rows.stop - rows.start
    dh = HEAD_DIM

    tail = tail_ref.shape[0]
    cur = qk_ref[0, rows, :]
    ext = jnp.concatenate([tail_ref[...], cur], axis=0)
    tail_ref[...] = cur[L - tail:, :]
    srow = lax.broadcasted_iota(jnp.int32, (L, tail + L), 0)
    scol = lax.broadcasted_iota(jnp.int32, (L, tail + L), 1)
    acc = cur.astype(F32) * cw_ref[CONV_WIDTH - 1:CONV_WIDTH, :] + cb_ref[...]
    for s in range(1, CONV_WIDTH):
        shift = (scol == srow + (tail - s)).astype(BF16)
        acc = acc + (jnp.dot(shift, ext, preferred_element_type=F32)
                     * cw_ref[CONV_WIDTH - 1 - s:CONV_WIDTH - s, :])
    qk = acc * _sigmoid(acc)
    q_all = qk[:, :MLSTM_WIDTH]
    k_all = qk[:, MLSTM_WIDTH:] * (dh ** -0.5)

    head_lane = lax.broadcasted_iota(jnp.int32, (L, LANES), 1) < MLSTM_HEADS
    gi = jnp.where(head_lane, if_ref[0, rows, :LANES] + gb_ref[:, :LANES], 0.0)
    lf = jnp.where(head_lane, _log_sigmoid(if_ref[0, rows, LANES:] + gb_ref[:, LANES:]), 0.0)
    row = lax.broadcasted_iota(jnp.int32, (L, L), 0)
    col = lax.broadcasted_iota(jnp.int32, (L, L), 1)
    causal = col <= row
    cum = jnp.dot(causal.astype(F32), lf, preferred_element_type=F32,
                  precision=lax.Precision.HIGHEST)
    rmat = gi - cum
    cmax = rmat
    trow = lax.broadcasted_iota(jnp.int32, (L, LANES), 0)
    step = 1
    while step < L:
        cmax = jnp.maximum(cmax, jnp.where(trow >= step, pltpu.roll(cmax, step, axis=0), -jnp.inf))
        step *= 2
    m_prev = m_ref[...]
    gmax = jnp.maximum(m_prev, cmax)
    s_inter = jnp.exp(m_prev - gmax)
    inv_scale = jnp.exp(-(cum + gmax))
    b_last = cum[L - 1:L, :]
    m_new = b_last + gmax[L - 1:L, :]
    w_state = jnp.exp(b_last + rmat - m_new)
    decay = jnp.exp(b_last + m_prev - m_new)
    m_ref[...] = m_new
    rmat_t = rmat.T
    ones_blk = jnp.ones((L, dh), BF16)

    def column(mat, h):
        return jnp.broadcast_to(mat[:, h:h + 1], (L, dh))

    for h in range(MLSTM_HEADS):
        hs = slice(h * dh, (h + 1) * dh)
        p = jnp.exp(jnp.where(causal, rmat_t[h:h + 1, :] - column(gmax, h), NEG))
        qh = q_all[:, hs]
        kh = k_all[:, hs]
        s = lax.dot_general(qh.astype(BF16), kh.astype(BF16), (((1,), (1,)), ((), ())),
                            preferred_element_type=F32)
        vaug = jnp.concatenate([v_ref[0, rows, hs], ones_blk], axis=1)
        c_prev = c_ref[h]
        lhs = jnp.concatenate([(p * s).astype(BF16), (column(s_inter, h) * qh).astype(BF16)],
                              axis=1)
        rhs = jnp.concatenate([vaug, c_prev.astype(BF16)], axis=0)
        nd = jnp.dot(lhs, rhs, preferred_element_type=F32)
        hh = nd[:, :dh] / jnp.maximum(jnp.abs(nd[:, dh:]), column(inv_scale, h))

        kw = (column(w_state, h) * kh).astype(BF16)
        kv = lax.dot_general(kw, vaug, (((0,), (0,)), ((), ())), preferred_element_type=F32)
        c_ref[h] = decay[:, h:h + 1] * c_prev + kv

        y = hh * lax.rsqrt(jnp.mean(hh * hh, axis=-1, keepdims=True) + RMS_EPS) * ng_ref[:, hs]
        y = y * _sigmoid(og_ref[0, rows, hs].astype(F32))
        out_ref[0, rows, hs] = y.astype(BF16)
        next(other_work, None)


def _mixer_kernel(x_ref, xg_ref, w0_ref, w1_ref, w2_ref,
                  qk_ref, v_ref, og_ref, if_ref, cw_ref, cb_ref, gb_ref, ng_ref,
                  hm_ref, a0_ref, a1_ref, a2_ref, tail_ref, c_ref, m_ref, *, chunk):
    @pl.when(pl.program_id(1) == 0)
    def _():
        tail_ref[...] = jnp.zeros_like(tail_ref)
        c_ref[...] = jnp.zeros_like(c_ref)
        m_ref[...] = jnp.zeros_like(m_ref)

    x = x_ref[...]
    h = (x * lax.rsqrt(jnp.mean(x * x, axis=-1, keepdims=True) + RMS_EPS) * xg_ref[...]).astype(BF16)
    pieces = itertools.chain(_attn_proj_pieces(h, w0_ref, a0_ref), _attn_proj_pieces(h, w1_ref, a1_ref),
                             _attn_proj_pieces(h, w2_ref, a2_ref))
    n_pieces = 3 * (QKV_WIDTH // 256) + 2
    n_heads = (x_ref.shape[0] // chunk) * MLSTM_HEADS
    def some_pieces():
        done = 0
        for head in range(1, n_heads + 1):
            while done * n_heads < head * n_pieces:
                next(pieces, None)
                done += 1
            yield

    other_work = some_pieces()
    for r0 in range(0, x_ref.shape[0], chunk):
        _mlstm_chunk(slice(r0, r0 + chunk), qk_ref, v_ref, og_ref, if_ref, cw_ref, cb_ref, gb_ref,
                     ng_ref, hm_ref, tail_ref, c_ref, m_ref, other_work)
    for _ in pieces:
        pass


def _mixer_inputs(x2, norm_g, w_qkv, proj3, ifg3, conv_w, conv_b, gate_b, mnorm_g, *,
                  rows=256, chunk=MLSTM_CHUNK):
    bsz, s, _ = proj3.shape
    w = MLSTM_WIDTH
    n_blk = s // rows
    dilations = [d for _, d in ATTN_GROUPS]

    def const(shape):
        return pl.BlockSpec(shape, lambda b, i: (0,) * len(shape))

    def weight():
        return pl.BlockSpec((D_MODEL, QKV_WIDTH), lambda b, i: (0, 0), pipeline_mode=pl.Buffered(1))

    kern = functools.partial(_mixer_kernel, chunk=chunk)
    outs = pl.pallas_call(
        kern,
        grid=(bsz, n_blk),
        in_specs=[
            pl.BlockSpec((rows, D_MODEL), lambda b, i: (b * n_blk + i, 0)),
            const((1, D_MODEL)),
            weight(), weight(), weight(),
            pl.BlockSpec((1, rows, 2 * w), lambda b, i: (b, i, COL_QK // (2 * w))),
            pl.BlockSpec((1, rows, w), lambda b, i: (b, i, COL_V // w)),
            pl.BlockSpec((1, rows, w), lambda b, i: (b, i, COL_O // w)),
            pl.BlockSpec((1, rows, IF_WIDTH), lambda b, i: (b, i, 0)),
            const((CONV_WIDTH, 2 * w)), const((1, 2 * w)), const((1, IF_WIDTH)), const((1, w)),
        ],
        out_specs=[pl.BlockSpec((1, rows, w), lambda b, i: (b, i, 0))] + [
            pl.BlockSpec((1, d, rows // d, QKV_WIDTH), lambda b, i: (b, 0, i, 0)) for d in dilations],
        out_shape=[jax.ShapeDtypeStruct((bsz, s, w), BF16)] + [
            jax.ShapeDtypeStruct((bsz, d, s // d, QKV_WIDTH), BF16) for d in dilations],
        scratch_shapes=[
            pltpu.VMEM((16, 2 * w), BF16),
            pltpu.VMEM((MLSTM_HEADS, HEAD_DIM, 2 * HEAD_DIM), F32),
            pltpu.VMEM((1, LANES), F32),
        ],
        compiler_params=_params(("arbitrary", "arbitrary")),
        name="mixer_inputs",
    )(x2, norm_g, *w_qkv, proj3, proj3, proj3, ifg3, conv_w, conv_b, gate_b, mnorm_g)
    return outs[0], outs[1:]


def _attn_kernel(q_ref, kp_ref, kc_ref, vp_ref, vc_ref, bias_ref, o_ref, lse_ref):
    qb = ATTN_BLOCK
    dh = HEAD_DIM
    n_q = q_ref.shape[2] // qb
    first = pl.program_id(2) == 0
    kcol = lax.broadcasted_iota(jnp.int32, (qb, 2 * qb), 1)
    dead = jnp.logical_and(first, kcol < qb)
    lane = lax.broadcasted_iota(jnp.int32, (qb, LANES), 1)
    for jq in range(n_q):
        rows = slice(jq * qb, (jq + 1) * qb)
        lse_all = jnp.zeros((qb, LANES), F32)
        for j in range(HEADS_PER_GROUP):
            hs = slice(j * dh, (j + 1) * dh)
            q = q_ref[0, 0][rows, hs]
            if jq == 0:
                k = jnp.concatenate([kp_ref[0, 0][:, hs], kc_ref[0, 0][0:qb, hs]], axis=0)
                v = jnp.concatenate([vp_ref[0, 0][:, hs], vc_ref[0, 0][0:qb, hs]], axis=0)
            else:
                k = kc_ref[0, 0][(jq - 1) * qb:(jq + 1) * qb, hs]
                v = vc_ref[0, 0][(jq - 1) * qb:(jq + 1) * qb, hs]
            s = lax.dot_general(q, k, (((1,), (1,)), ((), ())), preferred_element_type=F32)
            s = s * (dh ** -0.5) + bias_ref[j]
            if jq == 0:
                s = jnp.where(dead, NEG, s)
            mx = jnp.max(s, axis=1, keepdims=True)
            p = jnp.exp(s - mx)
            den = jnp.sum(p, axis=1, keepdims=True)
            o = jnp.dot(p.astype(BF16), v, preferred_element_type=F32) / den
            o_ref[0, 0, rows, hs] = o.astype(BF16)
            lse_all = jnp.where(lane == j, mx + jnp.log(den), lse_all)
        lse_ref[0, 0, rows, :] = lse_all


def _attn_group(qkv, bias, g, *, q_blocks=4):
    bsz, dilation, n, _ = qkv.shape
    gw = GROUP_WIDTH
    q_blocks = min(q_blocks, n // ATTN_BLOCK)
    run = q_blocks * ATTN_BLOCK

    def own(part, width=gw):
        return pl.BlockSpec((1, 1, run, width), lambda b, r, i: (b, r, i, part))

    def previous(part):
        return pl.BlockSpec((1, 1, ATTN_BLOCK, gw),
                            lambda b, r, i: (b, r, jnp.maximum(i * q_blocks - 1, 0), part))

    return pl.pallas_call(
        _attn_kernel,
        grid=(bsz, dilation, n // run),
        in_specs=[
            own(0),
            previous(1), own(1),
            previous(2), own(2),
            pl.BlockSpec((HEADS_PER_GROUP, ATTN_BLOCK, 2 * ATTN_BLOCK), lambda b, r, i: (0, 0, 0)),
        ],
        out_specs=[own(0), own(0, LANES)],
        out_shape=[
            jax.ShapeDtypeStruct((bsz, dilation, n, gw), BF16),
            jax.ShapeDtypeStruct((bsz, dilation, n, LANES), F32),
        ],
        compiler_params=_params(("arbitrary", "arbitrary", "arbitrary")),
        name=f"attn_g{g}",
    )(qkv, qkv, qkv, qkv, qkv, bias)


def _t5_bucket(dist):
    max_exact = REL_BUCKETS // 2
    d_f = jnp.maximum(dist, 1).astype(F32)
    large = max_exact + (jnp.log(d_f / max_exact) / math.log(REL_MAX_DIST / max_exact)
                         * (REL_BUCKETS - max_exact)).astype(jnp.int32)
    large = jnp.minimum(large, REL_BUCKETS - 1)
    return jnp.where(dist < max_exact, dist, large)


def _attn_bias(rel_bias, g, dilation):
    span = ATTN_SPAN
    buckets = _t5_bucket(jnp.arange(span + 1, dtype=jnp.int32) * dilation)
    vec = rel_bias[buckets][:, g * HEADS_PER_GROUP:(g + 1) * HEADS_PER_GROUP].T.astype(F32)
    qpos = jnp.arange(ATTN_BLOCK)[:, None]
    kpos = jnp.arange(2 * ATTN_BLOCK)[None, :]
    dist = qpos + ATTN_BLOCK - kpos
    valid = (dist >= 0) & (dist <= span)
    onehot = (dist[:, :, None] == jnp.arange(span + 1)[None, None, :]).astype(F32)
    table = jnp.einsum('qkj,hj->hqk', onehot, vec, precision=lax.Precision.HIGHEST)
    return jnp.where(valid[None], table, NEG)


def _merge_kernel(hm_ref, o0_ref, o1_ref, o2_ref, l0_ref, l1_ref, l2_ref, gm_ref, ga_ref, x_ref,
                  wbm_ref, wba_ref, wo_ref, ng_ref, rw_ref, rb_ref,
                  x1_ref, h2_ref, idx_ref, gate_ref, rank_ref, count_ref, cnt_ref):
    dh = HEAD_DIM
    bm = x_ref.shape[0]

    def token_order(ref):
        dilation = ref.shape[1]
        if dilation == 1:
            return ref[0, 0]
        perm = _residue_perm(bm, dilation, True).astype(BF16)
        blk = ref[0].reshape(bm, ref.shape[3])
        parts = [blk] if blk.dtype == BF16 else _bf16_parts(blk, 3)
        out = None
        for p in parts:
            moved = jnp.dot(perm, p, preferred_element_type=F32)
            out = moved if out is None else out + moved
        return out

    l0, l1, l2 = token_order(l0_ref), token_order(l1_ref), token_order(l2_ref)
    o0, o1, o2 = token_order(o0_ref), token_order(o1_ref), token_order(o2_ref)
    mx = jnp.maximum(jnp.maximum(l0, l1), l2)
    e0, e1, e2 = jnp.exp(l0 - mx), jnp.exp(l1 - mx), jnp.exp(l2 - mx)
    den = e0 + e1 + e2
    w0, w1, w2 = e0 / den, e1 / den, e2 / den
    parts = []
    for j in range(HEADS_PER_GROUP):
        hs = slice(j * dh, (j + 1) * dh)
        parts.append(w0[:, j:j + 1] * o0[:, hs] + w1[:, j:j + 1] * o1[:, hs]
                     + w2[:, j:j + 1] * o2[:, hs])
    ha = jnp.concatenate(parts, axis=1).astype(BF16)
    ym = jnp.dot(hm_ref[...], wbm_ref[...], preferred_element_type=F32)
    ya = jnp.dot(ha, wba_ref[...], preferred_element_type=F32)
    merged = (_sigmoid(gm_ref[...].astype(F32)) * ym + _sigmoid(ga_ref[...].astype(F32)) * ya)
    x1 = x_ref[...] + jnp.dot(merged.astype(BF16), wo_ref[...], preferred_element_type=F32)
    x1_ref[...] = x1
    h2 = x1 * lax.rsqrt(jnp.mean(x1 * x1, axis=-1, keepdims=True) + RMS_EPS) * ng_ref[...]
    _store_token_tiles(h2_ref, _pack_pairs(h2))
    h_hi, h_lo = _bf16_parts(h2, 2)
    logits = jnp.dot(jnp.concatenate([h_hi, h_lo, h_hi], axis=1), rw_ref[...],
                     preferred_element_type=F32) + rb_ref[...]
    lane = lax.broadcasted_iota(jnp.int32, logits.shape, 1)
    idx_all = jnp.zeros(logits.shape, jnp.int32)
    val_all = jnp.zeros(logits.shape, F32)
    top0 = None
    esum = None
    picks = []
    chosen = jnp.zeros(logits.shape, F32)
    for k in range(TOP_K):
        m = jnp.max(logits, axis=1, keepdims=True)
        sel = jnp.min(jnp.where(logits == m, lane, LANES), axis=1, keepdims=True)
        if k == 0:
            top0 = m
        e = jnp.exp(m - top0)
        esum = e if k == 0 else esum + e
        idx_all = jnp.where(lane == k, sel, idx_all)
        val_all = jnp.where(lane == k, e, val_all)
        pick = lane == sel
        picks.append(pick)
        chosen = chosen + pick.astype(F32)
        logits = jnp.where(pick, -jnp.inf, logits)
    idx_ref[...] = idx_all
    gate_ref[...] = val_all / esum

    @pl.when(pl.program_id(0) == 0)
    def _():
        cnt_ref[...] = jnp.zeros_like(cnt_ref)

    trow = lax.broadcasted_iota(jnp.int32, (bm, bm), 0)
    tcol = lax.broadcasted_iota(jnp.int32, (bm, bm), 1)
    earlier = jnp.dot((tcol < trow).astype(BF16), chosen.astype(BF16),
                      preferred_element_type=F32) + cnt_ref[...]
    rank_all = jnp.zeros(logits.shape, jnp.int32)
    for k in range(TOP_K):
        r = jnp.sum(jnp.where(picks[k], earlier, 0.0), axis=1, keepdims=True)
        rank_all = jnp.where(lane == k, r.astype(jnp.int32), rank_all)
    rank_ref[...] = rank_all
    cnt_ref[...] = cnt_ref[...] + jnp.sum(chosen, axis=0, keepdims=True)
    count_ref[...] = jnp.broadcast_to(cnt_ref[...], count_ref.shape)


def _merge(hm, outs, lses, proj, x2, wbm, wba, wo, ng, rw, rb, *, bm=256):
    n = x2.shape[0]
    d = D_MODEL
    gcol = COL_GATES // d
    n_blk = outs[0].shape[2] // bm

    def rows(width):
        return pl.BlockSpec((bm, width), lambda i: (i, 0))

    def full(a, b):
        return pl.BlockSpec((a, b), lambda i: (0, 0), pipeline_mode=pl.Buffered(1))

    def residue(arr):
        dilation, width = arr.shape[1], arr.shape[3]
        return pl.BlockSpec((1, dilation, bm // dilation, width),
                            lambda i: (i // n_blk, 0, i % n_blk, 0))

    return pl.pallas_call(
        _merge_kernel,
        grid=(n // bm,),
        in_specs=[
            rows(MLSTM_WIDTH),
            residue(outs[0]), residue(outs[1]), residue(outs[2]),
            residue(lses[0]), residue(lses[1]), residue(lses[2]),
            pl.BlockSpec((bm, d), lambda i: (i, gcol)),
            pl.BlockSpec((bm, d), lambda i: (i, gcol + 1)),
            rows(d),
            full(MLSTM_WIDTH, d), full(GROUP_WIDTH, d), full(d, d),
            full(1, d), full(3 * d, LANES), full(1, LANES),
        ],
        out_specs=[rows(d), pl.BlockSpec((bm * PACK_ROWS, LANES), lambda i: (i, 0)),
                   rows(LANES), rows(LANES), rows(LANES),
                   pl.BlockSpec((8, LANES), lambda i: (0, 0))],
        out_shape=[
            jax.ShapeDtypeStruct((n, d), F32),
            jax.ShapeDtypeStruct((n * PACK_ROWS, LANES), jnp.uint32),
            jax.ShapeDtypeStruct((n, LANES), jnp.int32),
            jax.ShapeDtypeStruct((n, LANES), F32),
            jax.ShapeDtypeStruct((n, LANES), jnp.int32),
            jax.ShapeDtypeStruct((8, LANES), F32),
        ],
        scratch_shapes=[pltpu.VMEM((1, LANES), F32)],
        compiler_params=_params(("arbitrary",)),
        name="merge",
    )(hm, outs[0], outs[1], outs[2], lses[0], lses[1], lses[2], proj, proj, x2,
      wbm, wba, wo, ng, rw, rb)


PACK_ROWS = 8


def _pack_pairs(x):
    w = x.shape[1] // 2
    lo = lax.bitcast_convert_type(x[:, :w].astype(BF16).astype(F32), jnp.uint32) >> 16
    hi = lax.bitcast_convert_type(x[:, w:].astype(BF16).astype(F32), jnp.uint32)
    return (hi & jnp.uint32(0xFFFF0000)) | lo


def _unpack_pairs(words):
    lo = lax.bitcast_convert_type(words << 16, F32)
    hi = lax.bitcast_convert_type(words & jnp.uint32(0xFFFF0000), F32)
    return lo, hi


def _store_token_tiles(ref, words):
    rows = words.shape[0]
    for s in range(PACK_ROWS):
        ref[pl.ds(s, rows, stride=PACK_ROWS), :] = words[:, s * LANES:(s + 1) * LANES]


def _dispatch_kernel(fs_ref, fl_ref, nu_ref, dest_ref, hp_ref, xb_hbm, zero_ref, sem, zsem, *,
                     tokens, tm, n_blk):
    pr = PACK_ROWS
    fill_sizes = [1 << b for b in reversed(range((tm - 1).bit_length()))]

    def fill_copies(e):
        off = fs_ref[e]
        for p in fill_sizes:
            take = (fl_ref[e] & p) != 0
            dst = xb_hbm.at[pl.ds(pl.multiple_of(off * pr, pr), p * pr), :]
            yield take, pltpu.make_async_copy(zero_ref.at[pl.ds(0, p * pr), :], dst, zsem)
            off = off + jnp.where(take, p, 0)

    def block_copy(b):
        dst = xb_hbm.at[pl.ds(pl.multiple_of(b * (tm * pr), tm * pr), tm * pr), :]
        return pltpu.make_async_copy(zero_ref, dst, zsem)

    @pl.when(pl.program_id(0) == 0)
    def _():
        zero_ref[...] = jnp.zeros_like(zero_ref)
        for start in (True, False):
            def per_expert(e, carry):
                for take, cp in fill_copies(e):
                    @pl.when(take)
                    def _():
                        cp.start() if start else cp.wait()
                return carry

            def per_block(b, carry):
                block_copy(b).start() if start else block_copy(b).wait()
                return carry

            lax.fori_loop(0, N_EXPERTS, per_expert, 0)
            lax.fori_loop(nu_ref[0], n_blk, per_block, 0)

    def issue(t, carry):
        src = hp_ref.at[pl.ds(pl.multiple_of(t * pr, pr), pr), :]
        for k in range(TOP_K):
            slot = dest_ref[t * TOP_K + k]
            pltpu.make_async_copy(src, xb_hbm.at[pl.ds(pl.multiple_of(slot * pr, pr), pr), :],
                                  sem).start(priority=k % 2)
        return carry

    lax.fori_loop(0, tokens, issue, 0)
    for k in range(TOP_K):
        pltpu.make_async_copy(hp_ref, xb_hbm.at[pl.ds(0, tokens * pr), :], sem).wait()


def _dispatch(fill_start, fill_len, n_used, dest, h2p, n_pad, *, tokens=512, tm=MOE_TILE):
    pr = PACK_ROWS
    n_tok = h2p.shape[0] // pr
    kern = functools.partial(_dispatch_kernel, tokens=tokens, tm=tm, n_blk=n_pad // tm)
    return pl.pallas_call(
        kern,
        grid_spec=pltpu.PrefetchScalarGridSpec(
            num_scalar_prefetch=3,
            grid=(n_tok // tokens,),
            in_specs=[
                pl.BlockSpec((tokens * TOP_K,), lambda i, fs, fl, nu: (i,),
                             memory_space=pltpu.SMEM),
                pl.BlockSpec((tokens * pr, LANES), lambda i, fs, fl, nu: (i, 0)),
            ],
            out_specs=pl.BlockSpec(memory_space=pl.ANY),
            scratch_shapes=[pltpu.VMEM((tm * pr, LANES), jnp.uint32),
                            pltpu.SemaphoreType.DMA(()), pltpu.SemaphoreType.DMA(())],
        ),
        out_shape=jax.ShapeDtypeStruct((n_pad * pr, LANES), jnp.uint32),
        compiler_params=_params(("arbitrary",)),
        name="dispatch",
    )(fill_start, fill_len, n_used, dest, h2p)


def _cast_rows(src_ref, dst_ref, rows):
    total = src_ref.shape[0]

    def body(i, carry):
        r = pl.multiple_of(i * rows, rows)
        dst_ref[pl.ds(r, rows), :] = src_ref[pl.ds(r, rows), :].astype(BF16)
        return carry

    lax.fori_loop(0, total // rows, body, 0)


def _stream_expert_weights(be_ref, nu_ref, ne_ref, copies, convert):
    k, j = pl.program_id(0), pl.program_id(1)
    expert = be_ref[j]
    used = j < nu_ref[0]
    run_start = jnp.logical_or(j == 0, expert != be_ref[jnp.maximum(j - 1, 0)])

    @pl.when(jnp.logical_and(k == 0, j == 0))
    def _():
        for cp in copies(expert, k):
            cp.start()

    @pl.when(jnp.logical_and(used, run_start))
    def _():
        for cp in copies(expert, k):
            cp.wait()
        convert()
        last_run = ne_ref[j] < 0
        nxt_e = jnp.where(last_run, be_ref[0], ne_ref[j])
        nxt_k = jnp.where(last_run, k + 1, k)

        @pl.when(nxt_k < pl.num_programs(0))
        def _():
            for cp in copies(nxt_e, nxt_k):
                cp.start()

    return used


def _per_block_rows(used, bv_ref, o_ref, compute, store=None, rows_per_token=1):
    tm = o_ref.shape[0] // rows_per_token
    need = (bv_ref[pl.program_id(1)] + (MOE_ROW_STEP - 1)) // MOE_ROW_STEP

    def run(rows):
        res = compute(rows)
        head = o_ref.at[pl.ds(0, rows * rows_per_token), :]
        if store is None:
            head[...] = res
        else:
            store(head, res)
        if rows < tm:
            tail = o_ref.at[pl.ds(rows * rows_per_token, (tm - rows) * rows_per_token), :]
            tail[...] = jnp.zeros_like(tail)

    for q in range(1, tm // MOE_ROW_STEP + 1):
        @pl.when(jnp.logical_and(used, need == q))
        def _():
            run(q * MOE_ROW_STEP)

    @pl.when(jnp.logical_not(used))
    def _():
        o_ref[...] = jnp.zeros_like(o_ref)


def _gate_up_kernel(be_ref, nu_ref, ne_ref, bv_ref, x_ref, w_hbm, bg_ref, bu_ref, o_ref,
                    stage_ref, w_bf, sem):
    tf = o_ref.shape[1]
    kt = pl.num_programs(0)

    def copies(expert, k):
        return [pltpu.make_async_copy(
            w_hbm.at[expert, :, pl.ds(pl.multiple_of((half * kt + k) * tf, tf), tf)],
            stage_ref.at[half], sem.at[half]) for half in range(2)]

    def convert():
        for half in range(2):
            _cast_rows(stage_ref.at[half], w_bf.at[half], 256)

    used = _stream_expert_weights(be_ref, nu_ref, ne_ref, copies, convert)

    def compute(rows):
        halves = [_unpack_pairs(x_ref[pl.ds(s, rows, stride=PACK_ROWS), :])
                  for s in range(PACK_ROWS)]
        x = jnp.concatenate([h[0].astype(BF16) for h in halves]
                            + [h[1].astype(BF16) for h in halves], axis=1)
        gate = jnp.dot(x, w_bf[0], preferred_element_type=F32) + bg_ref[0]
        up = jnp.dot(x, w_bf[1], preferred_element_type=F32) + bu_ref[0]
        gate = jnp.minimum(gate, SWIGLU_LIMIT)
        up = jnp.clip(up, -SWIGLU_LIMIT, SWIGLU_LIMIT)
        glu = gate * _sigmoid(SWIGLU_ALPHA * gate)
        return ((up + 1.0) * glu).astype(BF16)

    _per_block_rows(used, bv_ref, o_ref, compute)


def _used_block(j, nu):
    return jnp.minimum(j, nu[0] - 1)


def _gate_up(sched, xb, w_gate_up, b_gate_up3, *, tm=MOE_TILE, tf=1024):
    n_pad, d = xb.shape[0] // PACK_ROWS, D_MODEL
    kt = D_FF // tf

    def bspec(off):
        return pl.BlockSpec((1, 1, tf),
                            lambda k, j, be, nu, ne, bv: (be[_used_block(j, nu)], 0, off + k))

    return pl.pallas_call(
        _gate_up_kernel,
        grid_spec=pltpu.PrefetchScalarGridSpec(
            num_scalar_prefetch=4,
            grid=(kt, n_pad // tm),
            in_specs=[
                pl.BlockSpec((tm * PACK_ROWS, LANES),
                             lambda k, j, be, nu, ne, bv: (_used_block(j, nu), 0)),
                pl.BlockSpec(memory_space=pl.ANY),
                bspec(0), bspec(kt),
            ],
            out_specs=pl.BlockSpec((tm, tf), lambda k, j, be, nu, ne, bv: (j, k)),
            scratch_shapes=[pltpu.VMEM((2, d, tf), F32), pltpu.VMEM((2, d, tf), BF16),
                            pltpu.SemaphoreType.DMA((2,))],
        ),
        out_shape=jax.ShapeDtypeStruct((n_pad, D_FF), BF16),
        compiler_params=_params(("arbitrary", "arbitrary")),
        name="gate_up",
    )(*sched, xb, w_gate_up, b_gate_up3, b_gate_up3)


def _down_kernel(be_ref, nu_ref, ne_ref, bv_ref, h_ref, w_hbm, bd_ref, o_ref, stage_ref, w_bf,
                 sem):
    def copies(expert, k):
        del k
        return [pltpu.make_async_copy(w_hbm.at[expert], stage_ref, sem.at[0])]

    def convert():
        _cast_rows(stage_ref, w_bf, 256)

    used = _stream_expert_weights(be_ref, nu_ref, ne_ref, copies, convert)

    def compute(rows):
        y = jnp.dot(h_ref[0:rows, :], w_bf[...], preferred_element_type=F32) + bd_ref[0]
        return _pack_pairs(y)

    _per_block_rows(used, bv_ref, o_ref, compute, store=_store_token_tiles,
                    rows_per_token=PACK_ROWS)


def _down(sched, hb, w_down, b_down3, *, tm=MOE_TILE):
    n_pad, f = hb.shape
    d = D_MODEL
    return pl.pallas_call(
        _down_kernel,
        grid_spec=pltpu.PrefetchScalarGridSpec(
            num_scalar_prefetch=4,
            grid=(1, n_pad // tm),
            in_specs=[
                pl.BlockSpec((tm, f), lambda k, j, be, nu, ne, bv: (_used_block(j, nu), 0)),
                pl.BlockSpec(memory_space=pl.ANY),
                pl.BlockSpec((1, 1, d),
                             lambda k, j, be, nu, ne, bv: (be[_used_block(j, nu)], 0, 0)),
            ],
            out_specs=pl.BlockSpec((tm * PACK_ROWS, LANES), lambda k, j, be, nu, ne, bv: (j, 0)),
            scratch_shapes=[pltpu.VMEM((f, d), F32), pltpu.VMEM((f, d), BF16),
                            pltpu.SemaphoreType.DMA((1,))],
        ),
        out_shape=jax.ShapeDtypeStruct((n_pad * PACK_ROWS, LANES), jnp.uint32),
        compiler_params=_params(("arbitrary", "arbitrary")),
        name="down",
    )(*sched, hb, w_down, b_down3)


def _combine_kernel(dest_ref, dest_next_ref, y_hbm, gate_ref, x1_ref, ng_ref, o_ref, buf_ref, sem,
                    *, rows):
    pr = PACK_ROWS
    slab = 64
    half = x1_ref.shape[1] // 2
    i = pl.program_id(0)
    slot_rows = TOP_K * rows * pr
    cur = i % 2
    base = pl.multiple_of(cur * slot_rows, slot_rows)
    nxt_base = pl.multiple_of((1 - cur) * slot_rows, slot_rows)

    def gather(idx_ref, dst_base, buf, t):
        for k in range(TOP_K):
            slot = idx_ref[t * TOP_K + k]
            pltpu.make_async_copy(
                y_hbm.at[pl.ds(pl.multiple_of(slot * pr, pr), pr), :],
                buf_ref.at[pl.ds(pl.multiple_of(dst_base + (k * rows + t) * pr, pr), pr), :],
                sem.at[buf]).start(priority=k % 2)

    @pl.when(i == 0)
    def _():
        def first(t, carry):
            gather(dest_ref, 0, 0, t)
            return carry

        lax.fori_loop(0, rows, first, 0)

    pltpu.make_async_copy(y_hbm.at[pl.ds(0, slot_rows), :],
                          buf_ref.at[pl.ds(base, slot_rows), :], sem.at[cur]).wait()

    def reduce_slab(t0):
        gates = [gate_ref[pl.ds(t0, slab), k:k + 1] for k in range(TOP_K)]
        lows, highs = [], []
        for s in range(pr):
            lo = x1_ref[pl.ds(t0, slab), s * LANES:(s + 1) * LANES]
            hi = x1_ref[pl.ds(t0, slab), half + s * LANES:half + (s + 1) * LANES]
            for k in range(TOP_K):
                y_lo, y_hi = _unpack_pairs(
                    buf_ref[pl.ds(base + (k * rows + t0) * pr + s, slab, stride=pr), :])
                lo = lo + gates[k] * y_lo
                hi = hi + gates[k] * y_hi
            lows.append(lo)
            highs.append(hi)
        acc = jnp.concatenate(lows + highs, axis=1)
        o_ref[pl.ds(t0, slab), :] = (
            acc * lax.rsqrt(jnp.mean(acc * acc, axis=-1, keepdims=True) + RMS_EPS) * ng_ref[...])

    def trip_with_prefetch(j, carry):
        t0 = pl.multiple_of(j * slab, slab)
        for tt in range(slab):
            gather(dest_next_ref, nxt_base, 1 - cur, t0 + tt)
        reduce_slab(t0)
        return carry

    def trip(j, carry):
        reduce_slab(pl.multiple_of(j * slab, slab))
        return carry

    has_next = i + 1 < pl.num_programs(0)

    @pl.when(has_next)
    def _():
        lax.fori_loop(0, rows // slab, trip_with_prefetch, 0)

    @pl.when(jnp.logical_not(has_next))
    def _():
        lax.fori_loop(0, rows // slab, trip, 0)


def _combine(dest, yb, gates, x1, ng, *, rows=256):
    n, d = x1.shape
    kern = functools.partial(_combine_kernel, rows=rows)
    n_steps = n // rows
    return pl.pallas_call(
        kern,
        grid=(n_steps,),
        in_specs=[
            pl.BlockSpec((rows * TOP_K,), lambda i: (i,), memory_space=pltpu.SMEM),
            pl.BlockSpec((rows * TOP_K,), lambda i: (jnp.minimum(i + 1, n_steps - 1),),
                         memory_space=pltpu.SMEM),
            pl.BlockSpec(memory_space=pl.ANY),
            pl.BlockSpec((rows, LANES), lambda i: (i, 0)),
            pl.BlockSpec((rows, d), lambda i: (i, 0)),
            pl.BlockSpec((1, d), lambda i: (0, 0)),
        ],
        out_specs=pl.BlockSpec((rows, d), lambda i: (i, 0)),
        out_shape=jax.ShapeDtypeStruct((n, d), F32),
        scratch_shapes=[pltpu.VMEM((2 * TOP_K * rows * PACK_ROWS, LANES), jnp.uint32),
                        pltpu.SemaphoreType.DMA((2,))],
        compiler_params=_params(("arbitrary",)),
        name="combine",
    )(dest, dest, yb, gates, x1, ng)


def _routing(top_idx, rank, counts, tm):
    n_tok = top_idx.shape[0]
    n_asg = n_tok * TOP_K
    e_flat = top_idx.reshape(n_asg)
    padded = (counts + tm - 1) // tm * tm
    pend = jnp.cumsum(padded)
    pstart = pend - padded
    experts = jnp.arange(N_EXPERTS, dtype=jnp.int32)
    start_of = jnp.sum(jnp.where(e_flat[:, None] == experts[None, :], pstart[None, :], 0), axis=1)
    dest = (start_of + rank.reshape(n_asg)).astype(jnp.int32)
    n_blk = -(-(n_asg + N_EXPERTS * (tm - 1)) // tm)
    n_pad = n_blk * tm
    fill_start = (pstart + counts).astype(jnp.int32)
    fill_len = (padded - counts).astype(jnp.int32)
    block_start = jnp.arange(n_blk, dtype=jnp.int32) * tm
    block_e = jnp.minimum(
        jnp.sum((pend[None, :] <= block_start[:, None]).astype(jnp.int32), axis=1),
        N_EXPERTS - 1).astype(jnp.int32)
    n_used = (pend[-1:] // tm).astype(jnp.int32)
    run_end = pend[block_e] // tm
    next_e = jnp.where(run_end < n_used[0], block_e[jnp.minimum(run_end, n_blk - 1)], -1)
    block_rows = jnp.clip(fill_start[block_e] - block_start, 0, tm)
    sched = (block_e, n_used, next_e.astype(jnp.int32), block_rows.astype(jnp.int32))
    return dest, fill_start, fill_len, sched, n_pad


def _layer(x, norm_mix_g, w_in, conv_w, conv_b, igate_b, fgate_b, mlstm_norm_g, rel_bias,
           w_branch_mlstm, w_branch_attn, w_out, norm_moe_g, router_w, router_b,
           w_gate_up, b_gate_up, w_down, b_down, out_norm_g):
    bsz, s, d = x.shape
    n = bsz * s
    x2 = x.reshape(n, d)
    w_main, w_if, w_qkv = _split_w_in(jnp.swapaxes(w_in, 0, 1))
    bpad = jnp.zeros((LANES - MLSTM_HEADS,), F32)
    gate_b = jnp.concatenate([igate_b, bpad, fgate_b, bpad]).reshape(1, IF_WIDTH)

    norm_g = norm_mix_g.reshape(1, d)
    proj, ifg = _in_proj(x2, norm_g, w_main, w_if)
    proj3 = proj.reshape(bsz, s, D_MAIN)
    hm, qkvs = _mixer_inputs(x2, norm_g, w_qkv, proj3, ifg.reshape(bsz, s, IF_WIDTH), conv_w,
                             conv_b.reshape(1, -1), gate_b, mlstm_norm_g.reshape(1, -1))
    hm = hm.reshape(n, MLSTM_WIDTH)

    outs, lses = [], []
    for g, (_, dilation) in enumerate(ATTN_GROUPS):
        o_g_, lse_g = _attn_group(qkvs[g], _attn_bias(rel_bias, g, dilation), g)
        outs.append(o_g_)
        lses.append(lse_g)

    rw = jnp.concatenate([router_w, jnp.zeros((d, LANES - N_EXPERTS), F32)], axis=1)
    rw_hi = rw.astype(BF16)
    rw_lo = (rw - rw_hi.astype(F32)).astype(BF16)
    rw = jnp.concatenate([rw_hi, rw_hi, rw_lo], axis=0)
    rb = jnp.concatenate([router_b, jnp.full((LANES - N_EXPERTS,), NEG, F32)]).reshape(1, LANES)
    x1, h2, idx, gates, rank, counts = _merge(
        hm, outs, lses, proj, x2, w_branch_mlstm.astype(BF16), w_branch_attn.astype(BF16),
        w_out.astype(BF16), norm_moe_g.reshape(1, d), rw, rb)

    dest, fill_start, fill_len, sched, n_pad = _routing(
        idx[:, :TOP_K], rank[:, :TOP_K], counts[0, :N_EXPERTS].astype(jnp.int32), MOE_TILE)
    xb = _dispatch(fill_start, fill_len, sched[1], dest, h2, n_pad)
    hb = _gate_up(sched, xb, w_gate_up, b_gate_up.reshape(N_EXPERTS, 1, 2 * D_FF))
    yb = _down(sched, hb, w_down, b_down.reshape(N_EXPERTS, 1, d))
    out = _combine(dest, yb, gates, x1, out_norm_g.reshape(1, d))
    return out.reshape(bsz, s, d)


def kernel(x, norm_mix_g, w_in, conv_w, conv_b, igate_b, fgate_b, mlstm_norm_g, rel_bias,
           w_branch_mlstm, w_branch_attn, w_out, norm_moe_g, router_w, router_b,
           w_gate_up, b_gate_up, w_down, b_down, norm_final_g):
    assert w_in.shape[0] == 1, "single-layer block"
    return _layer(x, norm_mix_g[0], w_in[0], conv_w[0], conv_b[0], igate_b[0], fgate_b[0],
                  mlstm_norm_g[0], rel_bias, w_branch_mlstm[0], w_branch_attn[0], w_out[0],
                  norm_moe_g[0], router_w[0], router_b[0], w_gate_up[0], b_gate_up[0],
                  w_down[0], b_down[0], norm_final_g)
```

```python
import functools
import itertools
import math

import jax
import jax.numpy as jnp
from jax import lax
from jax.experimental import pallas as pl
from jax.experimental.pallas import tpu as pltpu

F32 = jnp.float32
BF16 = jnp.bfloat16

D_MODEL = 2048
MLSTM_HEADS = 8
HEAD_DIM = 128
MLSTM_WIDTH = MLSTM_HEADS * HEAD_DIM
CONV_WIDTH = 4
ATTN_GROUPS = ((128, 1), (512, 4), (2048, 16))
N_GROUPS = 3
HEADS_PER_GROUP = 4
ATTN_HEADS = HEADS_PER_GROUP * N_GROUPS
ATTN_WIDTH = ATTN_HEADS * HEAD_DIM
GROUP_WIDTH = HEADS_PER_GROUP * HEAD_DIM
ATTN_BLOCK = 128
ATTN_SPAN = 128
REL_BUCKETS = 32
REL_MAX_DIST = 2048
N_EXPERTS = 32
TOP_K = 4
D_FF = D_MODEL
SWIGLU_LIMIT = 7.0
SWIGLU_ALPHA = 1.702
RMS_EPS = 1e-6
NEG = -1e30

COL_QK = 0
COL_V = 2 * MLSTM_WIDTH
COL_O = COL_V + MLSTM_WIDTH
COL_GATES = COL_O + MLSTM_WIDTH
D_MAIN = COL_GATES + 2 * D_MODEL
QKV_WIDTH = 3 * GROUP_WIDTH
LANES = 128
IF_WIDTH = 2 * LANES

MLSTM_CHUNK = 128
MOE_TILE = 512
MOE_ROW_STEP = 128
VMEM_LIMIT = 56 * 1024 * 1024


def _sigmoid(x):
    return 0.5 * jnp.tanh(0.5 * x) + 0.5


def _bf16_parts(x, n_parts):
    parts = []
    for _ in range(n_parts):
        p = x.astype(BF16)
        parts.append(p)
        x = x - p.astype(F32)
    return parts


def _log_sigmoid(x):
    return -(jnp.maximum(-x, 0.0) + jnp.log1p(jnp.exp(-jnp.abs(x))))


def _params(sem):
    return pltpu.CompilerParams(dimension_semantics=sem, vmem_limit_bytes=VMEM_LIMIT)


def _in_proj_kernel(x_ref, g_ref, w_ref, wif_ref, o_ref, oif_ref, h_ref, *, bm, rows):
    @pl.when(pl.program_id(1) == 0)
    def _():
        for r in range(0, bm, rows):
            x = x_ref[r:r + rows, :]
            ms = jnp.mean(x * x, axis=-1, keepdims=True)
            h = (x * lax.rsqrt(ms + RMS_EPS) * g_ref[...]).astype(BF16)
            h_ref[r:r + rows, :] = h
            oif_ref[r:r + rows, :] = jnp.dot(h, wif_ref[...], preferred_element_type=F32)

    o_ref[...] = jnp.dot(h_ref[...], w_ref[...], preferred_element_type=F32).astype(BF16)


def _residue_perm(size, dilation, inverse):
    per = size // dilation
    i = lax.broadcasted_iota(jnp.int32, (size, size), 0)
    j = lax.broadcasted_iota(jnp.int32, (size, size), 1)
    if inverse:
        src = (i & (dilation - 1)) * per + (i >> (dilation.bit_length() - 1))
    else:
        src = (i & (per - 1)) * dilation + (i >> (per.bit_length() - 1))
    return j == src


def _attn_proj_pieces(h, w_ref, o_ref, width=256):
    dilation = o_ref.shape[1]
    per = o_ref.shape[2]
    if dilation > 1:
        perm = _residue_perm(h.shape[0], dilation, False).astype(BF16)
        h = jnp.dot(perm, h, preferred_element_type=F32).astype(BF16)
        yield
    for c in range(0, o_ref.shape[3], width):
        res = jnp.dot(h, w_ref[:, c:c + width], preferred_element_type=F32).astype(BF16)
        for r in range(dilation):
            o_ref[0, r, :, c:c + width] = res[r * per:(r + 1) * per, :]
        yield


_O_I = 4 * MLSTM_WIDTH
_O_QA = _O_I + 2 * MLSTM_HEADS
_O_GATES = _O_QA + 3 * ATTN_WIDTH
D_IN = _O_GATES + 2 * D_MODEL


SPLIT_CHUNK = 512


def _split_w_in_kernel(off_ref, wt_hbm, main_ref, if_ref, q0_ref, q1_ref, q2_ref, buf_ref, sem):
    c = pl.program_id(0)
    n_main = D_MAIN // SPLIT_CHUNK
    gw = GROUP_WIDTH

    def fetch(step):
        slot = step % 2
        rows = pl.ds(pl.multiple_of(off_ref[step], 8), SPLIT_CHUNK)
        return pltpu.make_async_copy(wt_hbm.at[rows, :], buf_ref.at[slot], sem.at[slot])

    @pl.when(c == 0)
    def _():
        fetch(c).start()

    @pl.when(c + 1 < pl.num_programs(0))
    def _():
        fetch(c + 1).start()

    fetch(c).wait()
    piece = buf_ref[c % 2].T

    @pl.when(c < n_main)
    def _():
        main_ref[...] = piece.astype(BF16)

    for g, q_ref in enumerate((q0_ref, q1_ref, q2_ref)):
        for part in range(3):
            @pl.when(c == n_main + 3 * g + part)
            def _():
                q_ref[:, part * gw:(part + 1) * gw] = piece.astype(BF16)

    @pl.when(c == n_main + 3 * N_GROUPS)
    def _():
        window = piece[:, :LANES]
        lane = lax.broadcasted_iota(jnp.int32, window.shape, 1)
        i_part = jnp.where(lane < MLSTM_HEADS, window, 0.0)
        f_part = jnp.where(lane < MLSTM_HEADS,
                           pltpu.roll(window, LANES - MLSTM_HEADS, axis=1), 0.0)
        if_ref[...] = jnp.concatenate([i_part, f_part], axis=1).astype(BF16)


def _split_w_in(w_in_t):
    d = w_in_t.shape[1]
    assert w_in_t.shape[0] == D_IN
    ch = SPLIT_CHUNK
    n_main = D_MAIN // ch
    offsets = ([k * ch for k in range(_O_I // ch)]
               + [_O_GATES + k * ch for k in range(2 * D_MODEL // ch)]
               + [_O_QA + part * ATTN_WIDTH + g * GROUP_WIDTH
                  for g in range(N_GROUPS) for part in range(3)]
               + [_O_I])
    assert len(offsets) == n_main + 3 * N_GROUPS + 1 and GROUP_WIDTH == ch

    def whole(width):
        return pl.BlockSpec((d, width), lambda i, off: (0, 0))

    outs = pl.pallas_call(
        _split_w_in_kernel,
        grid_spec=pltpu.PrefetchScalarGridSpec(
            num_scalar_prefetch=1,
            grid=(len(offsets),),
            in_specs=[pl.BlockSpec(memory_space=pl.ANY)],
            out_specs=[pl.BlockSpec((d, ch), lambda i, off: (0, jnp.minimum(i, n_main - 1))),
                       whole(IF_WIDTH), whole(QKV_WIDTH), whole(QKV_WIDTH), whole(QKV_WIDTH)],
            scratch_shapes=[pltpu.VMEM((2, ch, d), F32), pltpu.SemaphoreType.DMA((2,))],
        ),
        out_shape=[jax.ShapeDtypeStruct((d, width), BF16)
                   for width in (D_MAIN, IF_WIDTH, QKV_WIDTH, QKV_WIDTH, QKV_WIDTH)],
        compiler_params=_params(("arbitrary",)),
        name="split_w_in",
    )(jnp.asarray(offsets, jnp.int32), w_in_t)
    return outs[0], outs[1], outs[2:]


def _in_proj(x2, g, w_main, w_if, *, bm=1024, bn=2048):
    n = x2.shape[0]
    bm = min(bm, n)
    kern = functools.partial(_in_proj_kernel, bm=bm, rows=256)
    return pl.pallas_call(
        kern,
        grid=(n // bm, D_MAIN // bn),
        in_specs=[
            pl.BlockSpec((bm, D_MODEL), lambda i, j: (i, 0)),
            pl.BlockSpec((1, D_MODEL), lambda i, j: (0, 0)),
            pl.BlockSpec((D_MODEL, bn), lambda i, j: (0, j)),
            pl.BlockSpec((D_MODEL, IF_WIDTH), lambda i, j: (0, 0)),
        ],
        out_specs=[
            pl.BlockSpec((bm, bn), lambda i, j: (i, j)),
            pl.BlockSpec((bm, IF_WIDTH), lambda i, j: (i, 0)),
        ],
        out_shape=[
            jax.ShapeDtypeStruct((n, D_MAIN), BF16),
            jax.ShapeDtypeStruct((n, IF_WIDTH), F32),
        ],
        scratch_shapes=[pltpu.VMEM((bm, D_MODEL), BF16)],
        compiler_params=_params(("arbitrary", "arbitrary")),
        name="in_proj",
    )(x2, g, w_main, w_if)


def _mlstm_chunk(rows, qk_ref, v_ref, og_ref, if_ref, cw_ref, cb_ref, gb_ref, ng_ref, out_ref,
                 tail_ref, c_ref, m_ref, other_work):
    L = rows.stop - rows.start
    dh = HEAD_DIM

    tail = tail_ref.shape[0]
    cur = qk_ref[0, rows, :]
    ext = jnp.concatenate([tail_ref[...], cur], axis=0)
    tail_ref[...] = cur[L - tail:, :]
    srow = lax.broadcasted_iota(jnp.int32, (L, tail + L), 0)
    scol = lax.broadcasted_iota(jnp.int32, (L, tail + L), 1)
    acc = cur.astype(F32) * cw_ref[CONV_WIDTH - 1:CONV_WIDTH, :] + cb_ref[...]
    for s in range(1, CONV_WIDTH):
        shift = (scol == srow + (tail - s)).astype(BF16)
        acc = acc + (jnp.dot(shift, ext, preferred_element_type=F32)
                     * cw_ref[CONV_WIDTH - 1 - s:CONV_WIDTH - s, :])
    qk = acc * _sigmoid(acc)
    q_all = qk[:, :MLSTM_WIDTH]
    k_all = qk[:, MLSTM_WIDTH:] * (dh ** -0.5)

    head_lane = lax.broadcasted_iota(jnp.int32, (L, LANES), 1) < MLSTM_HEADS
    gi = jnp.where(head_lane, if_ref[0, rows, :LANES] + gb_ref[:, :LANES], 0.0)
    lf = jnp.where(head_lane, _log_sigmoid(if_ref[0, rows, LANES:] + gb_ref[:, LANES:]), 0.0)
    row = lax.broadcasted_iota(jnp.int32, (L, L), 0)
    col = lax.broadcasted_iota(jnp.int32, (L, L), 1)
    causal = col <= row
    cum = jnp.dot(causal.astype(F32), lf, preferred_element_type=F32,
                  precision=lax.Precision.HIGHEST)
    rmat = gi - cum
    cmax = rmat
    trow = lax.broadcasted_iota(jnp.int32, (L, LANES), 0)
    step = 1
    while step < L:
        cmax = jnp.maximum(cmax, jnp.where(trow >= step, pltpu.roll(cmax, step, axis=0), -jnp.inf))
        step *= 2
    m_prev = m_ref[...]
    gmax = jnp.maximum(m_prev, cmax)
    s_inter = jnp.exp(m_prev - gmax)
    inv_scale = jnp.exp(-(cum + gmax))
    b_last = cum[L - 1:L, :]
    m_new = b_last + gmax[L - 1:L, :]
    w_state = jnp.exp(b_last + rmat - m_new)
    decay = jnp.exp(b_last + m_prev - m_new)
    m_ref[...] = m_new
    rmat_t = rmat.T
    ones_blk = jnp.ones((L, dh), BF16)

    def column(mat, h):
        return jnp.broadcast_to(mat[:, h:h + 1], (L, dh))

    for h in range(MLSTM_HEADS):
        hs = slice(h * dh, (h + 1) * dh)
        p = jnp.exp(jnp.where(causal, rmat_t[h:h + 1, :] - column(gmax, h), NEG))
        qh = q_all[:, hs]
        kh = k_all[:, hs]
        s = lax.dot_general(qh.astype(BF16), kh.astype(BF16), (((1,), (1,)), ((), ())),
                            preferred_element_type=F32)
        vaug = jnp.concatenate([v_ref[0, rows, hs], ones_blk], axis=1)
        c_prev = c_ref[h]
        lhs = jnp.concatenate([(p * s).astype(BF16), (column(s_inter, h) * qh).astype(BF16)],
                              axis=1)
        rhs = jnp.concatenate([vaug, c_prev.astype(BF16)], axis=0)
        nd = jnp.dot(lhs, rhs, preferred_element_type=F32)
        hh = nd[:, :dh] / jnp.maximum(jnp.abs(nd[:, dh:]), column(inv_scale, h))

        kw = (column(w_state, h) * kh).astype(BF16)
        kv = lax.dot_general(kw, vaug, (((0,), (0,)), ((), ())), preferred_element_type=F32)
        c_ref[h] = decay[:, h:h + 1] * c_prev + kv

        y = hh * lax.rsqrt(jnp.mean(hh * hh, axis=-1, keepdims=True) + RMS_EPS) * ng_ref[:, hs]
        y = y * _sigmoid(og_ref[0, rows, hs].astype(F32))
        out_ref[0, rows, hs] = y.astype(BF16)
        next(other_work, None)


def _mixer_kernel(x_ref, xg_ref, w0_ref, w1_ref, w2_ref,
                  qk_ref, v_ref, og_ref, if_ref, cw_ref, cb_ref, gb_ref, ng_ref,
                  hm_ref, a0_ref, a1_ref, a2_ref, tail_ref, c_ref, m_ref, *, chunk):
    @pl.when(pl.program_id(1) == 0)
    def _():
        tail_ref[...] = jnp.zeros_like(tail_ref)
        c_ref[...] = jnp.zeros_like(c_ref)
        m_ref[...] = jnp.zeros_like(m_ref)

    x = x_ref[...]
    h = (x * lax.rsqrt(jnp.mean(x * x, axis=-1, keepdims=True) + RMS_EPS) * xg_ref[...]).astype(BF16)
    pieces = itertools.chain(_attn_proj_pieces(h, w0_ref, a0_ref), _attn_proj_pieces(h, w1_ref, a1_ref),
                             _attn_proj_pieces(h, w2_ref, a2_ref))
    n_pieces = 3 * (QKV_WIDTH // 256) + 2
    n_heads = (x_ref.shape[0] // chunk) * MLSTM_HEADS
    def some_pieces():
        done = 0
        for head in range(1, n_heads + 1):
            while done * n_heads < head * n_pieces:
                next(pieces, None)
                done += 1
            yield

    other_work = some_pieces()
    for r0 in range(0, x_ref.shape[0], chunk):
        _mlstm_chunk(slice(r0, r0 + chunk), qk_ref, v_ref, og_ref, if_ref, cw_ref, cb_ref, gb_ref,
                     ng_ref, hm_ref, tail_ref, c_ref, m_ref, other_work)
    for _ in pieces:
        pass


def _mixer_inputs(x2, norm_g, w_qkv, proj3, ifg3, conv_w, conv_b, gate_b, mnorm_g, *,
                  rows=256, chunk=MLSTM_CHUNK):
    bsz, s, _ = proj3.shape
    w = MLSTM_WIDTH
    n_blk = s // rows
    dilations = [d for _, d in ATTN_GROUPS]

    def const(shape):
        return pl.BlockSpec(shape, lambda b, i: (0,) * len(shape))

    def weight():
        return pl.BlockSpec((D_MODEL, QKV_WIDTH), lambda b, i: (0, 0), pipeline_mode=pl.Buffered(1))

    kern = functools.partial(_mixer_kernel, chunk=chunk)
    outs = pl.pallas_call(
        kern,
        grid=(bsz, n_blk),
        in_specs=[
            pl.BlockSpec((rows, D_MODEL), lambda b, i: (b * n_blk + i, 0)),
            const((1, D_MODEL)),
            weight(), weight(), weight(),
            pl.BlockSpec((1, rows, 2 * w), lambda b, i: (b, i, COL_QK // (2 * w))),
            pl.BlockSpec((1, rows, w), lambda b, i: (b, i, COL_V // w)),
            pl.BlockSpec((1, rows, w), lambda b, i: (b, i, COL_O // w)),
            pl.BlockSpec((1, rows, IF_WIDTH), lambda b, i: (b, i, 0)),
            const((CONV_WIDTH, 2 * w)), const((1, 2 * w)), const((1, IF_WIDTH)), const((1, w)),
        ],
        out_specs=[pl.BlockSpec((1, rows, w), lambda b, i: (b, i, 0))] + [
            pl.BlockSpec((1, d, rows // d, QKV_WIDTH), lambda b, i: (b, 0, i, 0)) for d in dilations],
        out_shape=[jax.ShapeDtypeStruct((bsz, s, w), BF16)] + [
            jax.ShapeDtypeStruct((bsz, d, s // d, QKV_WIDTH), BF16) for d in dilations],
        scratch_shapes=[
            pltpu.VMEM((16, 2 * w), BF16),
            pltpu.VMEM((MLSTM_HEADS, HEAD_DIM, 2 * HEAD_DIM), F32),
            pltpu.VMEM((1, LANES), F32),
        ],
        compiler_params=_params(("arbitrary", "arbitrary")),
        name="mixer_inputs",
    )(x2, norm_g, *w_qkv, proj3, proj3, proj3, ifg3, conv_w, conv_b, gate_b, mnorm_g)
    return outs[0], outs[1:]


def _attn_kernel(q_ref, kp_ref, kc_ref, vp_ref, vc_ref, bias_ref, o_ref, lse_ref):
    qb = ATTN_BLOCK
    dh = HEAD_DIM
    n_q = q_ref.shape[2] // qb
    first = pl.program_id(2) == 0
    kcol = lax.broadcasted_iota(jnp.int32, (qb, 2 * qb), 1)
    dead = jnp.logical_and(first, kcol < qb)
    lane = lax.broadcasted_iota(jnp.int32, (qb, LANES), 1)
    for jq in range(n_q):
        rows = slice(jq * qb, (jq + 1) * qb)
        lse_all = jnp.zeros((qb, LANES), F32)
        for j in range(HEADS_PER_GROUP):
            hs = slice(j * dh, (j + 1) * dh)
            q = q_ref[0, 0][rows, hs]
            if jq == 0:
                k = jnp.concatenate([kp_ref[0, 0][:, hs], kc_ref[0, 0][0:qb, hs]], axis=0)
                v = jnp.concatenate([vp_ref[0, 0][:, hs], vc_ref[0, 0][0:qb, hs]], axis=0)
            else:
                k = kc_ref[0, 0][(jq - 1) * qb:(jq + 1) * qb, hs]
                v = vc_ref[0, 0][(jq - 1) * qb:(jq + 1) * qb, hs]
            s = lax.dot_general(q, k, (((1,), (1,)), ((), ())), preferred_element_type=F32)
            s = s * (dh ** -0.5) + bias_ref[j]
            if jq == 0:
                s = jnp.where(dead, NEG, s)
            mx = jnp.max(s, axis=1, keepdims=True)
            p = jnp.exp(s - mx)
            den = jnp.sum(p, axis=1, keepdims=True)
            o = jnp.dot(p.astype(BF16), v, preferred_element_type=F32) / den
            o_ref[0, 0, rows, hs] = o.astype(BF16)
            lse_all = jnp.where(lane == j, mx + jnp.log(den), lse_all)
        lse_ref[0, 0, rows, :] = lse_all


def _attn_group(qkv, bias, g, *, q_blocks=4):
    bsz, dilation, n, _ = qkv.shape
    gw = GROUP_WIDTH
    q_blocks = min(q_blocks, n // ATTN_BLOCK)
    run = q_blocks * ATTN_BLOCK

    def own(part, width=gw):
        return pl.BlockSpec((1, 1, run, width), lambda b, r, i: (b, r, i, part))

    def previous(part):
        return pl.BlockSpec((1, 1, ATTN_BLOCK, gw),
                            lambda b, r, i: (b, r, jnp.maximum(i * q_blocks - 1, 0), part))

    return pl.pallas_call(
        _attn_kernel,
        grid=(bsz, dilation, n // run),
        in_specs=[
            own(0),
            previous(1), own(1),
            previous(2), own(2),
            pl.BlockSpec((HEADS_PER_GROUP, ATTN_BLOCK, 2 * ATTN_BLOCK), lambda b, r, i: (0, 0, 0)),
        ],
        out_specs=[own(0), own(0, LANES)],
        out_shape=[
            jax.ShapeDtypeStruct((bsz, dilation, n, gw), BF16),
            jax.ShapeDtypeStruct((bsz, dilation, n, LANES), F32),
        ],
        compiler_params=_params(("arbitrary", "arbitrary", "arbitrary")),
        name=f"attn_g{g}",
    )(qkv, qkv, qkv, qkv, qkv, bias)


def _t5_bucket(dist):
    max_exact = REL_BUCKETS // 2
    d_f = jnp.maximum(dist, 1).astype(F32)
    large = max_exact + (jnp.log(d_f / max_exact) / math.log(REL_MAX_DIST / max_exact)
                         * (REL_BUCKETS - max_exact)).astype(jnp.int32)
    large = jnp.minimum(large, REL_BUCKETS - 1)
    return jnp.where(dist < max_exact, dist, large)


def _attn_bias(rel_bias, g, dilation):
    span = ATTN_SPAN
    buckets = _t5_bucket(jnp.arange(span + 1, dtype=jnp.int32) * dilation)
    vec = rel_bias[buckets][:, g * HEADS_PER_GROUP:(g + 1) * HEADS_PER_GROUP].T.astype(F32)
    qpos = jnp.arange(ATTN_BLOCK)[:, None]
    kpos = jnp.arange(2 * ATTN_BLOCK)[None, :]
    dist = qpos + ATTN_BLOCK - kpos
    valid = (dist >= 0) & (dist <= span)
    onehot = (dist[:, :, None] == jnp.arange(span + 1)[None, None, :]).astype(F32)
    table = jnp.einsum('qkj,hj->hqk', onehot, vec, precision=lax.Precision.HIGHEST)
    return jnp.where(valid[None], table, NEG)


def _merge_kernel(hm_ref, o0_ref, o1_ref, o2_ref, l0_ref, l1_ref, l2_ref, gm_ref, ga_ref, x_ref,
                  wbm_ref, wba_ref, wo_ref, ng_ref, rw_ref, rb_ref,
                  x1_ref, h2_ref, idx_ref, gate_ref, rank_ref, count_ref, cnt_ref):
    dh = HEAD_DIM
    bm = x_ref.shape[0]

    def token_order(ref):
        dilation = ref.shape[1]
        if dilation == 1:
            return ref[0, 0]
        perm = _residue_perm(bm, dilation, True).astype(BF16)
        blk = ref[0].reshape(bm, ref.shape[3])
        parts = [blk] if blk.dtype == BF16 else _bf16_parts(blk, 3)
        out = None
        for p in parts:
            moved = jnp.dot(perm, p, preferred_element_type=F32)
            out = moved if out is None else out + moved
        return out

    l0, l1, l2 = token_order(l0_ref), token_order(l1_ref), token_order(l2_ref)
    o0, o1, o2 = token_order(o0_ref), token_order(o1_ref), token_order(o2_ref)
    mx = jnp.maximum(jnp.maximum(l0, l1), l2)
    e0, e1, e2 = jnp.exp(l0 - mx), jnp.exp(l1 - mx), jnp.exp(l2 - mx)
    den = e0 + e1 + e2
    w0, w1, w2 = e0 / den, e1 / den, e2 / den
    parts = []
    for j in range(HEADS_PER_GROUP):
        hs = slice(j * dh, (j + 1) * dh)
        parts.append(w0[:, j:j + 1] * o0[:, hs] + w1[:, j:j + 1] * o1[:, hs]
                     + w2[:, j:j + 1] * o2[:, hs])
    ha = jnp.concatenate(parts, axis=1).astype(BF16)
    ym = jnp.dot(hm_ref[...], wbm_ref[...], preferred_element_type=F32)
    ya = jnp.dot(ha, wba_ref[...], preferred_element_type=F32)
    merged = (_sigmoid(gm_ref[...].astype(F32)) * ym + _sigmoid(ga_ref[...].astype(F32)) * ya)
    x1 = x_ref[...] + jnp.dot(merged.astype(BF16), wo_ref[...], preferred_element_type=F32)
    x1_ref[...] = x1
    h2 = x1 * lax.rsqrt(jnp.mean(x1 * x1, axis=-1, keepdims=True) + RMS_EPS) * ng_ref[...]
    _store_token_tiles(h2_ref, _pack_pairs(h2))
    h_hi, h_lo = _bf16_parts(h2, 2)
    logits = jnp.dot(jnp.concatenate([h_hi, h_lo, h_hi], axis=1), rw_ref[...],
                     preferred_element_type=F32) + rb_ref[...]
    lane = lax.broadcasted_iota(jnp.int32, logits.shape, 1)
    idx_all = jnp.zeros(logits.shape, jnp.int32)
    val_all = jnp.zeros(logits.shape, F32)
    top0 = None
    esum = None
    picks = []
    chosen = jnp.zeros(logits.shape, F32)
    for k in range(TOP_K):
        m = jnp.max(logits, axis=1, keepdims=True)
        sel = jnp.min(jnp.where(logits == m, lane, LANES), axis=1, keepdims=True)
        if k == 0:
            top0 = m
        e = jnp.exp(m - top0)
        esum = e if k == 0 else esum + e
        idx_all = jnp.where(lane == k, sel, idx_all)
        val_all = jnp.where(lane == k, e, val_all)
        pick = lane == sel
        picks.append(pick)
        chosen = chosen + pick.astype(F32)
        logits = jnp.where(pick, -jnp.inf, logits)
    idx_ref[...] = idx_all
    gate_ref[...] = val_all / esum

    @pl.when(pl.program_id(0) == 0)
    def _():
        cnt_ref[...] = jnp.zeros_like(cnt_ref)

    trow = lax.broadcasted_iota(jnp.int32, (bm, bm), 0)
    tcol = lax.broadcasted_iota(jnp.int32, (bm, bm), 1)
    earlier = jnp.dot((tcol < trow).astype(BF16), chosen.astype(BF16),
                      preferred_element_type=F32) + cnt_ref[...]
    rank_all = jnp.zeros(logits.shape, jnp.int32)
    for k in range(TOP_K):
        r = jnp.sum(jnp.where(picks[k], earlier, 0.0), axis=1, keepdims=True)
        rank_all = jnp.where(lane == k, r.astype(jnp.int32), rank_all)
    rank_ref[...] = rank_all
    cnt_ref[...] = cnt_ref[...] + jnp.sum(chosen, axis=0, keepdims=True)
    count_ref[...] = jnp.broadcast_to(cnt_ref[...], count_ref.shape)


def _merge(hm, outs, lses, proj, x2, wbm, wba, wo, ng, rw, rb, *, bm=256):
    n = x2.shape[0]
    d = D_MODEL
    gcol = COL_GATES // d
    n_blk = outs[0].shape[2] // bm

    def rows(width):
        return pl.BlockSpec((bm, width), lambda i: (i, 0))

    def full(a, b):
        return pl.BlockSpec((a, b), lambda i: (0, 0), pipeline_mode=pl.Buffered(1))

    def residue(arr):
        dilation, width = arr.shape[1], arr.shape[3]
        return pl.BlockSpec((1, dilation, bm // dilation, width),
                            lambda i: (i // n_blk, 0, i % n_blk, 0))

    return pl.pallas_call(
        _merge_kernel,
        grid=(n // bm,),
        in_specs=[
            rows(MLSTM_WIDTH),
            residue(outs[0]), residue(outs[1]), residue(outs[2]),
            residue(lses[0]), residue(lses[1]), residue(lses[2]),
            pl.BlockSpec((bm, d), lambda i: (i, gcol)),
            pl.BlockSpec((bm, d), lambda i: (i, gcol + 1)),
            rows(d),
            full(MLSTM_WIDTH, d), full(GROUP_WIDTH, d), full(d, d),
            full(1, d), full(3 * d, LANES), full(1, LANES),
        ],
        out_specs=[rows(d), pl.BlockSpec((bm * PACK_ROWS, LANES), lambda i: (i, 0)),
                   rows(LANES), rows(LANES), rows(LANES),
                   pl.BlockSpec((8, LANES), lambda i: (0, 0))],
        out_shape=[
            jax.ShapeDtypeStruct((n, d), F32),
            jax.ShapeDtypeStruct((n * PACK_ROWS, LANES), jnp.uint32),
            jax.ShapeDtypeStruct((n, LANES), jnp.int32),
            jax.ShapeDtypeStruct((n, LANES), F32),
            jax.ShapeDtypeStruct((n, LANES), jnp.int32),
            jax.ShapeDtypeStruct((8, LANES), F32),
        ],
        scratch_shapes=[pltpu.VMEM((1, LANES), F32)],
        compiler_params=_params(("arbitrary",)),
        name="merge",
    )(hm, outs[0], outs[1], outs[2], lses[0], lses[1], lses[2], proj, proj, x2,
      wbm, wba, wo, ng, rw, rb)


PACK_ROWS = 8


def _pack_pairs(x):
    w = x.shape[1] // 2
    lo = lax.bitcast_convert_type(x[:, :w].astype(BF16).astype(F32), jnp.uint32) >> 16
    hi = lax.bitcast_convert_type(x[:, w:].astype(BF16).astype(F32), jnp.uint32)
    return (hi & jnp.uint32(0xFFFF0000)) | lo


def _unpack_pairs(words):
    lo = lax.bitcast_convert_type(words << 16, F32)
    hi = lax.bitcast_convert_type(words & jnp.uint32(0xFFFF0000), F32)
    return lo, hi


def _store_token_tiles(ref, words):
    rows = words.shape[0]
    for s in range(PACK_ROWS):
        ref[pl.ds(s, rows, stride=PACK_ROWS), :] = words[:, s * LANES:(s + 1) * LANES]


def _dispatch_kernel(fs_ref, fl_ref, nu_ref, dest_ref, hp_ref, xb_hbm, zero_ref, sem, zsem, *,
                     tokens, tm, n_blk):
    pr = PACK_ROWS
    fill_sizes = [1 << b for b in reversed(range((tm - 1).bit_length()))]

    def fill_copies(e):
        off = fs_ref[e]
        for p in fill_sizes:
            take = (fl_ref[e] & p) != 0
            dst = xb_hbm.at[pl.ds(pl.multiple_of(off * pr, pr), p * pr), :]
            yield take, pltpu.make_async_copy(zero_ref.at[pl.ds(0, p * pr), :], dst, zsem)
            off = off + jnp.where(take, p, 0)

    def block_copy(b):
        dst = xb_hbm.at[pl.ds(pl.multiple_of(b * (tm * pr), tm * pr), tm * pr), :]
        return pltpu.make_async_copy(zero_ref, dst, zsem)

    @pl.when(pl.program_id(0) == 0)
    def _():
        zero_ref[...] = jnp.zeros_like(zero_ref)
        for start in (True, False):
            def per_expert(e, carry):
                for take, cp in fill_copies(e):
                    @pl.when(take)
                    def _():
                        cp.start() if start else cp.wait()
                return carry

            def per_block(b, carry):
                block_copy(b).start() if start else block_copy(b).wait()
                return carry

            lax.fori_loop(0, N_EXPERTS, per_expert, 0)
            lax.fori_loop(nu_ref[0], n_blk, per_block, 0)

    def issue(t, carry):
        src = hp_ref.at[pl.ds(pl.multiple_of(t * pr, pr), pr), :]
        for k in range(TOP_K):
            slot = dest_ref[t * TOP_K + k]
            pltpu.make_async_copy(src, xb_hbm.at[pl.ds(pl.multiple_of(slot * pr, pr), pr), :],
                                  sem).start(priority=k % 2)
        return carry

    lax.fori_loop(0, tokens, issue, 0)
    for k in range(TOP_K):
        pltpu.make_async_copy(hp_ref, xb_hbm.at[pl.ds(0, tokens * pr), :], sem).wait()


def _dispatch(fill_start, fill_len, n_used, dest, h2p, n_pad, *, tokens=512, tm=MOE_TILE):
    pr = PACK_ROWS
    n_tok = h2p.shape[0] // pr
    kern = functools.partial(_dispatch_kernel, tokens=tokens, tm=tm, n_blk=n_pad // tm)
    return pl.pallas_call(
        kern,
        grid_spec=pltpu.PrefetchScalarGridSpec(
            num_scalar_prefetch=3,
            grid=(n_tok // tokens,),
            in_specs=[
                pl.BlockSpec((tokens * TOP_K,), lambda i, fs, fl, nu: (i,),
                             memory_space=pltpu.SMEM),
                pl.BlockSpec((tokens * pr, LANES), lambda i, fs, fl, nu: (i, 0)),
            ],
            out_specs=pl.BlockSpec(memory_space=pl.ANY),
            scratch_shapes=[pltpu.VMEM((tm * pr, LANES), jnp.uint32),
                            pltpu.SemaphoreType.DMA(()), pltpu.SemaphoreType.DMA(())],
        ),
        out_shape=jax.ShapeDtypeStruct((n_pad * pr, LANES), jnp.uint32),
        compiler_params=_params(("arbitrary",)),
        name="dispatch",
    )(fill_start, fill_len, n_used, dest, h2p)


def _cast_rows(src_ref, dst_ref, rows):
    total = src_ref.shape[0]

    def body(i, carry):
        r = pl.multiple_of(i * rows, rows)
        dst_ref[pl.ds(r, rows), :] = src_ref[pl.ds(r, rows), :].astype(BF16)
        return carry

    lax.fori_loop(0, total // rows, body, 0)


def _stream_expert_weights(be_ref, nu_ref, ne_ref, copies, convert):
    k, j = pl.program_id(0), pl.program_id(1)
    expert = be_ref[j]
    used = j < nu_ref[0]
    run_start = jnp.logical_or(j == 0, expert != be_ref[jnp.maximum(j - 1, 0)])

    @pl.when(jnp.logical_and(k == 0, j == 0))
    def _():
        for cp in copies(expert, k):
            cp.start()

    @pl.when(jnp.logical_and(used, run_start))
    def _():
        for cp in copies(expert, k):
            cp.wait()
        convert()
        last_run = ne_ref[j] < 0
        nxt_e = jnp.where(last_run, be_ref[0], ne_ref[j])
        nxt_k = jnp.where(last_run, k + 1, k)

        @pl.when(nxt_k < pl.num_programs(0))
        def _():
            for cp in copies(nxt_e, nxt_k):
                cp.start()

    return used


def _per_block_rows(used, bv_ref, o_ref, compute, store=None, rows_per_token=1):
    tm = o_ref.shape[0] // rows_per_token
    need = (bv_ref[pl.program_id(1)] + (MOE_ROW_STEP - 1)) // MOE_ROW_STEP

    def run(rows):
        res = compute(rows)
        head = o_ref.at[pl.ds(0, rows * rows_per_token), :]
        if store is None:
            head[...] = res
        else:
            store(head, res)
        if rows < tm:
            tail = o_ref.at[pl.ds(rows * rows_per_token, (tm - rows) * rows_per_token), :]
            tail[...] = jnp.zeros_like(tail)

    for q in range(1, tm // MOE_ROW_STEP + 1):
        @pl.when(jnp.logical_and(used, need == q))
        def _():
            run(q * MOE_ROW_STEP)

    @pl.when(jnp.logical_not(used))
    def _():
        o_ref[...] = jnp.zeros_like(o_ref)


def _gate_up_kernel(be_ref, nu_ref, ne_ref, bv_ref, x_ref, w_hbm, bg_ref, bu_ref, o_ref,
                    stage_ref, w_bf, sem):
    tf = o_ref.shape[1]
    kt = pl.num_programs(0)

    def copies(expert, k):
        return [pltpu.make_async_copy(
            w_hbm.at[expert, :, pl.ds(pl.multiple_of((half * kt + k) * tf, tf), tf)],
            stage_ref.at[half], sem.at[half]) for half in range(2)]

    def convert():
        for half in range(2):
            _cast_rows(stage_ref.at[half], w_bf.at[half], 256)

    used = _stream_expert_weights(be_ref, nu_ref, ne_ref, copies, convert)

    def compute(rows):
        halves = [_unpack_pairs(x_ref[pl.ds(s, rows, stride=PACK_ROWS), :])
                  for s in range(PACK_ROWS)]
        x = jnp.concatenate([h[0].astype(BF16) for h in halves]
                            + [h[1].astype(BF16) for h in halves], axis=1)
        gate = jnp.dot(x, w_bf[0], preferred_element_type=F32) + bg_ref[0]
        up = jnp.dot(x, w_bf[1], preferred_element_type=F32) + bu_ref[0]
        gate = jnp.minimum(gate, SWIGLU_LIMIT)
        up = jnp.clip(up, -SWIGLU_LIMIT, SWIGLU_LIMIT)
        glu = gate * _sigmoid(SWIGLU_ALPHA * gate)
        return ((up + 1.0) * glu).astype(BF16)

    _per_block_rows(used, bv_ref, o_ref, compute)


def _used_block(j, nu):
    return jnp.minimum(j, nu[0] - 1)


def _gate_up(sched, xb, w_gate_up, b_gate_up3, *, tm=MOE_TILE, tf=1024):
    n_pad, d = xb.shape[0] // PACK_ROWS, D_MODEL
    kt = D_FF // tf

    def bspec(off):
        return pl.BlockSpec((1, 1, tf),
                            lambda k, j, be, nu, ne, bv: (be[_used_block(j, nu)], 0, off + k))

    return pl.pallas_call(
        _gate_up_kernel,
        grid_spec=pltpu.PrefetchScalarGridSpec(
            num_scalar_prefetch=4,
            grid=(kt, n_pad // tm),
            in_specs=[
                pl.BlockSpec((tm * PACK_ROWS, LANES),
                             lambda k, j, be, nu, ne, bv: (_used_block(j, nu), 0)),
                pl.BlockSpec(memory_space=pl.ANY),
                bspec(0), bspec(kt),
            ],
            out_specs=pl.BlockSpec((tm, tf), lambda k, j, be, nu, ne, bv: (j, k)),
            scratch_shapes=[pltpu.VMEM((2, d, tf), F32), pltpu.VMEM((2, d, tf), BF16),
                            pltpu.SemaphoreType.DMA((2,))],
        ),
        out_shape=jax.ShapeDtypeStruct((n_pad, D_FF), BF16),
        compiler_params=_params(("arbitrary", "arbitrary")),
        name="gate_up",
    )(*sched, xb, w_gate_up, b_gate_up3, b_gate_up3)


def _down_kernel(be_ref, nu_ref, ne_ref, bv_ref, h_ref, w_hbm, bd_ref, o_ref, stage_ref, w_bf,
                 sem):
    def copies(expert, k):
        del k
        return [pltpu.make_async_copy(w_hbm.at[expert], stage_ref, sem.at[0])]

    def convert():
        _cast_rows(stage_ref, w_bf, 256)

    used = _stream_expert_weights(be_ref, nu_ref, ne_ref, copies, convert)

    def compute(rows):
        y = jnp.dot(h_ref[0:rows, :], w_bf[...], preferred_element_type=F32) + bd_ref[0]
        return _pack_pairs(y)

    _per_block_rows(used, bv_ref, o_ref, compute, store=_store_token_tiles,
                    rows_per_token=PACK_ROWS)


def _down(sched, hb, w_down, b_down3, *, tm=MOE_TILE):
    n_pad, f = hb.shape
    d = D_MODEL
    return pl.pallas_call(
        _down_kernel,
        grid_spec=pltpu.PrefetchScalarGridSpec(
            num_scalar_prefetch=4,
            grid=(1, n_pad // tm),
            in_specs=[
                pl.BlockSpec((tm, f), lambda k, j, be, nu, ne, bv: (_used_block(j, nu), 0)),
                pl.BlockSpec(memory_space=pl.ANY),
                pl.BlockSpec((1, 1, d),
                             lambda k, j, be, nu, ne, bv: (be[_used_block(j, nu)], 0, 0)),
            ],
            out_specs=pl.BlockSpec((tm * PACK_ROWS, LANES), lambda k, j, be, nu, ne, bv: (j, 0)),
            scratch_shapes=[pltpu.VMEM((f, d), F32), pltpu.VMEM((f, d), BF16),
                            pltpu.SemaphoreType.DMA((1,))],
        ),
        out_shape=jax.ShapeDtypeStruct((n_pad * PACK_ROWS, LANES), jnp.uint32),
        compiler_params=_params(("arbitrary", "arbitrary")),
        name="down",
    )(*sched, hb, w_down, b_down3)


def _combine_kernel(dest_ref, dest_next_ref, y_hbm, gate_ref, x1_ref, ng_ref, o_ref, buf_ref, sem,
                    *, rows):
    pr = PACK_ROWS
    slab = 64
    half = x1_ref.shape[1] // 2
    i = pl.program_id(0)
    slot_rows = TOP_K * rows * pr
    cur = i % 2
    base = pl.multiple_of(cur * slot_rows, slot_rows)
    nxt_base = pl.multiple_of((1 - cur) * slot_rows, slot_rows)

    def gather(idx_ref, dst_base, buf, t):
        for k in range(TOP_K):
            slot = idx_ref[t * TOP_K + k]
            pltpu.make_async_copy(
                y_hbm.at[pl.ds(pl.multiple_of(slot * pr, pr), pr), :],
                buf_ref.at[pl.ds(pl.multiple_of(dst_base + (k * rows + t) * pr, pr), pr), :],
                sem.at[buf]).start(priority=k % 2)

    @pl.when(i == 0)
    def _():
        def first(t, carry):
            gather(dest_ref, 0, 0, t)
            return carry

        lax.fori_loop(0, rows, first, 0)

    pltpu.make_async_copy(y_hbm.at[pl.ds(0, slot_rows), :],
                          buf_ref.at[pl.ds(base, slot_rows), :], sem.at[cur]).wait()

    def reduce_slab(t0):
        gates = [gate_ref[pl.ds(t0, slab), k:k + 1] for k in range(TOP_K)]
        lows, highs = [], []
        for s in range(pr):
            lo = x1_ref[pl.ds(t0, slab), s * LANES:(s + 1) * LANES]
            hi = x1_ref[pl.ds(t0, slab), half + s * LANES:half + (s + 1) * LANES]
            for k in range(TOP_K):
                y_lo, y_hi = _unpack_pairs(
                    buf_ref[pl.ds(base + (k * rows + t0) * pr + s, slab, stride=pr), :])
                lo = lo + gates[k] * y_lo
                hi = hi + gates[k] * y_hi
            lows.append(lo)
            highs.append(hi)
        acc = jnp.concatenate(lows + highs, axis=1)
        o_ref[pl.ds(t0, slab), :] = (
            acc * lax.rsqrt(jnp.mean(acc * acc, axis=-1, keepdims=True) + RMS_EPS) * ng_ref[...])

    def trip_with_prefetch(j, carry):
        t0 = pl.multiple_of(j * slab, slab)
        for tt in range(slab):
            gather(dest_next_ref, nxt_base, 1 - cur, t0 + tt)
        reduce_slab(t0)
        return carry

    def trip(j, carry):
        reduce_slab(pl.multiple_of(j * slab, slab))
        return carry

    has_next = i + 1 < pl.num_programs(0)

    @pl.when(has_next)
    def _():
        lax.fori_loop(0, rows // slab, trip_with_prefetch, 0)

    @pl.when(jnp.logical_not(has_next))
    def _():
        lax.fori_loop(0, rows // slab, trip, 0)


def _combine(dest, yb, gates, x1, ng, *, rows=512):
    n, d = x1.shape
    kern = functools.partial(_combine_kernel, rows=rows)
    n_steps = n // rows
    return pl.pallas_call(
        kern,
        grid=(n_steps,),
        in_specs=[
            pl.BlockSpec((rows * TOP_K,), lambda i: (i,), memory_space=pltpu.SMEM),
            pl.BlockSpec((rows * TOP_K,), lambda i: (jnp.minimum(i + 1, n_steps - 1),),
                         memory_space=pltpu.SMEM),
            pl.BlockSpec(memory_space=pl.ANY),
            pl.BlockSpec((rows, LANES), lambda i: (i, 0)),
            pl.BlockSpec((rows, d), lambda i: (i, 0)),
            pl.BlockSpec((1, d), lambda i: (0, 0)),
        ],
        out_specs=pl.BlockSpec((rows, d), lambda i: (i, 0)),
        out_shape=jax.ShapeDtypeStruct((n, d), F32),
        scratch_shapes=[pltpu.VMEM((2 * TOP_K * rows * PACK_ROWS, LANES), jnp.uint32),
                        pltpu.SemaphoreType.DMA((2,))],
        compiler_params=_params(("arbitrary",)),
        name="combine",
    )(dest, dest, yb, gates, x1, ng)


def _routing(top_idx, rank, counts, tm):
    n_tok = top_idx.shape[0]
    n_asg = n_tok * TOP_K
    e_flat = top_idx.reshape(n_asg)
    padded = (counts + tm - 1) // tm * tm
    pend = jnp.cumsum(padded)
    pstart = pend - padded
    experts = jnp.arange(N_EXPERTS, dtype=jnp.int32)
    start_of = jnp.sum(jnp.where(e_flat[:, None] == experts[None, :], pstart[None, :], 0), axis=1)
    dest = (start_of + rank.reshape(n_asg)).astype(jnp.int32)
    n_blk = -(-(n_asg + N_EXPERTS * (tm - 1)) // tm)
    n_pad = n_blk * tm
    fill_start = (pstart + counts).astype(jnp.int32)
    fill_len = (padded - counts).astype(jnp.int32)
    block_start = jnp.arange(n_blk, dtype=jnp.int32) * tm
    block_e = jnp.minimum(
        jnp.sum((pend[None, :] <= block_start[:, None]).astype(jnp.int32), axis=1),
        N_EXPERTS - 1).astype(jnp.int32)
    n_used = (pend[-1:] // tm).astype(jnp.int32)
    run_end = pend[block_e] // tm
    next_e = jnp.where(run_end < n_used[0], block_e[jnp.minimum(run_end, n_blk - 1)], -1)
    block_rows = jnp.clip(fill_start[block_e] - block_start, 0, tm)
    sched = (block_e, n_used, next_e.astype(jnp.int32), block_rows.astype(jnp.int32))
    return dest, fill_start, fill_len, sched, n_pad


def _layer(x, norm_mix_g, w_in, conv_w, conv_b, igate_b, fgate_b, mlstm_norm_g, rel_bias,
           w_branch_mlstm, w_branch_attn, w_out, norm_moe_g, router_w, router_b,
           w_gate_up, b_gate_up, w_down, b_down, out_norm_g):
    bsz, s, d = x.shape
    n = bsz * s
    x2 = x.reshape(n, d)
    w_main, w_if, w_qkv = _split_w_in(jnp.swapaxes(w_in, 0, 1))
    bpad = jnp.zeros((LANES - MLSTM_HEADS,), F32)
    gate_b = jnp.concatenate([igate_b, bpad, fgate_b, bpad]).reshape(1, IF_WIDTH)

    norm_g = norm_mix_g.reshape(1, d)
    proj, ifg = _in_proj(x2, norm_g, w_main, w_if)
    proj3 = proj.reshape(bsz, s, D_MAIN)
    hm, qkvs = _mixer_inputs(x2, norm_g, w_qkv, proj3, ifg.reshape(bsz, s, IF_WIDTH), conv_w,
                             conv_b.reshape(1, -1), gate_b, mlstm_norm_g.reshape(1, -1))
    hm = hm.reshape(n, MLSTM_WIDTH)

    outs, lses = [], []
    for g, (_, dilation) in enumerate(ATTN_GROUPS):
        o_g_, lse_g = _attn_group(qkvs[g], _attn_bias(rel_bias, g, dilation), g)
        outs.append(o_g_)
        lses.append(lse_g)

    rw = jnp.concatenate([router_w, jnp.zeros((d, LANES - N_EXPERTS), F32)], axis=1)
    rw_hi = rw.astype(BF16)
    rw_lo = (rw - rw_hi.astype(F32)).astype(BF16)
    rw = jnp.concatenate([rw_hi, rw_hi, rw_lo], axis=0)
    rb = jnp.concatenate([router_b, jnp.full((LANES - N_EXPERTS,), NEG, F32)]).reshape(1, LANES)
    x1, h2, idx, gates, rank, counts = _merge(
        hm, outs, lses, proj, x2, w_branch_mlstm.astype(BF16), w_branch_attn.astype(BF16),
        w_out.astype(BF16), norm_moe_g.reshape(1, d), rw, rb)

    dest, fill_start, fill_len, sched, n_pad = _routing(
        idx[:, :TOP_K], rank[:, :TOP_K], counts[0, :N_EXPERTS].astype(jnp.int32), MOE_TILE)
    xb = _dispatch(fill_start, fill_len, sched[1], dest, h2, n_pad)
    hb = _gate_up(sched, xb, w_gate_up, b_gate_up.reshape(N_EXPERTS, 1, 2 * D_FF))
    yb = _down(sched, hb, w_down, b_down.reshape(N_EXPERTS, 1, d))
    out = _combine(dest, yb, gates, x1, out_norm_g.reshape(1, d))
    return out.reshape(bsz, s, d)


def kernel(x, norm_mix_g, w_in, conv_w, conv_b, igate_b, fgate_b, mlstm_norm_g, rel_bias,
           w_branch_mlstm, w_branch_attn, w_out, norm_moe_g, router_w, router_b,
           w_gate_up, b_gate_up, w_down, b_down, norm_final_g):
    assert w_in.shape[0] == 1, "single-layer block"
    return _layer(x, norm_mix_g[0], w_in[0], conv_w[0], conv_b[0], igate_b[0], fgate_b[0],
                  mlstm_norm_g[0], rel_bias, w_branch_mlstm[0], w_branch_attn[0], w_out[0],
                  norm_moe_g[0], router_w[0], router_b[0], w_gate_up[0], b_gate_up[0],
                  w_down[0], b_down[0], norm_final_g)
```

```python
import functools
import itertools
import math

import jax
import jax.numpy as jnp
from jax import lax
from jax.experimental import pallas as pl
from jax.experimental.pallas import tpu as pltpu

F32 = jnp.float32
BF16 = jnp.bfloat16

D_MODEL = 2048
MLSTM_HEADS = 8
HEAD_DIM = 128
MLSTM_WIDTH = MLSTM_HEADS * HEAD_DIM
CONV_WIDTH = 4
ATTN_GROUPS = ((128, 1), (512, 4), (2048, 16))
N_GROUPS = 3
HEADS_PER_GROUP = 4
ATTN_HEADS = HEADS_PER_GROUP * N_GROUPS
ATTN_WIDTH = ATTN_HEADS * HEAD_DIM
GROUP_WIDTH = HEADS_PER_GROUP * HEAD_DIM
ATTN_BLOCK = 128
ATTN_SPAN = 128
REL_BUCKETS = 32
REL_MAX_DIST = 2048
N_EXPERTS = 32
TOP_K = 4
D_FF = D_MODEL
SWIGLU_LIMIT = 7.0
SWIGLU_ALPHA = 1.702
RMS_EPS = 1e-6
NEG = -1e30

COL_QK = 0
COL_V = 2 * MLSTM_WIDTH
COL_O = COL_V + MLSTM_WIDTH
COL_GATES = COL_O + MLSTM_WIDTH
D_MAIN = COL_GATES + 2 * D_MODEL
QKV_WIDTH = 3 * GROUP_WIDTH
LANES = 128
IF_WIDTH = 2 * LANES

MLSTM_CHUNK = 128
MOE_TILE = 512
MOE_ROW_STEP = 128
VMEM_LIMIT = 56 * 1024 * 1024


def _sigmoid(x):
    return 0.5 * jnp.tanh(0.5 * x) + 0.5


def _bf16_parts(x, n_parts):
    parts = []
    for _ in range(n_parts):
        p = x.astype(BF16)
        parts.append(p)
        x = x - p.astype(F32)
    return parts


def _log_sigmoid(x):
    return -(jnp.maximum(-x, 0.0) + jnp.log1p(jnp.exp(-jnp.abs(x))))


def _params(sem):
    return pltpu.CompilerParams(dimension_semantics=sem, vmem_limit_bytes=VMEM_LIMIT)


def _in_proj_kernel(x_ref, g_ref, w_ref, wif_ref, o_ref, oif_ref, h_ref, *, bm, rows):
    @pl.when(pl.program_id(1) == 0)
    def _():
        for r in range(0, bm, rows):
            x = x_ref[r:r + rows, :]
            ms = jnp.mean(x * x, axis=-1, keepdims=True)
            h = (x * lax.rsqrt(ms + RMS_EPS) * g_ref[...]).astype(BF16)
            h_ref[r:r + rows, :] = h
            oif_ref[r:r + rows, :] = jnp.dot(h, wif_ref[...], preferred_element_type=F32)

    o_ref[...] = jnp.dot(h_ref[...], w_ref[...], preferred_element_type=F32).astype(BF16)


def _residue_perm(size, dilation, inverse):
    per = size // dilation
    i = lax.broadcasted_iota(jnp.int32, (size, size), 0)
    j = lax.broadcasted_iota(jnp.int32, (size, size), 1)
    if inverse:
        src = (i & (dilation - 1)) * per + (i >> (dilation.bit_length() - 1))
    else:
        src = (i & (per - 1)) * dilation + (i >> (per.bit_length() - 1))
    return j == src


def _attn_proj_pieces(h, w_ref, o_ref, width=256):
    dilation = o_ref.shape[1]
    per = o_ref.shape[2]
    if dilation > 1:
        perm = _residue_perm(h.shape[0], dilation, False).astype(BF16)
        h = jnp.dot(perm, h, preferred_element_type=F32).astype(BF16)
        yield
    for c in range(0, o_ref.shape[3], width):
        res = jnp.dot(h, w_ref[:, c:c + width], preferred_element_type=F32).astype(BF16)
        for r in range(dilation):
            o_ref[0, r, :, c:c + width] = res[r * per:(r + 1) * per, :]
        yield


_O_I = 4 * MLSTM_WIDTH
_O_QA = _O_I + 2 * MLSTM_HEADS
_O_GATES = _O_QA + 3 * ATTN_WIDTH
D_IN = _O_GATES + 2 * D_MODEL


SPLIT_CHUNK = 512


def _split_w_in_kernel(off_ref, wt_hbm, main_ref, if_ref, q0_ref, q1_ref, q2_ref, buf_ref, sem):
    c = pl.program_id(0)
    n_main = D_MAIN // SPLIT_CHUNK
    gw = GROUP_WIDTH

    def fetch(step):
        slot = step % 2
        rows = pl.ds(pl.multiple_of(off_ref[step], 8), SPLIT_CHUNK)
        return pltpu.make_async_copy(wt_hbm.at[rows, :], buf_ref.at[slot], sem.at[slot])

    @pl.when(c == 0)
    def _():
        fetch(c).start()

    @pl.when(c + 1 < pl.num_programs(0))
    def _():
        fetch(c + 1).start()

    fetch(c).wait()
    piece = buf_ref[c % 2].T

    @pl.when(c < n_main)
    def _():
        main_ref[...] = piece.astype(BF16)

    for g, q_ref in enumerate((q0_ref, q1_ref, q2_ref)):
        for part in range(3):
            @pl.when(c == n_main + 3 * g + part)
            def _():
                q_ref[:, part * gw:(part + 1) * gw] = piece.astype(BF16)

    @pl.when(c == n_main + 3 * N_GROUPS)
    def _():
        window = piece[:, :LANES]
        lane = lax.broadcasted_iota(jnp.int32, window.shape, 1)
        i_part = jnp.where(lane < MLSTM_HEADS, window, 0.0)
        f_part = jnp.where(lane < MLSTM_HEADS,
                           pltpu.roll(window, LANES - MLSTM_HEADS, axis=1), 0.0)
        if_ref[...] = jnp.concatenate([i_part, f_part], axis=1).astype(BF16)


def _split_w_in(w_in_t):
    d = w_in_t.shape[1]
    assert w_in_t.shape[0] == D_IN
    ch = SPLIT_CHUNK
    n_main = D_MAIN // ch
    offsets = ([k * ch for k in range(_O_I // ch)]
               + [_O_GATES + k * ch for k in range(2 * D_MODEL // ch)]
               + [_O_QA + part * ATTN_WIDTH + g * GROUP_WIDTH
                  for g in range(N_GROUPS) for part in range(3)]
               + [_O_I])
    assert len(offsets) == n_main + 3 * N_GROUPS + 1 and GROUP_WIDTH == ch

    def whole(width):
        return pl.BlockSpec((d, width), lambda i, off: (0, 0))

    outs = pl.pallas_call(
        _split_w_in_kernel,
        grid_spec=pltpu.PrefetchScalarGridSpec(
            num_scalar_prefetch=1,
            grid=(len(offsets),),
            in_specs=[pl.BlockSpec(memory_space=pl.ANY)],
            out_specs=[pl.BlockSpec((d, ch), lambda i, off: (0, jnp.minimum(i, n_main - 1))),
                       whole(IF_WIDTH), whole(QKV_WIDTH), whole(QKV_WIDTH), whole(QKV_WIDTH)],
            scratch_shapes=[pltpu.VMEM((2, ch, d), F32), pltpu.SemaphoreType.DMA((2,))],
        ),
        out_shape=[jax.ShapeDtypeStruct((d, width), BF16)
                   for width in (D_MAIN, IF_WIDTH, QKV_WIDTH, QKV_WIDTH, QKV_WIDTH)],
        compiler_params=_params(("arbitrary",)),
        name="split_w_in",
    )(jnp.asarray(offsets, jnp.int32), w_in_t)
    return outs[0], outs[1], outs[2:]


def _in_proj(x2, g, w_main, w_if, *, bm=1024, bn=2048):
    n = x2.shape[0]
    bm = min(bm, n)
    kern = functools.partial(_in_proj_kernel, bm=bm, rows=256)
    return pl.pallas_call(
        kern,
        grid=(n // bm, D_MAIN // bn),
        in_specs=[
            pl.BlockSpec((bm, D_MODEL), lambda i, j: (i, 0)),
            pl.BlockSpec((1, D_MODEL), lambda i, j: (0, 0)),
            pl.BlockSpec((D_MODEL, bn), lambda i, j: (0, j)),
            pl.BlockSpec((D_MODEL, IF_WIDTH), lambda i, j: (0, 0)),
        ],
        out_specs=[
            pl.BlockSpec((bm, bn), lambda i, j: (i, j)),
            pl.BlockSpec((bm, IF_WIDTH), lambda i, j: (i, 0)),
        ],
        out_shape=[
            jax.ShapeDtypeStruct((n, D_MAIN), BF16),
            jax.ShapeDtypeStruct((n, IF_WIDTH), F32),
        ],
        scratch_shapes=[pltpu.VMEM((bm, D_MODEL), BF16)],
        compiler_params=_params(("arbitrary", "arbitrary")),
        name="in_proj",
    )(x2, g, w_main, w_if)


def _mlstm_chunk(rows, qk_ref, v_ref, og_ref, if_ref, cw_ref, cb_ref, gb_ref, ng_ref, out_ref,
                 tail_ref, c_ref, m_ref, other_work):
    L = rows.stop - rows.start
    dh = HEAD_DIM

    tail = tail_ref.shape[0]
    cur = qk_ref[0, rows, :]
    ext = jnp.concatenate([tail_ref[...], cur], axis=0)
    tail_ref[...] = cur[L - tail:, :]
    srow = lax.broadcasted_iota(jnp.int32, (L, tail + L), 0)
    scol = lax.broadcasted_iota(jnp.int32, (L, tail + L), 1)
    acc = cur.astype(F32) * cw_ref[CONV_WIDTH - 1:CONV_WIDTH, :] + cb_ref[...]
    for s in range(1, CONV_WIDTH):
        shift = (scol == srow + (tail - s)).astype(BF16)
        acc = acc + (jnp.dot(shift, ext, preferred_element_type=F32)
                     * cw_ref[CONV_WIDTH - 1 - s:CONV_WIDTH - s, :])
    qk = acc * _sigmoid(acc)
    q_all = qk[:, :MLSTM_WIDTH]
    k_all = qk[:, MLSTM_WIDTH:] * (dh ** -0.5)

    head_lane = lax.broadcasted_iota(jnp.int32, (L, LANES), 1) < MLSTM_HEADS
    gi = jnp.where(head_lane, if_ref[0, rows, :LANES] + gb_ref[:, :LANES], 0.0)
    lf = jnp.where(head_lane, _log_sigmoid(if_ref[0, rows, LANES:] + gb_ref[:, LANES:]), 0.0)
    row = lax.broadcasted_iota(jnp.int32, (L, L), 0)
    col = lax.broadcasted_iota(jnp.int32, (L, L), 1)
    causal = col <= row
    cum = jnp.dot(causal.astype(F32), lf, preferred_element_type=F32,
                  precision=lax.Precision.HIGHEST)
    rmat = gi - cum
    cmax = rmat
    trow = lax.broadcasted_iota(jnp.int32, (L, LANES), 0)
    step = 1
    while step < L:
        cmax = jnp.maximum(cmax, jnp.where(trow >= step, pltpu.roll(cmax, step, axis=0), -jnp.inf))
        step *= 2
    m_prev = m_ref[...]
    gmax = jnp.maximum(m_prev, cmax)
    s_inter = jnp.exp(m_prev - gmax)
    inv_scale = jnp.exp(-(cum + gmax))
    b_last = cum[L - 1:L, :]
    m_new = b_last + gmax[L - 1:L, :]
    w_state = jnp.exp(b_last + rmat - m_new)
    decay = jnp.exp(b_last + m_prev - m_new)
    m_ref[...] = m_new
    rmat_t = rmat.T
    ones_blk = jnp.ones((L, dh), BF16)

    def column(mat, h):
        return jnp.broadcast_to(mat[:, h:h + 1], (L, dh))

    for h in range(MLSTM_HEADS):
        hs = slice(h * dh, (h + 1) * dh)
        p = jnp.exp(jnp.where(causal, rmat_t[h:h + 1, :] - column(gmax, h), NEG))
        qh = q_all[:, hs]
        kh = k_all[:, hs]
        s = lax.dot_general(qh.astype(BF16), kh.astype(BF16), (((1,), (1,)), ((), ())),
                            preferred_element_type=F32)
        vaug = jnp.concatenate([v_ref[0, rows, hs], ones_blk], axis=1)
        c_prev = c_ref[h]
        lhs = jnp.concatenate([(p * s).astype(BF16), (column(s_inter, h) * qh).astype(BF16)],
                              axis=1)
        rhs = jnp.concatenate([vaug, c_prev.astype(BF16)], axis=0)
        nd = jnp.dot(lhs, rhs, preferred_element_type=F32)
        hh = nd[:, :dh] / jnp.maximum(jnp.abs(nd[:, dh:]), column(inv_scale, h))

        kw = (column(w_state, h) * kh).astype(BF16)
        kv = lax.dot_general(kw, vaug, (((0,), (0,)), ((), ())), preferred_element_type=F32)
        c_ref[h] = decay[:, h:h + 1] * c_prev + kv

        y = hh * lax.rsqrt(jnp.mean(hh * hh, axis=-1, keepdims=True) + RMS_EPS) * ng_ref[:, hs]
        y = y * _sigmoid(og_ref[0, rows, hs].astype(F32))
        out_ref[0, rows, hs] = y.astype(BF16)
        next(other_work, None)


def _mixer_kernel(x_ref, xg_ref, w0_ref, w1_ref, w2_ref,
                  qk_ref, v_ref, og_ref, if_ref, cw_ref, cb_ref, gb_ref, ng_ref,
                  hm_ref, a0_ref, a1_ref, a2_ref, tail_ref, c_ref, m_ref, *, chunk):
    @pl.when(pl.program_id(1) == 0)
    def _():
        tail_ref[...] = jnp.zeros_like(tail_ref)
        c_ref[...] = jnp.zeros_like(c_ref)
        m_ref[...] = jnp.zeros_like(m_ref)

    x = x_ref[...]
    h = (x * lax.rsqrt(jnp.mean(x * x, axis=-1, keepdims=True) + RMS_EPS) * xg_ref[...]).astype(BF16)
    pieces = itertools.chain(_attn_proj_pieces(h, w0_ref, a0_ref), _attn_proj_pieces(h, w1_ref, a1_ref),
                             _attn_proj_pieces(h, w2_ref, a2_ref))
    n_pieces = 3 * (QKV_WIDTH // 256) + 2
    n_heads = (x_ref.shape[0] // chunk) * MLSTM_HEADS
    def some_pieces():
        done = 0
        for head in range(1, n_heads + 1):
            while done * n_heads < head * n_pieces:
                next(pieces, None)
                done += 1
            yield

    other_work = some_pieces()
    for r0 in range(0, x_ref.shape[0], chunk):
        _mlstm_chunk(slice(r0, r0 + chunk), qk_ref, v_ref, og_ref, if_ref, cw_ref, cb_ref, gb_ref,
                     ng_ref, hm_ref, tail_ref, c_ref, m_ref, other_work)
    for _ in pieces:
        pass


def _mixer_inputs(x2, norm_g, w_qkv, proj3, ifg3, conv_w, conv_b, gate_b, mnorm_g, *,
                  rows=256, chunk=MLSTM_CHUNK):
    bsz, s, _ = proj3.shape
    w = MLSTM_WIDTH
    n_blk = s // rows
    dilations = [d for _, d in ATTN_GROUPS]

    def const(shape):
        return pl.BlockSpec(shape, lambda b, i: (0,) * len(shape))

    def weight():
        return pl.BlockSpec((D_MODEL, QKV_WIDTH), lambda b, i: (0, 0), pipeline_mode=pl.Buffered(1))

    kern = functools.partial(_mixer_kernel, chunk=chunk)
    outs = pl.pallas_call(
        kern,
        grid=(bsz, n_blk),
        in_specs=[
            pl.BlockSpec((rows, D_MODEL), lambda b, i: (b * n_blk + i, 0)),
            const((1, D_MODEL)),
            weight(), weight(), weight(),
            pl.BlockSpec((1, rows, 2 * w), lambda b, i: (b, i, COL_QK // (2 * w))),
            pl.BlockSpec((1, rows, w), lambda b, i: (b, i, COL_V // w)),
            pl.BlockSpec((1, rows, w), lambda b, i: (b, i, COL_O // w)),
            pl.BlockSpec((1, rows, IF_WIDTH), lambda b, i: (b, i, 0)),
            const((CONV_WIDTH, 2 * w)), const((1, 2 * w)), const((1, IF_WIDTH)), const((1, w)),
        ],
        out_specs=[pl.BlockSpec((1, rows, w), lambda b, i: (b, i, 0))] + [
            pl.BlockSpec((1, d, rows // d, QKV_WIDTH), lambda b, i: (b, 0, i, 0)) for d in dilations],
        out_shape=[jax.ShapeDtypeStruct((bsz, s, w), BF16)] + [
            jax.ShapeDtypeStruct((bsz, d, s // d, QKV_WIDTH), BF16) for d in dilations],
        scratch_shapes=[
            pltpu.VMEM((16, 2 * w), BF16),
            pltpu.VMEM((MLSTM_HEADS, HEAD_DIM, 2 * HEAD_DIM), F32),
            pltpu.VMEM((1, LANES), F32),
        ],
        compiler_params=_params(("arbitrary", "arbitrary")),
        name="mixer_inputs",
    )(x2, norm_g, *w_qkv, proj3, proj3, proj3, ifg3, conv_w, conv_b, gate_b, mnorm_g)
    return outs[0], outs[1:]


def _attn_kernel(q_ref, kp_ref, kc_ref, vp_ref, vc_ref, bias_ref, o_ref, lse_ref):
    qb = ATTN_BLOCK
    dh = HEAD_DIM
    n_q = q_ref.shape[2] // qb
    first = pl.program_id(2) == 0
    kcol = lax.broadcasted_iota(jnp.int32, (qb, 2 * qb), 1)
    dead = jnp.logical_and(first, kcol < qb)
    lane = lax.broadcasted_iota(jnp.int32, (qb, LANES), 1)
    for jq in range(n_q):
        rows = slice(jq * qb, (jq + 1) * qb)
        lse_all = jnp.zeros((qb, LANES), F32)
        for j in range(HEADS_PER_GROUP):
            hs = slice(j * dh, (j + 1) * dh)
            q = q_ref[0, 0][rows, hs]
            if jq == 0:
                k = jnp.concatenate([kp_ref[0, 0][:, hs], kc_ref[0, 0][0:qb, hs]], axis=0)
                v = jnp.concatenate([vp_ref[0, 0][:, hs], vc_ref[0, 0][0:qb, hs]], axis=0)
            else:
                k = kc_ref[0, 0][(jq - 1) * qb:(jq + 1) * qb, hs]
                v = vc_ref[0, 0][(jq - 1) * qb:(jq + 1) * qb, hs]
            s = lax.dot_general(q, k, (((1,), (1,)), ((), ())), preferred_element_type=F32)
            s = s * (dh ** -0.5) + bias_ref[j]
            if jq == 0:
                s = jnp.where(dead, NEG, s)
            mx = jnp.max(s, axis=1, keepdims=True)
            p = jnp.exp(s - mx)
            den = jnp.sum(p, axis=1, keepdims=True)
            o = jnp.dot(p.astype(BF16), v, preferred_element_type=F32) / den
            o_ref[0, 0, rows, hs] = o.astype(BF16)
            lse_all = jnp.where(lane == j, mx + jnp.log(den), lse_all)
        lse_ref[0, 0, rows, :] = lse_all


def _attn_group(qkv, bias, g, *, q_blocks=4):
    bsz, dilation, n, _ = qkv.shape
    gw = GROUP_WIDTH
    q_blocks = min(q_blocks, n // ATTN_BLOCK)
    run = q_blocks * ATTN_BLOCK

    def own(part, width=gw):
        return pl.BlockSpec((1, 1, run, width), lambda b, r, i: (b, r, i, part))

    def previous(part):
        return pl.BlockSpec((1, 1, ATTN_BLOCK, gw),
                            lambda b, r, i: (b, r, jnp.maximum(i * q_blocks - 1, 0), part))

    return pl.pallas_call(
        _attn_kernel,
        grid=(bsz, dilation, n // run),
        in_specs=[
            own(0),
            previous(1), own(1),
            previous(2), own(2),
            pl.BlockSpec((HEADS_PER_GROUP, ATTN_BLOCK, 2 * ATTN_BLOCK), lambda b, r, i: (0, 0, 0)),
        ],
        out_specs=[own(0), own(0, LANES)],
        out_shape=[
            jax.ShapeDtypeStruct((bsz, dilation, n, gw), BF16),
            jax.ShapeDtypeStruct((bsz, dilation, n, LANES), F32),
        ],
        compiler_params=_params(("arbitrary", "arbitrary", "arbitrary")),
        name=f"attn_g{g}",
    )(qkv, qkv, qkv, qkv, qkv, bias)


def _t5_bucket(dist):
    max_exact = REL_BUCKETS // 2
    d_f = jnp.maximum(dist, 1).astype(F32)
    large = max_exact + (jnp.log(d_f / max_exact) / math.log(REL_MAX_DIST / max_exact)
                         * (REL_BUCKETS - max_exact)).astype(jnp.int32)
    large = jnp.minimum(large, REL_BUCKETS - 1)
    return jnp.where(dist < max_exact, dist, large)


def _attn_bias(rel_bias, g, dilation):
    span = ATTN_SPAN
    buckets = _t5_bucket(jnp.arange(span + 1, dtype=jnp.int32) * dilation)
    vec = rel_bias[buckets][:, g * HEADS_PER_GROUP:(g + 1) * HEADS_PER_GROUP].T.astype(F32)
    qpos = jnp.arange(ATTN_BLOCK)[:, None]
    kpos = jnp.arange(2 * ATTN_BLOCK)[None, :]
    dist = qpos + ATTN_BLOCK - kpos
    valid = (dist >= 0) & (dist <= span)
    onehot = (dist[:, :, None] == jnp.arange(span + 1)[None, None, :]).astype(F32)
    table = jnp.einsum('qkj,hj->hqk', onehot, vec, precision=lax.Precision.HIGHEST)
    return jnp.where(valid[None], table, NEG)


def _merge_kernel(hm_ref, o0_ref, o1_ref, o2_ref, l0_ref, l1_ref, l2_ref, gm_ref, ga_ref, x_ref,
                  wbm_ref, wba_ref, wo_ref, ng_ref, rw_ref, rb_ref,
                  x1_ref, h2_ref, idx_ref, gate_ref, rank_ref, count_ref, cnt_ref):
    dh = HEAD_DIM
    bm = x_ref.shape[0]

    def token_order(ref):
        dilation = ref.shape[1]
        if dilation == 1:
            return ref[0, 0]
        perm = _residue_perm(bm, dilation, True).astype(BF16)
        blk = ref[0].reshape(bm, ref.shape[3])
        parts = [blk] if blk.dtype == BF16 else _bf16_parts(blk, 3)
        out = None
        for p in parts:
            moved = jnp.dot(perm, p, preferred_element_type=F32)
            out = moved if out is None else out + moved
        return out

    l0, l1, l2 = token_order(l0_ref), token_order(l1_ref), token_order(l2_ref)
    o0, o1, o2 = token_order(o0_ref), token_order(o1_ref), token_order(o2_ref)
    mx = jnp.maximum(jnp.maximum(l0, l1), l2)
    e0, e1, e2 = jnp.exp(l0 - mx), jnp.exp(l1 - mx), jnp.exp(l2 - mx)
    den = e0 + e1 + e2
    w0, w1, w2 = e0 / den, e1 / den, e2 / den
    parts = []
    for j in range(HEADS_PER_GROUP):
        hs = slice(j * dh, (j + 1) * dh)
        parts.append(w0[:, j:j + 1] * o0[:, hs] + w1[:, j:j + 1] * o1[:, hs]
                     + w2[:, j:j + 1] * o2[:, hs])
    ha = jnp.concatenate(parts, axis=1).astype(BF16)
    ym = jnp.dot(hm_ref[...], wbm_ref[...], preferred_element_type=F32)
    ya = jnp.dot(ha, wba_ref[...], preferred_element_type=F32)
    merged = (_sigmoid(gm_ref[...].astype(F32)) * ym + _sigmoid(ga_ref[...].astype(F32)) * ya)
    x1 = x_ref[...] + jnp.dot(merged.astype(BF16), wo_ref[...], preferred_element_type=F32)
    x1_ref[...] = x1
    h2 = x1 * lax.rsqrt(jnp.mean(x1 * x1, axis=-1, keepdims=True) + RMS_EPS) * ng_ref[...]
    _store_token_tiles(h2_ref, _pack_pairs(h2))
    h_hi, h_lo = _bf16_parts(h2, 2)
    logits = jnp.dot(jnp.concatenate([h_hi, h_lo, h_hi], axis=1), rw_ref[...],
                     preferred_element_type=F32) + rb_ref[...]
    lane = lax.broadcasted_iota(jnp.int32, logits.shape, 1)
    idx_all = jnp.zeros(logits.shape, jnp.int32)
    val_all = jnp.zeros(logits.shape, F32)
    top0 = None
    esum = None
    picks = []
    chosen = jnp.zeros(logits.shape, F32)
    for k in range(TOP_K):
        m = jnp.max(logits, axis=1, keepdims=True)
        sel = jnp.min(jnp.where(logits == m, lane, LANES), axis=1, keepdims=True)
        if k == 0:
            top0 = m
        e = jnp.exp(m - top0)
        esum = e if k == 0 else esum + e
        idx_all = jnp.where(lane == k, sel, idx_all)
        val_all = jnp.where(lane == k, e, val_all)
        pick = lane == sel
        picks.append(pick)
        chosen = chosen + pick.astype(F32)
        logits = jnp.where(pick, -jnp.inf, logits)
    idx_ref[...] = idx_all
    gate_ref[...] = val_all / esum

    @pl.when(pl.program_id(0) == 0)
    def _():
        cnt_ref[...] = jnp.zeros_like(cnt_ref)

    trow = lax.broadcasted_iota(jnp.int32, (bm, bm), 0)
    tcol = lax.broadcasted_iota(jnp.int32, (bm, bm), 1)
    earlier = jnp.dot((tcol < trow).astype(BF16), chosen.astype(BF16),
                      preferred_element_type=F32) + cnt_ref[...]
    rank_all = jnp.zeros(logits.shape, jnp.int32)
    for k in range(TOP_K):
        r = jnp.sum(jnp.where(picks[k], earlier, 0.0), axis=1, keepdims=True)
        rank_all = jnp.where(lane == k, r.astype(jnp.int32), rank_all)
    rank_ref[...] = rank_all
    cnt_ref[...] = cnt_ref[...] + jnp.sum(chosen, axis=0, keepdims=True)
    count_ref[...] = jnp.broadcast_to(cnt_ref[...], count_ref.shape)


def _merge(hm, outs, lses, proj, x2, wbm, wba, wo, ng, rw, rb, *, bm=256):
    n = x2.shape[0]
    d = D_MODEL
    gcol = COL_GATES // d
    n_blk = outs[0].shape[2] // bm

    def rows(width):
        return pl.BlockSpec((bm, width), lambda i: (i, 0))

    def full(a, b):
        return pl.BlockSpec((a, b), lambda i: (0, 0), pipeline_mode=pl.Buffered(1))

    def residue(arr):
        dilation, width = arr.shape[1], arr.shape[3]
        return pl.BlockSpec((1, dilation, bm // dilation, width),
                            lambda i: (i // n_blk, 0, i % n_blk, 0))

    return pl.pallas_call(
        _merge_kernel,
        grid=(n // bm,),
        in_specs=[
            rows(MLSTM_WIDTH),
            residue(outs[0]), residue(outs[1]), residue(outs[2]),
            residue(lses[0]), residue(lses[1]), residue(lses[2]),
            pl.BlockSpec((bm, d), lambda i: (i, gcol)),
            pl.BlockSpec((bm, d), lambda i: (i, gcol + 1)),
            rows(d),
            full(MLSTM_WIDTH, d), full(GROUP_WIDTH, d), full(d, d),
            full(1, d), full(3 * d, LANES), full(1, LANES),
        ],
        out_specs=[rows(d), pl.BlockSpec((bm * PACK_ROWS, LANES), lambda i: (i, 0)),
                   rows(LANES), rows(LANES), rows(LANES),
                   pl.BlockSpec((8, LANES), lambda i: (0, 0))],
        out_shape=[
            jax.ShapeDtypeStruct((n, d), F32),
            jax.ShapeDtypeStruct((n * PACK_ROWS, LANES), jnp.uint32),
            jax.ShapeDtypeStruct((n, LANES), jnp.int32),
            jax.ShapeDtypeStruct((n, LANES), F32),
            jax.ShapeDtypeStruct((n, LANES), jnp.int32),
            jax.ShapeDtypeStruct((8, LANES), F32),
        ],
        scratch_shapes=[pltpu.VMEM((1, LANES), F32)],
        compiler_params=_params(("arbitrary",)),
        name="merge",
    )(hm, outs[0], outs[1], outs[2], lses[0], lses[1], lses[2], proj, proj, x2,
      wbm, wba, wo, ng, rw, rb)


PACK_ROWS = 8


def _pack_pairs(x):
    w = x.shape[1] // 2
    lo = lax.bitcast_convert_type(x[:, :w].astype(BF16).astype(F32), jnp.uint32) >> 16
    hi = lax.bitcast_convert_type(x[:, w:].astype(BF16).astype(F32), jnp.uint32)
    return (hi & jnp.uint32(0xFFFF0000)) | lo


def _unpack_pairs(words):
    lo = lax.bitcast_convert_type(words << 16, F32)
    hi = lax.bitcast_convert_type(words & jnp.uint32(0xFFFF0000), F32)
    return lo, hi


def _store_token_tiles(ref, words):
    rows = words.shape[0]
    for s in range(PACK_ROWS):
        ref[pl.ds(s, rows, stride=PACK_ROWS), :] = words[:, s * LANES:(s + 1) * LANES]


def _dispatch_kernel(fs_ref, fl_ref, nu_ref, dest_ref, hp_ref, xb_hbm, zero_ref, sem, zsem, *,
                     tokens, tm, n_blk):
    pr = PACK_ROWS
    fill_sizes = [1 << b for b in reversed(range((tm - 1).bit_length()))]

    def fill_copies(e):
        off = fs_ref[e]
        for p in fill_sizes:
            take = (fl_ref[e] & p) != 0
            dst = xb_hbm.at[pl.ds(pl.multiple_of(off * pr, pr), p * pr), :]
            yield take, pltpu.make_async_copy(zero_ref.at[pl.ds(0, p * pr), :], dst, zsem)
            off = off + jnp.where(take, p, 0)

    def block_copy(b):
        dst = xb_hbm.at[pl.ds(pl.multiple_of(b * (tm * pr), tm * pr), tm * pr), :]
        return pltpu.make_async_copy(zero_ref, dst, zsem)

    @pl.when(pl.program_id(0) == 0)
    def _():
        zero_ref[...] = jnp.zeros_like(zero_ref)
        for start in (True, False):
            def per_expert(e, carry):
                for take, cp in fill_copies(e):
                    @pl.when(take)
                    def _():
                        cp.start() if start else cp.wait()
                return carry

            def per_block(b, carry):
                block_copy(b).start() if start else block_copy(b).wait()
                return carry

            lax.fori_loop(0, N_EXPERTS, per_expert, 0)
            lax.fori_loop(nu_ref[0], n_blk, per_block, 0)

    def issue(t, carry):
        src = hp_ref.at[pl.ds(pl.multiple_of(t * pr, pr), pr), :]
        for k in range(TOP_K):
            slot = dest_ref[t * TOP_K + k]
            pltpu.make_async_copy(src, xb_hbm.at[pl.ds(pl.multiple_of(slot * pr, pr), pr), :],
                                  sem).start(priority=k % 2)
        return carry

    lax.fori_loop(0, tokens, issue, 0)
    for k in range(TOP_K):
        pltpu.make_async_copy(hp_ref, xb_hbm.at[pl.ds(0, tokens * pr), :], sem).wait()


def _dispatch(fill_start, fill_len, n_used, dest, h2p, n_pad, *, tokens=512, tm=MOE_TILE):
    pr = PACK_ROWS
    n_tok = h2p.shape[0] // pr
    kern = functools.partial(_dispatch_kernel, tokens=tokens, tm=tm, n_blk=n_pad // tm)
    return pl.pallas_call(
        kern,
        grid_spec=pltpu.PrefetchScalarGridSpec(
            num_scalar_prefetch=3,
            grid=(n_tok // tokens,),
            in_specs=[
                pl.BlockSpec((tokens * TOP_K,), lambda i, fs, fl, nu: (i,),
                             memory_space=pltpu.SMEM),
                pl.BlockSpec((tokens * pr, LANES), lambda i, fs, fl, nu: (i, 0)),
            ],
            out_specs=pl.BlockSpec(memory_space=pl.ANY),
            scratch_shapes=[pltpu.VMEM((tm * pr, LANES), jnp.uint32),
                            pltpu.SemaphoreType.DMA(()), pltpu.SemaphoreType.DMA(())],
        ),
        out_shape=jax.ShapeDtypeStruct((n_pad * pr, LANES), jnp.uint32),
        compiler_params=_params(("arbitrary",)),
        name="dispatch",
    )(fill_start, fill_len, n_used, dest, h2p)


def _cast_rows(src_ref, dst_ref, rows):
    total = src_ref.shape[0]

    def body(i, carry):
        r = pl.multiple_of(i * rows, rows)
        dst_ref[pl.ds(r, rows), :] = src_ref[pl.ds(r, rows), :].astype(BF16)
        return carry

    lax.fori_loop(0, total // rows, body, 0)


def _stream_expert_weights(be_ref, nu_ref, ne_ref, copies, convert):
    k, j = pl.program_id(0), pl.program_id(1)
    expert = be_ref[j]
    used = j < nu_ref[0]
    run_start = jnp.logical_or(j == 0, expert != be_ref[jnp.maximum(j - 1, 0)])

    @pl.when(jnp.logical_and(k == 0, j == 0))
    def _():
        for cp in copies(expert, k):
            cp.start()

    @pl.when(jnp.logical_and(used, run_start))
    def _():
        for cp in copies(expert, k):
            cp.wait()
        convert()
        last_run = ne_ref[j] < 0
        nxt_e = jnp.where(last_run, be_ref[0], ne_ref[j])
        nxt_k = jnp.where(last_run, k + 1, k)

        @pl.when(nxt_k < pl.num_programs(0))
        def _():
            for cp in copies(nxt_e, nxt_k):
                cp.start()

    return used


def _per_block_rows(used, bv_ref, o_ref, compute, store=None, rows_per_token=1):
    tm = o_ref.shape[0] // rows_per_token
    need = (bv_ref[pl.program_id(1)] + (MOE_ROW_STEP - 1)) // MOE_ROW_STEP

    def run(rows):
        res = compute(rows)
        head = o_ref.at[pl.ds(0, rows * rows_per_token), :]
        if store is None:
            head[...] = res
        else:
            store(head, res)
        if rows < tm:
            tail = o_ref.at[pl.ds(rows * rows_per_token, (tm - rows) * rows_per_token), :]
            tail[...] = jnp.zeros_like(tail)

    for q in range(1, tm // MOE_ROW_STEP + 1):
        @pl.when(jnp.logical_and(used, need == q))
        def _():
            run(q * MOE_ROW_STEP)

    @pl.when(jnp.logical_not(used))
    def _():
        o_ref[...] = jnp.zeros_like(o_ref)


def _gate_up_kernel(be_ref, nu_ref, ne_ref, bv_ref, x_ref, w_hbm, bg_ref, bu_ref, o_ref,
                    stage_ref, w_bf, sem):
    tf = o_ref.shape[1]
    kt = pl.num_programs(0)

    def copies(expert, k):
        return [pltpu.make_async_copy(
            w_hbm.at[expert, :, pl.ds(pl.multiple_of((half * kt + k) * tf, tf), tf)],
            stage_ref.at[half], sem.at[half]) for half in range(2)]

    def convert():
        for half in range(2):
            _cast_rows(stage_ref.at[half], w_bf.at[half], 256)

    used = _stream_expert_weights(be_ref, nu_ref, ne_ref, copies, convert)

    def compute(rows):
        halves = [_unpack_pairs(x_ref[pl.ds(s, rows, stride=PACK_ROWS), :])
                  for s in range(PACK_ROWS)]
        x = jnp.concatenate([h[0].astype(BF16) for h in halves]
                            + [h[1].astype(BF16) for h in halves], axis=1)
        gate = jnp.dot(x, w_bf[0], preferred_element_type=F32) + bg_ref[0]
        up = jnp.dot(x, w_bf[1], preferred_element_type=F32) + bu_ref[0]
        gate = jnp.minimum(gate, SWIGLU_LIMIT)
        up = jnp.clip(up, -SWIGLU_LIMIT, SWIGLU_LIMIT)
        glu = gate * _sigmoid(SWIGLU_ALPHA * gate)
        return ((up + 1.0) * glu).astype(BF16)

    _per_block_rows(used, bv_ref, o_ref, compute)


def _used_block(j, nu):
    return jnp.minimum(j, nu[0] - 1)


def _gate_up(sched, xb, w_gate_up, b_gate_up3, *, tm=MOE_TILE, tf=1024):
    n_pad, d = xb.shape[0] // PACK_ROWS, D_MODEL
    kt = D_FF // tf

    def bspec(off):
        return pl.BlockSpec((1, 1, tf),
                            lambda k, j, be, nu, ne, bv: (be[_used_block(j, nu)], 0, off + k))

    return pl.pallas_call(
        _gate_up_kernel,
        grid_spec=pltpu.PrefetchScalarGridSpec(
            num_scalar_prefetch=4,
            grid=(kt, n_pad // tm),
            in_specs=[
                pl.BlockSpec((tm * PACK_ROWS, LANES),
                             lambda k, j, be, nu, ne, bv: (_used_block(j, nu), 0)),
                pl.BlockSpec(memory_space=pl.ANY),
                bspec(0), bspec(kt),
            ],
            out_specs=pl.BlockSpec((tm, tf), lambda k, j, be, nu, ne, bv: (j, k)),
            scratch_shapes=[pltpu.VMEM((2, d, tf), F32), pltpu.VMEM((2, d, tf), BF16),
                            pltpu.SemaphoreType.DMA((2,))],
        ),
        out_shape=jax.ShapeDtypeStruct((n_pad, D_FF), BF16),
        compiler_params=_params(("arbitrary", "arbitrary")),
        name="gate_up",
    )(*sched, xb, w_gate_up, b_gate_up3, b_gate_up3)


def _down_kernel(be_ref, nu_ref, ne_ref, bv_ref, h_ref, w_hbm, bd_ref, o_ref, stage_ref, w_bf,
                 sem):
    def copies(expert, k):
        del k
        return [pltpu.make_async_copy(w_hbm.at[expert], stage_ref, sem.at[0])]

    def convert():
        _cast_rows(stage_ref, w_bf, 256)

    used = _stream_expert_weights(be_ref, nu_ref, ne_ref, copies, convert)

    def compute(rows):
        y = jnp.dot(h_ref[0:rows, :], w_bf[...], preferred_element_type=F32) + bd_ref[0]
        return _pack_pairs(y)

    _per_block_rows(used, bv_ref, o_ref, compute, store=_store_token_tiles,
                    rows_per_token=PACK_ROWS)


def _down(sched, hb, w_down, b_down3, *, tm=MOE_TILE):
    n_pad, f = hb.shape
    d = D_MODEL
    return pl.pallas_call(
        _down_kernel,
        grid_spec=pltpu.PrefetchScalarGridSpec(
            num_scalar_prefetch=4,
            grid=(1, n_pad // tm),
            in_specs=[
                pl.BlockSpec((tm, f), lambda k, j, be, nu, ne, bv: (_used_block(j, nu), 0)),
                pl.BlockSpec(memory_space=pl.ANY),
                pl.BlockSpec((1, 1, d),
                             lambda k, j, be, nu, ne, bv: (be[_used_block(j, nu)], 0, 0)),
            ],
            out_specs=pl.BlockSpec((tm * PACK_ROWS, LANES), lambda k, j, be, nu, ne, bv: (j, 0)),
            scratch_shapes=[pltpu.VMEM((f, d), F32), pltpu.VMEM((f, d), BF16),
                            pltpu.SemaphoreType.DMA((1,))],
        ),
        out_shape=jax.ShapeDtypeStruct((n_pad * PACK_ROWS, LANES), jnp.uint32),
        compiler_params=_params(("arbitrary", "arbitrary")),
        name="down",
    )(*sched, hb, w_down, b_down3)


def _combine_kernel(dest_ref, dest_next_ref, y_hbm, gate_ref, x1_ref, ng_ref, o_ref, buf_ref, sem,
                    *, rows):
    pr = PACK_ROWS
    slab = 64
    half = x1_ref.shape[1] // 2
    i = pl.program_id(0)
    slot_rows = TOP_K * rows * pr
    cur = i % 2
    base = pl.multiple_of(cur * slot_rows, slot_rows)
    nxt_base = pl.multiple_of((1 - cur) * slot_rows, slot_rows)

    def gather(idx_ref, dst_base, buf, t):
        for k in range(TOP_K):
            slot = idx_ref[t * TOP_K + k]
            pltpu.make_async_copy(
                y_hbm.at[pl.ds(pl.multiple_of(slot * pr, pr), pr), :],
                buf_ref.at[pl.ds(pl.multiple_of(dst_base + (k * rows + t) * pr, pr), pr), :],
                sem.at[buf]).start(priority=k % 2)

    @pl.when(i == 0)
    def _():
        def first(t, carry):
            gather(dest_ref, 0, 0, t)
            return carry

        lax.fori_loop(0, rows, first, 0)

    pltpu.make_async_copy(y_hbm.at[pl.ds(0, slot_rows), :],
                          buf_ref.at[pl.ds(base, slot_rows), :], sem.at[cur]).wait()

    def reduce_slab(t0):
        gates = [gate_ref[pl.ds(t0, slab), k:k + 1] for k in range(TOP_K)]
        lows, highs = [], []
        for s in range(pr):
            lo = x1_ref[pl.ds(t0, slab), s * LANES:(s + 1) * LANES]
            hi = x1_ref[pl.ds(t0, slab), half + s * LANES:half + (s + 1) * LANES]
            for k in range(TOP_K):
                y_lo, y_hi = _unpack_pairs(
                    buf_ref[pl.ds(base + (k * rows + t0) * pr + s, slab, stride=pr), :])
                lo = lo + gates[k] * y_lo
                hi = hi + gates[k] * y_hi
            lows.append(lo)
            highs.append(hi)
        acc = jnp.concatenate(lows + highs, axis=1)
        o_ref[pl.ds(t0, slab), :] = (
            acc * lax.rsqrt(jnp.mean(acc * acc, axis=-1, keepdims=True) + RMS_EPS) * ng_ref[...])

    def trip_with_prefetch(j, carry):
        t0 = pl.multiple_of(j * slab, slab)
        for tt in range(slab):
            gather(dest_next_ref, nxt_base, 1 - cur, t0 + tt)
        reduce_slab(t0)
        return carry

    def trip(j, carry):
        reduce_slab(pl.multiple_of(j * slab, slab))
        return carry

    has_next = i + 1 < pl.num_programs(0)

    @pl.when(has_next)
    def _():
        lax.fori_loop(0, rows // slab, trip_with_prefetch, 0)

    @pl.when(jnp.logical_not(has_next))
    def _():
        lax.fori_loop(0, rows // slab, trip, 0)


def _combine(dest, yb, gates, x1, ng, *, rows=512):
    n, d = x1.shape
    kern = functools.partial(_combine_kernel, rows=rows)
    n_steps = n // rows
    return pl.pallas_call(
        kern,
        grid=(n_steps,),
        in_specs=[
            pl.BlockSpec((rows * TOP_K,), lambda i: (i,), memory_space=pltpu.SMEM),
            pl.BlockSpec((rows * TOP_K,), lambda i: (jnp.minimum(i + 1, n_steps - 1),),
                         memory_space=pltpu.SMEM),
            pl.BlockSpec(memory_space=pl.ANY),
            pl.BlockSpec((rows, LANES), lambda i: (i, 0)),
            pl.BlockSpec((rows, d), lambda i: (i, 0)),
            pl.BlockSpec((1, d), lambda i: (0, 0)),
        ],
        out_specs=pl.BlockSpec((rows, d), lambda i: (i, 0)),
        out_shape=jax.ShapeDtypeStruct((n, d), F32),
        scratch_shapes=[pltpu.VMEM((2 * TOP_K * rows * PACK_ROWS, LANES), jnp.uint32),
                        pltpu.SemaphoreType.DMA((2,))],
        compiler_params=_params(("arbitrary",)),
        name="combine",
    )(dest, dest, yb, gates, x1, ng)


def _routing(top_idx, rank, counts, tm):
    n_tok = top_idx.shape[0]
    n_asg = n_tok * TOP_K
    e_flat = top_idx.reshape(n_asg)
    padded = (counts + tm - 1) // tm * tm
    pend = jnp.cumsum(padded)
    pstart = pend - padded
    experts = jnp.arange(N_EXPERTS, dtype=jnp.int32)
    start_of = jnp.sum(jnp.where(e_flat[:, None] == experts[None, :], pstart[None, :], 0), axis=1)
    dest = (start_of + rank.reshape(n_asg)).astype(jnp.int32)
    n_blk = -(-(n_asg + N_EXPERTS * (tm - 1)) // tm)
    n_pad = n_blk * tm
    fill_start = (pstart + counts).astype(jnp.int32)
    fill_len = (padded - counts).astype(jnp.int32)
    block_start = jnp.arange(n_blk, dtype=jnp.int32) * tm
    def expert_at(slot):
        return jnp.minimum(jnp.sum((pend[None, :] <= slot[:, None]).astype(jnp.int32), axis=1),
                           N_EXPERTS - 1).astype(jnp.int32)

    def of_block_expert(table):
        return jnp.sum(jnp.where(block_e[:, None] == experts[None, :], table[None, :], 0), axis=1)

    block_e = expert_at(block_start)
    n_used = (pend[-1:] // tm).astype(jnp.int32)
    run_end = of_block_expert(pend)
    next_e = jnp.where(run_end < pend[-1], expert_at(run_end), -1)
    block_rows = jnp.clip(of_block_expert(fill_start) - block_start, 0, tm)
    sched = (block_e, n_used, next_e.astype(jnp.int32), block_rows.astype(jnp.int32))
    return dest, fill_start, fill_len, sched, n_pad


def _layer(x, norm_mix_g, w_in, conv_w, conv_b, igate_b, fgate_b, mlstm_norm_g, rel_bias,
           w_branch_mlstm, w_branch_attn, w_out, norm_moe_g, router_w, router_b,
           w_gate_up, b_gate_up, w_down, b_down, out_norm_g):
    bsz, s, d = x.shape
    n = bsz * s
    x2 = x.reshape(n, d)
    w_main, w_if, w_qkv = _split_w_in(jnp.swapaxes(w_in, 0, 1))
    bpad = jnp.zeros((LANES - MLSTM_HEADS,), F32)
    gate_b = jnp.concatenate([igate_b, bpad, fgate_b, bpad]).reshape(1, IF_WIDTH)

    norm_g = norm_mix_g.reshape(1, d)
    proj, ifg = _in_proj(x2, norm_g, w_main, w_if)
    proj3 = proj.reshape(bsz, s, D_MAIN)
    hm, qkvs = _mixer_inputs(x2, norm_g, w_qkv, proj3, ifg.reshape(bsz, s, IF_WIDTH), conv_w,
                             conv_b.reshape(1, -1), gate_b, mlstm_norm_g.reshape(1, -1))
    hm = hm.reshape(n, MLSTM_WIDTH)

    outs, lses = [], []
    for g, (_, dilation) in enumerate(ATTN_GROUPS):
        o_g_, lse_g = _attn_group(qkvs[g], _attn_bias(rel_bias, g, dilation), g)
        outs.append(o_g_)
        lses.append(lse_g)

    rw = jnp.concatenate([router_w, jnp.zeros((d, LANES - N_EXPERTS), F32)], axis=1)
    rw_hi = rw.astype(BF16)
    rw_lo = (rw - rw_hi.astype(F32)).astype(BF16)
    rw = jnp.concatenate([rw_hi, rw_hi, rw_lo], axis=0)
    rb = jnp.concatenate([router_b, jnp.full((LANES - N_EXPERTS,), NEG, F32)]).reshape(1, LANES)
    x1, h2, idx, gates, rank, counts = _merge(
        hm, outs, lses, proj, x2, w_branch_mlstm.astype(BF16), w_branch_attn.astype(BF16),
        w_out.astype(BF16), norm_moe_g.reshape(1, d), rw, rb)

    dest, fill_start, fill_len, sched, n_pad = _routing(
        idx[:, :TOP_K], rank[:, :TOP_K], counts[0, :N_EXPERTS].astype(jnp.int32), MOE_TILE)
    xb = _dispatch(fill_start, fill_len, sched[1], dest, h2, n_pad)
    hb = _gate_up(sched, xb, w_gate_up, b_gate_up.reshape(N_EXPERTS, 1, 2 * D_FF))
    yb = _down(sched, hb, w_down, b_down.reshape(N_EXPERTS, 1, d))
    out = _combine(dest, yb, gates, x1, out_norm_g.reshape(1, d))
    return out.reshape(bsz, s, d)


def kernel(x, norm_mix_g, w_in, conv_w, conv_b, igate_b, fgate_b, mlstm_norm_g, rel_bias,
           w_branch_mlstm, w_branch_attn, w_out, norm_moe_g, router_w, router_b,
           w_gate_up, b_gate_up, w_down, b_down, norm_final_g):
    assert w_in.shape[0] == 1, "single-layer block"
    return _layer(x, norm_mix_g[0], w_in[0], conv_w[0], conv_b[0], igate_b[0], fgate_b[0],
                  mlstm_norm_g[0], rel_bias, w_branch_mlstm[0], w_branch_attn[0], w_out[0],
                  norm_moe_g[0], router_w[0], router_b[0], w_gate_up[0], b_gate_up[0],
                  w_down[0], b_down[0], norm_final_g)
```
